```python
import math
import jax
import jax.numpy as jnp
from jax import lax
import numpy as np


D_MODEL = 1024
BATCH = 8
SEQ = 4096
DEPTH = 2

GRID_W = 64
CTX_LEN = 256
HEAD_DIM = 64
ATTN_W = D_MODEL // 2
N_Q_HEADS = ATTN_W // HEAD_DIM
N_KV_HEADS = max(1, N_Q_HEADS // 4)
GQA_GROUP = N_Q_HEADS // N_KV_HEADS
KV_W = N_KV_HEADS * HEAD_DIM
ROPE_AXIS_DIM = HEAD_DIM // 2
ROPE_THETA = 10000.0
Q_BLOCK = 128
FOURIER_W = D_MODEL // 4
FOURIER_GROUPS = 4
FOURIER_GROUP_DIM = FOURIER_W // FOURIER_GROUPS
HYENA_W = D_MODEL - ATTN_W - FOURIER_W
HYENA_BANDS = 16
HYENA_EMB = 1 + 2 * HYENA_BANDS
HYENA_FILTER_W = 64
HYENA_FAST_DECAY = 0.3
HYENA_SLOW_DECAY = 1.5
HYENA_TARGET = 1e-2
MIX_W = ATTN_W + FOURIER_W + HYENA_W
Q0 = 0
K0 = ATTN_W
V0 = K0 + KV_W
F0 = V0 + KV_W
H0 = F0 + FOURIER_W
IN_W = H0 + 3 * HYENA_W
D_FF = int(math.ceil(8 * D_MODEL / 3 / 128)) * 128
NORM_EPS = 1e-6

kernel_name = 'hybrid_attn_fourier_hyena_dit'


def rmsnorm(x, g):
    xf = x.astype(jnp.float32)
    y = xf * lax.rsqrt(jnp.mean(xf * xf, axis=-1, keepdims=True) + NORM_EPS)
    return (y * g.astype(jnp.float32)).astype(x.dtype)


def dwconv3(u, w, b):
    up = jnp.pad(u, ((0, 0), (1, 1), (0, 0)))
    return up[:, :-2] * w[0] + up[:, 1:-1] * w[1] + up[:, 2:] * w[2] + b


def rope2d(x, ang_r, ang_c):
    def rot(xa, ang):
        cos = jnp.cos(ang)[:, None, :].astype(xa.dtype)
        sin = jnp.sin(ang)[:, None, :].astype(xa.dtype)
        x1, x2 = jnp.split(xa, 2, axis=-1)
        return jnp.concatenate([x1 * cos - x2 * sin, x2 * cos + x1 * sin], axis=-1)
    xr, xc = jnp.split(x, 2, axis=-1)
    return jnp.concatenate([rot(xr, ang_r), rot(xc, ang_c)], axis=-1)


def attend(q, k, v):
    s = jnp.einsum('bqhgd,bkhd->bhgqk', q, k).astype(jnp.float32) * (HEAD_DIM ** -0.5)
    p = jax.nn.softmax(s, axis=-1).astype(v.dtype)
    return jnp.einsum('bhgqk,bkhd->bqhgd', p, v)


def blocked_attention(q, k, v):
    B, L = q.shape[0], q.shape[1]
    nb = L // Q_BLOCK
    qb = q.reshape(B, nb, Q_BLOCK, N_KV_HEADS, GQA_GROUP, HEAD_DIM).swapaxes(0, 1)
    ob = lax.map(lambda qq: attend(qq, k, v), qb)
    return ob.swapaxes(0, 1).reshape(B, L, ATTN_W)


def fourier_mix(f, w):
    B, L, _ = f.shape
    fg = f.astype(jnp.float32).reshape(B, L, FOURIER_GROUPS, FOURIER_GROUP_DIM)
    y = jnp.fft.fft2(fg, axes=(1, 3), norm='ortho').real.astype(f.dtype)
    return y.reshape(B, L, FOURIER_W) @ w


def hyena_filter(L, w1, b1, fr1, w2, b2, fr2, w3):
    f32 = jnp.float32
    t = jnp.linspace(0.0, 1.0, L, dtype=f32)[:, None]
    bands = jnp.linspace(1e-4, HYENA_BANDS - 1, HYENA_BANDS, dtype=f32)
    ang = (2.0 * math.pi) * jnp.arange(L, dtype=f32)[:, None] / L * bands[None, :]
    z = jnp.concatenate([t, jnp.cos(ang), -jnp.sin(ang)], axis=-1)
    h = jnp.sin(fr1.astype(f32) * (z @ w1.astype(f32) + b1.astype(f32)))
    h = jnp.sin(fr2.astype(f32) * (h @ w2.astype(f32) + b2.astype(f32)))
    h = h @ w3.astype(f32)
    deltas = jnp.abs(jnp.linspace(math.log(HYENA_TARGET) / HYENA_SLOW_DECAY,
                                  math.log(HYENA_TARGET) / HYENA_FAST_DECAY, HYENA_W, dtype=f32))
    decay = jnp.exp(-t * deltas[None, :])
    h_fwd = h[:, :HYENA_W] * decay
    h_bwd = h[:, HYENA_W:] * decay
    two = jnp.concatenate([h_fwd, jnp.zeros((1, HYENA_W), f32), h_bwd[:0:-1]], axis=0)
    return two / jnp.sum(jnp.abs(two), axis=0, keepdims=True)


def hyena_longconv(u, filt, bias):
    L = u.shape[1]
    n = 2 * L
    uf = jnp.fft.rfft(u.astype(jnp.float32), n=n, axis=1)
    kf = jnp.fft.rfft(filt, n=n, axis=0)
    y = jnp.fft.irfft(uf * kf[None], n=n, axis=1)[:, :L]
    return (y + u.astype(jnp.float32) * bias.astype(jnp.float32)).astype(u.dtype)


def hyena_mix(p, conv_w, conv_b, filt, bias):
    uc = dwconv3(p, conv_w, conv_b)
    x0, x1, v = jnp.split(uc, 3, axis=-1)
    return x0 * hyena_longconv(x1 * v, filt, bias)


def conv_ffn(h, w_up, w_c, b_c, w_down):
    u = dwconv3(h @ w_up, w_c, b_c)
    gate, val = jnp.split(u, 2, axis=-1)
    return (jax.nn.silu(gate) * val) @ w_down


def setup_inputs(seed: int = 0) -> dict:
    key = jax.random.key(seed)
    ks = jax.random.split(key, 32)
    f32 = jnp.float32

    def nrm(k, shape, scale):
        return jax.random.normal(k, shape, f32) * scale

    def gain(k, shape):
        return 1.0 + 0.05 * jax.random.normal(k, shape, f32)

    return {
        'x': nrm(ks[0], (BATCH, SEQ, D_MODEL), 1.0),
        'c': nrm(ks[1], (BATCH, D_MODEL), 1.0),
        'ctx': nrm(ks[2], (BATCH, CTX_LEN, D_MODEL), 1.0),
        'c_ctx': nrm(ks[3], (D_MODEL,), 1.0),
        'w_mod': nrm(ks[4], (DEPTH, D_MODEL, 6 * D_MODEL), 0.5 * D_MODEL ** -0.5),
        'b_mod': nrm(ks[5], (DEPTH, 6 * D_MODEL), 0.02),
        'g_pre_mix': gain(ks[6], (DEPTH, D_MODEL)),
        'g_post_mix': gain(ks[7], (DEPTH, D_MODEL)),
        'g_pre_ffn': gain(ks[8], (DEPTH, D_MODEL)),
        'g_post_ffn': gain(ks[9], (DEPTH, D_MODEL)),
        'w_in': nrm(ks[10], (DEPTH, D_MODEL, IN_W), D_MODEL ** -0.5),
        'g_q': gain(ks[11], (DEPTH, HEAD_DIM)),
        'g_k': gain(ks[12], (DEPTH, HEAD_DIM)),
        'w_fourier': nrm(ks[13], (DEPTH, FOURIER_W, FOURIER_W), FOURIER_W ** -0.5),
        'w_hy_conv': nrm(ks[14], (DEPTH, 3, 3 * HYENA_W), 3 ** -0.5),
        'b_hy_conv': nrm(ks[15], (DEPTH, 3 * HYENA_W), 0.02),
        'hy_w1': nrm(ks[16], (DEPTH, HYENA_EMB, HYENA_FILTER_W), HYENA_EMB ** -0.5),
        'hy_b1': nrm(ks[17], (DEPTH, HYENA_FILTER_W), 0.1),
        'hy_fr1': gain(ks[18], (DEPTH, HYENA_FILTER_W)),
        'hy_w2': nrm(ks[19], (DEPTH, HYENA_FILTER_W, HYENA_FILTER_W), HYENA_FILTER_W ** -0.5),
        'hy_b2': nrm(ks[20], (DEPTH, HYENA_FILTER_W), 0.1),
        'hy_fr2': gain(ks[21], (DEPTH, HYENA_FILTER_W)),
        'hy_w3': nrm(ks[22], (DEPTH, HYENA_FILTER_W, 2 * HYENA_W), HYENA_FILTER_W ** -0.5),
        'hy_bias': nrm(ks[23], (DEPTH, HYENA_W), 0.1),
        'w_out': nrm(ks[24], (DEPTH, MIX_W, D_MODEL), MIX_W ** -0.5),
        'w_up': nrm(ks[25], (DEPTH, D_MODEL, 2 * D_FF), D_MODEL ** -0.5),
        'w_ffn_conv': nrm(ks[26], (DEPTH, 3, 2 * D_FF), 3 ** -0.5),
        'b_ffn_conv': nrm(ks[27], (DEPTH, 2 * D_FF), 0.02),
        'w_down': nrm(ks[28], (DEPTH, D_FF, D_MODEL), D_FF ** -0.5),
    }


def reference(x, c, ctx, c_ctx, w_mod, b_mod, g_pre_mix, g_post_mix, g_pre_ffn, g_post_ffn,
              w_in, g_q, g_k, w_fourier, w_hy_conv, b_hy_conv, hy_w1, hy_b1, hy_fr1,
              hy_w2, hy_b2, hy_fr2, hy_w3, hy_bias, w_out, w_up, w_ffn_conv, b_ffn_conv, w_down):
    B, L, _ = x.shape
    C = ctx.shape[1]
    ROWS = L // GRID_W
    row = jnp.repeat(jnp.arange(ROWS, dtype=jnp.float32), GRID_W)
    col = jnp.tile(jnp.arange(GRID_W, dtype=jnp.float32), ROWS)
    inv = ROPE_THETA ** (-jnp.arange(0, ROPE_AXIS_DIM, 2, dtype=jnp.float32) / ROPE_AXIS_DIM)
    ang_r = row[:, None] * inv[None, :]
    ang_c = col[:, None] * inv[None, :]

    for i in range(DEPTH):
        last = i == DEPTH - 1
        mod_x = (jax.nn.silu(c) @ w_mod[i] + b_mod[i])[:, None, :]
        mod_c = (jax.nn.silu(c_ctx) @ w_mod[i] + b_mod[i])[None, None, :]
        sh1, sc1, ga1, sh2, sc2, ga2 = jnp.split(mod_x, 6, axis=-1)
        csh1, csc1, cga1, csh2, csc2, cga2 = jnp.split(mod_c, 6, axis=-1)
        hy_params = (hy_w1[i], hy_b1[i], hy_fr1[i], hy_w2[i], hy_b2[i], hy_fr2[i], hy_w3[i])

        hx = rmsnorm(x, g_pre_mix[i]) * (1.0 + sc1) + sh1
        hc = rmsnorm(ctx, g_pre_mix[i]) * (1.0 + csc1) + csh1
        px = hx @ w_in[i]
        pc = hc @ (w_in[i][:, K0:F0] if last else w_in[i])
        pc_kv = pc if last else pc[..., K0:F0]

        kc = rmsnorm(pc_kv[..., :KV_W].reshape(B, C, N_KV_HEADS, HEAD_DIM), g_k[i])
        vc = pc_kv[..., KV_W:].reshape(B, C, N_KV_HEADS, HEAD_DIM)
        q = rope2d(rmsnorm(px[..., Q0:K0].reshape(B, L, N_Q_HEADS, HEAD_DIM), g_q[i]), ang_r, ang_c)
        k = rope2d(rmsnorm(px[..., K0:V0].reshape(B, L, N_KV_HEADS, HEAD_DIM), g_k[i]), ang_r, ang_c)
        v = px[..., V0:F0].reshape(B, L, N_KV_HEADS, HEAD_DIM)
        k_all = jnp.concatenate([kc, k], axis=1)
        v_all = jnp.concatenate([vc, v], axis=1)
        attn_x = blocked_attention(q.reshape(B, L, N_KV_HEADS, GQA_GROUP, HEAD_DIM), k_all, v_all)
        four_x = fourier_mix(px[..., F0:H0], w_fourier[i])
        hy_x = hyena_mix(px[..., H0:], w_hy_conv[i], b_hy_conv[i], hyena_filter(L, *hy_params), hy_bias[i])
        mix_x = jnp.concatenate([attn_x, four_x, hy_x], axis=-1) @ w_out[i]
        x = x + ga1 * rmsnorm(mix_x, g_post_mix[i])

        if not last:
            qc = rmsnorm(pc[..., Q0:K0].reshape(B, C, N_Q_HEADS, HEAD_DIM), g_q[i])
            attn_c = attend(qc.reshape(B, C, N_KV_HEADS, GQA_GROUP, HEAD_DIM), kc, vc).reshape(B, C, ATTN_W)
            four_c = fourier_mix(pc[..., F0:H0], w_fourier[i])
            hy_c = hyena_mix(pc[..., H0:], w_hy_conv[i], b_hy_conv[i], hyena_filter(C, *hy_params), hy_bias[i])
            mix_c = jnp.concatenate([attn_c, four_c, hy_c], axis=-1) @ w_out[i]
            ctx = ctx + cga1 * rmsnorm(mix_c, g_post_mix[i])

        fx = rmsnorm(x, g_pre_ffn[i]) * (1.0 + sc2) + sh2
        yx = conv_ffn(fx, w_up[i], w_ffn_conv[i], b_ffn_conv[i], w_down[i])
        x = x + ga2 * rmsnorm(yx, g_post_ffn[i])
        if not last:
            fc = rmsnorm(ctx, g_pre_ffn[i]) * (1.0 + csc2) + csh2
            yc = conv_ffn(fc, w_up[i], w_ffn_conv[i], b_ffn_conv[i], w_down[i])
            ctx = ctx + cga2 * rmsnorm(yc, g_post_ffn[i])

    return x
```

```python
import functools
import math

import numpy as np
import jax
import jax.numpy as jnp
from jax import lax
from jax.experimental import pallas as pl
from jax.experimental.pallas import tpu as pltpu

F32 = jnp.float32
BF16 = jnp.bfloat16

HEAD_DIM = 64
GQA_GROUP = 4
N_KV_HEADS = 2
GRID_W = 64
ROPE_THETA = 10000.0
FOURIER_GROUP_DIM = 64
HYENA_BANDS = 16
HYENA_FAST_DECAY = 0.3
HYENA_SLOW_DECAY = 1.5
HYENA_TARGET = 1e-2
NORM_EPS = 1e-6
LANES = 128
VMEM_LIMIT = 56 * 1024 * 1024


def _cp(sem, vmem=VMEM_LIMIT):
    return pltpu.CompilerParams(dimension_semantics=sem, vmem_limit_bytes=vmem)


def _dot(a, b):
    return jnp.dot(a, b, preferred_element_type=F32)


def _dot_nt(a, b):
    return lax.dot_general(a, b, (((1,), (1,)), ((), ())), preferred_element_type=F32)


def _split(a):
    hi = a.astype(BF16)
    lo = (a - hi.astype(F32)).astype(BF16)
    return hi, lo


def _dot3(a, b):
    ah, al = _split(a)
    bh, bl = _split(b)
    return _dot(ah, bh) + _dot(ah, bl) + _dot(al, bh)


def _rms(x, g):
    ms = jnp.mean(x * x, axis=-1, keepdims=True)
    return x * lax.rsqrt(ms + NORM_EPS) * g


def _tables_kernel(hc_ref, hs_ref, fc_ref, fs_ref, *, n, tk):
    i = pl.program_id(0)
    k = lax.broadcasted_iota(jnp.int32, (tk, n), 0) + i * tk
    s = lax.broadcasted_iota(jnp.int32, (tk, n), 1)
    ph = ((2 * k + 1) * (2 * s + 1)) & (8 * n - 1)
    a = ph.astype(F32) * (2.0 * math.pi / (8 * n))
    hc_ref[...] = jnp.cos(a).astype(BF16)
    hs_ref[...] = jnp.sin(a).astype(BF16)
    pf = (k * s) & (n - 1)
    b = pf.astype(F32) * (2.0 * math.pi / n)
    fc_ref[...] = jnp.cos(b).astype(BF16)
    fs_ref[...] = jnp.sin(b).astype(BF16)


def _make_tables(n):
    tk = min(n, 256)
    spec = pl.BlockSpec((tk, n), lambda i: (i, 0))
    shp = jax.ShapeDtypeStruct((n, n), BF16)
    return pl.pallas_call(
        functools.partial(_tables_kernel, n=n, tk=tk),
        out_shape=(shp, shp, shp, shp),
        grid=(n // tk,),
        out_specs=(spec, spec, spec, spec),
        compiler_params=_cp(("parallel",)),
        name=f"dft_tables_{n}",
    )()


def _mod_kernel(c_ref, w_ref, b_ref, o_ref):
    c = c_ref[...]
    a = c * jax.nn.sigmoid(c)
    o_ref[0] = _dot3(a, w_ref[0]) + b_ref[0]


def _modulation(cc, w_mod, b_mod):
    depth, d, n6 = w_mod.shape
    tn = 1536
    rows = cc.shape[0]
    return pl.pallas_call(
        _mod_kernel,
        out_shape=jax.ShapeDtypeStruct((depth, rows, n6), F32),
        grid=(depth, n6 // tn),
        in_specs=[
            pl.BlockSpec((rows, d), lambda l, j: (0, 0)),
            pl.BlockSpec((1, d, tn), lambda l, j: (l, 0, j)),
            pl.BlockSpec((1, 1, tn), lambda l, j: (l, 0, j)),
        ],
        out_specs=pl.BlockSpec((1, rows, tn), lambda l, j: (l, 0, j)),
        compiler_params=_cp(("parallel", "parallel")),
        name="modulation",
    )(cc, w_mod, b_mod.reshape(depth, 1, n6))


def _head_norm(p, gain, ones_bd):
    hi, lo = _split(p * p)
    ss = _dot(hi, ones_bd) + _dot(lo, ones_bd)
    return p * lax.rsqrt(ss * (1.0 / HEAD_DIM) + NORM_EPS) * gain


def _rope(xn, cos_t, sin_t):
    lane = lax.broadcasted_iota(jnp.int32, (1, LANES), 1)
    first = (lane % 32) < 16
    outs = []
    for j in range(xn.shape[1] // LANES):
        c = xn[:, j * LANES:(j + 1) * LANES]
        sw = jnp.where(first, pltpu.roll(c, LANES - 16, 1), pltpu.roll(c, 16, 1))
        outs.append(c * cos_t + sw * sin_t)
    return outs[0] if len(outs) == 1 else jnp.concatenate(outs, axis=1)


def _tile_heads(kv):
    lane = lax.broadcasted_iota(jnp.int32, (1, LANES), 1)
    low = lane < HEAD_DIM
    r = pltpu.roll(kv, HEAD_DIM, 1)
    h0 = jnp.where(low, kv, r)
    h1 = jnp.where(low, r, kv)
    return jnp.concatenate([h0, h0], axis=1), jnp.concatenate([h1, h1], axis=1)


def _proj_kernel(x_ref, sc_ref, sh_ref, g_ref, w_ref, gq_ref, gk_ref, cos_ref, sin_ref,
                 ones_ref, cs_ref, *outs, rope, kv_only):
    x = x_ref[0]
    h = _rms(x, g_ref[...]) * (1.0 + sc_ref[0]) + sh_ref[0]
    px = _dot(h.astype(BF16), w_ref[...])
    ones_bd = ones_ref[...]
    if kv_only:
        k4_ref, v4_ref = outs
        k = px[:, 0:128]
        v = px[:, 128:256]
    else:
        q_ref, k4_ref, v4_ref, fa_ref, ph_ref = outs
        aw = ones_bd.shape[0]
        q = _head_norm(px[:, 0:aw], gq_ref[...], ones_bd)
        if rope:
            q = _rope(q, cos_ref[...], sin_ref[...])
        q_ref[0] = (q * (HEAD_DIM ** -0.5)).astype(BF16)
        k = px[:, aw:aw + 128]
        v = px[:, aw + 128:aw + 256]
        f0 = aw + 256
        fw = cs_ref.shape[0]
        fa_ref[0] = _dot(px[:, f0:f0 + fw].astype(BF16), cs_ref[...]).astype(BF16)
        ph_ref[0] = px[:, f0 + fw:].astype(BF16)
    k = _head_norm(k, gk_ref[...], ones_bd[0:128, 0:128])
    if rope:
        k = _rope(k, cos_ref[...], sin_ref[...])
    k0, k1 = _tile_heads(k)
    k4_ref[0, 0] = k0.astype(BF16)
    k4_ref[0, 1] = k1.astype(BF16)
    v0, v1 = _tile_heads(v)
    v4_ref[0, 0] = v0.astype(BF16)
    v4_ref[0, 1] = v1.astype(BF16)


def _project(x, sc, sh, g, w, gq, gk, cos_t, sin_t, ones_bd, cs, *, rope, kv_only):
    b, n, d = x.shape
    tm = min(n, 512)
    wn = w.shape[1]
    aw = ones_bd.shape[0]
    fw = cs.shape[0]
    vec = lambda width: pl.BlockSpec((1, 1, width), lambda bi, i: (bi, 0, 0))
    full = lambda a: pl.BlockSpec(a.shape, lambda bi, i: (0,) * a.ndim)
    kv_spec = pl.BlockSpec((1, N_KV_HEADS, tm, 256), lambda bi, i: (bi, 0, i, 0))
    kv_shape = jax.ShapeDtypeStruct((b, N_KV_HEADS, n, 256), BF16)
    row = lambda width: pl.BlockSpec((1, tm, width), lambda bi, i: (bi, i, 0))
    if kv_only:
        out_shape = (kv_shape, kv_shape)
        out_specs = (kv_spec, kv_spec)
    else:
        hw = wn - aw - 256 - fw
        out_shape = (jax.ShapeDtypeStruct((b, n, aw), BF16), kv_shape, kv_shape,
                     jax.ShapeDtypeStruct((b, n, 2 * fw), BF16),
                     jax.ShapeDtypeStruct((b, n, hw), BF16))
        out_specs = (row(aw), kv_spec, kv_spec, row(2 * fw), row(hw))
    return pl.pallas_call(
        functools.partial(_proj_kernel, rope=rope, kv_only=kv_only),
        out_shape=out_shape,
        grid=(b, n // tm),
        in_specs=[
            row(d), vec(d), vec(d), full(g), full(w), full(gq), full(gk),
            pl.BlockSpec((tm, LANES), lambda bi, i: (i, 0)),
            pl.BlockSpec((tm, LANES), lambda bi, i: (i, 0)),
            full(ones_bd), full(cs),
        ],
        out_specs=out_specs,
        compiler_params=_cp(("parallel", "parallel")),
        name="proj_kv" if kv_only else ("proj_rope" if rope else "proj_ctx"),
    )(x, sc, sh, g, w, gq, gk, cos_t, sin_t, ones_bd, cs)


def _attn_kernel(q_ref, *refs, lks, tq):
    nsrc = len(lks)
    k_refs = refs[0:2 * nsrc:2]
    v_refs = refs[1:2 * nsrc:2]
    o_ref = refs[2 * nsrc]
    s_scr = refs[2 * nsrc + 1]
    rows = GQA_GROUP * tq
    q = q_ref[0]
    group = lax.broadcasted_iota(jnp.int32, (1, GQA_GROUP * HEAD_DIM), 1) // HEAD_DIM
    zero = jnp.zeros_like(q)
    qm = jnp.concatenate([jnp.where(group == g, q, zero) for g in range(GQA_GROUP)], axis=0)

    chunks = []
    off = 0
    for j, lk in enumerate(lks):
        tkc = min(lk, 512)
        for c in range(lk // tkc):
            chunks.append((j, c * tkc, tkc, off))
            off += tkc

    mpart = jnp.full((rows, LANES), -jnp.inf, F32)
    for j, start, tkc, o in chunks:
        s = _dot_nt(qm, k_refs[j][0, 0, start:start + tkc, :])
        s_scr[:, o:o + tkc] = s
        for c in range(tkc // LANES):
            mpart = jnp.maximum(mpart, s[:, c * LANES:(c + 1) * LANES])
    m = jnp.max(mpart, axis=-1, keepdims=True)
    mb = jnp.broadcast_to(m, (rows, LANES))

    lpart = jnp.zeros((rows, LANES), F32)
    acc = jnp.zeros((rows, GQA_GROUP * HEAD_DIM), F32)
    for j, start, tkc, o in chunks:
        ps = []
        for c in range(tkc // LANES):
            pc = jnp.exp(s_scr[:, o + c * LANES:o + (c + 1) * LANES] - mb)
            lpart = lpart + pc
            ps.append(pc.astype(BF16))
        p = jnp.concatenate(ps, axis=1)
        acc = acc + _dot(p, v_refs[j][0, 0, start:start + tkc, :])
    l = jnp.sum(lpart, axis=-1, keepdims=True)
    o_all = acc / l
    out = jnp.zeros((tq, GQA_GROUP * HEAD_DIM), F32)
    for g in range(GQA_GROUP):
        out = out + jnp.where(group == g, o_all[g * tq:(g + 1) * tq], 0.0)
    o_ref[0] = out.astype(BF16)


def _attention(q, kvs):
    b, lq, aw = q.shape
    tq = min(lq, 128)
    lks = tuple(k.shape[2] for k, _ in kvs)
    gw = GQA_GROUP * HEAD_DIM
    in_specs = [pl.BlockSpec((1, tq, gw), lambda bi, h, i: (bi, i, h))]
    args = [q]
    for k4, v4 in kvs:
        lk = k4.shape[2]
        spec = pl.BlockSpec((1, 1, lk, gw), lambda bi, h, i: (bi, h, 0, 0))
        in_specs += [spec, spec]
        args += [k4, v4]
    return pl.pallas_call(
        functools.partial(_attn_kernel, lks=lks, tq=tq),
        out_shape=jax.ShapeDtypeStruct((b, lq, aw), BF16),
        grid=(b, N_KV_HEADS, lq // tq),
        in_specs=in_specs,
        out_specs=pl.BlockSpec((1, tq, gw), lambda bi, h, i: (bi, i, h)),
        scratch_shapes=[pltpu.VMEM((GQA_GROUP * tq, sum(lks)), F32)],
        compiler_params=_cp(("parallel", "parallel", "arbitrary")),
        name=f"attention_{lq}",
    )(*args)


def _fourier_kernel(c_ref, s_ref, fa_ref, w_ref, o_ref, *, nb, scale):
    fw = w_ref.shape[0]
    ct = c_ref[...]
    st = s_ref[...]
    for j in range(nb):
        y = _dot(ct, fa_ref[j, :, 0:fw]) - _dot(st, fa_ref[j, :, fw:2 * fw])
        o_ref[j] = _dot((y * scale).astype(BF16), w_ref[...]).astype(BF16)


def _fourier(fa, ctab, stab, w_f):
    b, n, fw2 = fa.shape
    fw = fw2 // 2
    nb = 2
    tk = min(n, 512)
    scale = 1.0 / math.sqrt(n * FOURIER_GROUP_DIM)
    return pl.pallas_call(
        functools.partial(_fourier_kernel, nb=nb, scale=scale),
        out_shape=jax.ShapeDtypeStruct((b, n, fw), BF16),
        grid=(b // nb, n // tk),
        in_specs=[
            pl.BlockSpec((tk, n), lambda g, i: (i, 0)),
            pl.BlockSpec((tk, n), lambda g, i: (i, 0)),
            pl.BlockSpec((nb, n, fw2), lambda g, i: (g, 0, 0)),
            pl.BlockSpec((fw, fw), lambda g, i: (0, 0)),
        ],
        out_specs=pl.BlockSpec((nb, tk, fw), lambda g, i: (g, i, 0)),
        compiler_params=_cp(("parallel", "arbitrary")),
        name=f"fourier_{n}",
    )(ctab, stab, fa, w_f)


def _conv3(p, w, bias):
    n = p.shape[0]
    row = lax.broadcasted_iota(jnp.int32, (n, 1), 0)
    prev = jnp.where(row == 0, 0.0, pltpu.roll(p, 1, 0))
    nxt = jnp.where(row == n - 1, 0.0, pltpu.roll(p, n - 1, 0))
    return prev * w[0:1] + p * w[1:2] + nxt * w[2:3] + bias


def _hyena_pre_kernel(p0_ref, p1_ref, p2_ref, w0_ref, w1_ref, w2_ref, b0_ref, b1_ref, b2_ref,
                      z_ref, x0_ref):
    x0 = _conv3(p0_ref[0].astype(F32), w0_ref[...], b0_ref[...])
    x1 = _conv3(p1_ref[0].astype(F32), w1_ref[...], b1_ref[...])
    v = _conv3(p2_ref[0].astype(F32), w2_ref[...], b2_ref[...])
    z_ref[0] = (x1 * v).astype(BF16)
    x0_ref[0] = x0.astype(BF16)


def _hyena_pre(ph, w, bias):
    b, n, hw3 = ph.shape
    hw = hw3 // 3
    nc = hw // LANES
    pspec = lambda s: pl.BlockSpec((1, n, LANES), lambda bi, c: (bi, 0, s * nc + c))
    wspec = lambda s: pl.BlockSpec((3, LANES), lambda bi, c: (0, s * nc + c))
    bspec = lambda s: pl.BlockSpec((1, LANES), lambda bi, c: (0, s * nc + c))
    ospec = pl.BlockSpec((1, n, LANES), lambda bi, c: (bi, 0, c))
    oshape = jax.ShapeDtypeStruct((b, n, hw), BF16)
    return pl.pallas_call(
        _hyena_pre_kernel,
        out_shape=(oshape, oshape),
        grid=(b, nc),
        in_specs=[pspec(0), pspec(1), pspec(2), wspec(0), wspec(1), wspec(2),
                  bspec(0), bspec(1), bspec(2)],
        out_specs=(ospec, ospec),
        compiler_params=_cp(("parallel", "parallel")),
        name=f"hyena_pre_{n}",
    )(ph, ph, ph, w, w, w, bias, bias, bias)


def _filter_kernel(w1t_ref, w1c_ref, w1s_ref, b1_ref, fr1_ref, w2_ref, b2_ref, fr2_ref, w3_ref,
                   ghi_ref, glo_ref, *, n):
    hw = w3_ref.shape[1] // 2
    i = lax.broadcasted_iota(jnp.int32, (n, 1), 0).astype(F32)
    t = i / float(n - 1)
    jb = lax.broadcasted_iota(jnp.int32, (1, HYENA_BANDS), 1).astype(F32)
    bands = 1e-4 + jb * ((HYENA_BANDS - 1 - 1e-4) / (HYENA_BANDS - 1))
    ang = ((2.0 * math.pi) * i / float(n)) * bands
    pre = t * w1t_ref[...] + _dot3(jnp.cos(ang), w1c_ref[...]) - _dot3(jnp.sin(ang), w1s_ref[...])
    h = jnp.sin(fr1_ref[...] * (pre + b1_ref[...]))
    h = jnp.sin(fr2_ref[...] * (_dot3(h, w2_ref[...]) + b2_ref[...]))
    h = _dot3(h, w3_ref[...])
    d0 = math.log(HYENA_TARGET) / HYENA_SLOW_DECAY
    d1 = math.log(HYENA_TARGET) / HYENA_FAST_DECAY
    jd = lax.broadcasted_iota(jnp.int32, (1, hw), 1).astype(F32)
    deltas = jnp.abs(d0 + jd * ((d1 - d0) / (hw - 1)))
    decay = jnp.exp(-t * deltas)
    hf = h[:, 0:hw] * decay
    hb = jnp.where(i == 0.0, 0.0, h[:, hw:2 * hw] * decay)
    total = jnp.sum(jnp.abs(hf), axis=0, keepdims=True) + jnp.sum(jnp.abs(hb), axis=0, keepdims=True)
    g = jnp.concatenate([hf / total, hb / total], axis=1)
    hi, lo = _split(g)
    ghi_ref[...] = hi
    glo_ref[...] = lo


def _hyena_filter(n, w1, b1, fr1, w2, b2, fr2, w3):
    nb = HYENA_BANDS
    r = lambda a: a.reshape(1, -1)
    shp = jax.ShapeDtypeStruct((n, w3.shape[1]), BF16)
    return pl.pallas_call(
        functools.partial(_filter_kernel, n=n),
        out_shape=(shp, shp),
        compiler_params=pltpu.CompilerParams(vmem_limit_bytes=VMEM_LIMIT),
        name=f"hyena_filter_{n}",
    )(w1[0:1], w1[1:1 + nb], w1[1 + nb:1 + 2 * nb], r(b1), r(fr1), w2, r(b2), r(fr2), w3)


def _spectrum_kernel(c_ref, s_ref, ghi_ref, glo_ref, ck_ref, sk_ref, kre_ref, kim_ref, *, scale):
    hw = kre_ref.shape[1]
    ct = c_ref[...]
    st = s_ref[...]
    a = _dot(ct, ghi_ref[...]) + _dot(ct, glo_ref[...])
    b = _dot(st, ghi_ref[...]) + _dot(st, glo_ref[...])
    are, bre = a[:, 0:hw], a[:, hw:2 * hw]
    aim, bim = -b[:, 0:hw], -b[:, hw:2 * hw]
    ck = ck_ref[...]
    sk = sk_ref[...]
    kre_ref[...] = (ck * (are + bre) - sk * (aim + bim)) * scale
    kim_ref[...] = (ck * (aim - bim) + sk * (are - bre)) * scale


def _filter_spectrum(ghi, glo, ctab, stab):
    n, hw2 = ghi.shape
    hw = hw2 // 2
    tk = min(n, 512)
    half = (2.0 * np.arange(n, dtype=np.float64) + 1.0) * (2.0 * np.pi / (8 * n))
    ck = jnp.asarray(np.cos(half).reshape(n, 1), F32)
    sk = jnp.asarray(np.sin(half).reshape(n, 1), F32)
    tspec = pl.BlockSpec((tk, n), lambda i: (i, 0))
    gspec = pl.BlockSpec((n, hw2), lambda i: (0, 0))
    vspec = pl.BlockSpec((tk, 1), lambda i: (i, 0))
    ospec = pl.BlockSpec((tk, hw), lambda i: (i, 0))
    oshape = jax.ShapeDtypeStruct((n, hw), F32)
    return pl.pallas_call(
        functools.partial(_spectrum_kernel, scale=1.0 / n),
        out_shape=(oshape, oshape),
        grid=(n // tk,),
        in_specs=[tspec, tspec, gspec, gspec, vspec, vspec],
        out_specs=(ospec, ospec),
        compiler_params=_cp(("parallel",)),
        name=f"filter_spectrum_{n}",
    )(ctab, stab, ghi, glo, ck, sk)


def _hyena_kernel(c_ref, s_ref, z_ref, zt_ref, x0_ref, kre_ref, kim_ref, bias_ref, o_ref,
                  yre_scr, yim_scr, *, nb, tk):
    phase = pl.program_id(1)
    i = pl.program_id(2)
    ct = c_ref[...]
    st = s_ref[...]

    @pl.when(phase == 0)
    def _():
        kre = kre_ref[...]
        kim = kim_ref[...]
        r0 = pl.multiple_of(i * tk, tk)
        for j in range(nb):
            ure = _dot(ct, z_ref[j])
            uim = -_dot(st, z_ref[j])
            yre_scr[j, pl.ds(r0, tk), :] = (kre * ure - kim * uim).astype(BF16)
            yim_scr[j, pl.ds(r0, tk), :] = (kre * uim + kim * ure).astype(BF16)

    @pl.when(phase == 1)
    def _():
        for j in range(nb):
            y = _dot(ct, yre_scr[j]) - _dot(st, yim_scr[j])
            u = zt_ref[j].astype(F32)
            o_ref[j] = (x0_ref[j].astype(F32) * (y + u * bias_ref[...])).astype(BF16)


def _hyena(z, x0c, kre, kim, bias, ctab, stab):
    b, n, hw = z.shape
    nb = 2
    tk = min(n, 512)
    tspec = pl.BlockSpec((tk, n), lambda g, p, i: (i, 0))
    tile = pl.BlockSpec((nb, tk, hw), lambda g, p, i: (g, i * p, 0))
    kspec = pl.BlockSpec((tk, hw), lambda g, p, i: (i * (1 - p), 0))
    return pl.pallas_call(
        functools.partial(_hyena_kernel, nb=nb, tk=tk),
        out_shape=jax.ShapeDtypeStruct((b, n, hw), BF16),
        grid=(b // nb, 2, n // tk),
        in_specs=[
            tspec, tspec,
            pl.BlockSpec((nb, n, hw), lambda g, p, i: (g, 0, 0)),
            tile, tile, kspec, kspec,
            pl.BlockSpec((1, hw), lambda g, p, i: (0, 0)),
        ],
        out_specs=tile,
        scratch_shapes=[pltpu.VMEM((nb, n, hw), BF16), pltpu.VMEM((nb, n, hw), BF16)],
        compiler_params=_cp(("parallel", "arbitrary", "arbitrary")),
        name=f"hyena_conv_{n}",
    )(ctab, stab, z, z, x0c, kre, kim, bias)


def _mixout_kernel(a_ref, f_ref, h_ref, w_ref, x_ref, ga_ref, g_ref, o_ref):
    aw = a_ref.shape[2]
    fw = f_ref.shape[2]
    mix = (_dot(a_ref[0], w_ref[0:aw, :]) + _dot(f_ref[0], w_ref[aw:aw + fw, :])
           + _dot(h_ref[0], w_ref[aw + fw:, :]))
    o_ref[0] = x_ref[0] + ga_ref[0] * _rms(mix, g_ref[...])


def _mixout(attn, four, hy, w_out, x, ga, g):
    b, n, d = x.shape
    tm = min(n, 512)
    row = lambda width: pl.BlockSpec((1, tm, width), lambda bi, i: (bi, i, 0))
    return pl.pallas_call(
        _mixout_kernel,
        out_shape=jax.ShapeDtypeStruct((b, n, d), F32),
        grid=(b, n // tm),
        in_specs=[
            row(attn.shape[2]), row(four.shape[2]), row(hy.shape[2]),
            pl.BlockSpec(w_out.shape, lambda bi, i: (0, 0)),
            row(d),
            pl.BlockSpec((1, 1, d), lambda bi, i: (bi, 0, 0)),
            pl.BlockSpec((1, d), lambda bi, i: (0, 0)),
        ],
        out_specs=row(d),
        compiler_params=_cp(("parallel", "parallel")),
        name=f"mixout_{n}",
    )(attn, four, hy, w_out, x, ga, g)


HALO = 8


def _ffn_kernel(xp_ref, x_ref, xn_ref, sc_ref, sh_ref, g_ref, wg_ref, wv_ref, cg_ref, cv_ref,
                bg_ref, bv_ref, wd_ref, ga_ref, go_ref, o_ref, fx_scr, acc_scr, *, tm):
    i = pl.program_id(1)
    j = pl.program_id(2)
    nrow = pl.num_programs(1)
    rows = tm + 2 * HALO

    @pl.when(j == 0)
    def _():
        g = g_ref[...]
        sc = 1.0 + sc_ref[0]
        sh = sh_ref[0]
        keep_p = jnp.where(i > 0, 1.0, 0.0)
        keep_n = jnp.where(i < nrow - 1, 1.0, 0.0)
        fx_scr[0:HALO, :] = (_rms(xp_ref[0], g) * sc + sh) * keep_p
        fx_scr[HALO:HALO + tm, :] = _rms(x_ref[0], g) * sc + sh
        fx_scr[HALO + tm:rows, :] = (_rms(xn_ref[0], g) * sc + sh) * keep_n
        acc_scr[...] = jnp.zeros_like(acc_scr)

    fx = fx_scr[...].astype(BF16)

    def conv(u, w, bias):
        prev = pltpu.roll(u, 1, 0)[HALO:HALO + tm]
        nxt = pltpu.roll(u, rows - 1, 0)[HALO:HALO + tm]
        return prev * w[0:1] + u[HALO:HALO + tm] * w[1:2] + nxt * w[2:3] + bias

    gate = conv(_dot(fx, wg_ref[...]), cg_ref[...], bg_ref[...])
    val = conv(_dot(fx, wv_ref[...]), cv_ref[...], bv_ref[...])
    act = (gate * jax.nn.sigmoid(gate) * val).astype(BF16)
    acc_scr[...] += _dot(act, wd_ref[...])

    @pl.when(j == pl.num_programs(2) - 1)
    def _():
        o_ref[0] = x_ref[0] + ga_ref[0] * _rms(acc_scr[...], go_ref[...])


def _ffn(x, sc, sh, g, w_up, w_conv, b_conv, w_down, ga, g_post):
    b, n, d = x.shape
    dff = w_down.shape[0]
    tm = min(n, 1024)
    cf = 256
    nff = dff // cf
    hb = tm // HALO
    nhb = n // HALO
    vec = pl.BlockSpec((1, 1, d), lambda bi, i, j: (bi, 0, 0))
    gvec = pl.BlockSpec((1, d), lambda bi, i, j: (0, 0))
    return pl.pallas_call(
        functools.partial(_ffn_kernel, tm=tm),
        out_shape=jax.ShapeDtypeStruct((b, n, d), F32),
        grid=(b, n // tm, nff),
        in_specs=[
            pl.BlockSpec((1, HALO, d), lambda bi, i, j: (bi, jnp.maximum(i * hb - 1, 0), 0)),
            pl.BlockSpec((1, tm, d), lambda bi, i, j: (bi, i, 0)),
            pl.BlockSpec((1, HALO, d), lambda bi, i, j: (bi, jnp.minimum((i + 1) * hb, nhb - 1), 0)),
            vec, vec, gvec,
            pl.BlockSpec((d, cf), lambda bi, i, j: (0, j)),
            pl.BlockSpec((d, cf), lambda bi, i, j: (0, nff + j)),
            pl.BlockSpec((3, cf), lambda bi, i, j: (0, j)),
            pl.BlockSpec((3, cf), lambda bi, i, j: (0, nff + j)),
            pl.BlockSpec((1, cf), lambda bi, i, j: (0, j)),
            pl.BlockSpec((1, cf), lambda bi, i, j: (0, nff + j)),
            pl.BlockSpec((cf, d), lambda bi, i, j: (j, 0)),
            vec, gvec,
        ],
        out_specs=pl.BlockSpec((1, tm, d), lambda bi, i, j: (bi, i, 0)),
        scratch_shapes=[pltpu.VMEM((tm + 2 * HALO, d), F32), pltpu.VMEM((tm, d), F32)],
        compiler_params=_cp(("parallel", "parallel", "arbitrary")),
        name=f"conv_ffn_{n}",
    )(x, x, x, sc, sh, g, w_up, w_up, w_conv, w_conv, b_conv, b_conv, w_down, ga, g_post)


def _rope_tables(n):
    half = HEAD_DIM // 4
    inv = ROPE_THETA ** (-jnp.arange(0, 2 * half, 2, dtype=F32) / (2 * half))
    pos = jnp.arange(n, dtype=jnp.int32)
    row = (pos // GRID_W).astype(F32)
    col = (pos % GRID_W).astype(F32)
    ang_r = row[:, None] * inv[None, :]
    ang_c = col[:, None] * inv[None, :]
    cr, sr, cc, sc = jnp.cos(ang_r), jnp.sin(ang_r), jnp.cos(ang_c), jnp.sin(ang_c)
    cos64 = jnp.concatenate([cr, cr, cc, cc], axis=-1)
    sin64 = jnp.concatenate([-sr, sr, -sc, sc], axis=-1)
    return jnp.tile(cos64, (1, 2)), jnp.tile(sin64, (1, 2))


def _head_ones(width):
    idx = np.arange(width) // HEAD_DIM
    return jnp.asarray((idx[:, None] == idx[None, :]).astype(np.float32), BF16)


def _channel_dft(width):
    c = np.arange(width)
    same = (c[:, None] // FOURIER_GROUP_DIM) == (c[None, :] // FOURIER_GROUP_DIM)
    ang = 2.0 * np.pi * ((c[:, None] % FOURIER_GROUP_DIM) * (c[None, :] % FOURIER_GROUP_DIM)
                         % FOURIER_GROUP_DIM) / FOURIER_GROUP_DIM
    cs = np.concatenate([np.where(same, np.cos(ang), 0.0), np.where(same, np.sin(ang), 0.0)], axis=1)
    return jnp.asarray(cs.astype(np.float32), BF16)


def kernel(x, c, ctx, c_ctx, w_mod, b_mod, g_pre_mix, g_post_mix, g_pre_ffn, g_post_ffn, w_in, g_q, g_k, w_fourier, w_hy_conv, b_hy_conv, hy_w1, hy_b1, hy_fr1, hy_w2, hy_b2, hy_fr2, hy_w3, hy_bias, w_out, w_up, w_ffn_conv, b_ffn_conv, w_down):
    bsz, seq, d = x.shape
    clen = ctx.shape[1]
    depth = w_mod.shape[0]
    fw = w_fourier.shape[1]
    hw = hy_bias.shape[1]
    aw = w_in.shape[2] - 2 * N_KV_HEADS * HEAD_DIM - fw - 3 * hw
    k0 = aw
    f0 = aw + 2 * N_KV_HEADS * HEAD_DIM

    nrows = -(-(bsz + 1) // 8) * 8
    cc = jnp.zeros((nrows, d), F32).at[:bsz].set(c).at[bsz].set(c_ctx)
    mods = _modulation(cc, w_mod, b_mod)

    cos_x, sin_x = _rope_tables(seq)
    cos_c = jnp.ones((clen, LANES), F32)
    sin_c = jnp.zeros((clen, LANES), F32)
    ones_bd = _head_ones(aw)
    cs = _channel_dft(fw)
    hc_x, hs_x, fc_x, fs_x = _make_tables(seq)
    hc_c, hs_c, fc_c, fs_c = _make_tables(clen)

    def row1(v):
        return v.reshape(1, -1)

    for i in range(depth):
        last = i == depth - 1
        mx = mods[i, :bsz].reshape(bsz, 1, 6, d)
        mc = jnp.broadcast_to(mods[i, bsz].reshape(1, 1, 6, d), (bsz, 1, 6, d))
        sh1, sc1, ga1, sh2, sc2, ga2 = (mx[:, :, t] for t in range(6))
        csh1, csc1, cga1, csh2, csc2, cga2 = (mc[:, :, t] for t in range(6))
        w_in_b = w_in[i].astype(BF16)
        w_out_b = w_out[i].astype(BF16)
        w_up_b = w_up[i].astype(BF16)
        w_down_b = w_down[i].astype(BF16)
        w_f_b = w_fourier[i].astype(BF16)
        gq_t = jnp.tile(g_q[i], aw // HEAD_DIM).reshape(1, aw)
        gk_t = jnp.tile(g_k[i], N_KV_HEADS).reshape(1, N_KV_HEADS * HEAD_DIM)
        g_pre = row1(g_pre_mix[i])
        g_post = row1(g_post_mix[i])
        hy_params = (hy_w1[i], hy_b1[i], hy_fr1[i], hy_w2[i], hy_b2[i], hy_fr2[i], hy_w3[i])
        hbias = row1(hy_bias[i])
        hcw = w_hy_conv[i]
        hcb = row1(b_hy_conv[i])

        q, k4, v4, fa, ph = _project(x, sc1, sh1, g_pre, w_in_b, gq_t, gk_t, cos_x, sin_x,
                                     ones_bd, cs, rope=True, kv_only=False)
        if last:
            kc4, vc4 = _project(ctx, csc1, csh1, g_pre, w_in_b[:, k0:f0], gq_t, gk_t, cos_c, sin_c,
                                ones_bd, cs, rope=False, kv_only=True)
        else:
            qc, kc4, vc4, fac, phc = _project(ctx, csc1, csh1, g_pre, w_in_b, gq_t, gk_t, cos_c,
                                              sin_c, ones_bd, cs, rope=False, kv_only=False)
        attn_x = _attention(q, [(k4, v4), (kc4, vc4)])
        four_x = _fourier(fa, fc_x, fs_x, w_f_b)
        ghi, glo = _hyena_filter(seq, *hy_params)
        kre, kim = _filter_spectrum(ghi, glo, hc_x, hs_x)
        z, x0c = _hyena_pre(ph, hcw, hcb)
        hy_x = _hyena(z, x0c, kre, kim, hbias, hc_x, hs_x)
        x_new = _mixout(attn_x, four_x, hy_x, w_out_b, x, ga1, g_post)

        if not last:
            attn_c = _attention(qc, [(kc4, vc4)])
            four_c = _fourier(fac, fc_c, fs_c, w_f_b)
            ghi_c, glo_c = _hyena_filter(clen, *hy_params)
            kre_c, kim_c = _filter_spectrum(ghi_c, glo_c, hc_c, hs_c)
            zc, x0cc = _hyena_pre(phc, hcw, hcb)
            hy_c = _hyena(zc, x0cc, kre_c, kim_c, hbias, hc_c, hs_c)
            ctx = _mixout(attn_c, four_c, hy_c, w_out_b, ctx, cga1, g_post)
        x = x_new

        g_pf = row1(g_pre_ffn[i])
        g_of = row1(g_post_ffn[i])
        fcw = w_ffn_conv[i]
        fcb = row1(b_ffn_conv[i])
        x = _ffn(x, sc2, sh2, g_pf, w_up_b, fcw, fcb, w_down_b, ga2, g_of)
        if not last:
            ctx = _ffn(ctx, csc2, csh2, g_pf, w_up_b, fcw, fcb, w_down_b, cga2, g_of)

    return x
```

```python
import functools
import math

import numpy as np
import jax
import jax.numpy as jnp
from jax import lax
from jax.experimental import pallas as pl
from jax.experimental.pallas import tpu as pltpu

F32 = jnp.float32
BF16 = jnp.bfloat16

HEAD_DIM = 64
GQA_GROUP = 4
N_KV_HEADS = 2
GRID_W = 64
ROPE_THETA = 10000.0
FOURIER_GROUP_DIM = 64
HYENA_BANDS = 16
HYENA_FAST_DECAY = 0.3
HYENA_SLOW_DECAY = 1.5
HYENA_TARGET = 1e-2
NORM_EPS = 1e-6
LANES = 128
VMEM_LIMIT = 56 * 1024 * 1024


def _cp(sem, vmem=VMEM_LIMIT):
    return pltpu.CompilerParams(dimension_semantics=sem, vmem_limit_bytes=vmem)


def _dot(a, b):
    return jnp.dot(a, b, preferred_element_type=F32)


def _dot_nt(a, b):
    return lax.dot_general(a, b, (((1,), (1,)), ((), ())), preferred_element_type=F32)


def _split(a):
    hi = a.astype(BF16)
    lo = (a - hi.astype(F32)).astype(BF16)
    return hi, lo


def _dot3(a, b):
    ah, al = _split(a)
    bh, bl = _split(b)
    return _dot(ah, bh) + _dot(ah, bl) + _dot(al, bh)


def _rms(x, g):
    ms = jnp.mean(x * x, axis=-1, keepdims=True)
    return x * lax.rsqrt(ms + NORM_EPS) * g


def _tables_kernel(hc_ref, hs_ref, fc_ref, fs_ref, hca, hsa, fca, fsa, *, n, tk):
    i = pl.program_id(0)
    hyena_step = 2.0 * math.pi / (8 * n)
    fourier_step = 2.0 * math.pi / n

    @pl.when(i == 0)
    def _():
        k = lax.broadcasted_iota(jnp.int32, (tk, n), 0)
        s = lax.broadcasted_iota(jnp.int32, (tk, n), 1)
        a = (((2 * k + 1) * (2 * s + 1)) & (8 * n - 1)).astype(F32) * hyena_step
        hca[...] = jnp.cos(a)
        hsa[...] = jnp.sin(a)
        b = ((k * s) & (n - 1)).astype(F32) * fourier_step
        fca[...] = jnp.cos(b)
        fsa[...] = jnp.sin(b)

    s1 = lax.broadcasted_iota(jnp.int32, (1, n), 1)
    rot = (((2 * tk * i) * (2 * s1 + 1)) & (8 * n - 1)).astype(F32) * hyena_step
    cb, sb = jnp.cos(rot), jnp.sin(rot)
    hc_ref[...] = (hca[...] * cb - hsa[...] * sb).astype(BF16)
    hs_ref[...] = (hsa[...] * cb + hca[...] * sb).astype(BF16)
    rot = (((tk * i) * s1) & (n - 1)).astype(F32) * fourier_step
    cb, sb = jnp.cos(rot), jnp.sin(rot)
    fc_ref[...] = (fca[...] * cb - fsa[...] * sb).astype(BF16)
    fs_ref[...] = (fsa[...] * cb + fca[...] * sb).astype(BF16)


def _make_tables(n):
    tk = min(n, 256)
    spec = pl.BlockSpec((tk, n), lambda i: (i, 0))
    shp = jax.ShapeDtypeStruct((n, n), BF16)
    return pl.pallas_call(
        functools.partial(_tables_kernel, n=n, tk=tk),
        out_shape=(shp, shp, shp, shp),
        grid=(n // tk,),
        out_specs=(spec, spec, spec, spec),
        scratch_shapes=[pltpu.VMEM((tk, n), F32)] * 4,
        compiler_params=_cp(("arbitrary",)),
        name=f"dft_tables_{n}",
    )()


def _mod_kernel(c_ref, w_ref, b_ref, o_ref):
    c = c_ref[...]
    a = c * jax.nn.sigmoid(c)
    o_ref[0] = _dot3(a, w_ref[0]) + b_ref[0]


def _modulation(cc, w_mod, b_mod):
    depth, d, n6 = w_mod.shape
    tn = 1536
    rows = cc.shape[0]
    return pl.pallas_call(
        _mod_kernel,
        out_shape=jax.ShapeDtypeStruct((depth, rows, n6), F32),
        grid=(depth, n6 // tn),
        in_specs=[
            pl.BlockSpec((rows, d), lambda l, j: (0, 0)),
            pl.BlockSpec((1, d, tn), lambda l, j: (l, 0, j)),
            pl.BlockSpec((1, 1, tn), lambda l, j: (l, 0, j)),
        ],
        out_specs=pl.BlockSpec((1, rows, tn), lambda l, j: (l, 0, j)),
        compiler_params=_cp(("parallel", "parallel")),
        name="modulation",
    )(cc, w_mod, b_mod.reshape(depth, 1, n6))


def _head_norm(p, gain, ones_bd):
    hi, lo = _split(p * p)
    ss = _dot(hi, ones_bd) + _dot(lo, ones_bd)
    return p * lax.rsqrt(ss * (1.0 / HEAD_DIM) + NORM_EPS) * gain


def _rope(xn, cos_t, sin_t):
    lane = lax.broadcasted_iota(jnp.int32, (1, LANES), 1)
    first = (lane % 32) < 16
    outs = []
    for j in range(xn.shape[1] // LANES):
        c = xn[:, j * LANES:(j + 1) * LANES]
        sw = jnp.where(first, pltpu.roll(c, LANES - 16, 1), pltpu.roll(c, 16, 1))
        outs.append(c * cos_t + sw * sin_t)
    return outs[0] if len(outs) == 1 else jnp.concatenate(outs, axis=1)


def _tile_heads(kv):
    lane = lax.broadcasted_iota(jnp.int32, (1, LANES), 1)
    low = lane < HEAD_DIM
    r = pltpu.roll(kv, HEAD_DIM, 1)
    h0 = jnp.where(low, kv, r)
    h1 = jnp.where(low, r, kv)
    return jnp.concatenate([h0, h0], axis=1), jnp.concatenate([h1, h1], axis=1)


def _proj_kernel(x_ref, sc_ref, sh_ref, g_ref, w_ref, gq_ref, gk_ref, cos_ref, sin_ref,
                 ones_ref, cs_ref, *outs, rope, kv_only):
    x = x_ref[0]
    h = _rms(x, g_ref[...]) * (1.0 + sc_ref[0]) + sh_ref[0]
    px = _dot(h.astype(BF16), w_ref[...])
    ones_bd = ones_ref[...]
    if kv_only:
        k4_ref, v4_ref = outs
        k = px[:, 0:128]
        v = px[:, 128:256]
    else:
        q_ref, k4_ref, v4_ref, fa_ref, ph_ref = outs
        aw = ones_bd.shape[0]
        q = _head_norm(px[:, 0:aw], gq_ref[...], ones_bd)
        if rope:
            q = _rope(q, cos_ref[...], sin_ref[...])
        q_ref[0] = (q * (HEAD_DIM ** -0.5 * math.log2(math.e))).astype(BF16)
        k = px[:, aw:aw + 128]
        v = px[:, aw + 128:aw + 256]
        f0 = aw + 256
        fw = cs_ref.shape[0]
        fa_ref[0] = _dot(px[:, f0:f0 + fw].astype(BF16), cs_ref[...]).astype(BF16)
        ph_ref[0] = px[:, f0 + fw:].astype(BF16)
    k = _head_norm(k, gk_ref[...], ones_bd[0:128, 0:128])
    if rope:
        k = _rope(k, cos_ref[...], sin_ref[...])
    k0, k1 = _tile_heads(k)
    k4_ref[0, 0] = k0.astype(BF16)
    k4_ref[0, 1] = k1.astype(BF16)
    v0, v1 = _tile_heads(v)
    v4_ref[0, 0] = v0.astype(BF16)
    v4_ref[0, 1] = v1.astype(BF16)


def _project(x, sc, sh, g, w, gq, gk, cos_t, sin_t, ones_bd, cs, *, rope, kv_only):
    b, n, d = x.shape
    tm = min(n, 512)
    wn = w.shape[1]
    aw = ones_bd.shape[0]
    fw = cs.shape[0]
    vec = lambda width: pl.BlockSpec((1, 1, width), lambda bi, i: (bi, 0, 0))
    full = lambda a: pl.BlockSpec(a.shape, lambda bi, i: (0,) * a.ndim)
    kv_spec = pl.BlockSpec((1, N_KV_HEADS, tm, 256), lambda bi, i: (bi, 0, i, 0))
    kv_shape = jax.ShapeDtypeStruct((b, N_KV_HEADS, n, 256), BF16)
    row = lambda width: pl.BlockSpec((1, tm, width), lambda bi, i: (bi, i, 0))
    if kv_only:
        out_shape = (kv_shape, kv_shape)
        out_specs = (kv_spec, kv_spec)
    else:
        hw = wn - aw - 256 - fw
        out_shape = (jax.ShapeDtypeStruct((b, n, aw), BF16), kv_shape, kv_shape,
                     jax.ShapeDtypeStruct((b, n, 2 * fw), BF16),
                     jax.ShapeDtypeStruct((b, n, hw), BF16))
        out_specs = (row(aw), kv_spec, kv_spec, row(2 * fw), row(hw))
    return pl.pallas_call(
        functools.partial(_proj_kernel, rope=rope, kv_only=kv_only),
        out_shape=out_shape,
        grid=(b, n // tm),
        in_specs=[
            row(d), vec(d), vec(d), full(g), full(w), full(gq), full(gk),
            pl.BlockSpec((tm, LANES), lambda bi, i: (i, 0)),
            pl.BlockSpec((tm, LANES), lambda bi, i: (i, 0)),
            full(ones_bd), full(cs),
        ],
        out_specs=out_specs,
        compiler_params=_cp(("parallel", "parallel")),
        name="proj_kv" if kv_only else ("proj_rope" if rope else "proj_ctx"),
    )(x, sc, sh, g, w, gq, gk, cos_t, sin_t, ones_bd, cs)


ATTN_SUB = 128


def _attn_kernel(q_ref, *refs, lks, nsub):
    nsrc = len(lks)
    k_refs = refs[0:2 * nsrc:2]
    v_refs = refs[1:2 * nsrc:2]
    o_ref = refs[2 * nsrc]
    s_scrs = refs[2 * nsrc + 1:]
    sub = ATTN_SUB
    rows = GQA_GROUP * sub
    gw = GQA_GROUP * HEAD_DIM
    group = lax.broadcasted_iota(jnp.int32, (1, gw), 1) // HEAD_DIM

    chunks = []
    off = 0
    for j, lk in enumerate(lks):
        tkc = min(lk, 512)
        for c in range(lk // tkc):
            chunks.append((j, c * tkc, tkc, off))
            off += tkc

    for u in range(nsub):
        s_scr = s_scrs[u]
        q = q_ref[0, u * sub:(u + 1) * sub, :]
        zero = jnp.zeros_like(q)
        qm = jnp.concatenate([jnp.where(group == g, q, zero) for g in range(GQA_GROUP)], axis=0)

        mpart = jnp.full((rows, LANES), -jnp.inf, F32)
        for j, start, tkc, o in chunks:
            s = _dot_nt(qm, k_refs[j][0, 0, start:start + tkc, :])
            s_scr[:, o:o + tkc] = s
            for c in range(tkc // LANES):
                mpart = jnp.maximum(mpart, s[:, c * LANES:(c + 1) * LANES])
        m = jnp.max(mpart, axis=-1, keepdims=True)
        mb = jnp.broadcast_to(m, (rows, LANES))

        lpart = jnp.zeros((rows, LANES), F32)
        acc = jnp.zeros((rows, gw), F32)
        for j, start, tkc, o in chunks:
            ps = []
            for c in range(tkc // LANES):
                pc = jnp.exp2(s_scr[:, o + c * LANES:o + (c + 1) * LANES] - mb)
                lpart = lpart + pc
                ps.append(pc.astype(BF16))
            p = jnp.concatenate(ps, axis=1)
            acc = acc + _dot(p, v_refs[j][0, 0, start:start + tkc, :])
        l = jnp.sum(lpart, axis=-1, keepdims=True)
        o_all = acc / l
        out = jnp.zeros((sub, gw), F32)
        for g in range(GQA_GROUP):
            out = out + jnp.where(group == g, o_all[g * sub:(g + 1) * sub], 0.0)
        o_ref[0, u * sub:(u + 1) * sub, :] = out.astype(BF16)


def _attention(q, kvs):
    b, lq, aw = q.shape
    tq = min(lq, 4 * ATTN_SUB)
    nsub = tq // ATTN_SUB
    lks = tuple(k.shape[2] for k, _ in kvs)
    gw = GQA_GROUP * HEAD_DIM
    in_specs = [pl.BlockSpec((1, tq, gw), lambda bi, h, i: (bi, i, h))]
    args = [q]
    for k4, v4 in kvs:
        lk = k4.shape[2]
        spec = pl.BlockSpec((1, 1, lk, gw), lambda bi, h, i: (bi, h, 0, 0),
                            pipeline_mode=pl.Buffered(1))
        in_specs += [spec, spec]
        args += [k4, v4]
    return pl.pallas_call(
        functools.partial(_attn_kernel, lks=lks, nsub=nsub),
        out_shape=jax.ShapeDtypeStruct((b, lq, aw), BF16),
        grid=(b, N_KV_HEADS, lq // tq),
        in_specs=in_specs,
        out_specs=pl.BlockSpec((1, tq, gw), lambda bi, h, i: (bi, i, h)),
        scratch_shapes=[pltpu.VMEM((GQA_GROUP * ATTN_SUB, sum(lks)), F32)] * nsub,
        compiler_params=_cp(("parallel", "parallel", "arbitrary")),
        name=f"attention_{lq}",
    )(*args)


def _fourier_kernel(c_ref, s_ref, fa_ref, w_ref, o_ref, *, nb, scale):
    fw = w_ref.shape[0]
    ct = c_ref[...]
    st = s_ref[...]
    for j in range(nb):
        y = _dot(ct, fa_ref[j, :, 0:fw]) - _dot(st, fa_ref[j, :, fw:2 * fw])
        o_ref[j] = _dot((y * scale).astype(BF16), w_ref[...]).astype(BF16)


def _fourier(fa, ctab, stab, w_f):
    b, n, fw2 = fa.shape
    fw = fw2 // 2
    nb = 2
    tk = min(n, 512)
    scale = 1.0 / math.sqrt(n * FOURIER_GROUP_DIM)
    return pl.pallas_call(
        functools.partial(_fourier_kernel, nb=nb, scale=scale),
        out_shape=jax.ShapeDtypeStruct((b, n, fw), BF16),
        grid=(b // nb, n // tk),
        in_specs=[
            pl.BlockSpec((tk, n), lambda g, i: (i, 0)),
            pl.BlockSpec((tk, n), lambda g, i: (i, 0)),
            pl.BlockSpec((nb, n, fw2), lambda g, i: (g, 0, 0)),
            pl.BlockSpec((fw, fw), lambda g, i: (0, 0)),
        ],
        out_specs=pl.BlockSpec((nb, tk, fw), lambda g, i: (g, i, 0)),
        compiler_params=_cp(("parallel", "arbitrary")),
        name=f"fourier_{n}",
    )(ctab, stab, fa, w_f)


def _conv3(p, w, bias):
    n = p.shape[0]
    row = lax.broadcasted_iota(jnp.int32, (n, 1), 0)
    prev = jnp.where(row == 0, 0.0, pltpu.roll(p, 1, 0))
    nxt = jnp.where(row == n - 1, 0.0, pltpu.roll(p, n - 1, 0))
    return prev * w[0:1] + p * w[1:2] + nxt * w[2:3] + bias


def _hyena_pre_kernel(p0_ref, p1_ref, p2_ref, w0_ref, w1_ref, w2_ref, b0_ref, b1_ref, b2_ref,
                      z_ref, x0_ref):
    x0 = _conv3(p0_ref[0].astype(F32), w0_ref[...], b0_ref[...])
    x1 = _conv3(p1_ref[0].astype(F32), w1_ref[...], b1_ref[...])
    v = _conv3(p2_ref[0].astype(F32), w2_ref[...], b2_ref[...])
    z_ref[0] = (x1 * v).astype(BF16)
    x0_ref[0] = x0.astype(BF16)


def _hyena_pre(ph, w, bias):
    b, n, hw3 = ph.shape
    hw = hw3 // 3
    nc = hw // LANES
    pspec = lambda s: pl.BlockSpec((1, n, LANES), lambda bi, c: (bi, 0, s * nc + c))
    wspec = lambda s: pl.BlockSpec((3, LANES), lambda bi, c: (0, s * nc + c))
    bspec = lambda s: pl.BlockSpec((1, LANES), lambda bi, c: (0, s * nc + c))
    ospec = pl.BlockSpec((1, n, LANES), lambda bi, c: (bi, 0, c))
    oshape = jax.ShapeDtypeStruct((b, n, hw), BF16)
    return pl.pallas_call(
        _hyena_pre_kernel,
        out_shape=(oshape, oshape),
        grid=(b, nc),
        in_specs=[pspec(0), pspec(1), pspec(2), wspec(0), wspec(1), wspec(2),
                  bspec(0), bspec(1), bspec(2)],
        out_specs=(ospec, ospec),
        compiler_params=_cp(("parallel", "parallel")),
        name=f"hyena_pre_{n}",
    )(ph, ph, ph, w, w, w, bias, bias, bias)


def _filter_kernel(w1t_ref, w1c_ref, w1s_ref, b1_ref, fr1_ref, w2_ref, b2_ref, fr2_ref, w3_ref,
                   ghi_ref, glo_ref, *, n):
    hw = w3_ref.shape[1] // 2
    i = lax.broadcasted_iota(jnp.int32, (n, 1), 0).astype(F32)
    t = i / float(n - 1)
    jb = lax.broadcasted_iota(jnp.int32, (1, HYENA_BANDS), 1).astype(F32)
    bands = 1e-4 + jb * ((HYENA_BANDS - 1 - 1e-4) / (HYENA_BANDS - 1))
    ang = ((2.0 * math.pi) * i / float(n)) * bands
    pre = t * w1t_ref[...] + _dot3(jnp.cos(ang), w1c_ref[...]) - _dot3(jnp.sin(ang), w1s_ref[...])
    h = jnp.sin(fr1_ref[...] * (pre + b1_ref[...]))
    h = jnp.sin(fr2_ref[...] * (_dot3(h, w2_ref[...]) + b2_ref[...]))
    h = _dot3(h, w3_ref[...])
    d0 = math.log(HYENA_TARGET) / HYENA_SLOW_DECAY
    d1 = math.log(HYENA_TARGET) / HYENA_FAST_DECAY
    jd = lax.broadcasted_iota(jnp.int32, (1, hw), 1).astype(F32)
    deltas = jnp.abs(d0 + jd * ((d1 - d0) / (hw - 1)))
    decay = jnp.exp(-t * deltas)
    hf = h[:, 0:hw] * decay
    hb = jnp.where(i == 0.0, 0.0, h[:, hw:2 * hw] * decay)
    total = jnp.sum(jnp.abs(hf), axis=0, keepdims=True) + jnp.sum(jnp.abs(hb), axis=0, keepdims=True)
    g = jnp.concatenate([hf / total, hb / total], axis=1)
    hi, lo = _split(g)
    ghi_ref[...] = hi
    glo_ref[...] = lo


def _hyena_filter(n, w1, b1, fr1, w2, b2, fr2, w3):
    nb = HYENA_BANDS
    r = lambda a: a.reshape(1, -1)
    shp = jax.ShapeDtypeStruct((n, w3.shape[1]), BF16)
    return pl.pallas_call(
        functools.partial(_filter_kernel, n=n),
        out_shape=(shp, shp),
        compiler_params=pltpu.CompilerParams(vmem_limit_bytes=VMEM_LIMIT),
        name=f"hyena_filter_{n}",
    )(w1[0:1], w1[1:1 + nb], w1[1 + nb:1 + 2 * nb], r(b1), r(fr1), w2, r(b2), r(fr2), w3)


def _spectrum_kernel(c_ref, s_ref, ghi_ref, glo_ref, ck_ref, sk_ref, kre_ref, kim_ref, *, scale):
    hw = kre_ref.shape[1]
    ct = c_ref[...]
    st = s_ref[...]
    a = _dot(ct, ghi_ref[...]) + _dot(ct, glo_ref[...])
    b = _dot(st, ghi_ref[...]) + _dot(st, glo_ref[...])
    are, bre = a[:, 0:hw], a[:, hw:2 * hw]
    aim, bim = -b[:, 0:hw], -b[:, hw:2 * hw]
    ck = ck_ref[...]
    sk = sk_ref[...]
    kre_ref[...] = (ck * (are + bre) - sk * (aim + bim)) * scale
    kim_ref[...] = (ck * (aim - bim) + sk * (are - bre)) * scale


def _filter_spectrum(ghi, glo, ctab, stab):
    n, hw2 = ghi.shape
    hw = hw2 // 2
    tk = min(n, 512)
    half = (2.0 * np.arange(n, dtype=np.float64) + 1.0) * (2.0 * np.pi / (8 * n))
    ck = jnp.asarray(np.cos(half).reshape(n, 1), F32)
    sk = jnp.asarray(np.sin(half).reshape(n, 1), F32)
    tspec = pl.BlockSpec((tk, n), lambda i: (i, 0))
    gspec = pl.BlockSpec((n, hw2), lambda i: (0, 0))
    vspec = pl.BlockSpec((tk, 1), lambda i: (i, 0))
    ospec = pl.BlockSpec((tk, hw), lambda i: (i, 0))
    oshape = jax.ShapeDtypeStruct((n, hw), F32)
    return pl.pallas_call(
        functools.partial(_spectrum_kernel, scale=1.0 / n),
        out_shape=(oshape, oshape),
        grid=(n // tk,),
        in_specs=[tspec, tspec, gspec, gspec, vspec, vspec],
        out_specs=(ospec, ospec),
        compiler_params=_cp(("parallel",)),
        name=f"filter_spectrum_{n}",
    )(ctab, stab, ghi, glo, ck, sk)


def _hyena_kernel(c_ref, s_ref, z_ref, zt_ref, x0_ref, kre_ref, kim_ref, bias_ref, o_ref,
                  yre_scr, yim_scr, *, nb, tk):
    phase = pl.program_id(1)
    i = pl.program_id(2)
    ct = c_ref[...]
    st = s_ref[...]

    @pl.when(phase == 0)
    def _():
        kre = kre_ref[...]
        kim = kim_ref[...]
        r0 = pl.multiple_of(i * tk, tk)
        for j in range(nb):
            ure = _dot(ct, z_ref[j])
            uim = -_dot(st, z_ref[j])
            yre_scr[j, pl.ds(r0, tk), :] = (kre * ure - kim * uim).astype(BF16)
            yim_scr[j, pl.ds(r0, tk), :] = (kre * uim + kim * ure).astype(BF16)

    @pl.when(phase == 1)
    def _():
        for j in range(nb):
            y = _dot(ct, yre_scr[j]) - _dot(st, yim_scr[j])
            u = zt_ref[j].astype(F32)
            o_ref[j] = (x0_ref[j].astype(F32) * (y + u * bias_ref[...])).astype(BF16)


def _hyena(z, x0c, kre, kim, bias, ctab, stab):
    b, n, hw = z.shape
    nb = 2
    tk = min(n, 512)
    tspec = pl.BlockSpec((tk, n), lambda g, p, i: (i, 0))
    tile = pl.BlockSpec((nb, tk, hw), lambda g, p, i: (g, i * p, 0))
    kspec = pl.BlockSpec((tk, hw), lambda g, p, i: (i * (1 - p), 0))
    return pl.pallas_call(
        functools.partial(_hyena_kernel, nb=nb, tk=tk),
        out_shape=jax.ShapeDtypeStruct((b, n, hw), BF16),
        grid=(b // nb, 2, n // tk),
        in_specs=[
            tspec, tspec,
            pl.BlockSpec((nb, n, hw), lambda g, p, i: (g, 0, 0)),
            tile, tile, kspec, kspec,
            pl.BlockSpec((1, hw), lambda g, p, i: (0, 0)),
        ],
        out_specs=tile,
        scratch_shapes=[pltpu.VMEM((nb, n, hw), BF16), pltpu.VMEM((nb, n, hw), BF16)],
        compiler_params=_cp(("parallel", "arbitrary", "arbitrary")),
        name=f"hyena_conv_{n}",
    )(ctab, stab, z, z, x0c, kre, kim, bias)


def _mixout_kernel(a_ref, f_ref, h_ref, w_ref, x_ref, ga_ref, g_ref, o_ref):
    aw = a_ref.shape[2]
    fw = f_ref.shape[2]
    mix = (_dot(a_ref[0], w_ref[0:aw, :]) + _dot(f_ref[0], w_ref[aw:aw + fw, :])
           + _dot(h_ref[0], w_ref[aw + fw:, :]))
    o_ref[0] = x_ref[0] + ga_ref[0] * _rms(mix, g_ref[...])


def _mixout(attn, four, hy, w_out, x, ga, g):
    b, n, d = x.shape
    tm = min(n, 512)
    row = lambda width: pl.BlockSpec((1, tm, width), lambda bi, i: (bi, i, 0))
    return pl.pallas_call(
        _mixout_kernel,
        out_shape=jax.ShapeDtypeStruct((b, n, d), F32),
        grid=(b, n // tm),
        in_specs=[
            row(attn.shape[2]), row(four.shape[2]), row(hy.shape[2]),
            pl.BlockSpec(w_out.shape, lambda bi, i: (0, 0)),
            row(d),
            pl.BlockSpec((1, 1, d), lambda bi, i: (bi, 0, 0)),
            pl.BlockSpec((1, d), lambda bi, i: (0, 0)),
        ],
        out_specs=row(d),
        compiler_params=_cp(("parallel", "parallel")),
        name=f"mixout_{n}",
    )(attn, four, hy, w_out, x, ga, g)


HALO = 8


FFN_CHUNK = 256


def _ffn_kernel(xp_ref, x_ref, xn_ref, sc_ref, sh_ref, g_ref, wu_ref, cw_ref, cb_ref, wd_ref,
                ga_ref, go_ref, o_ref, *, tm):
    i = pl.program_id(1)
    nrow = pl.num_programs(1)
    rows = tm + 2 * HALO
    dff = wd_ref.shape[0]
    cf = FFN_CHUNK
    g = g_ref[...]
    sc = 1.0 + sc_ref[0]
    sh = sh_ref[0]
    keep_p = jnp.where(i > 0, 1.0, 0.0)
    keep_n = jnp.where(i < nrow - 1, 1.0, 0.0)
    x = x_ref[0]
    fx = jnp.concatenate([(_rms(xp_ref[0], g) * sc + sh) * keep_p,
                          _rms(x, g) * sc + sh,
                          (_rms(xn_ref[0], g) * sc + sh) * keep_n], axis=0).astype(BF16)

    def conv(u, w, bias):
        prev = pltpu.roll(u, 1, 0)[HALO:HALO + tm]
        nxt = pltpu.roll(u, rows - 1, 0)[HALO:HALO + tm]
        return prev * w[0:1] + u[HALO:HALO + tm] * w[1:2] + nxt * w[2:3] + bias

    acts = []
    for c in range(dff // cf):
        lo, hi = c * cf, (c + 1) * cf
        gate = conv(_dot(fx, wu_ref[:, lo:hi]), cw_ref[:, lo:hi], cb_ref[:, lo:hi])
        val = conv(_dot(fx, wu_ref[:, dff + lo:dff + hi]), cw_ref[:, dff + lo:dff + hi],
                   cb_ref[:, dff + lo:dff + hi])
        acts.append((gate * jax.nn.sigmoid(gate) * val).astype(BF16))
    y = _dot(jnp.concatenate(acts, axis=1), wd_ref[...])
    o_ref[0] = x + ga_ref[0] * _rms(y, go_ref[...])


def _ffn(x, sc, sh, g, w_up, w_conv, b_conv, w_down, ga, g_post):
    b, n, d = x.shape
    tm = min(n, 512)
    hb = tm // HALO
    nhb = n // HALO
    vec = pl.BlockSpec((1, 1, d), lambda bi, i: (bi, 0, 0))
    gvec = pl.BlockSpec((1, d), lambda bi, i: (0, 0))
    resident = lambda a: pl.BlockSpec(a.shape, lambda bi, i: (0, 0), pipeline_mode=pl.Buffered(1))
    return pl.pallas_call(
        functools.partial(_ffn_kernel, tm=tm),
        out_shape=jax.ShapeDtypeStruct((b, n, d), F32),
        grid=(b, n // tm),
        in_specs=[
            pl.BlockSpec((1, HALO, d), lambda bi, i: (bi, jnp.maximum(i * hb - 1, 0), 0)),
            pl.BlockSpec((1, tm, d), lambda bi, i: (bi, i, 0)),
            pl.BlockSpec((1, HALO, d), lambda bi, i: (bi, jnp.minimum((i + 1) * hb, nhb - 1), 0)),
            vec, vec, gvec,
            resident(w_up), resident(w_conv), resident(b_conv), resident(w_down),
            vec, gvec,
        ],
        out_specs=pl.BlockSpec((1, tm, d), lambda bi, i: (bi, i, 0)),
        compiler_params=_cp(("parallel", "parallel")),
        name=f"conv_ffn_{n}",
    )(x, x, x, sc, sh, g, w_up, w_conv, b_conv, w_down, ga, g_post)


def _rope_tables(n):
    half = HEAD_DIM // 4
    inv = ROPE_THETA ** (-jnp.arange(0, 2 * half, 2, dtype=F32) / (2 * half))
    pos = jnp.arange(n, dtype=jnp.int32)
    row = (pos // GRID_W).astype(F32)
    col = (pos % GRID_W).astype(F32)
    ang_r = row[:, None] * inv[None, :]
    ang_c = col[:, None] * inv[None, :]
    cr, sr, cc, sc = jnp.cos(ang_r), jnp.sin(ang_r), jnp.cos(ang_c), jnp.sin(ang_c)
    cos64 = jnp.concatenate([cr, cr, cc, cc], axis=-1)
    sin64 = jnp.concatenate([-sr, sr, -sc, sc], axis=-1)
    return jnp.tile(cos64, (1, 2)), jnp.tile(sin64, (1, 2))


def _head_ones(width):
    idx = np.arange(width) // HEAD_DIM
    return jnp.asarray((idx[:, None] == idx[None, :]).astype(np.float32), BF16)


def _channel_dft(width):
    c = np.arange(width)
    same = (c[:, None] // FOURIER_GROUP_DIM) == (c[None, :] // FOURIER_GROUP_DIM)
    ang = 2.0 * np.pi * ((c[:, None] % FOURIER_GROUP_DIM) * (c[None, :] % FOURIER_GROUP_DIM)
                         % FOURIER_GROUP_DIM) / FOURIER_GROUP_DIM
    cs = np.concatenate([np.where(same, np.cos(ang), 0.0), np.where(same, np.sin(ang), 0.0)], axis=1)
    return jnp.asarray(cs.astype(np.float32), BF16)


def kernel(x, c, ctx, c_ctx, w_mod, b_mod, g_pre_mix, g_post_mix, g_pre_ffn, g_post_ffn, w_in, g_q, g_k, w_fourier, w_hy_conv, b_hy_conv, hy_w1, hy_b1, hy_fr1, hy_w2, hy_b2, hy_fr2, hy_w3, hy_bias, w_out, w_up, w_ffn_conv, b_ffn_conv, w_down):
    bsz, seq, d = x.shape
    clen = ctx.shape[1]
    depth = w_mod.shape[0]
    fw = w_fourier.shape[1]
    hw = hy_bias.shape[1]
    aw = w_in.shape[2] - 2 * N_KV_HEADS * HEAD_DIM - fw - 3 * hw
    k0 = aw
    f0 = aw + 2 * N_KV_HEADS * HEAD_DIM

    nrows = -(-(bsz + 1) // 8) * 8
    cc = jnp.zeros((nrows, d), F32).at[:bsz].set(c).at[bsz].set(c_ctx)
    mods = _modulation(cc, w_mod, b_mod)

    cos_x, sin_x = _rope_tables(seq)
    cos_c = jnp.ones((clen, LANES), F32)
    sin_c = jnp.zeros((clen, LANES), F32)
    ones_bd = _head_ones(aw)
    cs = _channel_dft(fw)
    hc_x, hs_x, fc_x, fs_x = _make_tables(seq)
    hc_c, hs_c, fc_c, fs_c = _make_tables(clen)

    def row1(v):
        return v.reshape(1, -1)

    for i in range(depth):
        last = i == depth - 1
        mx = mods[i, :bsz].reshape(bsz, 1, 6, d)
        mc = jnp.broadcast_to(mods[i, bsz].reshape(1, 1, 6, d), (bsz, 1, 6, d))
        sh1, sc1, ga1, sh2, sc2, ga2 = (mx[:, :, t] for t in range(6))
        csh1, csc1, cga1, csh2, csc2, cga2 = (mc[:, :, t] for t in range(6))
        w_in_b = w_in[i].astype(BF16)
        w_out_b = w_out[i].astype(BF16)
        w_up_b = w_up[i].astype(BF16)
        w_down_b = w_down[i].astype(BF16)
        w_f_b = w_fourier[i].astype(BF16)
        gq_t = jnp.tile(g_q[i], aw // HEAD_DIM).reshape(1, aw)
        gk_t = jnp.tile(g_k[i], N_KV_HEADS).reshape(1, N_KV_HEADS * HEAD_DIM)
        g_pre = row1(g_pre_mix[i])
        g_post = row1(g_post_mix[i])
        hy_params = (hy_w1[i], hy_b1[i], hy_fr1[i], hy_w2[i], hy_b2[i], hy_fr2[i], hy_w3[i])
        hbias = row1(hy_bias[i])
        hcw = w_hy_conv[i]
        hcb = row1(b_hy_conv[i])

        q, k4, v4, fa, ph = _project(x, sc1, sh1, g_pre, w_in_b, gq_t, gk_t, cos_x, sin_x,
                                     ones_bd, cs, rope=True, kv_only=False)
        if last:
            kc4, vc4 = _project(ctx, csc1, csh1, g_pre, w_in_b[:, k0:f0], gq_t, gk_t, cos_c, sin_c,
                                ones_bd, cs, rope=False, kv_only=True)
        else:
            qc, kc4, vc4, fac, phc = _project(ctx, csc1, csh1, g_pre, w_in_b, gq_t, gk_t, cos_c,
                                              sin_c, ones_bd, cs, rope=False, kv_only=False)
        attn_x = _attention(q, [(k4, v4), (kc4, vc4)])
        four_x = _fourier(fa, fc_x, fs_x, w_f_b)
        ghi, glo = _hyena_filter(seq, *hy_params)
        kre, kim = _filter_spectrum(ghi, glo, hc_x, hs_x)
        z, x0c = _hyena_pre(ph, hcw, hcb)
        hy_x = _hyena(z, x0c, kre, kim, hbias, hc_x, hs_x)
        x_new = _mixout(attn_x, four_x, hy_x, w_out_b, x, ga1, g_post)

        if not last:
            attn_c = _attention(qc, [(kc4, vc4)])
            four_c = _fourier(fac, fc_c, fs_c, w_f_b)
            ghi_c, glo_c = _hyena_filter(clen, *hy_params)
            kre_c, kim_c = _filter_spectrum(ghi_c, glo_c, hc_c, hs_c)
            zc, x0cc = _hyena_pre(phc, hcw, hcb)
            hy_c = _hyena(zc, x0cc, kre_c, kim_c, hbias, hc_c, hs_c)
            ctx = _mixout(attn_c, four_c, hy_c, w_out_b, ctx, cga1, g_post)
        x = x_new

        g_pf = row1(g_pre_ffn[i])
        g_of = row1(g_post_ffn[i])
        fcw = w_ffn_conv[i]
        fcb = row1(b_ffn_conv[i])
        x = _ffn(x, sc2, sh2, g_pf, w_up_b, fcw, fcb, w_down_b, ga2, g_of)
        if not last:
            ctx = _ffn(ctx, csc2, csh2, g_pf, w_up_b, fcw, fcb, w_down_b, cga2, g_of)

    return x
```

```python
import functools
import math

import numpy as np
import jax
import jax.numpy as jnp
from jax import lax
from jax.experimental import pallas as pl
from jax.experimental.pallas import tpu as pltpu

F32 = jnp.float32
BF16 = jnp.bfloat16

HEAD_DIM = 64
GQA_GROUP = 4
N_KV_HEADS = 2
GRID_W = 64
ROPE_THETA = 10000.0
FOURIER_GROUP_DIM = 64
HYENA_BANDS = 16
HYENA_FAST_DECAY = 0.3
HYENA_SLOW_DECAY = 1.5
HYENA_TARGET = 1e-2
NORM_EPS = 1e-6
LANES = 128
VMEM_LIMIT = 56 * 1024 * 1024


def _cp(sem, vmem=VMEM_LIMIT):
    return pltpu.CompilerParams(dimension_semantics=sem, vmem_limit_bytes=vmem)


def _dot(a, b):
    return jnp.dot(a, b, preferred_element_type=F32)


def _dot_nt(a, b):
    return lax.dot_general(a, b, (((1,), (1,)), ((), ())), preferred_element_type=F32)


def _split(a):
    hi = a.astype(BF16)
    lo = (a - hi.astype(F32)).astype(BF16)
    return hi, lo


def _dot3(a, b):
    ah, al = _split(a)
    bh, bl = _split(b)
    return _dot(ah, bh) + _dot(ah, bl) + _dot(al, bh)


def _rms(x, g):
    ms = jnp.mean(x * x, axis=-1, keepdims=True)
    return x * lax.rsqrt(ms + NORM_EPS) * g


def _tables_kernel(hc_ref, hs_ref, fc_ref, fs_ref, hca, hsa, fca, fsa, *, n, tk):
    i = pl.program_id(0)
    hyena_step = 2.0 * math.pi / (8 * n)
    fourier_step = 2.0 * math.pi / n

    @pl.when(i == 0)
    def _():
        k = lax.broadcasted_iota(jnp.int32, (tk, n), 0)
        s = lax.broadcasted_iota(jnp.int32, (tk, n), 1)
        a = (((2 * k + 1) * (2 * s + 1)) & (8 * n - 1)).astype(F32) * hyena_step
        hca[...] = jnp.cos(a)
        hsa[...] = jnp.sin(a)
        b = ((k * s) & (n - 1)).astype(F32) * fourier_step
        fca[...] = jnp.cos(b)
        fsa[...] = jnp.sin(b)

    s1 = lax.broadcasted_iota(jnp.int32, (1, n), 1)
    rot = (((2 * tk * i) * (2 * s1 + 1)) & (8 * n - 1)).astype(F32) * hyena_step
    cb, sb = jnp.cos(rot), jnp.sin(rot)
    hc_ref[...] = (hca[...] * cb - hsa[...] * sb).astype(BF16)
    hs_ref[...] = (hsa[...] * cb + hca[...] * sb).astype(BF16)
    rot = (((tk * i) * s1) & (n - 1)).astype(F32) * fourier_step
    cb, sb = jnp.cos(rot), jnp.sin(rot)
    fc_ref[...] = (fca[...] * cb - fsa[...] * sb).astype(BF16)
    fs_ref[...] = (fsa[...] * cb + fca[...] * sb).astype(BF16)


def _make_tables(n):
    tk = min(n, 256)
    spec = pl.BlockSpec((tk, n), lambda i: (i, 0))
    shp = jax.ShapeDtypeStruct((n, n), BF16)
    return pl.pallas_call(
        functools.partial(_tables_kernel, n=n, tk=tk),
        out_shape=(shp, shp, shp, shp),
        grid=(n // tk,),
        out_specs=(spec, spec, spec, spec),
        scratch_shapes=[pltpu.VMEM((tk, n), F32)] * 4,
        compiler_params=_cp(("arbitrary",)),
        name=f"dft_tables_{n}",
    )()


def _mod_kernel(c_ref, w_ref, b_ref, o_ref):
    c = c_ref[...]
    a = c * jax.nn.sigmoid(c)
    o_ref[0] = _dot3(a, w_ref[0]) + b_ref[0]


def _modulation(cc, w_mod, b_mod):
    depth, d, n6 = w_mod.shape
    tn = 1536
    rows = cc.shape[0]
    return pl.pallas_call(
        _mod_kernel,
        out_shape=jax.ShapeDtypeStruct((depth, rows, n6), F32),
        grid=(depth, n6 // tn),
        in_specs=[
            pl.BlockSpec((rows, d), lambda l, j: (0, 0)),
            pl.BlockSpec((1, d, tn), lambda l, j: (l, 0, j)),
            pl.BlockSpec((1, 1, tn), lambda l, j: (l, 0, j)),
        ],
        out_specs=pl.BlockSpec((1, rows, tn), lambda l, j: (l, 0, j)),
        compiler_params=_cp(("parallel", "parallel")),
        name="modulation",
    )(cc, w_mod, b_mod.reshape(depth, 1, n6))


def _head_norm(p, gain, ones_bd):
    hi, lo = _split(p * p)
    ss = _dot(hi, ones_bd) + _dot(lo, ones_bd)
    return p * lax.rsqrt(ss * (1.0 / HEAD_DIM) + NORM_EPS) * gain


def _rope(xn, cos_t, sin_t):
    lane = lax.broadcasted_iota(jnp.int32, (1, LANES), 1)
    first = (lane % 32) < 16
    outs = []
    for j in range(xn.shape[1] // LANES):
        c = xn[:, j * LANES:(j + 1) * LANES]
        sw = jnp.where(first, pltpu.roll(c, LANES - 16, 1), pltpu.roll(c, 16, 1))
        outs.append(c * cos_t + sw * sin_t)
    return outs[0] if len(outs) == 1 else jnp.concatenate(outs, axis=1)


def _tile_heads(kv):
    lane = lax.broadcasted_iota(jnp.int32, (1, LANES), 1)
    low = lane < HEAD_DIM
    r = pltpu.roll(kv, HEAD_DIM, 1)
    h0 = jnp.where(low, kv, r)
    h1 = jnp.where(low, r, kv)
    return jnp.concatenate([h0, h0], axis=1), jnp.concatenate([h1, h1], axis=1)


def _value_heads(v):
    lane = lax.broadcasted_iota(jnp.int32, (1, LANES), 1)
    low = lane < HEAD_DIM
    r = pltpu.roll(v, HEAD_DIM, 1)
    return jnp.where(low, v, 1.0), jnp.where(low, r, 1.0)


def _proj_kernel(x_ref, sc_ref, sh_ref, g_ref, w_ref, gq_ref, gk_ref, cos_ref, sin_ref,
                 ones_ref, cs_ref, *outs, rope, kv_only):
    x = x_ref[0]
    h = _rms(x, g_ref[...]) * (1.0 + sc_ref[0]) + sh_ref[0]
    px = _dot(h.astype(BF16), w_ref[...])
    ones_bd = ones_ref[...]
    if kv_only:
        k4_ref, v4_ref = outs
        k = px[:, 0:128]
        v = px[:, 128:256]
    else:
        q_ref, k4_ref, v4_ref, fa_ref, ph_ref = outs
        aw = ones_bd.shape[0]
        q = _head_norm(px[:, 0:aw], gq_ref[...], ones_bd)
        if rope:
            q = _rope(q, cos_ref[...], sin_ref[...])
        q_ref[0] = (q * (HEAD_DIM ** -0.5 * math.log2(math.e))).astype(BF16)
        k = px[:, aw:aw + 128]
        v = px[:, aw + 128:aw + 256]
        f0 = aw + 256
        fw = cs_ref.shape[0]
        fa_ref[0] = _dot(px[:, f0:f0 + fw].astype(BF16), cs_ref[...]).astype(BF16)
        ph_ref[0] = px[:, f0 + fw:].astype(BF16)
    k = _head_norm(k, gk_ref[...], ones_bd[0:128, 0:128])
    if rope:
        k = _rope(k, cos_ref[...], sin_ref[...])
    k0, k1 = _tile_heads(k)
    k4_ref[0, 0] = k0.astype(BF16)
    k4_ref[0, 1] = k1.astype(BF16)
    v0, v1 = _value_heads(v)
    v4_ref[0, 0] = v0.astype(BF16)
    v4_ref[0, 1] = v1.astype(BF16)


def _project(x, sc, sh, g, w, gq, gk, cos_t, sin_t, ones_bd, cs, *, rope, kv_only):
    b, n, d = x.shape
    tm = min(n, 512)
    wn = w.shape[1]
    aw = ones_bd.shape[0]
    fw = cs.shape[0]
    vec = lambda width: pl.BlockSpec((1, 1, width), lambda bi, i: (bi, 0, 0))
    full = lambda a: pl.BlockSpec(a.shape, lambda bi, i: (0,) * a.ndim)
    kv_spec = pl.BlockSpec((1, N_KV_HEADS, tm, 256), lambda bi, i: (bi, 0, i, 0))
    kv_shape = jax.ShapeDtypeStruct((b, N_KV_HEADS, n, 256), BF16)
    vl_spec = pl.BlockSpec((1, N_KV_HEADS, tm, LANES), lambda bi, i: (bi, 0, i, 0))
    vl_shape = jax.ShapeDtypeStruct((b, N_KV_HEADS, n, LANES), BF16)
    row = lambda width: pl.BlockSpec((1, tm, width), lambda bi, i: (bi, i, 0))
    if kv_only:
        out_shape = (kv_shape, vl_shape)
        out_specs = (kv_spec, vl_spec)
    else:
        hw = wn - aw - 256 - fw
        out_shape = (jax.ShapeDtypeStruct((b, n, aw), BF16), kv_shape, vl_shape,
                     jax.ShapeDtypeStruct((b, n, 2 * fw), BF16),
                     jax.ShapeDtypeStruct((b, n, hw), BF16))
        out_specs = (row(aw), kv_spec, vl_spec, row(2 * fw), row(hw))
    return pl.pallas_call(
        functools.partial(_proj_kernel, rope=rope, kv_only=kv_only),
        out_shape=out_shape,
        grid=(b, n // tm),
        in_specs=[
            row(d), vec(d), vec(d), full(g), full(w), full(gq), full(gk),
            pl.BlockSpec((tm, LANES), lambda bi, i: (i, 0)),
            pl.BlockSpec((tm, LANES), lambda bi, i: (i, 0)),
            full(ones_bd), full(cs),
        ],
        out_specs=out_specs,
        compiler_params=_cp(("parallel", "parallel")),
        name="proj_kv" if kv_only else ("proj_rope" if rope else "proj_ctx"),
    )(x, sc, sh, g, w, gq, gk, cos_t, sin_t, ones_bd, cs)


ATTN_SUB = 128


def _attn_kernel(q_ref, *refs, lks, nsub):
    nsrc = len(lks)
    k_refs = refs[0:2 * nsrc:2]
    v_refs = refs[1:2 * nsrc:2]
    o_ref = refs[2 * nsrc]
    s_scrs = refs[2 * nsrc + 1:]
    sub = ATTN_SUB
    rows = GQA_GROUP * sub
    gw = GQA_GROUP * HEAD_DIM
    group = lax.broadcasted_iota(jnp.int32, (1, gw), 1) // HEAD_DIM

    chunks = []
    off = 0
    for j, lk in enumerate(lks):
        tkc = min(lk, 512)
        for c in range(lk // tkc):
            chunks.append((j, c * tkc, tkc, off))
            off += tkc

    lane = lax.broadcasted_iota(jnp.int32, (1, LANES), 1)
    low = lane < HEAD_DIM
    for u in range(nsub):
        s_scr = s_scrs[u % len(s_scrs)]
        q = q_ref[0, u * sub:(u + 1) * sub, :]
        zero = jnp.zeros_like(q)
        qm = jnp.concatenate([jnp.where(group == g, q, zero) for g in range(GQA_GROUP)], axis=0)

        for j, start, tkc, o in chunks:
            s_scr[:, o:o + tkc] = _dot_nt(qm, k_refs[j][0, 0, start:start + tkc, :])
        m = jnp.max(s_scr[...], axis=-1, keepdims=True)
        mb = jnp.broadcast_to(m, (rows, LANES))

        acc = jnp.zeros((rows, LANES), F32)
        for j, start, tkc, o in chunks:
            p = jnp.concatenate(
                [jnp.exp2(s_scr[:, o + c * LANES:o + (c + 1) * LANES] - mb).astype(BF16)
                 for c in range(tkc // LANES)], axis=1)
            acc = acc + _dot(p, v_refs[j][0, 0, start:start + tkc, :])
        denom = jnp.broadcast_to(acc[:, HEAD_DIM:HEAD_DIM + 1], (rows, LANES))
        o_all = acc / denom
        heads = [o_all[g * sub:(g + 1) * sub] for g in range(GQA_GROUP)]
        left = jnp.where(low, heads[0], pltpu.roll(heads[1], HEAD_DIM, 1))
        right = jnp.where(low, heads[2], pltpu.roll(heads[3], HEAD_DIM, 1))
        o_ref[0, u * sub:(u + 1) * sub, :] = jnp.concatenate([left, right], axis=1).astype(BF16)


def _attention(q, kvs):
    b, lq, aw = q.shape
    tq = min(lq, 8 * ATTN_SUB)
    nsub = tq // ATTN_SUB
    nbuf = min(nsub, 3)
    lks = tuple(k.shape[2] for k, _ in kvs)
    gw = GQA_GROUP * HEAD_DIM
    in_specs = [pl.BlockSpec((1, tq, gw), lambda bi, h, i: (bi, i, h))]
    args = [q]
    for k4, vl in kvs:
        lk = k4.shape[2]
        for a in (k4, vl):
            in_specs.append(pl.BlockSpec((1, 1, lk, a.shape[3]), lambda bi, h, i: (bi, h, 0, 0),
                                         pipeline_mode=pl.Buffered(1)))
        args += [k4, vl]
    return pl.pallas_call(
        functools.partial(_attn_kernel, lks=lks, nsub=nsub),
        out_shape=jax.ShapeDtypeStruct((b, lq, aw), BF16),
        grid=(b, N_KV_HEADS, lq // tq),
        in_specs=in_specs,
        out_specs=pl.BlockSpec((1, tq, gw), lambda bi, h, i: (bi, i, h)),
        scratch_shapes=[pltpu.VMEM((GQA_GROUP * ATTN_SUB, sum(lks)), F32)] * nbuf,
        compiler_params=_cp(("parallel", "parallel", "arbitrary")),
        name=f"attention_{lq}",
    )(*args)


def _fourier_kernel(c_ref, s_ref, fa_ref, w_ref, o_ref, *, nb, scale):
    fw = w_ref.shape[0]
    ct = c_ref[...]
    st = s_ref[...]
    for j in range(nb):
        y = _dot(ct, fa_ref[j, :, 0:fw]) - _dot(st, fa_ref[j, :, fw:2 * fw])
        o_ref[j] = _dot((y * scale).astype(BF16), w_ref[...]).astype(BF16)


def _fourier(fa, ctab, stab, w_f):
    b, n, fw2 = fa.shape
    fw = fw2 // 2
    nb = 2
    tk = min(n, 512)
    scale = 1.0 / math.sqrt(n * FOURIER_GROUP_DIM)
    return pl.pallas_call(
        functools.partial(_fourier_kernel, nb=nb, scale=scale),
        out_shape=jax.ShapeDtypeStruct((b, n, fw), BF16),
        grid=(b // nb, n // tk),
        in_specs=[
            pl.BlockSpec((tk, n), lambda g, i: (i, 0)),
            pl.BlockSpec((tk, n), lambda g, i: (i, 0)),
            pl.BlockSpec((nb, n, fw2), lambda g, i: (g, 0, 0)),
            pl.BlockSpec((fw, fw), lambda g, i: (0, 0)),
        ],
        out_specs=pl.BlockSpec((nb, tk, fw), lambda g, i: (g, i, 0)),
        compiler_params=_cp(("parallel", "arbitrary")),
        name=f"fourier_{n}",
    )(ctab, stab, fa, w_f)


def _conv3(p, w, bias):
    n = p.shape[0]
    row = lax.broadcasted_iota(jnp.int32, (n, 1), 0)
    prev = jnp.where(row == 0, 0.0, pltpu.roll(p, 1, 0))
    nxt = jnp.where(row == n - 1, 0.0, pltpu.roll(p, n - 1, 0))
    return prev * w[0:1] + p * w[1:2] + nxt * w[2:3] + bias


def _hyena_pre_kernel(p0_ref, p1_ref, p2_ref, w0_ref, w1_ref, w2_ref, b0_ref, b1_ref, b2_ref,
                      z_ref, x0_ref):
    x0 = _conv3(p0_ref[0].astype(F32), w0_ref[...], b0_ref[...])
    x1 = _conv3(p1_ref[0].astype(F32), w1_ref[...], b1_ref[...])
    v = _conv3(p2_ref[0].astype(F32), w2_ref[...], b2_ref[...])
    z_ref[0] = (x1 * v).astype(BF16)
    x0_ref[0] = x0.astype(BF16)


def _hyena_pre(ph, w, bias):
    b, n, hw3 = ph.shape
    hw = hw3 // 3
    nc = hw // LANES
    pspec = lambda s: pl.BlockSpec((1, n, LANES), lambda bi, c: (bi, 0, s * nc + c))
    wspec = lambda s: pl.BlockSpec((3, LANES), lambda bi, c: (0, s * nc + c))
    bspec = lambda s: pl.BlockSpec((1, LANES), lambda bi, c: (0, s * nc + c))
    ospec = pl.BlockSpec((1, n, LANES), lambda bi, c: (bi, 0, c))
    oshape = jax.ShapeDtypeStruct((b, n, hw), BF16)
    return pl.pallas_call(
        _hyena_pre_kernel,
        out_shape=(oshape, oshape),
        grid=(b, nc),
        in_specs=[pspec(0), pspec(1), pspec(2), wspec(0), wspec(1), wspec(2),
                  bspec(0), bspec(1), bspec(2)],
        out_specs=(ospec, ospec),
        compiler_params=_cp(("parallel", "parallel")),
        name=f"hyena_pre_{n}",
    )(ph, ph, ph, w, w, w, bias, bias, bias)


def _filter_kernel(w1t_ref, w1c_ref, w1s_ref, b1_ref, fr1_ref, w2_ref, b2_ref, fr2_ref, w3_ref,
                   ghi_ref, glo_ref, *, n):
    hw = w3_ref.shape[1] // 2
    i = lax.broadcasted_iota(jnp.int32, (n, 1), 0).astype(F32)
    t = i / float(n - 1)
    jb = lax.broadcasted_iota(jnp.int32, (1, HYENA_BANDS), 1).astype(F32)
    bands = 1e-4 + jb * ((HYENA_BANDS - 1 - 1e-4) / (HYENA_BANDS - 1))
    ang = ((2.0 * math.pi) * i / float(n)) * bands
    pre = t * w1t_ref[...] + _dot3(jnp.cos(ang), w1c_ref[...]) - _dot3(jnp.sin(ang), w1s_ref[...])
    h = jnp.sin(fr1_ref[...] * (pre + b1_ref[...]))
    h = jnp.sin(fr2_ref[...] * (_dot3(h, w2_ref[...]) + b2_ref[...]))
    h = _dot3(h, w3_ref[...])
    d0 = math.log(HYENA_TARGET) / HYENA_SLOW_DECAY
    d1 = math.log(HYENA_TARGET) / HYENA_FAST_DECAY
    jd = lax.broadcasted_iota(jnp.int32, (1, hw), 1).astype(F32)
    deltas = jnp.abs(d0 + jd * ((d1 - d0) / (hw - 1)))
    decay = jnp.exp(-t * deltas)
    hf = h[:, 0:hw] * decay
    hb = jnp.where(i == 0.0, 0.0, h[:, hw:2 * hw] * decay)
    total = jnp.sum(jnp.abs(hf), axis=0, keepdims=True) + jnp.sum(jnp.abs(hb), axis=0, keepdims=True)
    g = jnp.concatenate([hf / total, hb / total], axis=1)
    hi, lo = _split(g)
    ghi_ref[...] = hi
    glo_ref[...] = lo


def _hyena_filter(n, w1, b1, fr1, w2, b2, fr2, w3):
    nb = HYENA_BANDS
    r = lambda a: a.reshape(1, -1)
    shp = jax.ShapeDtypeStruct((n, w3.shape[1]), BF16)
    return pl.pallas_call(
        functools.partial(_filter_kernel, n=n),
        out_shape=(shp, shp),
        compiler_params=pltpu.CompilerParams(vmem_limit_bytes=VMEM_LIMIT),
        name=f"hyena_filter_{n}",
    )(w1[0:1], w1[1:1 + nb], w1[1 + nb:1 + 2 * nb], r(b1), r(fr1), w2, r(b2), r(fr2), w3)


def _spectrum_kernel(c_ref, s_ref, ghi_ref, glo_ref, ck_ref, sk_ref, kre_ref, kim_ref, *, scale):
    hw = kre_ref.shape[1]
    ct = c_ref[...]
    st = s_ref[...]
    a = _dot(ct, ghi_ref[...]) + _dot(ct, glo_ref[...])
    b = _dot(st, ghi_ref[...]) + _dot(st, glo_ref[...])
    are, bre = a[:, 0:hw], a[:, hw:2 * hw]
    aim, bim = -b[:, 0:hw], -b[:, hw:2 * hw]
    ck = ck_ref[...]
    sk = sk_ref[...]
    kre_ref[...] = (ck * (are + bre) - sk * (aim + bim)) * scale
    kim_ref[...] = (ck * (aim - bim) + sk * (are - bre)) * scale


def _filter_spectrum(ghi, glo, ctab, stab):
    n, hw2 = ghi.shape
    hw = hw2 // 2
    tk = min(n, 512)
    half = (2.0 * np.arange(n, dtype=np.float64) + 1.0) * (2.0 * np.pi / (8 * n))
    ck = jnp.asarray(np.cos(half).reshape(n, 1), F32)
    sk = jnp.asarray(np.sin(half).reshape(n, 1), F32)
    tspec = pl.BlockSpec((tk, n), lambda i: (i, 0))
    gspec = pl.BlockSpec((n, hw2), lambda i: (0, 0))
    vspec = pl.BlockSpec((tk, 1), lambda i: (i, 0))
    ospec = pl.BlockSpec((tk, hw), lambda i: (i, 0))
    oshape = jax.ShapeDtypeStruct((n, hw), F32)
    return pl.pallas_call(
        functools.partial(_spectrum_kernel, scale=1.0 / n),
        out_shape=(oshape, oshape),
        grid=(n // tk,),
        in_specs=[tspec, tspec, gspec, gspec, vspec, vspec],
        out_specs=(ospec, ospec),
        compiler_params=_cp(("parallel",)),
        name=f"filter_spectrum_{n}",
    )(ctab, stab, ghi, glo, ck, sk)


def _hyena_kernel(c_ref, s_ref, z_ref, zt_ref, x0_ref, kre_ref, kim_ref, bias_ref, o_ref,
                  yre_scr, yim_scr, *, nb, tk):
    phase = pl.program_id(1)
    i = pl.program_id(2)
    ct = c_ref[...]
    st = s_ref[...]

    @pl.when(phase == 0)
    def _():
        kre = kre_ref[...]
        kim = kim_ref[...]
        r0 = pl.multiple_of(i * tk, tk)
        for j in range(nb):
            ure = _dot(ct, z_ref[j])
            uim = -_dot(st, z_ref[j])
            yre_scr[j, pl.ds(r0, tk), :] = (kre * ure - kim * uim).astype(BF16)
            yim_scr[j, pl.ds(r0, tk), :] = (kre * uim + kim * ure).astype(BF16)

    @pl.when(phase == 1)
    def _():
        for j in range(nb):
            y = _dot(ct, yre_scr[j]) - _dot(st, yim_scr[j])
            u = zt_ref[j].astype(F32)
            o_ref[j] = (x0_ref[j].astype(F32) * (y + u * bias_ref[...])).astype(BF16)


def _hyena(z, x0c, kre, kim, bias, ctab, stab):
    b, n, hw = z.shape
    nb = 2
    tk = min(n, 512)
    tspec = pl.BlockSpec((tk, n), lambda g, p, i: (i, 0))
    tile = pl.BlockSpec((nb, tk, hw), lambda g, p, i: (g, i * p, 0))
    kspec = pl.BlockSpec((tk, hw), lambda g, p, i: (i * (1 - p), 0))
    return pl.pallas_call(
        functools.partial(_hyena_kernel, nb=nb, tk=tk),
        out_shape=jax.ShapeDtypeStruct((b, n, hw), BF16),
        grid=(b // nb, 2, n // tk),
        in_specs=[
            tspec, tspec,
            pl.BlockSpec((nb, n, hw), lambda g, p, i: (g, 0, 0)),
            tile, tile, kspec, kspec,
            pl.BlockSpec((1, hw), lambda g, p, i: (0, 0)),
        ],
        out_specs=tile,
        scratch_shapes=[pltpu.VMEM((nb, n, hw), BF16), pltpu.VMEM((nb, n, hw), BF16)],
        compiler_params=_cp(("parallel", "arbitrary", "arbitrary")),
        name=f"hyena_conv_{n}",
    )(ctab, stab, z, z, x0c, kre, kim, bias)


def _mixout_kernel(a_ref, f_ref, h_ref, w_ref, x_ref, ga_ref, g_ref, o_ref):
    aw = a_ref.shape[2]
    fw = f_ref.shape[2]
    mix = (_dot(a_ref[0], w_ref[0:aw, :]) + _dot(f_ref[0], w_ref[aw:aw + fw, :])
           + _dot(h_ref[0], w_ref[aw + fw:, :]))
    o_ref[0] = x_ref[0] + ga_ref[0] * _rms(mix, g_ref[...])


def _mixout(attn, four, hy, w_out, x, ga, g):
    b, n, d = x.shape
    tm = min(n, 512)
    row = lambda width: pl.BlockSpec((1, tm, width), lambda bi, i: (bi, i, 0))
    return pl.pallas_call(
        _mixout_kernel,
        out_shape=jax.ShapeDtypeStruct((b, n, d), F32),
        grid=(b, n // tm),
        in_specs=[
            row(attn.shape[2]), row(four.shape[2]), row(hy.shape[2]),
            pl.BlockSpec(w_out.shape, lambda bi, i: (0, 0)),
            row(d),
            pl.BlockSpec((1, 1, d), lambda bi, i: (bi, 0, 0)),
            pl.BlockSpec((1, d), lambda bi, i: (0, 0)),
        ],
        out_specs=row(d),
        compiler_params=_cp(("parallel", "parallel")),
        name=f"mixout_{n}",
    )(attn, four, hy, w_out, x, ga, g)


HALO = 8


FFN_CHUNK = 256


def _ffn_kernel(xp_ref, x_ref, xn_ref, sc_ref, sh_ref, g_ref, wu_ref, cw_ref, cb_ref, wd_ref,
                ga_ref, go_ref, o_ref, *, tm):
    i = pl.program_id(1)
    nrow = pl.num_programs(1)
    rows = tm + 2 * HALO
    dff = wd_ref.shape[0]
    cf = FFN_CHUNK
    g = g_ref[...]
    sc = 1.0 + sc_ref[0]
    sh = sh_ref[0]
    keep_p = jnp.where(i > 0, 1.0, 0.0)
    keep_n = jnp.where(i < nrow - 1, 1.0, 0.0)
    x = x_ref[0]
    fx = jnp.concatenate([(_rms(xp_ref[0], g) * sc + sh) * keep_p,
                          _rms(x, g) * sc + sh,
                          (_rms(xn_ref[0], g) * sc + sh) * keep_n], axis=0).astype(BF16)

    def conv(u, w, bias):
        prev = pltpu.roll(u, 1, 0)[HALO:HALO + tm]
        nxt = pltpu.roll(u, rows - 1, 0)[HALO:HALO + tm]
        return prev * w[0:1] + u[HALO:HALO + tm] * w[1:2] + nxt * w[2:3] + bias

    acts = []
    for c in range(dff // cf):
        lo, hi = c * cf, (c + 1) * cf
        gate = conv(_dot(fx, wu_ref[:, lo:hi]), cw_ref[:, lo:hi], cb_ref[:, lo:hi])
        val = conv(_dot(fx, wu_ref[:, dff + lo:dff + hi]), cw_ref[:, dff + lo:dff + hi],
                   cb_ref[:, dff + lo:dff + hi])
        acts.append((gate * jax.nn.sigmoid(gate) * val).astype(BF16))
    y = _dot(jnp.concatenate(acts, axis=1), wd_ref[...])
    o_ref[0] = x + ga_ref[0] * _rms(y, go_ref[...])


def _ffn(x, sc, sh, g, w_up, w_conv, b_conv, w_down, ga, g_post):
    b, n, d = x.shape
    tm = min(n, 512)
    hb = tm // HALO
    nhb = n // HALO
    vec = pl.BlockSpec((1, 1, d), lambda bi, i: (bi, 0, 0))
    gvec = pl.BlockSpec((1, d), lambda bi, i: (0, 0))
    resident = lambda a: pl.BlockSpec(a.shape, lambda bi, i: (0, 0), pipeline_mode=pl.Buffered(1))
    return pl.pallas_call(
        functools.partial(_ffn_kernel, tm=tm),
        out_shape=jax.ShapeDtypeStruct((b, n, d), F32),
        grid=(b, n // tm),
        in_specs=[
            pl.BlockSpec((1, HALO, d), lambda bi, i: (bi, jnp.maximum(i * hb - 1, 0), 0)),
            pl.BlockSpec((1, tm, d), lambda bi, i: (bi, i, 0)),
            pl.BlockSpec((1, HALO, d), lambda bi, i: (bi, jnp.minimum((i + 1) * hb, nhb - 1), 0)),
            vec, vec, gvec,
            resident(w_up), resident(w_conv), resident(b_conv), resident(w_down),
            vec, gvec,
        ],
        out_specs=pl.BlockSpec((1, tm, d), lambda bi, i: (bi, i, 0)),
        compiler_params=_cp(("parallel", "parallel")),
        name=f"conv_ffn_{n}",
    )(x, x, x, sc, sh, g, w_up, w_conv, b_conv, w_down, ga, g_post)


def _rope_tables(n):
    half = HEAD_DIM // 4
    inv = ROPE_THETA ** (-jnp.arange(0, 2 * half, 2, dtype=F32) / (2 * half))
    pos = jnp.arange(n, dtype=jnp.int32)
    row = (pos // GRID_W).astype(F32)
    col = (pos % GRID_W).astype(F32)
    ang_r = row[:, None] * inv[None, :]
    ang_c = col[:, None] * inv[None, :]
    cr, sr, cc, sc = jnp.cos(ang_r), jnp.sin(ang_r), jnp.cos(ang_c), jnp.sin(ang_c)
    cos64 = jnp.concatenate([cr, cr, cc, cc], axis=-1)
    sin64 = jnp.concatenate([-sr, sr, -sc, sc], axis=-1)
    return jnp.tile(cos64, (1, 2)), jnp.tile(sin64, (1, 2))


def _head_ones(width):
    idx = np.arange(width) // HEAD_DIM
    return jnp.asarray((idx[:, None] == idx[None, :]).astype(np.float32), BF16)


def _channel_dft(width):
    c = np.arange(width)
    same = (c[:, None] // FOURIER_GROUP_DIM) == (c[None, :] // FOURIER_GROUP_DIM)
    ang = 2.0 * np.pi * ((c[:, None] % FOURIER_GROUP_DIM) * (c[None, :] % FOURIER_GROUP_DIM)
                         % FOURIER_GROUP_DIM) / FOURIER_GROUP_DIM
    cs = np.concatenate([np.where(same, np.cos(ang), 0.0), np.where(same, np.sin(ang), 0.0)], axis=1)
    return jnp.asarray(cs.astype(np.float32), BF16)


def kernel(x, c, ctx, c_ctx, w_mod, b_mod, g_pre_mix, g_post_mix, g_pre_ffn, g_post_ffn, w_in, g_q, g_k, w_fourier, w_hy_conv, b_hy_conv, hy_w1, hy_b1, hy_fr1, hy_w2, hy_b2, hy_fr2, hy_w3, hy_bias, w_out, w_up, w_ffn_conv, b_ffn_conv, w_down):
    bsz, seq, d = x.shape
    clen = ctx.shape[1]
    depth = w_mod.shape[0]
    fw = w_fourier.shape[1]
    hw = hy_bias.shape[1]
    aw = w_in.shape[2] - 2 * N_KV_HEADS * HEAD_DIM - fw - 3 * hw
    k0 = aw
    f0 = aw + 2 * N_KV_HEADS * HEAD_DIM

    nrows = -(-(bsz + 1) // 8) * 8
    cc = jnp.zeros((nrows, d), F32).at[:bsz].set(c).at[bsz].set(c_ctx)
    mods = _modulation(cc, w_mod, b_mod)

    cos_x, sin_x = _rope_tables(seq)
    cos_c = jnp.ones((clen, LANES), F32)
    sin_c = jnp.zeros((clen, LANES), F32)
    ones_bd = _head_ones(aw)
    cs = _channel_dft(fw)
    hc_x, hs_x, fc_x, fs_x = _make_tables(seq)
    hc_c, hs_c, fc_c, fs_c = _make_tables(clen)

    def row1(v):
        return v.reshape(1, -1)

    for i in range(depth):
        last = i == depth - 1
        mx = mods[i, :bsz].reshape(bsz, 1, 6, d)
        mc = jnp.broadcast_to(mods[i, bsz].reshape(1, 1, 6, d), (bsz, 1, 6, d))
        sh1, sc1, ga1, sh2, sc2, ga2 = (mx[:, :, t] for t in range(6))
        csh1, csc1, cga1, csh2, csc2, cga2 = (mc[:, :, t] for t in range(6))
        w_in_b = w_in[i].astype(BF16)
        w_out_b = w_out[i].astype(BF16)
        w_up_b = w_up[i].astype(BF16)
        w_down_b = w_down[i].astype(BF16)
        w_f_b = w_fourier[i].astype(BF16)
        gq_t = jnp.tile(g_q[i], aw // HEAD_DIM).reshape(1, aw)
        gk_t = jnp.tile(g_k[i], N_KV_HEADS).reshape(1, N_KV_HEADS * HEAD_DIM)
        g_pre = row1(g_pre_mix[i])
        g_post = row1(g_post_mix[i])
        hy_params = (hy_w1[i], hy_b1[i], hy_fr1[i], hy_w2[i], hy_b2[i], hy_fr2[i], hy_w3[i])
        hbias = row1(hy_bias[i])
        hcw = w_hy_conv[i]
        hcb = row1(b_hy_conv[i])

        q, k4, v4, fa, ph = _project(x, sc1, sh1, g_pre, w_in_b, gq_t, gk_t, cos_x, sin_x,
                                     ones_bd, cs, rope=True, kv_only=False)
        if last:
            kc4, vc4 = _project(ctx, csc1, csh1, g_pre, w_in_b[:, k0:f0], gq_t, gk_t, cos_c, sin_c,
                                ones_bd, cs, rope=False, kv_only=True)
        else:
            qc, kc4, vc4, fac, phc = _project(ctx, csc1, csh1, g_pre, w_in_b, gq_t, gk_t, cos_c,
                                              sin_c, ones_bd, cs, rope=False, kv_only=False)
        attn_x = _attention(q, [(k4, v4), (kc4, vc4)])
        four_x = _fourier(fa, fc_x, fs_x, w_f_b)
        ghi, glo = _hyena_filter(seq, *hy_params)
        kre, kim = _filter_spectrum(ghi, glo, hc_x, hs_x)
        z, x0c = _hyena_pre(ph, hcw, hcb)
        hy_x = _hyena(z, x0c, kre, kim, hbias, hc_x, hs_x)
        x_new = _mixout(attn_x, four_x, hy_x, w_out_b, x, ga1, g_post)

        if not last:
            attn_c = _attention(qc, [(kc4, vc4)])
            four_c = _fourier(fac, fc_c, fs_c, w_f_b)
            ghi_c, glo_c = _hyena_filter(clen, *hy_params)
            kre_c, kim_c = _filter_spectrum(ghi_c, glo_c, hc_c, hs_c)
            zc, x0cc = _hyena_pre(phc, hcw, hcb)
            hy_c = _hyena(zc, x0cc, kre_c, kim_c, hbias, hc_c, hs_c)
            ctx = _mixout(attn_c, four_c, hy_c, w_out_b, ctx, cga1, g_post)
        x = x_new

        g_pf = row1(g_pre_ffn[i])
        g_of = row1(g_post_ffn[i])
        fcw = w_ffn_conv[i]
        fcb = row1(b_ffn_conv[i])
        x = _ffn(x, sc2, sh2, g_pf, w_up_b, fcw, fcb, w_down_b, ga2, g_of)
        if not last:
            ctx = _ffn(ctx, csc2, csh2, g_pf, w_up_b, fcw, fcb, w_down_b, cga2, g_of)

    return x
```

```python
import functools
import math

import numpy as np
import jax
import jax.numpy as jnp
from jax import lax
from jax.experimental import pallas as pl
from jax.experimental.pallas import tpu as pltpu

F32 = jnp.float32
BF16 = jnp.bfloat16

HEAD_DIM = 64
GQA_GROUP = 4
N_KV_HEADS = 2
GRID_W = 64
ROPE_THETA = 10000.0
FOURIER_GROUP_DIM = 64
HYENA_BANDS = 16
HYENA_FAST_DECAY = 0.3
HYENA_SLOW_DECAY = 1.5
HYENA_TARGET = 1e-2
NORM_EPS = 1e-6
LANES = 128
VMEM_LIMIT = 56 * 1024 * 1024


def _cp(sem, vmem=VMEM_LIMIT):
    return pltpu.CompilerParams(dimension_semantics=sem, vmem_limit_bytes=vmem)


def _dot(a, b):
    return jnp.dot(a, b, preferred_element_type=F32)


def _dot_nt(a, b):
    return lax.dot_general(a, b, (((1,), (1,)), ((), ())), preferred_element_type=F32)


def _split(a):
    hi = a.astype(BF16)
    lo = (a - hi.astype(F32)).astype(BF16)
    return hi, lo


def _dot3(a, b):
    ah, al = _split(a)
    bh, bl = _split(b)
    return _dot(ah, bh) + _dot(ah, bl) + _dot(al, bh)


def _rms(x, g):
    ms = jnp.mean(x * x, axis=-1, keepdims=True)
    return x * lax.rsqrt(ms + NORM_EPS) * g


def _tables_kernel(hc_ref, hs_ref, fc_ref, fs_ref, hca, hsa, fca, fsa, *, n, tk):
    i = pl.program_id(0)
    hyena_step = 2.0 * math.pi / (8 * n)
    fourier_step = 2.0 * math.pi / n

    @pl.when(i == 0)
    def _():
        k = lax.broadcasted_iota(jnp.int32, (tk, n), 0)
        s = lax.broadcasted_iota(jnp.int32, (tk, n), 1)
        a = (((2 * k + 1) * (2 * s + 1)) & (8 * n - 1)).astype(F32) * hyena_step
        hca[...] = jnp.cos(a)
        hsa[...] = jnp.sin(a)
        b = ((k * s) & (n - 1)).astype(F32) * fourier_step
        fca[...] = jnp.cos(b)
        fsa[...] = jnp.sin(b)

    s1 = lax.broadcasted_iota(jnp.int32, (1, n), 1)
    rot = (((2 * tk * i) * (2 * s1 + 1)) & (8 * n - 1)).astype(F32) * hyena_step
    cb, sb = jnp.cos(rot), jnp.sin(rot)
    hc_ref[...] = (hca[...] * cb - hsa[...] * sb).astype(BF16)
    hs_ref[...] = (hsa[...] * cb + hca[...] * sb).astype(BF16)
    rot = (((tk * i) * s1) & (n - 1)).astype(F32) * fourier_step
    cb, sb = jnp.cos(rot), jnp.sin(rot)
    fc_ref[...] = (fca[...] * cb - fsa[...] * sb).astype(BF16)
    fs_ref[...] = (fsa[...] * cb + fca[...] * sb).astype(BF16)


def _make_tables(n):
    tk = min(n, 256)
    spec = pl.BlockSpec((tk, n), lambda i: (i, 0))
    shp = jax.ShapeDtypeStruct((n, n), BF16)
    return pl.pallas_call(
        functools.partial(_tables_kernel, n=n, tk=tk),
        out_shape=(shp, shp, shp, shp),
        grid=(n // tk,),
        out_specs=(spec, spec, spec, spec),
        scratch_shapes=[pltpu.VMEM((tk, n), F32)] * 4,
        compiler_params=_cp(("arbitrary",)),
        name=f"dft_tables_{n}",
    )()


def _mod_kernel(c_ref, w_ref, b_ref, o_ref):
    c = c_ref[...]
    a = c * jax.nn.sigmoid(c)
    o_ref[0] = _dot3(a, w_ref[0]) + b_ref[0]


def _modulation(cc, w_mod, b_mod):
    depth, d, n6 = w_mod.shape
    tn = 1536
    rows = cc.shape[0]
    return pl.pallas_call(
        _mod_kernel,
        out_shape=jax.ShapeDtypeStruct((depth, rows, n6), F32),
        grid=(depth, n6 // tn),
        in_specs=[
            pl.BlockSpec((rows, d), lambda l, j: (0, 0)),
            pl.BlockSpec((1, d, tn), lambda l, j: (l, 0, j)),
            pl.BlockSpec((1, 1, tn), lambda l, j: (l, 0, j)),
        ],
        out_specs=pl.BlockSpec((1, rows, tn), lambda l, j: (l, 0, j)),
        compiler_params=_cp(("parallel", "parallel")),
        name="modulation",
    )(cc, w_mod, b_mod.reshape(depth, 1, n6))


def _head_norm(p, gain, ones_bd):
    hi, lo = _split(p * p)
    ss = _dot(hi, ones_bd) + _dot(lo, ones_bd)
    return p * lax.rsqrt(ss * (1.0 / HEAD_DIM) + NORM_EPS) * gain


def _rope(xn, cos_t, sin_t):
    lane = lax.broadcasted_iota(jnp.int32, (1, LANES), 1)
    first = (lane % 32) < 16
    outs = []
    for j in range(xn.shape[1] // LANES):
        c = xn[:, j * LANES:(j + 1) * LANES]
        sw = jnp.where(first, pltpu.roll(c, LANES - 16, 1), pltpu.roll(c, 16, 1))
        outs.append(c * cos_t + sw * sin_t)
    return outs[0] if len(outs) == 1 else jnp.concatenate(outs, axis=1)


def _tile_heads(kv):
    lane = lax.broadcasted_iota(jnp.int32, (1, LANES), 1)
    low = lane < HEAD_DIM
    r = pltpu.roll(kv, HEAD_DIM, 1)
    h0 = jnp.where(low, kv, r)
    h1 = jnp.where(low, r, kv)
    return jnp.concatenate([h0, h0], axis=1), jnp.concatenate([h1, h1], axis=1)


def _value_heads(v):
    lane = lax.broadcasted_iota(jnp.int32, (1, LANES), 1)
    low = lane < HEAD_DIM
    r = pltpu.roll(v, HEAD_DIM, 1)
    return jnp.where(low, v, 1.0), jnp.where(low, r, 1.0)


def _proj_kernel(x_ref, sc_ref, sh_ref, g_ref, w_ref, gq_ref, gk_ref, cos_ref, sin_ref,
                 ones_ref, cs_ref, *outs, rope, kv_only):
    x = x_ref[0]
    h = _rms(x, g_ref[...]) * (1.0 + sc_ref[0]) + sh_ref[0]
    px = _dot(h.astype(BF16), w_ref[...])
    ones_bd = ones_ref[...]
    if kv_only:
        k4_ref, v4_ref = outs
        k = px[:, 0:128]
        v = px[:, 128:256]
    else:
        q_ref, k4_ref, v4_ref, fa_ref, ph_ref = outs
        aw = ones_bd.shape[0]
        q = _head_norm(px[:, 0:aw], gq_ref[...], ones_bd)
        if rope:
            q = _rope(q, cos_ref[...], sin_ref[...])
        q_ref[0] = (q * (HEAD_DIM ** -0.5 * math.log2(math.e))).astype(BF16)
        k = px[:, aw:aw + 128]
        v = px[:, aw + 128:aw + 256]
        f0 = aw + 256
        fw = cs_ref.shape[0]
        fa_ref[0] = _dot(px[:, f0:f0 + fw].astype(BF16), cs_ref[...]).astype(BF16)
        ph_ref[0] = px[:, f0 + fw:].astype(BF16)
    k = _head_norm(k, gk_ref[...], ones_bd[0:128, 0:128])
    if rope:
        k = _rope(k, cos_ref[...], sin_ref[...])
    k0, k1 = _tile_heads(k)
    k4_ref[0, 0] = k0.astype(BF16)
    k4_ref[0, 1] = k1.astype(BF16)
    v0, v1 = _value_heads(v)
    v4_ref[0, 0] = v0.T.astype(BF16)
    v4_ref[0, 1] = v1.T.astype(BF16)


def _project(x, sc, sh, g, w, gq, gk, cos_t, sin_t, ones_bd, cs, *, rope, kv_only):
    b, n, d = x.shape
    tm = min(n, 512)
    wn = w.shape[1]
    aw = ones_bd.shape[0]
    fw = cs.shape[0]
    vec = lambda width: pl.BlockSpec((1, 1, width), lambda bi, i: (bi, 0, 0))
    full = lambda a: pl.BlockSpec(a.shape, lambda bi, i: (0,) * a.ndim)
    kv_spec = pl.BlockSpec((1, N_KV_HEADS, tm, 256), lambda bi, i: (bi, 0, i, 0))
    kv_shape = jax.ShapeDtypeStruct((b, N_KV_HEADS, n, 256), BF16)
    vl_spec = pl.BlockSpec((1, N_KV_HEADS, LANES, tm), lambda bi, i: (bi, 0, 0, i))
    vl_shape = jax.ShapeDtypeStruct((b, N_KV_HEADS, LANES, n), BF16)
    row = lambda width: pl.BlockSpec((1, tm, width), lambda bi, i: (bi, i, 0))
    if kv_only:
        out_shape = (kv_shape, vl_shape)
        out_specs = (kv_spec, vl_spec)
    else:
        hw = wn - aw - 256 - fw
        out_shape = (jax.ShapeDtypeStruct((b, n, aw), BF16), kv_shape, vl_shape,
                     jax.ShapeDtypeStruct((b, n, 2 * fw), BF16),
                     jax.ShapeDtypeStruct((b, n, hw), BF16))
        out_specs = (row(aw), kv_spec, vl_spec, row(2 * fw), row(hw))
    return pl.pallas_call(
        functools.partial(_proj_kernel, rope=rope, kv_only=kv_only),
        out_shape=out_shape,
        grid=(b, n // tm),
        in_specs=[
            row(d), vec(d), vec(d), full(g), full(w), full(gq), full(gk),
            pl.BlockSpec((tm, LANES), lambda bi, i: (i, 0)),
            pl.BlockSpec((tm, LANES), lambda bi, i: (i, 0)),
            full(ones_bd), full(cs),
        ],
        out_specs=out_specs,
        compiler_params=_cp(("parallel", "parallel")),
        name="proj_kv" if kv_only else ("proj_rope" if rope else "proj_ctx"),
    )(x, sc, sh, g, w, gq, gk, cos_t, sin_t, ones_bd, cs)


ATTN_SUB = 128


def _attn_kernel(q_ref, *refs, lks, nsub):
    nsrc = len(lks)
    k_refs = refs[0:2 * nsrc:2]
    v_refs = refs[1:2 * nsrc:2]
    o_ref = refs[2 * nsrc]
    s_scrs = refs[2 * nsrc + 1:]
    sub = ATTN_SUB
    rows = GQA_GROUP * sub
    gw = GQA_GROUP * HEAD_DIM
    group = lax.broadcasted_iota(jnp.int32, (1, gw), 1) // HEAD_DIM

    chunks = []
    off = 0
    for j, lk in enumerate(lks):
        tkc = min(lk, 512)
        for c in range(lk // tkc):
            chunks.append((j, c * tkc, tkc, off))
            off += tkc

    lane = lax.broadcasted_iota(jnp.int32, (1, LANES), 1)
    low = lane < HEAD_DIM

    def scores(u):
        s_scr = s_scrs[u % len(s_scrs)]
        q = q_ref[0, u * sub:(u + 1) * sub, :]
        zero = jnp.zeros_like(q)
        qm = jnp.concatenate([jnp.where(group == g, q, zero) for g in range(GQA_GROUP)], axis=0)
        for j, start, tkc, o in chunks:
            s_scr[o:o + tkc, :] = _dot_nt(k_refs[j][0, 0, start:start + tkc, :], qm)
        return jnp.max(s_scr[...], axis=0, keepdims=True)

    m = scores(0)
    for u in range(nsub):
        s_scr = s_scrs[u % len(s_scrs)]
        acc = jnp.zeros((LANES, rows), F32)
        if u + 1 < nsub:
            m_next = scores(u + 1)
            bits = pltpu.bitcast(m_next, jnp.uint32)
            acc = acc + ((bits >> 16) >> 16).astype(F32)
        for j, start, tkc, o in chunks:
            p = jnp.exp2(s_scr[o:o + tkc, :] - m).astype(BF16)
            acc = acc + _dot(v_refs[j][0, 0, :, start:start + tkc], p)
        o_t = acc / acc[HEAD_DIM:HEAD_DIM + 1, :]
        heads = [o_t[:, g * sub:(g + 1) * sub].T for g in range(GQA_GROUP)]
        left = jnp.where(low, heads[0], pltpu.roll(heads[1], HEAD_DIM, 1))
        right = jnp.where(low, heads[2], pltpu.roll(heads[3], HEAD_DIM, 1))
        o_ref[0, u * sub:(u + 1) * sub, :] = jnp.concatenate([left, right], axis=1).astype(BF16)
        if u + 1 < nsub:
            m = m_next


def _attention(q, kvs):
    b, lq, aw = q.shape
    tq = min(lq, 8 * ATTN_SUB)
    nsub = tq // ATTN_SUB
    nbuf = min(nsub, 3)
    lks = tuple(k.shape[2] for k, _ in kvs)
    gw = GQA_GROUP * HEAD_DIM
    in_specs = [pl.BlockSpec((1, tq, gw), lambda bi, h, i: (bi, i, h))]
    args = [q]
    for k4, vlt in kvs:
        for a in (k4, vlt):
            in_specs.append(pl.BlockSpec((1, 1) + a.shape[2:], lambda bi, h, i: (bi, h, 0, 0),
                                         pipeline_mode=pl.Buffered(1)))
        args += [k4, vlt]
    return pl.pallas_call(
        functools.partial(_attn_kernel, lks=lks, nsub=nsub),
        out_shape=jax.ShapeDtypeStruct((b, lq, aw), BF16),
        grid=(b, N_KV_HEADS, lq // tq),
        in_specs=in_specs,
        out_specs=pl.BlockSpec((1, tq, gw), lambda bi, h, i: (bi, i, h)),
        scratch_shapes=[pltpu.VMEM((sum(lks), GQA_GROUP * ATTN_SUB), F32)] * nbuf,
        compiler_params=_cp(("parallel", "parallel", "arbitrary")),
        name=f"attention_{lq}",
    )(*args)


def _fourier_kernel(c_ref, s_ref, fa_ref, w_ref, o_ref, *, nb, scale):
    fw = w_ref.shape[0]
    ct = c_ref[...]
    st = s_ref[...]
    for j in range(nb):
        y = _dot(ct, fa_ref[j, :, 0:fw]) - _dot(st, fa_ref[j, :, fw:2 * fw])
        o_ref[j] = _dot((y * scale).astype(BF16), w_ref[...]).astype(BF16)


def _fourier(fa, ctab, stab, w_f):
    b, n, fw2 = fa.shape
    fw = fw2 // 2
    nb = 2
    tk = min(n, 512)
    scale = 1.0 / math.sqrt(n * FOURIER_GROUP_DIM)
    return pl.pallas_call(
        functools.partial(_fourier_kernel, nb=nb, scale=scale),
        out_shape=jax.ShapeDtypeStruct((b, n, fw), BF16),
        grid=(b // nb, n // tk),
        in_specs=[
            pl.BlockSpec((tk, n), lambda g, i: (i, 0)),
            pl.BlockSpec((tk, n), lambda g, i: (i, 0)),
            pl.BlockSpec((nb, n, fw2), lambda g, i: (g, 0, 0)),
            pl.BlockSpec((fw, fw), lambda g, i: (0, 0)),
        ],
        out_specs=pl.BlockSpec((nb, tk, fw), lambda g, i: (g, i, 0)),
        compiler_params=_cp(("parallel", "arbitrary")),
        name=f"fourier_{n}",
    )(ctab, stab, fa, w_f)


def _conv3(p, w, bias):
    n = p.shape[0]
    row = lax.broadcasted_iota(jnp.int32, (n, 1), 0)
    prev = jnp.where(row == 0, 0.0, pltpu.roll(p, 1, 0))
    nxt = jnp.where(row == n - 1, 0.0, pltpu.roll(p, n - 1, 0))
    return prev * w[0:1] + p * w[1:2] + nxt * w[2:3] + bias


def _hyena_pre_kernel(p0_ref, p1_ref, p2_ref, w0_ref, w1_ref, w2_ref, b0_ref, b1_ref, b2_ref,
                      z_ref, x0_ref):
    x0 = _conv3(p0_ref[0].astype(F32), w0_ref[...], b0_ref[...])
    x1 = _conv3(p1_ref[0].astype(F32), w1_ref[...], b1_ref[...])
    v = _conv3(p2_ref[0].astype(F32), w2_ref[...], b2_ref[...])
    z_ref[0] = (x1 * v).astype(BF16)
    x0_ref[0] = x0.astype(BF16)


def _hyena_pre(ph, w, bias):
    b, n, hw3 = ph.shape
    hw = hw3 // 3
    nc = hw // LANES
    pspec = lambda s: pl.BlockSpec((1, n, LANES), lambda bi, c: (bi, 0, s * nc + c))
    wspec = lambda s: pl.BlockSpec((3, LANES), lambda bi, c: (0, s * nc + c))
    bspec = lambda s: pl.BlockSpec((1, LANES), lambda bi, c: (0, s * nc + c))
    ospec = pl.BlockSpec((1, n, LANES), lambda bi, c: (bi, 0, c))
    oshape = jax.ShapeDtypeStruct((b, n, hw), BF16)
    return pl.pallas_call(
        _hyena_pre_kernel,
        out_shape=(oshape, oshape),
        grid=(b, nc),
        in_specs=[pspec(0), pspec(1), pspec(2), wspec(0), wspec(1), wspec(2),
                  bspec(0), bspec(1), bspec(2)],
        out_specs=(ospec, ospec),
        compiler_params=_cp(("parallel", "parallel")),
        name=f"hyena_pre_{n}",
    )(ph, ph, ph, w, w, w, bias, bias, bias)


def _filter_kernel(w1t_ref, w1c_ref, w1s_ref, b1_ref, fr1_ref, w2_ref, b2_ref, fr2_ref, w3_ref,
                   ghi_ref, glo_ref, *, n):
    hw = w3_ref.shape[1] // 2
    i = lax.broadcasted_iota(jnp.int32, (n, 1), 0).astype(F32)
    t = i / float(n - 1)
    jb = lax.broadcasted_iota(jnp.int32, (1, HYENA_BANDS), 1).astype(F32)
    bands = 1e-4 + jb * ((HYENA_BANDS - 1 - 1e-4) / (HYENA_BANDS - 1))
    ang = ((2.0 * math.pi) * i / float(n)) * bands
    pre = t * w1t_ref[...] + _dot3(jnp.cos(ang), w1c_ref[...]) - _dot3(jnp.sin(ang), w1s_ref[...])
    h = jnp.sin(fr1_ref[...] * (pre + b1_ref[...]))
    h = jnp.sin(fr2_ref[...] * (_dot3(h, w2_ref[...]) + b2_ref[...]))
    h = _dot3(h, w3_ref[...])
    d0 = math.log(HYENA_TARGET) / HYENA_SLOW_DECAY
    d1 = math.log(HYENA_TARGET) / HYENA_FAST_DECAY
    jd = lax.broadcasted_iota(jnp.int32, (1, hw), 1).astype(F32)
    deltas = jnp.abs(d0 + jd * ((d1 - d0) / (hw - 1)))
    decay = jnp.exp(-t * deltas)
    hf = h[:, 0:hw] * decay
    hb = jnp.where(i == 0.0, 0.0, h[:, hw:2 * hw] * decay)
    total = jnp.sum(jnp.abs(hf), axis=0, keepdims=True) + jnp.sum(jnp.abs(hb), axis=0, keepdims=True)
    g = jnp.concatenate([hf / total, hb / total], axis=1)
    hi, lo = _split(g)
    ghi_ref[...] = hi
    glo_ref[...] = lo


def _hyena_filter(n, w1, b1, fr1, w2, b2, fr2, w3):
    nb = HYENA_BANDS
    r = lambda a: a.reshape(1, -1)
    shp = jax.ShapeDtypeStruct((n, w3.shape[1]), BF16)
    return pl.pallas_call(
        functools.partial(_filter_kernel, n=n),
        out_shape=(shp, shp),
        compiler_params=pltpu.CompilerParams(vmem_limit_bytes=VMEM_LIMIT),
        name=f"hyena_filter_{n}",
    )(w1[0:1], w1[1:1 + nb], w1[1 + nb:1 + 2 * nb], r(b1), r(fr1), w2, r(b2), r(fr2), w3)


def _spectrum_kernel(c_ref, s_ref, ghi_ref, glo_ref, ck_ref, sk_ref, kre_ref, kim_ref, *, scale):
    hw = kre_ref.shape[1]
    ct = c_ref[...]
    st = s_ref[...]
    a = _dot(ct, ghi_ref[...]) + _dot(ct, glo_ref[...])
    b = _dot(st, ghi_ref[...]) + _dot(st, glo_ref[...])
    are, bre = a[:, 0:hw], a[:, hw:2 * hw]
    aim, bim = -b[:, 0:hw], -b[:, hw:2 * hw]
    ck = ck_ref[...]
    sk = sk_ref[...]
    kre_ref[...] = (ck * (are + bre) - sk * (aim + bim)) * scale
    kim_ref[...] = (ck * (aim - bim) + sk * (are - bre)) * scale


def _filter_spectrum(ghi, glo, ctab, stab):
    n, hw2 = ghi.shape
    hw = hw2 // 2
    tk = min(n, 512)
    half = (2.0 * np.arange(n, dtype=np.float64) + 1.0) * (2.0 * np.pi / (8 * n))
    ck = jnp.asarray(np.cos(half).reshape(n, 1), F32)
    sk = jnp.asarray(np.sin(half).reshape(n, 1), F32)
    tspec = pl.BlockSpec((tk, n), lambda i: (i, 0))
    gspec = pl.BlockSpec((n, hw2), lambda i: (0, 0))
    vspec = pl.BlockSpec((tk, 1), lambda i: (i, 0))
    ospec = pl.BlockSpec((tk, hw), lambda i: (i, 0))
    oshape = jax.ShapeDtypeStruct((n, hw), F32)
    return pl.pallas_call(
        functools.partial(_spectrum_kernel, scale=1.0 / n),
        out_shape=(oshape, oshape),
        grid=(n // tk,),
        in_specs=[tspec, tspec, gspec, gspec, vspec, vspec],
        out_specs=(ospec, ospec),
        compiler_params=_cp(("parallel",)),
        name=f"filter_spectrum_{n}",
    )(ctab, stab, ghi, glo, ck, sk)


def _hyena_kernel(c_ref, s_ref, z_ref, zt_ref, x0_ref, kre_ref, kim_ref, bias_ref, o_ref,
                  yre_scr, yim_scr, *, nb, tk):
    phase = pl.program_id(1)
    i = pl.program_id(2)
    ct = c_ref[...]
    st = s_ref[...]

    @pl.when(phase == 0)
    def _():
        kre = kre_ref[...]
        kim = kim_ref[...]
        r0 = pl.multiple_of(i * tk, tk)
        for j in range(nb):
            ure = _dot(ct, z_ref[j])
            uim = -_dot(st, z_ref[j])
            yre_scr[j, pl.ds(r0, tk), :] = (kre * ure - kim * uim).astype(BF16)
            yim_scr[j, pl.ds(r0, tk), :] = (kre * uim + kim * ure).astype(BF16)

    @pl.when(phase == 1)
    def _():
        for j in range(nb):
            y = _dot(ct, yre_scr[j]) - _dot(st, yim_scr[j])
            u = zt_ref[j].astype(F32)
            o_ref[j] = (x0_ref[j].astype(F32) * (y + u * bias_ref[...])).astype(BF16)


def _hyena(z, x0c, kre, kim, bias, ctab, stab):
    b, n, hw = z.shape
    nb = 2
    tk = min(n, 512)
    tspec = pl.BlockSpec((tk, n), lambda g, p, i: (i, 0))
    tile = pl.BlockSpec((nb, tk, hw), lambda g, p, i: (g, i * p, 0))
    kspec = pl.BlockSpec((tk, hw), lambda g, p, i: (i * (1 - p), 0))
    return pl.pallas_call(
        functools.partial(_hyena_kernel, nb=nb, tk=tk),
        out_shape=jax.ShapeDtypeStruct((b, n, hw), BF16),
        grid=(b // nb, 2, n // tk),
        in_specs=[
            tspec, tspec,
            pl.BlockSpec((nb, n, hw), lambda g, p, i: (g, 0, 0)),
            tile, tile, kspec, kspec,
            pl.BlockSpec((1, hw), lambda g, p, i: (0, 0)),
        ],
        out_specs=tile,
        scratch_shapes=[pltpu.VMEM((nb, n, hw), BF16), pltpu.VMEM((nb, n, hw), BF16)],
        compiler_params=_cp(("parallel", "arbitrary", "arbitrary")),
        name=f"hyena_conv_{n}",
    )(ctab, stab, z, z, x0c, kre, kim, bias)


def _mixout_kernel(a_ref, f_ref, h_ref, w_ref, x_ref, ga_ref, g_ref, o_ref):
    aw = a_ref.shape[2]
    fw = f_ref.shape[2]
    mix = (_dot(a_ref[0], w_ref[0:aw, :]) + _dot(f_ref[0], w_ref[aw:aw + fw, :])
           + _dot(h_ref[0], w_ref[aw + fw:, :]))
    o_ref[0] = x_ref[0] + ga_ref[0] * _rms(mix, g_ref[...])


def _mixout(attn, four, hy, w_out, x, ga, g):
    b, n, d = x.shape
    tm = min(n, 512)
    row = lambda width: pl.BlockSpec((1, tm, width), lambda bi, i: (bi, i, 0))
    return pl.pallas_call(
        _mixout_kernel,
        out_shape=jax.ShapeDtypeStruct((b, n, d), F32),
        grid=(b, n // tm),
        in_specs=[
            row(attn.shape[2]), row(four.shape[2]), row(hy.shape[2]),
            pl.BlockSpec(w_out.shape, lambda bi, i: (0, 0)),
            row(d),
            pl.BlockSpec((1, 1, d), lambda bi, i: (bi, 0, 0)),
            pl.BlockSpec((1, d), lambda bi, i: (0, 0)),
        ],
        out_specs=row(d),
        compiler_params=_cp(("parallel", "parallel")),
        name=f"mixout_{n}",
    )(attn, four, hy, w_out, x, ga, g)


HALO = 8


FFN_CHUNK = 256


def _ffn_kernel(xp_ref, x_ref, xn_ref, sc_ref, sh_ref, g_ref, wu_ref, cw_ref, cb_ref, wd_ref,
                ga_ref, go_ref, o_ref, *, tm):
    i = pl.program_id(1)
    nrow = pl.num_programs(1)
    rows = tm + 2 * HALO
    dff = wd_ref.shape[0]
    cf = FFN_CHUNK
    g = g_ref[...]
    sc = 1.0 + sc_ref[0]
    sh = sh_ref[0]
    keep_p = jnp.where(i > 0, 1.0, 0.0)
    keep_n = jnp.where(i < nrow - 1, 1.0, 0.0)
    x = x_ref[0]
    fx = jnp.concatenate([(_rms(xp_ref[0], g) * sc + sh) * keep_p,
                          _rms(x, g) * sc + sh,
                          (_rms(xn_ref[0], g) * sc + sh) * keep_n], axis=0).astype(BF16)

    def conv(u, w, bias):
        prev = pltpu.roll(u, 1, 0)[HALO:HALO + tm]
        nxt = pltpu.roll(u, rows - 1, 0)[HALO:HALO + tm]
        return prev * w[0:1] + u[HALO:HALO + tm] * w[1:2] + nxt * w[2:3] + bias

    acts = []
    for c in range(dff // cf):
        lo, hi = c * cf, (c + 1) * cf
        gate = conv(_dot(fx, wu_ref[:, lo:hi]), cw_ref[:, lo:hi], cb_ref[:, lo:hi])
        val = conv(_dot(fx, wu_ref[:, dff + lo:dff + hi]), cw_ref[:, dff + lo:dff + hi],
                   cb_ref[:, dff + lo:dff + hi])
        acts.append((gate * jax.nn.sigmoid(gate) * val).astype(BF16))
    y = _dot(jnp.concatenate(acts, axis=1), wd_ref[...])
    o_ref[0] = x + ga_ref[0] * _rms(y, go_ref[...])


def _ffn(x, sc, sh, g, w_up, w_conv, b_conv, w_down, ga, g_post):
    b, n, d = x.shape
    tm = min(n, 512)
    hb = tm // HALO
    nhb = n // HALO
    vec = pl.BlockSpec((1, 1, d), lambda bi, i: (bi, 0, 0))
    gvec = pl.BlockSpec((1, d), lambda bi, i: (0, 0))
    resident = lambda a: pl.BlockSpec(a.shape, lambda bi, i: (0, 0), pipeline_mode=pl.Buffered(1))
    return pl.pallas_call(
        functools.partial(_ffn_kernel, tm=tm),
        out_shape=jax.ShapeDtypeStruct((b, n, d), F32),
        grid=(b, n // tm),
        in_specs=[
            pl.BlockSpec((1, HALO, d), lambda bi, i: (bi, jnp.maximum(i * hb - 1, 0), 0)),
            pl.BlockSpec((1, tm, d), lambda bi, i: (bi, i, 0)),
            pl.BlockSpec((1, HALO, d), lambda bi, i: (bi, jnp.minimum((i + 1) * hb, nhb - 1), 0)),
            vec, vec, gvec,
            resident(w_up), resident(w_conv), resident(b_conv), resident(w_down),
            vec, gvec,
        ],
        out_specs=pl.BlockSpec((1, tm, d), lambda bi, i: (bi, i, 0)),
        compiler_params=_cp(("parallel", "parallel")),
        name=f"conv_ffn_{n}",
    )(x, x, x, sc, sh, g, w_up, w_conv, b_conv, w_down, ga, g_post)


def _rope_tables(n):
    half = HEAD_DIM // 4
    inv = ROPE_THETA ** (-jnp.arange(0, 2 * half, 2, dtype=F32) / (2 * half))
    pos = jnp.arange(n, dtype=jnp.int32)
    row = (pos // GRID_W).astype(F32)
    col = (pos % GRID_W).astype(F32)
    ang_r = row[:, None] * inv[None, :]
    ang_c = col[:, None] * inv[None, :]
    cr, sr, cc, sc = jnp.cos(ang_r), jnp.sin(ang_r), jnp.cos(ang_c), jnp.sin(ang_c)
    cos64 = jnp.concatenate([cr, cr, cc, cc], axis=-1)
    sin64 = jnp.concatenate([-sr, sr, -sc, sc], axis=-1)
    return jnp.tile(cos64, (1, 2)), jnp.tile(sin64, (1, 2))


def _head_ones(width):
    idx = np.arange(width) // HEAD_DIM
    return jnp.asarray((idx[:, None] == idx[None, :]).astype(np.float32), BF16)


def _channel_dft(width):
    c = np.arange(width)
    same = (c[:, None] // FOURIER_GROUP_DIM) == (c[None, :] // FOURIER_GROUP_DIM)
    ang = 2.0 * np.pi * ((c[:, None] % FOURIER_GROUP_DIM) * (c[None, :] % FOURIER_GROUP_DIM)
                         % FOURIER_GROUP_DIM) / FOURIER_GROUP_DIM
    cs = np.concatenate([np.where(same, np.cos(ang), 0.0), np.where(same, np.sin(ang), 0.0)], axis=1)
    return jnp.asarray(cs.astype(np.float32), BF16)


def kernel(x, c, ctx, c_ctx, w_mod, b_mod, g_pre_mix, g_post_mix, g_pre_ffn, g_post_ffn, w_in, g_q, g_k, w_fourier, w_hy_conv, b_hy_conv, hy_w1, hy_b1, hy_fr1, hy_w2, hy_b2, hy_fr2, hy_w3, hy_bias, w_out, w_up, w_ffn_conv, b_ffn_conv, w_down):
    bsz, seq, d = x.shape
    clen = ctx.shape[1]
    depth = w_mod.shape[0]
    fw = w_fourier.shape[1]
    hw = hy_bias.shape[1]
    aw = w_in.shape[2] - 2 * N_KV_HEADS * HEAD_DIM - fw - 3 * hw
    k0 = aw
    f0 = aw + 2 * N_KV_HEADS * HEAD_DIM

    nrows = -(-(bsz + 1) // 8) * 8
    cc = jnp.zeros((nrows, d), F32).at[:bsz].set(c).at[bsz].set(c_ctx)
    mods = _modulation(cc, w_mod, b_mod)

    cos_x, sin_x = _rope_tables(seq)
    cos_c = jnp.ones((clen, LANES), F32)
    sin_c = jnp.zeros((clen, LANES), F32)
    ones_bd = _head_ones(aw)
    cs = _channel_dft(fw)
    hc_x, hs_x, fc_x, fs_x = _make_tables(seq)
    hc_c, hs_c, fc_c, fs_c = _make_tables(clen)

    def row1(v):
        return v.reshape(1, -1)

    for i in range(depth):
        last = i == depth - 1
        mx = mods[i, :bsz].reshape(bsz, 1, 6, d)
        mc = jnp.broadcast_to(mods[i, bsz].reshape(1, 1, 6, d), (bsz, 1, 6, d))
        sh1, sc1, ga1, sh2, sc2, ga2 = (mx[:, :, t] for t in range(6))
        csh1, csc1, cga1, csh2, csc2, cga2 = (mc[:, :, t] for t in range(6))
        w_in_b = w_in[i].astype(BF16)
        w_out_b = w_out[i].astype(BF16)
        w_up_b = w_up[i].astype(BF16)
        w_down_b = w_down[i].astype(BF16)
        w_f_b = w_fourier[i].astype(BF16)
        gq_t = jnp.tile(g_q[i], aw // HEAD_DIM).reshape(1, aw)
        gk_t = jnp.tile(g_k[i], N_KV_HEADS).reshape(1, N_KV_HEADS * HEAD_DIM)
        g_pre = row1(g_pre_mix[i])
        g_post = row1(g_post_mix[i])
        hy_params = (hy_w1[i], hy_b1[i], hy_fr1[i], hy_w2[i], hy_b2[i], hy_fr2[i], hy_w3[i])
        hbias = row1(hy_bias[i])
        hcw = w_hy_conv[i]
        hcb = row1(b_hy_conv[i])

        q, k4, v4, fa, ph = _project(x, sc1, sh1, g_pre, w_in_b, gq_t, gk_t, cos_x, sin_x,
                                     ones_bd, cs, rope=True, kv_only=False)
        if last:
            kc4, vc4 = _project(ctx, csc1, csh1, g_pre, w_in_b[:, k0:f0], gq_t, gk_t, cos_c, sin_c,
                                ones_bd, cs, rope=False, kv_only=True)
        else:
            qc, kc4, vc4, fac, phc = _project(ctx, csc1, csh1, g_pre, w_in_b, gq_t, gk_t, cos_c,
                                              sin_c, ones_bd, cs, rope=False, kv_only=False)
        attn_x = _attention(q, [(k4, v4), (kc4, vc4)])
        four_x = _fourier(fa, fc_x, fs_x, w_f_b)
        ghi, glo = _hyena_filter(seq, *hy_params)
        kre, kim = _filter_spectrum(ghi, glo, hc_x, hs_x)
        z, x0c = _hyena_pre(ph, hcw, hcb)
        hy_x = _hyena(z, x0c, kre, kim, hbias, hc_x, hs_x)
        x_new = _mixout(attn_x, four_x, hy_x, w_out_b, x, ga1, g_post)

        if not last:
            attn_c = _attention(qc, [(kc4, vc4)])
            four_c = _fourier(fac, fc_c, fs_c, w_f_b)
            ghi_c, glo_c = _hyena_filter(clen, *hy_params)
            kre_c, kim_c = _filter_spectrum(ghi_c, glo_c, hc_c, hs_c)
            zc, x0cc = _hyena_pre(phc, hcw, hcb)
            hy_c = _hyena(zc, x0cc, kre_c, kim_c, hbias, hc_c, hs_c)
            ctx = _mixout(attn_c, four_c, hy_c, w_out_b, ctx, cga1, g_post)
        x = x_new

        g_pf = row1(g_pre_ffn[i])
        g_of = row1(g_post_ffn[i])
        fcw = w_ffn_conv[i]
        fcb = row1(b_ffn_conv[i])
        x = _ffn(x, sc2, sh2, g_pf, w_up_b, fcw, fcb, w_down_b, ga2, g_of)
        if not last:
            ctx = _ffn(ctx, csc2, csh2, g_pf, w_up_b, fcw, fcb, w_down_b, cga2, g_of)

    return x
```

```python
import functools
import math

import numpy as np
import jax
import jax.numpy as jnp
from jax import lax
from jax.experimental import pallas as pl
from jax.experimental.pallas import tpu as pltpu

F32 = jnp.float32
BF16 = jnp.bfloat16

HEAD_DIM = 64
GQA_GROUP = 4
N_KV_HEADS = 2
GRID_W = 64
ROPE_THETA = 10000.0
FOURIER_GROUP_DIM = 64
HYENA_BANDS = 16
HYENA_FAST_DECAY = 0.3
HYENA_SLOW_DECAY = 1.5
HYENA_TARGET = 1e-2
NORM_EPS = 1e-6
LANES = 128
VMEM_LIMIT = 56 * 1024 * 1024


def _cp(sem, vmem=VMEM_LIMIT):
    return pltpu.CompilerParams(dimension_semantics=sem, vmem_limit_bytes=vmem)


def _dot(a, b):
    return jnp.dot(a, b, preferred_element_type=F32)


def _dot_nt(a, b):
    return lax.dot_general(a, b, (((1,), (1,)), ((), ())), preferred_element_type=F32)


def _split(a):
    hi = a.astype(BF16)
    lo = (a - hi.astype(F32)).astype(BF16)
    return hi, lo


def _dot3(a, b):
    ah, al = _split(a)
    bh, bl = _split(b)
    return _dot(ah, bh) + _dot(ah, bl) + _dot(al, bh)


def _rms(x, g):
    ms = jnp.mean(x * x, axis=-1, keepdims=True)
    return x * lax.rsqrt(ms + NORM_EPS) * g


def _tables_kernel(hc_ref, hs_ref, fc_ref, fs_ref, hca, hsa, fca, fsa, *, n, tk):
    i = pl.program_id(0)
    hyena_step = 2.0 * math.pi / (8 * n)
    fourier_step = 2.0 * math.pi / n

    @pl.when(i == 0)
    def _():
        k = lax.broadcasted_iota(jnp.int32, (tk, n), 0)
        s = lax.broadcasted_iota(jnp.int32, (tk, n), 1)
        a = (((2 * k + 1) * (2 * s + 1)) & (8 * n - 1)).astype(F32) * hyena_step
        hca[...] = jnp.cos(a)
        hsa[...] = jnp.sin(a)
        b = ((k * s) & (n - 1)).astype(F32) * fourier_step
        fca[...] = jnp.cos(b)
        fsa[...] = jnp.sin(b)

    s1 = lax.broadcasted_iota(jnp.int32, (1, n), 1)
    rot = (((2 * tk * i) * (2 * s1 + 1)) & (8 * n - 1)).astype(F32) * hyena_step
    cb, sb = jnp.cos(rot), jnp.sin(rot)
    hc_ref[...] = (hca[...] * cb - hsa[...] * sb).astype(BF16)
    hs_ref[...] = (hsa[...] * cb + hca[...] * sb).astype(BF16)
    rot = (((tk * i) * s1) & (n - 1)).astype(F32) * fourier_step
    cb, sb = jnp.cos(rot), jnp.sin(rot)
    fc_ref[...] = (fca[...] * cb - fsa[...] * sb).astype(BF16)
    fs_ref[...] = (fsa[...] * cb + fca[...] * sb).astype(BF16)


def _make_tables(n):
    tk = min(n, 256)
    spec = pl.BlockSpec((tk, n), lambda i: (i, 0))
    shp = jax.ShapeDtypeStruct((n, n), BF16)
    return pl.pallas_call(
        functools.partial(_tables_kernel, n=n, tk=tk),
        out_shape=(shp, shp, shp, shp),
        grid=(n // tk,),
        out_specs=(spec, spec, spec, spec),
        scratch_shapes=[pltpu.VMEM((tk, n), F32)] * 4,
        compiler_params=_cp(("arbitrary",)),
        name=f"dft_tables_{n}",
    )()


def _mod_kernel(c_ref, w_ref, b_ref, o_ref):
    c = c_ref[...]
    a = c * jax.nn.sigmoid(c)
    o_ref[0] = _dot3(a, w_ref[0]) + b_ref[0]


def _modulation(cc, w_mod, b_mod):
    depth, d, n6 = w_mod.shape
    tn = 1536
    rows = cc.shape[0]
    return pl.pallas_call(
        _mod_kernel,
        out_shape=jax.ShapeDtypeStruct((depth, rows, n6), F32),
        grid=(depth, n6 // tn),
        in_specs=[
            pl.BlockSpec((rows, d), lambda l, j: (0, 0)),
            pl.BlockSpec((1, d, tn), lambda l, j: (l, 0, j)),
            pl.BlockSpec((1, 1, tn), lambda l, j: (l, 0, j)),
        ],
        out_specs=pl.BlockSpec((1, rows, tn), lambda l, j: (l, 0, j)),
        compiler_params=_cp(("parallel", "parallel")),
        name="modulation",
    )(cc, w_mod, b_mod.reshape(depth, 1, n6))


def _head_norm(p, gain, ones_bd):
    hi, lo = _split(p * p)
    ss = _dot(hi, ones_bd) + _dot(lo, ones_bd)
    return p * lax.rsqrt(ss * (1.0 / HEAD_DIM) + NORM_EPS) * gain


def _rope(xn, cos_t, sin_t):
    lane = lax.broadcasted_iota(jnp.int32, (1, LANES), 1)
    first = (lane % 32) < 16
    outs = []
    for j in range(xn.shape[1] // LANES):
        c = xn[:, j * LANES:(j + 1) * LANES]
        sw = jnp.where(first, pltpu.roll(c, LANES - 16, 1), pltpu.roll(c, 16, 1))
        outs.append(c * cos_t + sw * sin_t)
    return outs[0] if len(outs) == 1 else jnp.concatenate(outs, axis=1)


def _tile_heads(kv):
    lane = lax.broadcasted_iota(jnp.int32, (1, LANES), 1)
    low = lane < HEAD_DIM
    r = pltpu.roll(kv, HEAD_DIM, 1)
    h0 = jnp.where(low, kv, r)
    h1 = jnp.where(low, r, kv)
    return jnp.concatenate([h0, h0], axis=1), jnp.concatenate([h1, h1], axis=1)


def _value_heads(v):
    lane = lax.broadcasted_iota(jnp.int32, (1, LANES), 1)
    low = lane < HEAD_DIM
    r = pltpu.roll(v, HEAD_DIM, 1)
    return jnp.where(low, v, 1.0), jnp.where(low, r, 1.0)


def _proj_kernel(x_ref, sc_ref, sh_ref, g_ref, w_ref, gq_ref, gk_ref, cos_ref, sin_ref,
                 ones_ref, cs_ref, *outs, rope, kv_only):
    x = x_ref[0]
    h = _rms(x, g_ref[...]) * (1.0 + sc_ref[0]) + sh_ref[0]
    px = _dot(h.astype(BF16), w_ref[...])
    ones_bd = ones_ref[...]
    if kv_only:
        k4_ref, v4_ref = outs
        k = px[:, 0:128]
        v = px[:, 128:256]
    else:
        q_ref, k4_ref, v4_ref, fa_ref, ph_ref = outs
        aw = ones_bd.shape[0]
        q = _head_norm(px[:, 0:aw], gq_ref[...], ones_bd)
        if rope:
            q = _rope(q, cos_ref[...], sin_ref[...])
        q_ref[0] = (q * (HEAD_DIM ** -0.5 * math.log2(math.e))).astype(BF16)
        k = px[:, aw:aw + 128]
        v = px[:, aw + 128:aw + 256]
        f0 = aw + 256
        fw = cs_ref.shape[0]
        fa_ref[0] = _dot(px[:, f0:f0 + fw].astype(BF16), cs_ref[...]).astype(BF16)
        ph_ref[0] = px[:, f0 + fw:].astype(BF16)
    k = _head_norm(k, gk_ref[...], ones_bd[0:128, 0:128])
    if rope:
        k = _rope(k, cos_ref[...], sin_ref[...])
    k0, k1 = _tile_heads(k)
    k4_ref[0, 0] = k0.astype(BF16)
    k4_ref[0, 1] = k1.astype(BF16)
    v0, v1 = _value_heads(v)
    v4_ref[0, 0] = v0.T.astype(BF16)
    v4_ref[0, 1] = v1.T.astype(BF16)


def _project(x, sc, sh, g, w, gq, gk, cos_t, sin_t, ones_bd, cs, *, rope, kv_only):
    b, n, d = x.shape
    tm = min(n, 512)
    wn = w.shape[1]
    aw = ones_bd.shape[0]
    fw = cs.shape[0]
    vec = lambda width: pl.BlockSpec((1, 1, width), lambda bi, i: (bi, 0, 0))
    full = lambda a: pl.BlockSpec(a.shape, lambda bi, i: (0,) * a.ndim)
    kv_spec = pl.BlockSpec((1, N_KV_HEADS, tm, 256), lambda bi, i: (bi, 0, i, 0))
    kv_shape = jax.ShapeDtypeStruct((b, N_KV_HEADS, n, 256), BF16)
    vl_spec = pl.BlockSpec((1, N_KV_HEADS, LANES, tm), lambda bi, i: (bi, 0, 0, i))
    vl_shape = jax.ShapeDtypeStruct((b, N_KV_HEADS, LANES, n), BF16)
    row = lambda width: pl.BlockSpec((1, tm, width), lambda bi, i: (bi, i, 0))
    if kv_only:
        out_shape = (kv_shape, vl_shape)
        out_specs = (kv_spec, vl_spec)
    else:
        hw = wn - aw - 256 - fw
        out_shape = (jax.ShapeDtypeStruct((b, n, aw), BF16), kv_shape, vl_shape,
                     jax.ShapeDtypeStruct((b, n, 2 * fw), BF16),
                     jax.ShapeDtypeStruct((b, n, hw), BF16))
        out_specs = (row(aw), kv_spec, vl_spec, row(2 * fw), row(hw))
    return pl.pallas_call(
        functools.partial(_proj_kernel, rope=rope, kv_only=kv_only),
        out_shape=out_shape,
        grid=(b, n // tm),
        in_specs=[
            row(d), vec(d), vec(d), full(g), full(w), full(gq), full(gk),
            pl.BlockSpec((tm, LANES), lambda bi, i: (i, 0)),
            pl.BlockSpec((tm, LANES), lambda bi, i: (i, 0)),
            full(ones_bd), full(cs),
        ],
        out_specs=out_specs,
        compiler_params=_cp(("parallel", "parallel")),
        name="proj_kv" if kv_only else ("proj_rope" if rope else "proj_ctx"),
    )(x, sc, sh, g, w, gq, gk, cos_t, sin_t, ones_bd, cs)


ATTN_SUB = 128


def _attn_kernel(q_ref, qn_ref, *refs, lks, nsub):
    nsrc = len(lks)
    k_refs = refs[0:2 * nsrc:2]
    v_refs = refs[1:2 * nsrc:2]
    o_ref = refs[2 * nsrc]
    s_scrs = refs[2 * nsrc + 1:2 * nsrc + 3]
    m_scr = refs[2 * nsrc + 3]
    sub = ATTN_SUB
    rows = GQA_GROUP * sub
    gw = GQA_GROUP * HEAD_DIM
    group = lax.broadcasted_iota(jnp.int32, (1, gw), 1) // HEAD_DIM

    chunks = []
    off = 0
    for j, lk in enumerate(lks):
        tkc = min(lk, 512)
        for c in range(lk // tkc):
            chunks.append((j, c * tkc, tkc, off))
            off += tkc

    lane = lax.broadcasted_iota(jnp.int32, (1, LANES), 1)
    low = lane < HEAD_DIM

    def scores(q, s_scr):
        zero = jnp.zeros_like(q)
        qm = jnp.concatenate([jnp.where(group == g, q, zero) for g in range(GQA_GROUP)], axis=0)
        for j, start, tkc, o in chunks:
            s_scr[o:o + tkc, :] = _dot_nt(k_refs[j][0, 0, start:start + tkc, :], qm)
        return jnp.max(s_scr[...], axis=0, keepdims=True)

    @pl.when(pl.program_id(2) == 0)
    def _():
        m_scr[...] = scores(q_ref[0, 0:sub, :], s_scrs[0])

    m = m_scr[...]
    for u in range(nsub):
        s_scr = s_scrs[u % 2]
        nxt = q_ref[0, (u + 1) * sub:(u + 2) * sub, :] if u + 1 < nsub else qn_ref[0]
        m_next = scores(nxt, s_scrs[(u + 1) % 2])
        bits = pltpu.bitcast(m_next, jnp.uint32)
        acc = jnp.zeros((LANES, rows), F32) + ((bits >> 16) >> 16).astype(F32)
        for j, start, tkc, o in chunks:
            p = jnp.exp2(s_scr[o:o + tkc, :] - m).astype(BF16)
            acc = acc + _dot(v_refs[j][0, 0, :, start:start + tkc], p)
        o_t = acc / acc[HEAD_DIM:HEAD_DIM + 1, :]
        heads = [o_t[:, g * sub:(g + 1) * sub].T for g in range(GQA_GROUP)]
        left = jnp.where(low, heads[0], pltpu.roll(heads[1], HEAD_DIM, 1))
        right = jnp.where(low, heads[2], pltpu.roll(heads[3], HEAD_DIM, 1))
        o_ref[0, u * sub:(u + 1) * sub, :] = jnp.concatenate([left, right], axis=1).astype(BF16)
        m = m_next
    m_scr[...] = m


def _attention(q, kvs):
    b, lq, aw = q.shape
    tq = min(lq, 8 * ATTN_SUB)
    nsub = tq // ATTN_SUB
    assert nsub % 2 == 0 or lq == tq
    nblk = lq // ATTN_SUB
    lks = tuple(k.shape[2] for k, _ in kvs)
    gw = GQA_GROUP * HEAD_DIM
    in_specs = [pl.BlockSpec((1, tq, gw), lambda bi, h, i: (bi, i, h)),
                pl.BlockSpec((1, ATTN_SUB, gw),
                             lambda bi, h, i: (bi, jnp.minimum((i + 1) * nsub, nblk - 1), h))]
    args = [q, q]
    for k4, vlt in kvs:
        for a in (k4, vlt):
            in_specs.append(pl.BlockSpec((1, 1) + a.shape[2:], lambda bi, h, i: (bi, h, 0, 0),
                                         pipeline_mode=pl.Buffered(1)))
        args += [k4, vlt]
    return pl.pallas_call(
        functools.partial(_attn_kernel, lks=lks, nsub=nsub),
        out_shape=jax.ShapeDtypeStruct((b, lq, aw), BF16),
        grid=(b, N_KV_HEADS, lq // tq),
        in_specs=in_specs,
        out_specs=pl.BlockSpec((1, tq, gw), lambda bi, h, i: (bi, i, h)),
        scratch_shapes=[pltpu.VMEM((sum(lks), GQA_GROUP * ATTN_SUB), F32)] * 2
        + [pltpu.VMEM((1, GQA_GROUP * ATTN_SUB), F32)],
        compiler_params=_cp(("parallel", "parallel", "arbitrary")),
        name=f"attention_{lq}",
    )(*args)


def _fourier_kernel(c_ref, s_ref, fa_ref, w_ref, o_ref, *, nb, scale):
    fw = w_ref.shape[0]
    ct = c_ref[...]
    st = s_ref[...]
    for j in range(nb):
        y = _dot(ct, fa_ref[j, :, 0:fw]) - _dot(st, fa_ref[j, :, fw:2 * fw])
        o_ref[j] = _dot((y * scale).astype(BF16), w_ref[...]).astype(BF16)


def _fourier(fa, ctab, stab, w_f):
    b, n, fw2 = fa.shape
    fw = fw2 // 2
    nb = 2
    tk = min(n, 512)
    scale = 1.0 / math.sqrt(n * FOURIER_GROUP_DIM)
    return pl.pallas_call(
        functools.partial(_fourier_kernel, nb=nb, scale=scale),
        out_shape=jax.ShapeDtypeStruct((b, n, fw), BF16),
        grid=(b // nb, n // tk),
        in_specs=[
            pl.BlockSpec((tk, n), lambda g, i: (i, 0)),
            pl.BlockSpec((tk, n), lambda g, i: (i, 0)),
            pl.BlockSpec((nb, n, fw2), lambda g, i: (g, 0, 0)),
            pl.BlockSpec((fw, fw), lambda g, i: (0, 0)),
        ],
        out_specs=pl.BlockSpec((nb, tk, fw), lambda g, i: (g, i, 0)),
        compiler_params=_cp(("parallel", "arbitrary")),
        name=f"fourier_{n}",
    )(ctab, stab, fa, w_f)


def _conv3(p, w, bias):
    n = p.shape[0]
    row = lax.broadcasted_iota(jnp.int32, (n, 1), 0)
    prev = jnp.where(row == 0, 0.0, pltpu.roll(p, 1, 0))
    nxt = jnp.where(row == n - 1, 0.0, pltpu.roll(p, n - 1, 0))
    return prev * w[0:1] + p * w[1:2] + nxt * w[2:3] + bias


def _hyena_pre_kernel(p0_ref, p1_ref, p2_ref, w0_ref, w1_ref, w2_ref, b0_ref, b1_ref, b2_ref,
                      z_ref, x0_ref):
    x0 = _conv3(p0_ref[0].astype(F32), w0_ref[...], b0_ref[...])
    x1 = _conv3(p1_ref[0].astype(F32), w1_ref[...], b1_ref[...])
    v = _conv3(p2_ref[0].astype(F32), w2_ref[...], b2_ref[...])
    z_ref[0] = (x1 * v).astype(BF16)
    x0_ref[0] = x0.astype(BF16)


def _hyena_pre(ph, w, bias):
    b, n, hw3 = ph.shape
    hw = hw3 // 3
    nc = hw // LANES
    pspec = lambda s: pl.BlockSpec((1, n, LANES), lambda bi, c: (bi, 0, s * nc + c))
    wspec = lambda s: pl.BlockSpec((3, LANES), lambda bi, c: (0, s * nc + c))
    bspec = lambda s: pl.BlockSpec((1, LANES), lambda bi, c: (0, s * nc + c))
    ospec = pl.BlockSpec((1, n, LANES), lambda bi, c: (bi, 0, c))
    oshape = jax.ShapeDtypeStruct((b, n, hw), BF16)
    return pl.pallas_call(
        _hyena_pre_kernel,
        out_shape=(oshape, oshape),
        grid=(b, nc),
        in_specs=[pspec(0), pspec(1), pspec(2), wspec(0), wspec(1), wspec(2),
                  bspec(0), bspec(1), bspec(2)],
        out_specs=(ospec, ospec),
        compiler_params=_cp(("parallel", "parallel")),
        name=f"hyena_pre_{n}",
    )(ph, ph, ph, w, w, w, bias, bias, bias)


def _filter_kernel(w1t_ref, w1c_ref, w1s_ref, b1_ref, fr1_ref, w2_ref, b2_ref, fr2_ref, w3_ref,
                   ghi_ref, glo_ref, *, n):
    hw = w3_ref.shape[1] // 2
    i = lax.broadcasted_iota(jnp.int32, (n, 1), 0).astype(F32)
    t = i / float(n - 1)
    jb = lax.broadcasted_iota(jnp.int32, (1, HYENA_BANDS), 1).astype(F32)
    bands = 1e-4 + jb * ((HYENA_BANDS - 1 - 1e-4) / (HYENA_BANDS - 1))
    ang = ((2.0 * math.pi) * i / float(n)) * bands
    pre = t * w1t_ref[...] + _dot3(jnp.cos(ang), w1c_ref[...]) - _dot3(jnp.sin(ang), w1s_ref[...])
    h = jnp.sin(fr1_ref[...] * (pre + b1_ref[...]))
    h = jnp.sin(fr2_ref[...] * (_dot3(h, w2_ref[...]) + b2_ref[...]))
    h = _dot3(h, w3_ref[...])
    d0 = math.log(HYENA_TARGET) / HYENA_SLOW_DECAY
    d1 = math.log(HYENA_TARGET) / HYENA_FAST_DECAY
    jd = lax.broadcasted_iota(jnp.int32, (1, hw), 1).astype(F32)
    deltas = jnp.abs(d0 + jd * ((d1 - d0) / (hw - 1)))
    decay = jnp.exp(-t * deltas)
    hf = h[:, 0:hw] * decay
    hb = jnp.where(i == 0.0, 0.0, h[:, hw:2 * hw] * decay)
    total = jnp.sum(jnp.abs(hf), axis=0, keepdims=True) + jnp.sum(jnp.abs(hb), axis=0, keepdims=True)
    g = jnp.concatenate([hf / total, hb / total], axis=1)
    hi, lo = _split(g)
    ghi_ref[...] = hi
    glo_ref[...] = lo


def _hyena_filter(n, w1, b1, fr1, w2, b2, fr2, w3):
    nb = HYENA_BANDS
    r = lambda a: a.reshape(1, -1)
    shp = jax.ShapeDtypeStruct((n, w3.shape[1]), BF16)
    return pl.pallas_call(
        functools.partial(_filter_kernel, n=n),
        out_shape=(shp, shp),
        compiler_params=pltpu.CompilerParams(vmem_limit_bytes=VMEM_LIMIT),
        name=f"hyena_filter_{n}",
    )(w1[0:1], w1[1:1 + nb], w1[1 + nb:1 + 2 * nb], r(b1), r(fr1), w2, r(b2), r(fr2), w3)


def _spectrum_kernel(c_ref, s_ref, ghi_ref, glo_ref, ck_ref, sk_ref, kre_ref, kim_ref, *, scale):
    hw = kre_ref.shape[1]
    ct = c_ref[...]
    st = s_ref[...]
    a = _dot(ct, ghi_ref[...]) + _dot(ct, glo_ref[...])
    b = _dot(st, ghi_ref[...]) + _dot(st, glo_ref[...])
    are, bre = a[:, 0:hw], a[:, hw:2 * hw]
    aim, bim = -b[:, 0:hw], -b[:, hw:2 * hw]
    ck = ck_ref[...]
    sk = sk_ref[...]
    kre_ref[...] = (ck * (are + bre) - sk * (aim + bim)) * scale
    kim_ref[...] = (ck * (aim - bim) + sk * (are - bre)) * scale


def _filter_spectrum(ghi, glo, ctab, stab):
    n, hw2 = ghi.shape
    hw = hw2 // 2
    tk = min(n, 512)
    half = (2.0 * np.arange(n, dtype=np.float64) + 1.0) * (2.0 * np.pi / (8 * n))
    ck = jnp.asarray(np.cos(half).reshape(n, 1), F32)
    sk = jnp.asarray(np.sin(half).reshape(n, 1), F32)
    tspec = pl.BlockSpec((tk, n), lambda i: (i, 0))
    gspec = pl.BlockSpec((n, hw2), lambda i: (0, 0))
    vspec = pl.BlockSpec((tk, 1), lambda i: (i, 0))
    ospec = pl.BlockSpec((tk, hw), lambda i: (i, 0))
    oshape = jax.ShapeDtypeStruct((n, hw), F32)
    return pl.pallas_call(
        functools.partial(_spectrum_kernel, scale=1.0 / n),
        out_shape=(oshape, oshape),
        grid=(n // tk,),
        in_specs=[tspec, tspec, gspec, gspec, vspec, vspec],
        out_specs=(ospec, ospec),
        compiler_params=_cp(("parallel",)),
        name=f"filter_spectrum_{n}",
    )(ctab, stab, ghi, glo, ck, sk)


def _hyena_kernel(c_ref, s_ref, z_ref, zt_ref, x0_ref, kre_ref, kim_ref, bias_ref, o_ref,
                  yre_scr, yim_scr, *, nb, tk):
    phase = pl.program_id(1)
    i = pl.program_id(2)
    ct = c_ref[...]
    st = s_ref[...]

    @pl.when(phase == 0)
    def _():
        kre = kre_ref[...]
        kim = kim_ref[...]
        r0 = pl.multiple_of(i * tk, tk)
        for j in range(nb):
            ure = _dot(ct, z_ref[j])
            uim = -_dot(st, z_ref[j])
            yre_scr[j, pl.ds(r0, tk), :] = (kre * ure - kim * uim).astype(BF16)
            yim_scr[j, pl.ds(r0, tk), :] = (kre * uim + kim * ure).astype(BF16)

    @pl.when(phase == 1)
    def _():
        for j in range(nb):
            y = _dot(ct, yre_scr[j]) - _dot(st, yim_scr[j])
            u = zt_ref[j].astype(F32)
            o_ref[j] = (x0_ref[j].astype(F32) * (y + u * bias_ref[...])).astype(BF16)


def _hyena(z, x0c, kre, kim, bias, ctab, stab):
    b, n, hw = z.shape
    nb = 2
    tk = min(n, 512)
    tspec = pl.BlockSpec((tk, n), lambda g, p, i: (i, 0))
    tile = pl.BlockSpec((nb, tk, hw), lambda g, p, i: (g, i * p, 0))
    kspec = pl.BlockSpec((tk, hw), lambda g, p, i: (i * (1 - p), 0))
    return pl.pallas_call(
        functools.partial(_hyena_kernel, nb=nb, tk=tk),
        out_shape=jax.ShapeDtypeStruct((b, n, hw), BF16),
        grid=(b // nb, 2, n // tk),
        in_specs=[
            tspec, tspec,
            pl.BlockSpec((nb, n, hw), lambda g, p, i: (g, 0, 0)),
            tile, tile, kspec, kspec,
            pl.BlockSpec((1, hw), lambda g, p, i: (0, 0)),
        ],
        out_specs=tile,
        scratch_shapes=[pltpu.VMEM((nb, n, hw), BF16), pltpu.VMEM((nb, n, hw), BF16)],
        compiler_params=_cp(("parallel", "arbitrary", "arbitrary")),
        name=f"hyena_conv_{n}",
    )(ctab, stab, z, z, x0c, kre, kim, bias)


def _mixout_kernel(a_ref, f_ref, h_ref, w_ref, x_ref, ga_ref, g_ref, o_ref):
    aw = a_ref.shape[2]
    fw = f_ref.shape[2]
    mix = (_dot(a_ref[0], w_ref[0:aw, :]) + _dot(f_ref[0], w_ref[aw:aw + fw, :])
           + _dot(h_ref[0], w_ref[aw + fw:, :]))
    o_ref[0] = x_ref[0] + ga_ref[0] * _rms(mix, g_ref[...])


def _mixout(attn, four, hy, w_out, x, ga, g):
    b, n, d = x.shape
    tm = min(n, 512)
    row = lambda width: pl.BlockSpec((1, tm, width), lambda bi, i: (bi, i, 0))
    return pl.pallas_call(
        _mixout_kernel,
        out_shape=jax.ShapeDtypeStruct((b, n, d), F32),
        grid=(b, n // tm),
        in_specs=[
            row(attn.shape[2]), row(four.shape[2]), row(hy.shape[2]),
            pl.BlockSpec(w_out.shape, lambda bi, i: (0, 0)),
            row(d),
            pl.BlockSpec((1, 1, d), lambda bi, i: (bi, 0, 0)),
            pl.BlockSpec((1, d), lambda bi, i: (0, 0)),
        ],
        out_specs=row(d),
        compiler_params=_cp(("parallel", "parallel")),
        name=f"mixout_{n}",
    )(attn, four, hy, w_out, x, ga, g)


HALO = 8


FFN_CHUNK = 256


def _ffn_kernel(xp_ref, x_ref, xn_ref, sc_ref, sh_ref, g_ref, wu_ref, cw_ref, cb_ref, wd_ref,
                ga_ref, go_ref, o_ref, *, tm):
    i = pl.program_id(1)
    nrow = pl.num_programs(1)
    rows = tm + 2 * HALO
    dff = wd_ref.shape[0]
    cf = FFN_CHUNK
    g = g_ref[...]
    sc = 1.0 + sc_ref[0]
    sh = sh_ref[0]
    keep_p = jnp.where(i > 0, 1.0, 0.0)
    keep_n = jnp.where(i < nrow - 1, 1.0, 0.0)
    x = x_ref[0]
    fx = jnp.concatenate([(_rms(xp_ref[0], g) * sc + sh) * keep_p,
                          _rms(x, g) * sc + sh,
                          (_rms(xn_ref[0], g) * sc + sh) * keep_n], axis=0).astype(BF16)

    def conv(u, w, bias):
        prev = pltpu.roll(u, 1, 0)[HALO:HALO + tm]
        nxt = pltpu.roll(u, rows - 1, 0)[HALO:HALO + tm]
        return prev * w[0:1] + u[HALO:HALO + tm] * w[1:2] + nxt * w[2:3] + bias

    acts = []
    for c in range(dff // cf):
        lo, hi = c * cf, (c + 1) * cf
        gate = conv(_dot(fx, wu_ref[:, lo:hi]), cw_ref[:, lo:hi], cb_ref[:, lo:hi])
        val = conv(_dot(fx, wu_ref[:, dff + lo:dff + hi]), cw_ref[:, dff + lo:dff + hi],
                   cb_ref[:, dff + lo:dff + hi])
        acts.append((gate * jax.nn.sigmoid(gate) * val).astype(BF16))
    y = _dot(jnp.concatenate(acts, axis=1), wd_ref[...])
    o_ref[0] = x + ga_ref[0] * _rms(y, go_ref[...])


def _ffn(x, sc, sh, g, w_up, w_conv, b_conv, w_down, ga, g_post):
    b, n, d = x.shape
    tm = min(n, 512)
    hb = tm // HALO
    nhb = n // HALO
    vec = pl.BlockSpec((1, 1, d), lambda bi, i: (bi, 0, 0))
    gvec = pl.BlockSpec((1, d), lambda bi, i: (0, 0))
    resident = lambda a: pl.BlockSpec(a.shape, lambda bi, i: (0, 0), pipeline_mode=pl.Buffered(1))
    return pl.pallas_call(
        functools.partial(_ffn_kernel, tm=tm),
        out_shape=jax.ShapeDtypeStruct((b, n, d), F32),
        grid=(b, n // tm),
        in_specs=[
            pl.BlockSpec((1, HALO, d), lambda bi, i: (bi, jnp.maximum(i * hb - 1, 0), 0)),
            pl.BlockSpec((1, tm, d), lambda bi, i: (bi, i, 0)),
            pl.BlockSpec((1, HALO, d), lambda bi, i: (bi, jnp.minimum((i + 1) * hb, nhb - 1), 0)),
            vec, vec, gvec,
            resident(w_up), resident(w_conv), resident(b_conv), resident(w_down),
            vec, gvec,
        ],
        out_specs=pl.BlockSpec((1, tm, d), lambda bi, i: (bi, i, 0)),
        compiler_params=_cp(("parallel", "parallel")),
        name=f"conv_ffn_{n}",
    )(x, x, x, sc, sh, g, w_up, w_conv, b_conv, w_down, ga, g_post)


def _rope_tables(n):
    half = HEAD_DIM // 4
    inv = ROPE_THETA ** (-jnp.arange(0, 2 * half, 2, dtype=F32) / (2 * half))
    pos = jnp.arange(n, dtype=jnp.int32)
    row = (pos // GRID_W).astype(F32)
    col = (pos % GRID_W).astype(F32)
    ang_r = row[:, None] * inv[None, :]
    ang_c = col[:, None] * inv[None, :]
    cr, sr, cc, sc = jnp.cos(ang_r), jnp.sin(ang_r), jnp.cos(ang_c), jnp.sin(ang_c)
    cos64 = jnp.concatenate([cr, cr, cc, cc], axis=-1)
    sin64 = jnp.concatenate([-sr, sr, -sc, sc], axis=-1)
    return jnp.tile(cos64, (1, 2)), jnp.tile(sin64, (1, 2))


def _head_ones(width):
    idx = np.arange(width) // HEAD_DIM
    return jnp.asarray((idx[:, None] == idx[None, :]).astype(np.float32), BF16)


def _channel_dft(width):
    c = np.arange(width)
    same = (c[:, None] // FOURIER_GROUP_DIM) == (c[None, :] // FOURIER_GROUP_DIM)
    ang = 2.0 * np.pi * ((c[:, None] % FOURIER_GROUP_DIM) * (c[None, :] % FOURIER_GROUP_DIM)
                         % FOURIER_GROUP_DIM) / FOURIER_GROUP_DIM
    cs = np.concatenate([np.where(same, np.cos(ang), 0.0), np.where(same, np.sin(ang), 0.0)], axis=1)
    return jnp.asarray(cs.astype(np.float32), BF16)


def kernel(x, c, ctx, c_ctx, w_mod, b_mod, g_pre_mix, g_post_mix, g_pre_ffn, g_post_ffn, w_in, g_q, g_k, w_fourier, w_hy_conv, b_hy_conv, hy_w1, hy_b1, hy_fr1, hy_w2, hy_b2, hy_fr2, hy_w3, hy_bias, w_out, w_up, w_ffn_conv, b_ffn_conv, w_down):
    bsz, seq, d = x.shape
    clen = ctx.shape[1]
    depth = w_mod.shape[0]
    fw = w_fourier.shape[1]
    hw = hy_bias.shape[1]
    aw = w_in.shape[2] - 2 * N_KV_HEADS * HEAD_DIM - fw - 3 * hw
    k0 = aw
    f0 = aw + 2 * N_KV_HEADS * HEAD_DIM

    nrows = -(-(bsz + 1) // 8) * 8
    cc = jnp.zeros((nrows, d), F32).at[:bsz].set(c).at[bsz].set(c_ctx)
    mods = _modulation(cc, w_mod, b_mod)

    cos_x, sin_x = _rope_tables(seq)
    cos_c = jnp.ones((clen, LANES), F32)
    sin_c = jnp.zeros((clen, LANES), F32)
    ones_bd = _head_ones(aw)
    cs = _channel_dft(fw)
    hc_x, hs_x, fc_x, fs_x = _make_tables(seq)
    hc_c, hs_c, fc_c, fs_c = _make_tables(clen)

    def row1(v):
        return v.reshape(1, -1)

    for i in range(depth):
        last = i == depth - 1
        mx = mods[i, :bsz].reshape(bsz, 1, 6, d)
        mc = jnp.broadcast_to(mods[i, bsz].reshape(1, 1, 6, d), (bsz, 1, 6, d))
        sh1, sc1, ga1, sh2, sc2, ga2 = (mx[:, :, t] for t in range(6))
        csh1, csc1, cga1, csh2, csc2, cga2 = (mc[:, :, t] for t in range(6))
        w_in_b = w_in[i].astype(BF16)
        w_out_b = w_out[i].astype(BF16)
        w_up_b = w_up[i].astype(BF16)
        w_down_b = w_down[i].astype(BF16)
        w_f_b = w_fourier[i].astype(BF16)
        gq_t = jnp.tile(g_q[i], aw // HEAD_DIM).reshape(1, aw)
        gk_t = jnp.tile(g_k[i], N_KV_HEADS).reshape(1, N_KV_HEADS * HEAD_DIM)
        g_pre = row1(g_pre_mix[i])
        g_post = row1(g_post_mix[i])
        hy_params = (hy_w1[i], hy_b1[i], hy_fr1[i], hy_w2[i], hy_b2[i], hy_fr2[i], hy_w3[i])
        hbias = row1(hy_bias[i])
        hcw = w_hy_conv[i]
        hcb = row1(b_hy_conv[i])

        q, k4, v4, fa, ph = _project(x, sc1, sh1, g_pre, w_in_b, gq_t, gk_t, cos_x, sin_x,
                                     ones_bd, cs, rope=True, kv_only=False)
        if last:
            kc4, vc4 = _project(ctx, csc1, csh1, g_pre, w_in_b[:, k0:f0], gq_t, gk_t, cos_c, sin_c,
                                ones_bd, cs, rope=False, kv_only=True)
        else:
            qc, kc4, vc4, fac, phc = _project(ctx, csc1, csh1, g_pre, w_in_b, gq_t, gk_t, cos_c,
                                              sin_c, ones_bd, cs, rope=False, kv_only=False)
        attn_x = _attention(q, [(k4, v4), (kc4, vc4)])
        four_x = _fourier(fa, fc_x, fs_x, w_f_b)
        ghi, glo = _hyena_filter(seq, *hy_params)
        kre, kim = _filter_spectrum(ghi, glo, hc_x, hs_x)
        z, x0c = _hyena_pre(ph, hcw, hcb)
        hy_x = _hyena(z, x0c, kre, kim, hbias, hc_x, hs_x)
        x_new = _mixout(attn_x, four_x, hy_x, w_out_b, x, ga1, g_post)

        if not last:
            attn_c = _attention(qc, [(kc4, vc4)])
            four_c = _fourier(fac, fc_c, fs_c, w_f_b)
            ghi_c, glo_c = _hyena_filter(clen, *hy_params)
            kre_c, kim_c = _filter_spectrum(ghi_c, glo_c, hc_c, hs_c)
            zc, x0cc = _hyena_pre(phc, hcw, hcb)
            hy_c = _hyena(zc, x0cc, kre_c, kim_c, hbias, hc_c, hs_c)
            ctx = _mixout(attn_c, four_c, hy_c, w_out_b, ctx, cga1, g_post)
        x = x_new

        g_pf = row1(g_pre_ffn[i])
        g_of = row1(g_post_ffn[i])
        fcw = w_ffn_conv[i]
        fcb = row1(b_ffn_conv[i])
        x = _ffn(x, sc2, sh2, g_pf, w_up_b, fcw, fcb, w_down_b, ga2, g_of)
        if not last:
            ctx = _ffn(ctx, csc2, csh2, g_pf, w_up_b, fcw, fcb, w_down_b, cga2, g_of)

    return x
```

```python
import functools
import math

import numpy as np
import jax
import jax.numpy as jnp
from jax import lax
from jax.experimental import pallas as pl
from jax.experimental.pallas import tpu as pltpu

F32 = jnp.float32
BF16 = jnp.bfloat16

HEAD_DIM = 64
GQA_GROUP = 4
N_KV_HEADS = 2
GRID_W = 64
ROPE_THETA = 10000.0
FOURIER_GROUP_DIM = 64
HYENA_BANDS = 16
HYENA_FAST_DECAY = 0.3
HYENA_SLOW_DECAY = 1.5
HYENA_TARGET = 1e-2
NORM_EPS = 1e-6
LANES = 128
VMEM_LIMIT = 56 * 1024 * 1024


def _cp(sem, vmem=VMEM_LIMIT):
    return pltpu.CompilerParams(dimension_semantics=sem, vmem_limit_bytes=vmem)


def _dot(a, b):
    return jnp.dot(a, b, preferred_element_type=F32)


def _dot_nt(a, b):
    return lax.dot_general(a, b, (((1,), (1,)), ((), ())), preferred_element_type=F32)


def _split(a):
    hi = a.astype(BF16)
    lo = (a - hi.astype(F32)).astype(BF16)
    return hi, lo


def _dot3(a, b):
    ah, al = _split(a)
    bh, bl = _split(b)
    return _dot(ah, bh) + _dot(ah, bl) + _dot(al, bh)


def _rms(x, g):
    ms = jnp.mean(x * x, axis=-1, keepdims=True)
    return x * lax.rsqrt(ms + NORM_EPS) * g


def _tables_kernel(hc_ref, hs_ref, fc_ref, fs_ref, hca, hsa, fca, fsa, *, n, tk):
    i = pl.program_id(0)
    hyena_step = 2.0 * math.pi / (8 * n)
    fourier_step = 2.0 * math.pi / n

    @pl.when(i == 0)
    def _():
        k = lax.broadcasted_iota(jnp.int32, (tk, n), 0)
        s = lax.broadcasted_iota(jnp.int32, (tk, n), 1)
        a = (((2 * k + 1) * (2 * s + 1)) & (8 * n - 1)).astype(F32) * hyena_step
        hca[...] = jnp.cos(a)
        hsa[...] = jnp.sin(a)
        b = ((k * s) & (n - 1)).astype(F32) * fourier_step
        fca[...] = jnp.cos(b)
        fsa[...] = jnp.sin(b)

    s1 = lax.broadcasted_iota(jnp.int32, (1, n), 1)
    rot = (((2 * tk * i) * (2 * s1 + 1)) & (8 * n - 1)).astype(F32) * hyena_step
    cb, sb = jnp.cos(rot), jnp.sin(rot)
    hc_ref[...] = (hca[...] * cb - hsa[...] * sb).astype(BF16)
    hs_ref[...] = (hsa[...] * cb + hca[...] * sb).astype(BF16)
    rot = (((tk * i) * s1) & (n - 1)).astype(F32) * fourier_step
    cb, sb = jnp.cos(rot), jnp.sin(rot)
    fc_ref[...] = (fca[...] * cb - fsa[...] * sb).astype(BF16)
    fs_ref[...] = (fsa[...] * cb + fca[...] * sb).astype(BF16)


def _make_tables(n):
    tk = min(n, 256)
    spec = pl.BlockSpec((tk, n), lambda i: (i, 0))
    shp = jax.ShapeDtypeStruct((n, n), BF16)
    return pl.pallas_call(
        functools.partial(_tables_kernel, n=n, tk=tk),
        out_shape=(shp, shp, shp, shp),
        grid=(n // tk,),
        out_specs=(spec, spec, spec, spec),
        scratch_shapes=[pltpu.VMEM((tk, n), F32)] * 4,
        compiler_params=_cp(("arbitrary",)),
        name=f"dft_tables_{n}",
    )()


def _mod_kernel(c_ref, w_ref, b_ref, o_ref):
    c = c_ref[...]
    a = c * jax.nn.sigmoid(c)
    o_ref[0] = _dot3(a, w_ref[0]) + b_ref[0]


def _modulation(cc, w_mod, b_mod):
    depth, d, n6 = w_mod.shape
    tn = 1536
    rows = cc.shape[0]
    return pl.pallas_call(
        _mod_kernel,
        out_shape=jax.ShapeDtypeStruct((depth, rows, n6), F32),
        grid=(depth, n6 // tn),
        in_specs=[
            pl.BlockSpec((rows, d), lambda l, j: (0, 0)),
            pl.BlockSpec((1, d, tn), lambda l, j: (l, 0, j)),
            pl.BlockSpec((1, 1, tn), lambda l, j: (l, 0, j)),
        ],
        out_specs=pl.BlockSpec((1, rows, tn), lambda l, j: (l, 0, j)),
        compiler_params=_cp(("parallel", "parallel")),
        name="modulation",
    )(cc, w_mod, b_mod.reshape(depth, 1, n6))


def _head_norm(p, gain, ones_bd):
    hi, lo = _split(p * p)
    ss = _dot(hi, ones_bd) + _dot(lo, ones_bd)
    return p * lax.rsqrt(ss * (1.0 / HEAD_DIM) + NORM_EPS) * gain


def _rope(xn, cos_t, sin_t):
    lane = lax.broadcasted_iota(jnp.int32, (1, LANES), 1)
    first = (lane % 32) < 16
    outs = []
    for j in range(xn.shape[1] // LANES):
        c = xn[:, j * LANES:(j + 1) * LANES]
        sw = jnp.where(first, pltpu.roll(c, LANES - 16, 1), pltpu.roll(c, 16, 1))
        outs.append(c * cos_t + sw * sin_t)
    return outs[0] if len(outs) == 1 else jnp.concatenate(outs, axis=1)


def _tile_heads(kv):
    lane = lax.broadcasted_iota(jnp.int32, (1, LANES), 1)
    low = lane < HEAD_DIM
    r = pltpu.roll(kv, HEAD_DIM, 1)
    h0 = jnp.where(low, kv, r)
    h1 = jnp.where(low, r, kv)
    return jnp.concatenate([h0, h0], axis=1), jnp.concatenate([h1, h1], axis=1)


def _value_heads(v):
    lane = lax.broadcasted_iota(jnp.int32, (1, LANES), 1)
    low = lane < HEAD_DIM
    r = pltpu.roll(v, HEAD_DIM, 1)
    return jnp.where(low, v, 1.0), jnp.where(low, r, 1.0)


def _proj_kernel(x_ref, sc_ref, sh_ref, g_ref, w_ref, gq_ref, gk_ref, cos_ref, sin_ref,
                 ones_ref, cs_ref, *outs, rope, kv_only):
    x = x_ref[0]
    h = _rms(x, g_ref[...]) * (1.0 + sc_ref[0]) + sh_ref[0]
    px = _dot(h.astype(BF16), w_ref[...])
    ones_bd = ones_ref[...]
    if kv_only:
        k4_ref, v4_ref = outs
        k = px[:, 0:128]
        v = px[:, 128:256]
    else:
        q_ref, k4_ref, v4_ref, fa_ref, ph_ref = outs
        aw = ones_bd.shape[0]
        q = _head_norm(px[:, 0:aw], gq_ref[...], ones_bd)
        if rope:
            q = _rope(q, cos_ref[...], sin_ref[...])
        q_ref[0] = (q * (HEAD_DIM ** -0.5 * math.log2(math.e))).astype(BF16)
        k = px[:, aw:aw + 128]
        v = px[:, aw + 128:aw + 256]
        f0 = aw + 256
        fw = cs_ref.shape[0]
        fa_ref[0] = _dot(px[:, f0:f0 + fw].astype(BF16), cs_ref[...]).astype(BF16)
        ph_ref[0] = px[:, f0 + fw:].astype(BF16)
    k = _head_norm(k, gk_ref[...], ones_bd[0:128, 0:128])
    if rope:
        k = _rope(k, cos_ref[...], sin_ref[...])
    k0, k1 = _tile_heads(k)
    k4_ref[0, 0] = k0.astype(BF16)
    k4_ref[0, 1] = k1.astype(BF16)
    v0, v1 = _value_heads(v)
    v4_ref[0, 0] = v0.T.astype(BF16)
    v4_ref[0, 1] = v1.T.astype(BF16)


def _project(x, sc, sh, g, w, gq, gk, cos_t, sin_t, ones_bd, cs, *, rope, kv_only):
    b, n, d = x.shape
    tm = min(n, 512)
    wn = w.shape[1]
    aw = ones_bd.shape[0]
    fw = cs.shape[0]
    vec = lambda width: pl.BlockSpec((1, 1, width), lambda bi, i: (bi, 0, 0))
    full = lambda a: pl.BlockSpec(a.shape, lambda bi, i: (0,) * a.ndim)
    kv_spec = pl.BlockSpec((1, N_KV_HEADS, tm, 256), lambda bi, i: (bi, 0, i, 0))
    kv_shape = jax.ShapeDtypeStruct((b, N_KV_HEADS, n, 256), BF16)
    vl_spec = pl.BlockSpec((1, N_KV_HEADS, LANES, tm), lambda bi, i: (bi, 0, 0, i))
    vl_shape = jax.ShapeDtypeStruct((b, N_KV_HEADS, LANES, n), BF16)
    row = lambda width: pl.BlockSpec((1, tm, width), lambda bi, i: (bi, i, 0))
    if kv_only:
        out_shape = (kv_shape, vl_shape)
        out_specs = (kv_spec, vl_spec)
    else:
        hw = wn - aw - 256 - fw
        out_shape = (jax.ShapeDtypeStruct((b, n, aw), BF16), kv_shape, vl_shape,
                     jax.ShapeDtypeStruct((b, n, 2 * fw), BF16),
                     jax.ShapeDtypeStruct((b, n, hw), BF16))
        out_specs = (row(aw), kv_spec, vl_spec, row(2 * fw), row(hw))
    return pl.pallas_call(
        functools.partial(_proj_kernel, rope=rope, kv_only=kv_only),
        out_shape=out_shape,
        grid=(b, n // tm),
        in_specs=[
            row(d), vec(d), vec(d), full(g), full(w), full(gq), full(gk),
            pl.BlockSpec((tm, LANES), lambda bi, i: (i, 0)),
            pl.BlockSpec((tm, LANES), lambda bi, i: (i, 0)),
            full(ones_bd), full(cs),
        ],
        out_specs=out_specs,
        compiler_params=_cp(("parallel", "parallel")),
        name="proj_kv" if kv_only else ("proj_rope" if rope else "proj_ctx"),
    )(x, sc, sh, g, w, gq, gk, cos_t, sin_t, ones_bd, cs)


ATTN_SUB = 128


SOFTMAX_MIN_DENOM = 2.0 ** -80


def _attn_kernel(q_ref, *refs, lks, nsub):
    nsrc = len(lks)
    k_refs = refs[0:2 * nsrc:2]
    v_refs = refs[1:2 * nsrc:2]
    o_ref = refs[2 * nsrc]
    s_scrs = refs[2 * nsrc + 1:2 * nsrc + 3]
    kmax_scr = refs[2 * nsrc + 3]
    sub = ATTN_SUB
    rows = GQA_GROUP * sub
    gw = GQA_GROUP * HEAD_DIM
    group = lax.broadcasted_iota(jnp.int32, (1, gw), 1) // HEAD_DIM

    chunks = []
    off = 0
    for j, lk in enumerate(lks):
        tkc = min(lk, 512)
        for c in range(lk // tkc):
            chunks.append((j, c * tkc, tkc, off))
            off += tkc

    lane = lax.broadcasted_iota(jnp.int32, (1, LANES), 1)
    low = lane < HEAD_DIM

    def masked_q(u):
        q = q_ref[0, u * sub:(u + 1) * sub, :]
        zero = jnp.zeros_like(q)
        return jnp.concatenate([jnp.where(group == g, q, zero) for g in range(GQA_GROUP)], axis=0)

    def finish(u, acc):
        o_t = acc / acc[HEAD_DIM:HEAD_DIM + 1, :]
        heads = [o_t[:, g * sub:(g + 1) * sub].T for g in range(GQA_GROUP)]
        left = jnp.where(low, heads[0], pltpu.roll(heads[1], HEAD_DIM, 1))
        right = jnp.where(low, heads[2], pltpu.roll(heads[3], HEAD_DIM, 1))
        o_ref[0, u * sub:(u + 1) * sub, :] = jnp.concatenate([left, right], axis=1).astype(BF16)

    @pl.when(pl.program_id(2) == 0)
    def _():
        ones = jnp.ones((gw, LANES), BF16)
        best = jnp.zeros((1, LANES), F32)
        for j, start, tkc, o in chunks:
            k = k_refs[j][0, 0, start:start + tkc, :].astype(F32)
            best = jnp.maximum(best, jnp.max(_dot((k * k).astype(BF16), ones), axis=0, keepdims=True))
        kmax_scr[...] = best * (1.0 / GQA_GROUP)

    kmax2 = jnp.concatenate([kmax_scr[...]] * (rows // LANES), axis=1)
    ones_q = jnp.ones((8, gw), BF16)
    lmin = jnp.full((1, rows), jnp.inf, F32)
    for u in range(nsub):
        qm = masked_q(u)
        qf = qm.astype(F32)
        qn2 = _dot_nt(ones_q, (qf * qf).astype(BF16))[0:1]
        shift = jnp.sqrt(qn2 * kmax2) * 1.02
        acc = jnp.zeros((LANES, rows), F32)
        for j, start, tkc, o in chunks:
            s = _dot_nt(k_refs[j][0, 0, start:start + tkc, :], qm)
            p = jnp.exp2(s - shift).astype(BF16)
            acc = acc + _dot(v_refs[j][0, 0, :, start:start + tkc], p)
        lmin = jnp.minimum(lmin, acc[HEAD_DIM:HEAD_DIM + 1, :])
        finish(u, acc)

    @pl.when(jnp.logical_not(jnp.min(lmin) >= SOFTMAX_MIN_DENOM))
    def _():
        def scores(u):
            s_scr = s_scrs[u % 2]
            qm = masked_q(u)
            for j, start, tkc, o in chunks:
                s_scr[o:o + tkc, :] = _dot_nt(k_refs[j][0, 0, start:start + tkc, :], qm)
            return jnp.max(s_scr[...], axis=0, keepdims=True)

        m = scores(0)
        for u in range(nsub):
            s_scr = s_scrs[u % 2]
            acc = jnp.zeros((LANES, rows), F32)
            if u + 1 < nsub:
                m_next = scores(u + 1)
            for j, start, tkc, o in chunks:
                p = jnp.exp2(s_scr[o:o + tkc, :] - m).astype(BF16)
                acc = acc + _dot(v_refs[j][0, 0, :, start:start + tkc], p)
            finish(u, acc)
            if u + 1 < nsub:
                m = m_next


def _attention(q, kvs):
    b, lq, aw = q.shape
    tq = min(lq, 8 * ATTN_SUB)
    nsub = tq // ATTN_SUB
    lks = tuple(k.shape[2] for k, _ in kvs)
    gw = GQA_GROUP * HEAD_DIM
    in_specs = [pl.BlockSpec((1, tq, gw), lambda bi, h, i: (bi, i, h))]
    args = [q]
    for k4, vlt in kvs:
        for a in (k4, vlt):
            in_specs.append(pl.BlockSpec((1, 1) + a.shape[2:], lambda bi, h, i: (bi, h, 0, 0),
                                         pipeline_mode=pl.Buffered(1)))
        args += [k4, vlt]
    return pl.pallas_call(
        functools.partial(_attn_kernel, lks=lks, nsub=nsub),
        out_shape=jax.ShapeDtypeStruct((b, lq, aw), BF16),
        grid=(b, N_KV_HEADS, lq // tq),
        in_specs=in_specs,
        out_specs=pl.BlockSpec((1, tq, gw), lambda bi, h, i: (bi, i, h)),
        scratch_shapes=[pltpu.VMEM((sum(lks), GQA_GROUP * ATTN_SUB), F32)] * 2
        + [pltpu.VMEM((1, LANES), F32)],
        compiler_params=_cp(("parallel", "parallel", "arbitrary")),
        name=f"attention_{lq}",
    )(*args)


def _fourier_kernel(c_ref, s_ref, fa_ref, w_ref, o_ref, *, nb, scale):
    fw = w_ref.shape[0]
    ct = c_ref[...]
    st = s_ref[...]
    for j in range(nb):
        y = _dot(ct, fa_ref[j, :, 0:fw]) - _dot(st, fa_ref[j, :, fw:2 * fw])
        o_ref[j] = _dot((y * scale).astype(BF16), w_ref[...]).astype(BF16)


def _fourier(fa, ctab, stab, w_f):
    b, n, fw2 = fa.shape
    fw = fw2 // 2
    nb = 2
    tk = min(n, 512)
    scale = 1.0 / math.sqrt(n * FOURIER_GROUP_DIM)
    return pl.pallas_call(
        functools.partial(_fourier_kernel, nb=nb, scale=scale),
        out_shape=jax.ShapeDtypeStruct((b, n, fw), BF16),
        grid=(b // nb, n // tk),
        in_specs=[
            pl.BlockSpec((tk, n), lambda g, i: (i, 0)),
            pl.BlockSpec((tk, n), lambda g, i: (i, 0)),
            pl.BlockSpec((nb, n, fw2), lambda g, i: (g, 0, 0)),
            pl.BlockSpec((fw, fw), lambda g, i: (0, 0)),
        ],
        out_specs=pl.BlockSpec((nb, tk, fw), lambda g, i: (g, i, 0)),
        compiler_params=_cp(("parallel", "arbitrary")),
        name=f"fourier_{n}",
    )(ctab, stab, fa, w_f)


def _conv3(p, w, bias):
    n = p.shape[0]
    row = lax.broadcasted_iota(jnp.int32, (n, 1), 0)
    prev = jnp.where(row == 0, 0.0, pltpu.roll(p, 1, 0))
    nxt = jnp.where(row == n - 1, 0.0, pltpu.roll(p, n - 1, 0))
    return prev * w[0:1] + p * w[1:2] + nxt * w[2:3] + bias


def _hyena_pre_kernel(p0_ref, p1_ref, p2_ref, w0_ref, w1_ref, w2_ref, b0_ref, b1_ref, b2_ref,
                      z_ref, x0_ref):
    x0 = _conv3(p0_ref[0].astype(F32), w0_ref[...], b0_ref[...])
    x1 = _conv3(p1_ref[0].astype(F32), w1_ref[...], b1_ref[...])
    v = _conv3(p2_ref[0].astype(F32), w2_ref[...], b2_ref[...])
    z_ref[0] = (x1 * v).astype(BF16)
    x0_ref[0] = x0.astype(BF16)


def _hyena_pre(ph, w, bias):
    b, n, hw3 = ph.shape
    hw = hw3 // 3
    nc = hw // LANES
    pspec = lambda s: pl.BlockSpec((1, n, LANES), lambda bi, c: (bi, 0, s * nc + c))
    wspec = lambda s: pl.BlockSpec((3, LANES), lambda bi, c: (0, s * nc + c))
    bspec = lambda s: pl.BlockSpec((1, LANES), lambda bi, c: (0, s * nc + c))
    ospec = pl.BlockSpec((1, n, LANES), lambda bi, c: (bi, 0, c))
    oshape = jax.ShapeDtypeStruct((b, n, hw), BF16)
    return pl.pallas_call(
        _hyena_pre_kernel,
        out_shape=(oshape, oshape),
        grid=(b, nc),
        in_specs=[pspec(0), pspec(1), pspec(2), wspec(0), wspec(1), wspec(2),
                  bspec(0), bspec(1), bspec(2)],
        out_specs=(ospec, ospec),
        compiler_params=_cp(("parallel", "parallel")),
        name=f"hyena_pre_{n}",
    )(ph, ph, ph, w, w, w, bias, bias, bias)


def _filter_kernel(w1t_ref, w1c_ref, w1s_ref, b1_ref, fr1_ref, w2_ref, b2_ref, fr2_ref, w3_ref,
                   ghi_ref, glo_ref, *, n):
    hw = w3_ref.shape[1] // 2
    i = lax.broadcasted_iota(jnp.int32, (n, 1), 0).astype(F32)
    t = i / float(n - 1)
    jb = lax.broadcasted_iota(jnp.int32, (1, HYENA_BANDS), 1).astype(F32)
    bands = 1e-4 + jb * ((HYENA_BANDS - 1 - 1e-4) / (HYENA_BANDS - 1))
    ang = ((2.0 * math.pi) * i / float(n)) * bands
    pre = t * w1t_ref[...] + _dot3(jnp.cos(ang), w1c_ref[...]) - _dot3(jnp.sin(ang), w1s_ref[...])
    h = jnp.sin(fr1_ref[...] * (pre + b1_ref[...]))
    h = jnp.sin(fr2_ref[...] * (_dot3(h, w2_ref[...]) + b2_ref[...]))
    h = _dot3(h, w3_ref[...])
    d0 = math.log(HYENA_TARGET) / HYENA_SLOW_DECAY
    d1 = math.log(HYENA_TARGET) / HYENA_FAST_DECAY
    jd = lax.broadcasted_iota(jnp.int32, (1, hw), 1).astype(F32)
    deltas = jnp.abs(d0 + jd * ((d1 - d0) / (hw - 1)))
    decay = jnp.exp(-t * deltas)
    hf = h[:, 0:hw] * decay
    hb = jnp.where(i == 0.0, 0.0, h[:, hw:2 * hw] * decay)
    total = jnp.sum(jnp.abs(hf), axis=0, keepdims=True) + jnp.sum(jnp.abs(hb), axis=0, keepdims=True)
    g = jnp.concatenate([hf / total, hb / total], axis=1)
    hi, lo = _split(g)
    ghi_ref[...] = hi
    glo_ref[...] = lo


def _hyena_filter(n, w1, b1, fr1, w2, b2, fr2, w3):
    nb = HYENA_BANDS
    r = lambda a: a.reshape(1, -1)
    shp = jax.ShapeDtypeStruct((n, w3.shape[1]), BF16)
    return pl.pallas_call(
        functools.partial(_filter_kernel, n=n),
        out_shape=(shp, shp),
        compiler_params=pltpu.CompilerParams(vmem_limit_bytes=VMEM_LIMIT),
        name=f"hyena_filter_{n}",
    )(w1[0:1], w1[1:1 + nb], w1[1 + nb:1 + 2 * nb], r(b1), r(fr1), w2, r(b2), r(fr2), w3)


def _spectrum_kernel(c_ref, s_ref, ghi_ref, glo_ref, ck_ref, sk_ref, kre_ref, kim_ref, *, scale):
    hw = kre_ref.shape[1]
    ct = c_ref[...]
    st = s_ref[...]
    a = _dot(ct, ghi_ref[...]) + _dot(ct, glo_ref[...])
    b = _dot(st, ghi_ref[...]) + _dot(st, glo_ref[...])
    are, bre = a[:, 0:hw], a[:, hw:2 * hw]
    aim, bim = -b[:, 0:hw], -b[:, hw:2 * hw]
    ck = ck_ref[...]
    sk = sk_ref[...]
    kre_ref[...] = (ck * (are + bre) - sk * (aim + bim)) * scale
    kim_ref[...] = (ck * (aim - bim) + sk * (are - bre)) * scale


def _filter_spectrum(ghi, glo, ctab, stab):
    n, hw2 = ghi.shape
    hw = hw2 // 2
    tk = min(n, 512)
    half = (2.0 * np.arange(n, dtype=np.float64) + 1.0) * (2.0 * np.pi / (8 * n))
    ck = jnp.asarray(np.cos(half).reshape(n, 1), F32)
    sk = jnp.asarray(np.sin(half).reshape(n, 1), F32)
    tspec = pl.BlockSpec((tk, n), lambda i: (i, 0))
    gspec = pl.BlockSpec((n, hw2), lambda i: (0, 0))
    vspec = pl.BlockSpec((tk, 1), lambda i: (i, 0))
    ospec = pl.BlockSpec((tk, hw), lambda i: (i, 0))
    oshape = jax.ShapeDtypeStruct((n, hw), F32)
    return pl.pallas_call(
        functools.partial(_spectrum_kernel, scale=1.0 / n),
        out_shape=(oshape, oshape),
        grid=(n // tk,),
        in_specs=[tspec, tspec, gspec, gspec, vspec, vspec],
        out_specs=(ospec, ospec),
        compiler_params=_cp(("parallel",)),
        name=f"filter_spectrum_{n}",
    )(ctab, stab, ghi, glo, ck, sk)


def _hyena_kernel(c_ref, s_ref, z_ref, zt_ref, x0_ref, kre_ref, kim_ref, bias_ref, o_ref,
                  yre_scr, yim_scr, *, nb, tk):
    phase = pl.program_id(1)
    i = pl.program_id(2)
    ct = c_ref[...]
    st = s_ref[...]

    @pl.when(phase == 0)
    def _():
        kre = kre_ref[...]
        kim = kim_ref[...]
        r0 = pl.multiple_of(i * tk, tk)
        for j in range(nb):
            ure = _dot(ct, z_ref[j])
            uim = -_dot(st, z_ref[j])
            yre_scr[j, pl.ds(r0, tk), :] = (kre * ure - kim * uim).astype(BF16)
            yim_scr[j, pl.ds(r0, tk), :] = (kre * uim + kim * ure).astype(BF16)

    @pl.when(phase == 1)
    def _():
        for j in range(nb):
            y = _dot(ct, yre_scr[j]) - _dot(st, yim_scr[j])
            u = zt_ref[j].astype(F32)
            o_ref[j] = (x0_ref[j].astype(F32) * (y + u * bias_ref[...])).astype(BF16)


def _hyena(z, x0c, kre, kim, bias, ctab, stab):
    b, n, hw = z.shape
    nb = 2
    tk = min(n, 512)
    tspec = pl.BlockSpec((tk, n), lambda g, p, i: (i, 0))
    tile = pl.BlockSpec((nb, tk, hw), lambda g, p, i: (g, i * p, 0))
    kspec = pl.BlockSpec((tk, hw), lambda g, p, i: (i * (1 - p), 0))
    return pl.pallas_call(
        functools.partial(_hyena_kernel, nb=nb, tk=tk),
        out_shape=jax.ShapeDtypeStruct((b, n, hw), BF16),
        grid=(b // nb, 2, n // tk),
        in_specs=[
            tspec, tspec,
            pl.BlockSpec((nb, n, hw), lambda g, p, i: (g, 0, 0)),
            tile, tile, kspec, kspec,
            pl.BlockSpec((1, hw), lambda g, p, i: (0, 0)),
        ],
        out_specs=tile,
        scratch_shapes=[pltpu.VMEM((nb, n, hw), BF16), pltpu.VMEM((nb, n, hw), BF16)],
        compiler_params=_cp(("parallel", "arbitrary", "arbitrary")),
        name=f"hyena_conv_{n}",
    )(ctab, stab, z, z, x0c, kre, kim, bias)


def _mixout_kernel(a_ref, f_ref, h_ref, w_ref, x_ref, ga_ref, g_ref, o_ref):
    aw = a_ref.shape[2]
    fw = f_ref.shape[2]
    mix = (_dot(a_ref[0], w_ref[0:aw, :]) + _dot(f_ref[0], w_ref[aw:aw + fw, :])
           + _dot(h_ref[0], w_ref[aw + fw:, :]))
    o_ref[0] = x_ref[0] + ga_ref[0] * _rms(mix, g_ref[...])


def _mixout(attn, four, hy, w_out, x, ga, g):
    b, n, d = x.shape
    tm = min(n, 512)
    row = lambda width: pl.BlockSpec((1, tm, width), lambda bi, i: (bi, i, 0))
    return pl.pallas_call(
        _mixout_kernel,
        out_shape=jax.ShapeDtypeStruct((b, n, d), F32),
        grid=(b, n // tm),
        in_specs=[
            row(attn.shape[2]), row(four.shape[2]), row(hy.shape[2]),
            pl.BlockSpec(w_out.shape, lambda bi, i: (0, 0)),
            row(d),
            pl.BlockSpec((1, 1, d), lambda bi, i: (bi, 0, 0)),
            pl.BlockSpec((1, d), lambda bi, i: (0, 0)),
        ],
        out_specs=row(d),
        compiler_params=_cp(("parallel", "parallel")),
        name=f"mixout_{n}",
    )(attn, four, hy, w_out, x, ga, g)


HALO = 8


FFN_CHUNK = 256


def _ffn_kernel(xp_ref, x_ref, xn_ref, sc_ref, sh_ref, g_ref, wu_ref, cw_ref, cb_ref, wd_ref,
                ga_ref, go_ref, o_ref, *, tm):
    i = pl.program_id(1)
    nrow = pl.num_programs(1)
    rows = tm + 2 * HALO
    dff = wd_ref.shape[0]
    cf = FFN_CHUNK
    g = g_ref[...]
    sc = 1.0 + sc_ref[0]
    sh = sh_ref[0]
    keep_p = jnp.where(i > 0, 1.0, 0.0)
    keep_n = jnp.where(i < nrow - 1, 1.0, 0.0)
    x = x_ref[0]
    fx = jnp.concatenate([(_rms(xp_ref[0], g) * sc + sh) * keep_p,
                          _rms(x, g) * sc + sh,
                          (_rms(xn_ref[0], g) * sc + sh) * keep_n], axis=0).astype(BF16)

    def conv(u, w, bias):
        prev = pltpu.roll(u, 1, 0)[HALO:HALO + tm]
        nxt = pltpu.roll(u, rows - 1, 0)[HALO:HALO + tm]
        return prev * w[0:1] + u[HALO:HALO + tm] * w[1:2] + nxt * w[2:3] + bias

    acts = []
    for c in range(dff // cf):
        lo, hi = c * cf, (c + 1) * cf
        gate = conv(_dot(fx, wu_ref[:, lo:hi]), cw_ref[:, lo:hi], cb_ref[:, lo:hi])
        val = conv(_dot(fx, wu_ref[:, dff + lo:dff + hi]), cw_ref[:, dff + lo:dff + hi],
                   cb_ref[:, dff + lo:dff + hi])
        acts.append((gate * jax.nn.sigmoid(gate) * val).astype(BF16))
    y = _dot(jnp.concatenate(acts, axis=1), wd_ref[...])
    o_ref[0] = x + ga_ref[0] * _rms(y, go_ref[...])


def _ffn(x, sc, sh, g, w_up, w_conv, b_conv, w_down, ga, g_post):
    b, n, d = x.shape
    tm = min(n, 512)
    hb = tm // HALO
    nhb = n // HALO
    vec = pl.BlockSpec((1, 1, d), lambda bi, i: (bi, 0, 0))
    gvec = pl.BlockSpec((1, d), lambda bi, i: (0, 0))
    resident = lambda a: pl.BlockSpec(a.shape, lambda bi, i: (0, 0), pipeline_mode=pl.Buffered(1))
    return pl.pallas_call(
        functools.partial(_ffn_kernel, tm=tm),
        out_shape=jax.ShapeDtypeStruct((b, n, d), F32),
        grid=(b, n // tm),
        in_specs=[
            pl.BlockSpec((1, HALO, d), lambda bi, i: (bi, jnp.maximum(i * hb - 1, 0), 0)),
            pl.BlockSpec((1, tm, d), lambda bi, i: (bi, i, 0)),
            pl.BlockSpec((1, HALO, d), lambda bi, i: (bi, jnp.minimum((i + 1) * hb, nhb - 1), 0)),
            vec, vec, gvec,
            resident(w_up), resident(w_conv), resident(b_conv), resident(w_down),
            vec, gvec,
        ],
        out_specs=pl.BlockSpec((1, tm, d), lambda bi, i: (bi, i, 0)),
        compiler_params=_cp(("parallel", "parallel")),
        name=f"conv_ffn_{n}",
    )(x, x, x, sc, sh, g, w_up, w_conv, b_conv, w_down, ga, g_post)


def _rope_tables(n):
    half = HEAD_DIM // 4
    inv = ROPE_THETA ** (-jnp.arange(0, 2 * half, 2, dtype=F32) / (2 * half))
    pos = jnp.arange(n, dtype=jnp.int32)
    row = (pos // GRID_W).astype(F32)
    col = (pos % GRID_W).astype(F32)
    ang_r = row[:, None] * inv[None, :]
    ang_c = col[:, None] * inv[None, :]
    cr, sr, cc, sc = jnp.cos(ang_r), jnp.sin(ang_r), jnp.cos(ang_c), jnp.sin(ang_c)
    cos64 = jnp.concatenate([cr, cr, cc, cc], axis=-1)
    sin64 = jnp.concatenate([-sr, sr, -sc, sc], axis=-1)
    return jnp.tile(cos64, (1, 2)), jnp.tile(sin64, (1, 2))


def _head_ones(width):
    idx = np.arange(width) // HEAD_DIM
    return jnp.asarray((idx[:, None] == idx[None, :]).astype(np.float32), BF16)


def _channel_dft(width):
    c = np.arange(width)
    same = (c[:, None] // FOURIER_GROUP_DIM) == (c[None, :] // FOURIER_GROUP_DIM)
    ang = 2.0 * np.pi * ((c[:, None] % FOURIER_GROUP_DIM) * (c[None, :] % FOURIER_GROUP_DIM)
                         % FOURIER_GROUP_DIM) / FOURIER_GROUP_DIM
    cs = np.concatenate([np.where(same, np.cos(ang), 0.0), np.where(same, np.sin(ang), 0.0)], axis=1)
    return jnp.asarray(cs.astype(np.float32), BF16)


def kernel(x, c, ctx, c_ctx, w_mod, b_mod, g_pre_mix, g_post_mix, g_pre_ffn, g_post_ffn, w_in, g_q, g_k, w_fourier, w_hy_conv, b_hy_conv, hy_w1, hy_b1, hy_fr1, hy_w2, hy_b2, hy_fr2, hy_w3, hy_bias, w_out, w_up, w_ffn_conv, b_ffn_conv, w_down):
    bsz, seq, d = x.shape
    clen = ctx.shape[1]
    depth = w_mod.shape[0]
    fw = w_fourier.shape[1]
    hw = hy_bias.shape[1]
    aw = w_in.shape[2] - 2 * N_KV_HEADS * HEAD_DIM - fw - 3 * hw
    k0 = aw
    f0 = aw + 2 * N_KV_HEADS * HEAD_DIM

    nrows = -(-(bsz + 1) // 8) * 8
    cc = jnp.zeros((nrows, d), F32).at[:bsz].set(c).at[bsz].set(c_ctx)
    mods = _modulation(cc, w_mod, b_mod)

    cos_x, sin_x = _rope_tables(seq)
    cos_c = jnp.ones((clen, LANES), F32)
    sin_c = jnp.zeros((clen, LANES), F32)
    ones_bd = _head_ones(aw)
    cs = _channel_dft(fw)
    hc_x, hs_x, fc_x, fs_x = _make_tables(seq)
    hc_c, hs_c, fc_c, fs_c = _make_tables(clen)

    def row1(v):
        return v.reshape(1, -1)

    for i in range(depth):
        last = i == depth - 1
        mx = mods[i, :bsz].reshape(bsz, 1, 6, d)
        mc = jnp.broadcast_to(mods[i, bsz].reshape(1, 1, 6, d), (bsz, 1, 6, d))
        sh1, sc1, ga1, sh2, sc2, ga2 = (mx[:, :, t] for t in range(6))
        csh1, csc1, cga1, csh2, csc2, cga2 = (mc[:, :, t] for t in range(6))
        w_in_b = w_in[i].astype(BF16)
        w_out_b = w_out[i].astype(BF16)
        w_up_b = w_up[i].astype(BF16)
        w_down_b = w_down[i].astype(BF16)
        w_f_b = w_fourier[i].astype(BF16)
        gq_t = jnp.tile(g_q[i], aw // HEAD_DIM).reshape(1, aw)
        gk_t = jnp.tile(g_k[i], N_KV_HEADS).reshape(1, N_KV_HEADS * HEAD_DIM)
        g_pre = row1(g_pre_mix[i])
        g_post = row1(g_post_mix[i])
        hy_params = (hy_w1[i], hy_b1[i], hy_fr1[i], hy_w2[i], hy_b2[i], hy_fr2[i], hy_w3[i])
        hbias = row1(hy_bias[i])
        hcw = w_hy_conv[i]
        hcb = row1(b_hy_conv[i])

        q, k4, v4, fa, ph = _project(x, sc1, sh1, g_pre, w_in_b, gq_t, gk_t, cos_x, sin_x,
                                     ones_bd, cs, rope=True, kv_only=False)
        if last:
            kc4, vc4 = _project(ctx, csc1, csh1, g_pre, w_in_b[:, k0:f0], gq_t, gk_t, cos_c, sin_c,
                                ones_bd, cs, rope=False, kv_only=True)
        else:
            qc, kc4, vc4, fac, phc = _project(ctx, csc1, csh1, g_pre, w_in_b, gq_t, gk_t, cos_c,
                                              sin_c, ones_bd, cs, rope=False, kv_only=False)
        attn_x = _attention(q, [(k4, v4), (kc4, vc4)])
        four_x = _fourier(fa, fc_x, fs_x, w_f_b)
        ghi, glo = _hyena_filter(seq, *hy_params)
        kre, kim = _filter_spectrum(ghi, glo, hc_x, hs_x)
        z, x0c = _hyena_pre(ph, hcw, hcb)
        hy_x = _hyena(z, x0c, kre, kim, hbias, hc_x, hs_x)
        x_new = _mixout(attn_x, four_x, hy_x, w_out_b, x, ga1, g_post)

        if not last:
            attn_c = _attention(qc, [(kc4, vc4)])
            four_c = _fourier(fac, fc_c, fs_c, w_f_b)
            ghi_c, glo_c = _hyena_filter(clen, *hy_params)
            kre_c, kim_c = _filter_spectrum(ghi_c, glo_c, hc_c, hs_c)
            zc, x0cc = _hyena_pre(phc, hcw, hcb)
            hy_c = _hyena(zc, x0cc, kre_c, kim_c, hbias, hc_c, hs_c)
            ctx = _mixout(attn_c, four_c, hy_c, w_out_b, ctx, cga1, g_post)
        x = x_new

        g_pf = row1(g_pre_ffn[i])
        g_of = row1(g_post_ffn[i])
        fcw = w_ffn_conv[i]
        fcb = row1(b_ffn_conv[i])
        x = _ffn(x, sc2, sh2, g_pf, w_up_b, fcw, fcb, w_down_b, ga2, g_of)
        if not last:
            ctx = _ffn(ctx, csc2, csh2, g_pf, w_up_b, fcw, fcb, w_down_b, cga2, g_of)

    return x
```

```python
import functools
import math

import numpy as np
import jax
import jax.numpy as jnp
from jax import lax
from jax.experimental import pallas as pl
from jax.experimental.pallas import tpu as pltpu

F32 = jnp.float32
BF16 = jnp.bfloat16

HEAD_DIM = 64
GQA_GROUP = 4
N_KV_HEADS = 2
GRID_W = 64
ROPE_THETA = 10000.0
FOURIER_GROUP_DIM = 64
HYENA_BANDS = 16
HYENA_FAST_DECAY = 0.3
HYENA_SLOW_DECAY = 1.5
HYENA_TARGET = 1e-2
NORM_EPS = 1e-6
LANES = 128
VMEM_LIMIT = 56 * 1024 * 1024


def _cp(sem, vmem=VMEM_LIMIT):
    return pltpu.CompilerParams(dimension_semantics=sem, vmem_limit_bytes=vmem)


def _dot(a, b):
    return jnp.dot(a, b, preferred_element_type=F32)


def _dot_nt(a, b):
    return lax.dot_general(a, b, (((1,), (1,)), ((), ())), preferred_element_type=F32)


def _split(a):
    hi = a.astype(BF16)
    lo = (a - hi.astype(F32)).astype(BF16)
    return hi, lo


def _dot3(a, b):
    ah, al = _split(a)
    bh, bl = _split(b)
    return _dot(ah, bh) + _dot(ah, bl) + _dot(al, bh)


def _rms(x, g):
    ms = jnp.mean(x * x, axis=-1, keepdims=True)
    return x * lax.rsqrt(ms + NORM_EPS) * g


def _tables_kernel(hc_ref, hs_ref, fc_ref, fs_ref, hca, hsa, fca, fsa, *, n, tk):
    i = pl.program_id(0)
    hyena_step = 2.0 * math.pi / (8 * n)
    fourier_step = 2.0 * math.pi / n

    @pl.when(i == 0)
    def _():
        k = lax.broadcasted_iota(jnp.int32, (tk, n), 0)
        s = lax.broadcasted_iota(jnp.int32, (tk, n), 1)
        a = (((2 * k + 1) * (2 * s + 1)) & (8 * n - 1)).astype(F32) * hyena_step
        hca[...] = jnp.cos(a)
        hsa[...] = jnp.sin(a)
        b = ((k * s) & (n - 1)).astype(F32) * fourier_step
        fca[...] = jnp.cos(b)
        fsa[...] = jnp.sin(b)

    s1 = lax.broadcasted_iota(jnp.int32, (1, n), 1)
    rot = (((2 * tk * i) * (2 * s1 + 1)) & (8 * n - 1)).astype(F32) * hyena_step
    cb, sb = jnp.cos(rot), jnp.sin(rot)
    hc_ref[...] = (hca[...] * cb - hsa[...] * sb).astype(BF16)
    hs_ref[...] = (hsa[...] * cb + hca[...] * sb).astype(BF16)
    rot = (((tk * i) * s1) & (n - 1)).astype(F32) * fourier_step
    cb, sb = jnp.cos(rot), jnp.sin(rot)
    fc_ref[...] = (fca[...] * cb - fsa[...] * sb).astype(BF16)
    fs_ref[...] = (fsa[...] * cb + fca[...] * sb).astype(BF16)


def _make_tables(n):
    tk = min(n, 256)
    spec = pl.BlockSpec((tk, n), lambda i: (i, 0))
    shp = jax.ShapeDtypeStruct((n, n), BF16)
    return pl.pallas_call(
        functools.partial(_tables_kernel, n=n, tk=tk),
        out_shape=(shp, shp, shp, shp),
        grid=(n // tk,),
        out_specs=(spec, spec, spec, spec),
        scratch_shapes=[pltpu.VMEM((tk, n), F32)] * 4,
        compiler_params=_cp(("arbitrary",)),
        name=f"dft_tables_{n}",
    )()


def _mod_kernel(c_ref, w_ref, b_ref, o_ref):
    c = c_ref[...]
    a = c * jax.nn.sigmoid(c)
    o_ref[0] = _dot3(a, w_ref[0]) + b_ref[0]


def _modulation(cc, w_mod, b_mod):
    depth, d, n6 = w_mod.shape
    tn = 1536
    rows = cc.shape[0]
    return pl.pallas_call(
        _mod_kernel,
        out_shape=jax.ShapeDtypeStruct((depth, rows, n6), F32),
        grid=(depth, n6 // tn),
        in_specs=[
            pl.BlockSpec((rows, d), lambda l, j: (0, 0)),
            pl.BlockSpec((1, d, tn), lambda l, j: (l, 0, j)),
            pl.BlockSpec((1, 1, tn), lambda l, j: (l, 0, j)),
        ],
        out_specs=pl.BlockSpec((1, rows, tn), lambda l, j: (l, 0, j)),
        compiler_params=_cp(("parallel", "parallel")),
        name="modulation",
    )(cc, w_mod, b_mod.reshape(depth, 1, n6))


def _head_norm(p, gain, ones_bd):
    hi, lo = _split(p * p)
    ss = _dot(hi, ones_bd) + _dot(lo, ones_bd)
    return p * lax.rsqrt(ss * (1.0 / HEAD_DIM) + NORM_EPS) * gain


def _rope(xn, cos_t, sin_t):
    lane = lax.broadcasted_iota(jnp.int32, (1, LANES), 1)
    first = (lane % 32) < 16
    outs = []
    for j in range(xn.shape[1] // LANES):
        c = xn[:, j * LANES:(j + 1) * LANES]
        sw = jnp.where(first, pltpu.roll(c, LANES - 16, 1), pltpu.roll(c, 16, 1))
        outs.append(c * cos_t + sw * sin_t)
    return outs[0] if len(outs) == 1 else jnp.concatenate(outs, axis=1)


def _tile_heads(kv):
    lane = lax.broadcasted_iota(jnp.int32, (1, LANES), 1)
    low = lane < HEAD_DIM
    r = pltpu.roll(kv, HEAD_DIM, 1)
    h0 = jnp.where(low, kv, r)
    h1 = jnp.where(low, r, kv)
    return jnp.concatenate([h0, h0], axis=1), jnp.concatenate([h1, h1], axis=1)


def _value_heads(v):
    lane = lax.broadcasted_iota(jnp.int32, (1, LANES), 1)
    low = lane < HEAD_DIM
    r = pltpu.roll(v, HEAD_DIM, 1)
    return jnp.where(low, v, 1.0), jnp.where(low, r, 1.0)


def _proj_kernel(x_ref, sc_ref, sh_ref, g_ref, w_ref, gq_ref, gk_ref, cos_ref, sin_ref,
                 ones_ref, cs_ref, *outs, rope, kv_only):
    x = x_ref[0]
    h = _rms(x, g_ref[...]) * (1.0 + sc_ref[0]) + sh_ref[0]
    px = _dot(h.astype(BF16), w_ref[...])
    ones_bd = ones_ref[...]
    if kv_only:
        k4_ref, v4_ref = outs
        k = px[:, 0:128]
        v = px[:, 128:256]
    else:
        q_ref, k4_ref, v4_ref, fa_ref, ph_ref = outs
        aw = ones_bd.shape[0]
        q = _head_norm(px[:, 0:aw], gq_ref[...], ones_bd)
        if rope:
            q = _rope(q, cos_ref[...], sin_ref[...])
        q_ref[0] = (q * (HEAD_DIM ** -0.5 * math.log2(math.e))).astype(BF16)
        k = px[:, aw:aw + 128]
        v = px[:, aw + 128:aw + 256]
        f0 = aw + 256
        fw = cs_ref.shape[0]
        fa_ref[0] = _dot(px[:, f0:f0 + fw].astype(BF16), cs_ref[...]).astype(BF16)
        ph_ref[0] = px[:, f0 + fw:].astype(BF16)
    k = _head_norm(k, gk_ref[...], ones_bd[0:128, 0:128])
    if rope:
        k = _rope(k, cos_ref[...], sin_ref[...])
    k0, k1 = _tile_heads(k)
    k4_ref[0, 0] = k0.astype(BF16)
    k4_ref[0, 1] = k1.astype(BF16)
    v0, v1 = _value_heads(v)
    v4_ref[0, 0] = v0.T.astype(BF16)
    v4_ref[0, 1] = v1.T.astype(BF16)


def _project(x, sc, sh, g, w, gq, gk, cos_t, sin_t, ones_bd, cs, *, rope, kv_only):
    b, n, d = x.shape
    tm = min(n, 512)
    wn = w.shape[1]
    aw = ones_bd.shape[0]
    fw = cs.shape[0]
    vec = lambda width: pl.BlockSpec((1, 1, width), lambda bi, i: (bi, 0, 0))
    full = lambda a: pl.BlockSpec(a.shape, lambda bi, i: (0,) * a.ndim)
    kv_spec = pl.BlockSpec((1, N_KV_HEADS, tm, 256), lambda bi, i: (bi, 0, i, 0))
    kv_shape = jax.ShapeDtypeStruct((b, N_KV_HEADS, n, 256), BF16)
    vl_spec = pl.BlockSpec((1, N_KV_HEADS, LANES, tm), lambda bi, i: (bi, 0, 0, i))
    vl_shape = jax.ShapeDtypeStruct((b, N_KV_HEADS, LANES, n), BF16)
    row = lambda width: pl.BlockSpec((1, tm, width), lambda bi, i: (bi, i, 0))
    if kv_only:
        out_shape = (kv_shape, vl_shape)
        out_specs = (kv_spec, vl_spec)
    else:
        hw = wn - aw - 256 - fw
        out_shape = (jax.ShapeDtypeStruct((b, n, aw), BF16), kv_shape, vl_shape,
                     jax.ShapeDtypeStruct((b, n, 2 * fw), BF16),
                     jax.ShapeDtypeStruct((b, n, hw), BF16))
        out_specs = (row(aw), kv_spec, vl_spec, row(2 * fw), row(hw))
    return pl.pallas_call(
        functools.partial(_proj_kernel, rope=rope, kv_only=kv_only),
        out_shape=out_shape,
        grid=(b, n // tm),
        in_specs=[
            row(d), vec(d), vec(d), full(g), full(w), full(gq), full(gk),
            pl.BlockSpec((tm, LANES), lambda bi, i: (i, 0)),
            pl.BlockSpec((tm, LANES), lambda bi, i: (i, 0)),
            full(ones_bd), full(cs),
        ],
        out_specs=out_specs,
        compiler_params=_cp(("parallel", "parallel")),
        name="proj_kv" if kv_only else ("proj_rope" if rope else "proj_ctx"),
    )(x, sc, sh, g, w, gq, gk, cos_t, sin_t, ones_bd, cs)


ATTN_SUB = 128


ATTN_PROB_ROWS = 128


def _attn_kernel(q_ref, *refs, lks, nsub):
    nsrc = len(lks)
    k_refs = refs[0:2 * nsrc:2]
    v_refs = refs[1:2 * nsrc:2]
    o_ref = refs[2 * nsrc]
    s_scrs = refs[2 * nsrc + 1:2 * nsrc + 3]
    p_scrs = refs[2 * nsrc + 3:2 * nsrc + 5]
    sub = ATTN_SUB
    rows = GQA_GROUP * sub
    gw = GQA_GROUP * HEAD_DIM
    group = lax.broadcasted_iota(jnp.int32, (1, gw), 1) // HEAD_DIM

    chunks = []
    off = 0
    for j, lk in enumerate(lks):
        tkc = min(lk, 512)
        for c in range(lk // tkc):
            chunks.append((j, c * tkc, tkc, off))
            off += tkc

    lane = lax.broadcasted_iota(jnp.int32, (1, LANES), 1)
    low = lane < HEAD_DIM

    def masked_q(u):
        q = q_ref[0, u * sub:(u + 1) * sub, :]
        zero = jnp.zeros_like(q)
        return jnp.concatenate([jnp.where(group == g, q, zero) for g in range(GQA_GROUP)], axis=0)

    def finish(u, acc):
        o_t = acc / acc[HEAD_DIM:HEAD_DIM + 1, :]
        heads = [o_t[:, g * sub:(g + 1) * sub].T for g in range(GQA_GROUP)]
        left = jnp.where(low, heads[0], pltpu.roll(heads[1], HEAD_DIM, 1))
        right = jnp.where(low, heads[2], pltpu.roll(heads[3], HEAD_DIM, 1))
        o_ref[0, u * sub:(u + 1) * sub, :] = jnp.concatenate([left, right], axis=1).astype(BF16)

    def scores(u):
        s_scr = s_scrs[u % 2]
        qm = masked_q(u)
        for j, start, tkc, o in chunks:
            s_scr[o:o + tkc, :] = _dot_nt(k_refs[j][0, 0, start:start + tkc, :], qm)
        return jnp.max(s_scr[...], axis=0, keepdims=True)

    m = scores(0)
    for u in range(nsub):
        s_scr = s_scrs[u % 2]
        acc = jnp.zeros((LANES, rows), F32)
        if u + 1 < nsub:
            m_next = scores(u + 1)
            bits = pltpu.bitcast(m_next, jnp.uint32)
            acc = acc + ((bits >> 16) >> 16).astype(F32)
        for ci, (j, start, tkc, o) in enumerate(chunks):
            p_scr = p_scrs[ci % 2]
            for r in range(0, tkc, ATTN_PROB_ROWS):
                p_scr[r:r + ATTN_PROB_ROWS, :] = jnp.exp2(
                    s_scr[o + r:o + r + ATTN_PROB_ROWS, :] - m).astype(BF16)
            acc = acc + _dot(v_refs[j][0, 0, :, start:start + tkc], p_scr[0:tkc, :])
        finish(u, acc)
        if u + 1 < nsub:
            m = m_next


def _attention(q, kvs):
    b, lq, aw = q.shape
    tq = min(lq, 8 * ATTN_SUB)
    nsub = tq // ATTN_SUB
    lks = tuple(k.shape[2] for k, _ in kvs)
    gw = GQA_GROUP * HEAD_DIM
    in_specs = [pl.BlockSpec((1, tq, gw), lambda bi, h, i: (bi, i, h))]
    args = [q]
    for k4, vlt in kvs:
        for a in (k4, vlt):
            in_specs.append(pl.BlockSpec((1, 1) + a.shape[2:], lambda bi, h, i: (bi, h, 0, 0),
                                         pipeline_mode=pl.Buffered(1)))
        args += [k4, vlt]
    return pl.pallas_call(
        functools.partial(_attn_kernel, lks=lks, nsub=nsub),
        out_shape=jax.ShapeDtypeStruct((b, lq, aw), BF16),
        grid=(b, N_KV_HEADS, lq // tq),
        in_specs=in_specs,
        out_specs=pl.BlockSpec((1, tq, gw), lambda bi, h, i: (bi, i, h)),
        scratch_shapes=[pltpu.VMEM((sum(lks), GQA_GROUP * ATTN_SUB), F32)] * 2
        + [pltpu.VMEM((min(max(lks), 512), GQA_GROUP * ATTN_SUB), BF16)] * 2,
        compiler_params=_cp(("parallel", "parallel", "arbitrary")),
        name=f"attention_{lq}",
    )(*args)


def _fourier_kernel(c_ref, s_ref, fa_ref, w_ref, o_ref, *, nb, scale):
    fw = w_ref.shape[0]
    ct = c_ref[...]
    st = s_ref[...]
    for j in range(nb):
        y = _dot(ct, fa_ref[j, :, 0:fw]) - _dot(st, fa_ref[j, :, fw:2 * fw])
        o_ref[j] = _dot((y * scale).astype(BF16), w_ref[...]).astype(BF16)


def _fourier(fa, ctab, stab, w_f):
    b, n, fw2 = fa.shape
    fw = fw2 // 2
    nb = 2
    tk = min(n, 512)
    scale = 1.0 / math.sqrt(n * FOURIER_GROUP_DIM)
    return pl.pallas_call(
        functools.partial(_fourier_kernel, nb=nb, scale=scale),
        out_shape=jax.ShapeDtypeStruct((b, n, fw), BF16),
        grid=(b // nb, n // tk),
        in_specs=[
            pl.BlockSpec((tk, n), lambda g, i: (i, 0)),
            pl.BlockSpec((tk, n), lambda g, i: (i, 0)),
            pl.BlockSpec((nb, n, fw2), lambda g, i: (g, 0, 0)),
            pl.BlockSpec((fw, fw), lambda g, i: (0, 0)),
        ],
        out_specs=pl.BlockSpec((nb, tk, fw), lambda g, i: (g, i, 0)),
        compiler_params=_cp(("parallel", "arbitrary")),
        name=f"fourier_{n}",
    )(ctab, stab, fa, w_f)


def _conv3(p, w, bias):
    n = p.shape[0]
    row = lax.broadcasted_iota(jnp.int32, (n, 1), 0)
    prev = jnp.where(row == 0, 0.0, pltpu.roll(p, 1, 0))
    nxt = jnp.where(row == n - 1, 0.0, pltpu.roll(p, n - 1, 0))
    return prev * w[0:1] + p * w[1:2] + nxt * w[2:3] + bias


def _hyena_pre_kernel(p0_ref, p1_ref, p2_ref, w0_ref, w1_ref, w2_ref, b0_ref, b1_ref, b2_ref,
                      z_ref, x0_ref):
    x0 = _conv3(p0_ref[0].astype(F32), w0_ref[...], b0_ref[...])
    x1 = _conv3(p1_ref[0].astype(F32), w1_ref[...], b1_ref[...])
    v = _conv3(p2_ref[0].astype(F32), w2_ref[...], b2_ref[...])
    z_ref[0] = (x1 * v).astype(BF16)
    x0_ref[0] = x0.astype(BF16)


def _hyena_pre(ph, w, bias):
    b, n, hw3 = ph.shape
    hw = hw3 // 3
    nc = hw // LANES
    pspec = lambda s: pl.BlockSpec((1, n, LANES), lambda bi, c: (bi, 0, s * nc + c))
    wspec = lambda s: pl.BlockSpec((3, LANES), lambda bi, c: (0, s * nc + c))
    bspec = lambda s: pl.BlockSpec((1, LANES), lambda bi, c: (0, s * nc + c))
    ospec = pl.BlockSpec((1, n, LANES), lambda bi, c: (bi, 0, c))
    oshape = jax.ShapeDtypeStruct((b, n, hw), BF16)
    return pl.pallas_call(
        _hyena_pre_kernel,
        out_shape=(oshape, oshape),
        grid=(b, nc),
        in_specs=[pspec(0), pspec(1), pspec(2), wspec(0), wspec(1), wspec(2),
                  bspec(0), bspec(1), bspec(2)],
        out_specs=(ospec, ospec),
        compiler_params=_cp(("parallel", "parallel")),
        name=f"hyena_pre_{n}",
    )(ph, ph, ph, w, w, w, bias, bias, bias)


def _filter_kernel(w1t_ref, w1c_ref, w1s_ref, b1_ref, fr1_ref, w2_ref, b2_ref, fr2_ref, w3_ref,
                   ghi_ref, glo_ref, *, n):
    hw = w3_ref.shape[1] // 2
    i = lax.broadcasted_iota(jnp.int32, (n, 1), 0).astype(F32)
    t = i / float(n - 1)
    jb = lax.broadcasted_iota(jnp.int32, (1, HYENA_BANDS), 1).astype(F32)
    bands = 1e-4 + jb * ((HYENA_BANDS - 1 - 1e-4) / (HYENA_BANDS - 1))
    ang = ((2.0 * math.pi) * i / float(n)) * bands
    pre = t * w1t_ref[...] + _dot3(jnp.cos(ang), w1c_ref[...]) - _dot3(jnp.sin(ang), w1s_ref[...])
    h = jnp.sin(fr1_ref[...] * (pre + b1_ref[...]))
    h = jnp.sin(fr2_ref[...] * (_dot3(h, w2_ref[...]) + b2_ref[...]))
    h = _dot3(h, w3_ref[...])
    d0 = math.log(HYENA_TARGET) / HYENA_SLOW_DECAY
    d1 = math.log(HYENA_TARGET) / HYENA_FAST_DECAY
    jd = lax.broadcasted_iota(jnp.int32, (1, hw), 1).astype(F32)
    deltas = jnp.abs(d0 + jd * ((d1 - d0) / (hw - 1)))
    decay = jnp.exp(-t * deltas)
    hf = h[:, 0:hw] * decay
    hb = jnp.where(i == 0.0, 0.0, h[:, hw:2 * hw] * decay)
    total = jnp.sum(jnp.abs(hf), axis=0, keepdims=True) + jnp.sum(jnp.abs(hb), axis=0, keepdims=True)
    g = jnp.concatenate([hf / total, hb / total], axis=1)
    hi, lo = _split(g)
    ghi_ref[...] = hi
    glo_ref[...] = lo


def _hyena_filter(n, w1, b1, fr1, w2, b2, fr2, w3):
    nb = HYENA_BANDS
    r = lambda a: a.reshape(1, -1)
    shp = jax.ShapeDtypeStruct((n, w3.shape[1]), BF16)
    return pl.pallas_call(
        functools.partial(_filter_kernel, n=n),
        out_shape=(shp, shp),
        compiler_params=pltpu.CompilerParams(vmem_limit_bytes=VMEM_LIMIT),
        name=f"hyena_filter_{n}",
    )(w1[0:1], w1[1:1 + nb], w1[1 + nb:1 + 2 * nb], r(b1), r(fr1), w2, r(b2), r(fr2), w3)


def _spectrum_kernel(c_ref, s_ref, ghi_ref, glo_ref, ck_ref, sk_ref, kre_ref, kim_ref, *, scale):
    hw = kre_ref.shape[1]
    ct = c_ref[...]
    st = s_ref[...]
    a = _dot(ct, ghi_ref[...]) + _dot(ct, glo_ref[...])
    b = _dot(st, ghi_ref[...]) + _dot(st, glo_ref[...])
    are, bre = a[:, 0:hw], a[:, hw:2 * hw]
    aim, bim = -b[:, 0:hw], -b[:, hw:2 * hw]
    ck = ck_ref[...]
    sk = sk_ref[...]
    kre_ref[...] = (ck * (are + bre) - sk * (aim + bim)) * scale
    kim_ref[...] = (ck * (aim - bim) + sk * (are - bre)) * scale


def _filter_spectrum(ghi, glo, ctab, stab):
    n, hw2 = ghi.shape
    hw = hw2 // 2
    tk = min(n, 512)
    half = (2.0 * np.arange(n, dtype=np.float64) + 1.0) * (2.0 * np.pi / (8 * n))
    ck = jnp.asarray(np.cos(half).reshape(n, 1), F32)
    sk = jnp.asarray(np.sin(half).reshape(n, 1), F32)
    tspec = pl.BlockSpec((tk, n), lambda i: (i, 0))
    gspec = pl.BlockSpec((n, hw2), lambda i: (0, 0))
    vspec = pl.BlockSpec((tk, 1), lambda i: (i, 0))
    ospec = pl.BlockSpec((tk, hw), lambda i: (i, 0))
    oshape = jax.ShapeDtypeStruct((n, hw), F32)
    return pl.pallas_call(
        functools.partial(_spectrum_kernel, scale=1.0 / n),
        out_shape=(oshape, oshape),
        grid=(n // tk,),
        in_specs=[tspec, tspec, gspec, gspec, vspec, vspec],
        out_specs=(ospec, ospec),
        compiler_params=_cp(("parallel",)),
        name=f"filter_spectrum_{n}",
    )(ctab, stab, ghi, glo, ck, sk)


def _hyena_kernel(c_ref, s_ref, z_ref, zt_ref, x0_ref, kre_ref, kim_ref, bias_ref, o_ref,
                  yre_scr, yim_scr, *, nb, tk):
    phase = pl.program_id(1)
    i = pl.program_id(2)
    ct = c_ref[...]
    st = s_ref[...]

    @pl.when(phase == 0)
    def _():
        kre = kre_ref[...]
        kim = kim_ref[...]
        r0 = pl.multiple_of(i * tk, tk)
        for j in range(nb):
            ure = _dot(ct, z_ref[j])
            uim = -_dot(st, z_ref[j])
            yre_scr[j, pl.ds(r0, tk), :] = (kre * ure - kim * uim).astype(BF16)
            yim_scr[j, pl.ds(r0, tk), :] = (kre * uim + kim * ure).astype(BF16)

    @pl.when(phase == 1)
    def _():
        for j in range(nb):
            y = _dot(ct, yre_scr[j]) - _dot(st, yim_scr[j])
            u = zt_ref[j].astype(F32)
            o_ref[j] = (x0_ref[j].astype(F32) * (y + u * bias_ref[...])).astype(BF16)


def _hyena(z, x0c, kre, kim, bias, ctab, stab):
    b, n, hw = z.shape
    nb = 2
    tk = min(n, 512)
    tspec = pl.BlockSpec((tk, n), lambda g, p, i: (i, 0))
    tile = pl.BlockSpec((nb, tk, hw), lambda g, p, i: (g, i * p, 0))
    kspec = pl.BlockSpec((tk, hw), lambda g, p, i: (i * (1 - p), 0))
    return pl.pallas_call(
        functools.partial(_hyena_kernel, nb=nb, tk=tk),
        out_shape=jax.ShapeDtypeStruct((b, n, hw), BF16),
        grid=(b // nb, 2, n // tk),
        in_specs=[
            tspec, tspec,
            pl.BlockSpec((nb, n, hw), lambda g, p, i: (g, 0, 0)),
            tile, tile, kspec, kspec,
            pl.BlockSpec((1, hw), lambda g, p, i: (0, 0)),
        ],
        out_specs=tile,
        scratch_shapes=[pltpu.VMEM((nb, n, hw), BF16), pltpu.VMEM((nb, n, hw), BF16)],
        compiler_params=_cp(("parallel", "arbitrary", "arbitrary")),
        name=f"hyena_conv_{n}",
    )(ctab, stab, z, z, x0c, kre, kim, bias)


def _mixout_kernel(a_ref, f_ref, h_ref, w_ref, x_ref, ga_ref, g_ref, o_ref):
    aw = a_ref.shape[2]
    fw = f_ref.shape[2]
    mix = (_dot(a_ref[0], w_ref[0:aw, :]) + _dot(f_ref[0], w_ref[aw:aw + fw, :])
           + _dot(h_ref[0], w_ref[aw + fw:, :]))
    o_ref[0] = x_ref[0] + ga_ref[0] * _rms(mix, g_ref[...])


def _mixout(attn, four, hy, w_out, x, ga, g):
    b, n, d = x.shape
    tm = min(n, 512)
    row = lambda width: pl.BlockSpec((1, tm, width), lambda bi, i: (bi, i, 0))
    return pl.pallas_call(
        _mixout_kernel,
        out_shape=jax.ShapeDtypeStruct((b, n, d), F32),
        grid=(b, n // tm),
        in_specs=[
            row(attn.shape[2]), row(four.shape[2]), row(hy.shape[2]),
            pl.BlockSpec(w_out.shape, lambda bi, i: (0, 0)),
            row(d),
            pl.BlockSpec((1, 1, d), lambda bi, i: (bi, 0, 0)),
            pl.BlockSpec((1, d), lambda bi, i: (0, 0)),
        ],
        out_specs=row(d),
        compiler_params=_cp(("parallel", "parallel")),
        name=f"mixout_{n}",
    )(attn, four, hy, w_out, x, ga, g)


HALO = 8


FFN_CHUNK = 256


def _ffn_kernel(xp_ref, x_ref, xn_ref, sc_ref, sh_ref, g_ref, wu_ref, cw_ref, cb_ref, wd_ref,
                ga_ref, go_ref, o_ref, *, tm):
    i = pl.program_id(1)
    nrow = pl.num_programs(1)
    rows = tm + 2 * HALO
    dff = wd_ref.shape[0]
    cf = FFN_CHUNK
    g = g_ref[...]
    sc = 1.0 + sc_ref[0]
    sh = sh_ref[0]
    keep_p = jnp.where(i > 0, 1.0, 0.0)
    keep_n = jnp.where(i < nrow - 1, 1.0, 0.0)
    x = x_ref[0]
    fx = jnp.concatenate([(_rms(xp_ref[0], g) * sc + sh) * keep_p,
                          _rms(x, g) * sc + sh,
                          (_rms(xn_ref[0], g) * sc + sh) * keep_n], axis=0).astype(BF16)

    def conv(u, w, bias):
        prev = pltpu.roll(u, 1, 0)[HALO:HALO + tm]
        nxt = pltpu.roll(u, rows - 1, 0)[HALO:HALO + tm]
        return prev * w[0:1] + u[HALO:HALO + tm] * w[1:2] + nxt * w[2:3] + bias

    acts = []
    for c in range(dff // cf):
        lo, hi = c * cf, (c + 1) * cf
        gate = conv(_dot(fx, wu_ref[:, lo:hi]), cw_ref[:, lo:hi], cb_ref[:, lo:hi])
        val = conv(_dot(fx, wu_ref[:, dff + lo:dff + hi]), cw_ref[:, dff + lo:dff + hi],
                   cb_ref[:, dff + lo:dff + hi])
        acts.append((gate * jax.nn.sigmoid(gate) * val).astype(BF16))
    y = _dot(jnp.concatenate(acts, axis=1), wd_ref[...])
    o_ref[0] = x + ga_ref[0] * _rms(y, go_ref[...])


def _ffn(x, sc, sh, g, w_up, w_conv, b_conv, w_down, ga, g_post):
    b, n, d = x.shape
    tm = min(n, 512)
    hb = tm // HALO
    nhb = n // HALO
    vec = pl.BlockSpec((1, 1, d), lambda bi, i: (bi, 0, 0))
    gvec = pl.BlockSpec((1, d), lambda bi, i: (0, 0))
    resident = lambda a: pl.BlockSpec(a.shape, lambda bi, i: (0, 0), pipeline_mode=pl.Buffered(1))
    return pl.pallas_call(
        functools.partial(_ffn_kernel, tm=tm),
        out_shape=jax.ShapeDtypeStruct((b, n, d), F32),
        grid=(b, n // tm),
        in_specs=[
            pl.BlockSpec((1, HALO, d), lambda bi, i: (bi, jnp.maximum(i * hb - 1, 0), 0)),
            pl.BlockSpec((1, tm, d), lambda bi, i: (bi, i, 0)),
            pl.BlockSpec((1, HALO, d), lambda bi, i: (bi, jnp.minimum((i + 1) * hb, nhb - 1), 0)),
            vec, vec, gvec,
            resident(w_up), resident(w_conv), resident(b_conv), resident(w_down),
            vec, gvec,
        ],
        out_specs=pl.BlockSpec((1, tm, d), lambda bi, i: (bi, i, 0)),
        compiler_params=_cp(("parallel", "parallel")),
        name=f"conv_ffn_{n}",
    )(x, x, x, sc, sh, g, w_up, w_conv, b_conv, w_down, ga, g_post)


def _rope_tables(n):
    half = HEAD_DIM // 4
    inv = ROPE_THETA ** (-jnp.arange(0, 2 * half, 2, dtype=F32) / (2 * half))
    pos = jnp.arange(n, dtype=jnp.int32)
    row = (pos // GRID_W).astype(F32)
    col = (pos % GRID_W).astype(F32)
    ang_r = row[:, None] * inv[None, :]
    ang_c = col[:, None] * inv[None, :]
    cr, sr, cc, sc = jnp.cos(ang_r), jnp.sin(ang_r), jnp.cos(ang_c), jnp.sin(ang_c)
    cos64 = jnp.concatenate([cr, cr, cc, cc], axis=-1)
    sin64 = jnp.concatenate([-sr, sr, -sc, sc], axis=-1)
    return jnp.tile(cos64, (1, 2)), jnp.tile(sin64, (1, 2))


def _head_ones(width):
    idx = np.arange(width) // HEAD_DIM
    return jnp.asarray((idx[:, None] == idx[None, :]).astype(np.float32), BF16)


def _channel_dft(width):
    c = np.arange(width)
    same = (c[:, None] // FOURIER_GROUP_DIM) == (c[None, :] // FOURIER_GROUP_DIM)
    ang = 2.0 * np.pi * ((c[:, None] % FOURIER_GROUP_DIM) * (c[None, :] % FOURIER_GROUP_DIM)
                         % FOURIER_GROUP_DIM) / FOURIER_GROUP_DIM
    cs = np.concatenate([np.where(same, np.cos(ang), 0.0), np.where(same, np.sin(ang), 0.0)], axis=1)
    return jnp.asarray(cs.astype(np.float32), BF16)


def kernel(x, c, ctx, c_ctx, w_mod, b_mod, g_pre_mix, g_post_mix, g_pre_ffn, g_post_ffn, w_in, g_q, g_k, w_fourier, w_hy_conv, b_hy_conv, hy_w1, hy_b1, hy_fr1, hy_w2, hy_b2, hy_fr2, hy_w3, hy_bias, w_out, w_up, w_ffn_conv, b_ffn_conv, w_down):
    bsz, seq, d = x.shape
    clen = ctx.shape[1]
    depth = w_mod.shape[0]
    fw = w_fourier.shape[1]
    hw = hy_bias.shape[1]
    aw = w_in.shape[2] - 2 * N_KV_HEADS * HEAD_DIM - fw - 3 * hw
    k0 = aw
    f0 = aw + 2 * N_KV_HEADS * HEAD_DIM

    nrows = -(-(bsz + 1) // 8) * 8
    cc = jnp.zeros((nrows, d), F32).at[:bsz].set(c).at[bsz].set(c_ctx)
    mods = _modulation(cc, w_mod, b_mod)

    cos_x, sin_x = _rope_tables(seq)
    cos_c = jnp.ones((clen, LANES), F32)
    sin_c = jnp.zeros((clen, LANES), F32)
    ones_bd = _head_ones(aw)
    cs = _channel_dft(fw)
    hc_x, hs_x, fc_x, fs_x = _make_tables(seq)
    hc_c, hs_c, fc_c, fs_c = _make_tables(clen)

    def row1(v):
        return v.reshape(1, -1)

    for i in range(depth):
        last = i == depth - 1
        mx = mods[i, :bsz].reshape(bsz, 1, 6, d)
        mc = jnp.broadcast_to(mods[i, bsz].reshape(1, 1, 6, d), (bsz, 1, 6, d))
        sh1, sc1, ga1, sh2, sc2, ga2 = (mx[:, :, t] for t in range(6))
        csh1, csc1, cga1, csh2, csc2, cga2 = (mc[:, :, t] for t in range(6))
        w_in_b = w_in[i].astype(BF16)
        w_out_b = w_out[i].astype(BF16)
        w_up_b = w_up[i].astype(BF16)
        w_down_b = w_down[i].astype(BF16)
        w_f_b = w_fourier[i].astype(BF16)
        gq_t = jnp.tile(g_q[i], aw // HEAD_DIM).reshape(1, aw)
        gk_t = jnp.tile(g_k[i], N_KV_HEADS).reshape(1, N_KV_HEADS * HEAD_DIM)
        g_pre = row1(g_pre_mix[i])
        g_post = row1(g_post_mix[i])
        hy_params = (hy_w1[i], hy_b1[i], hy_fr1[i], hy_w2[i], hy_b2[i], hy_fr2[i], hy_w3[i])
        hbias = row1(hy_bias[i])
        hcw = w_hy_conv[i]
        hcb = row1(b_hy_conv[i])

        q, k4, v4, fa, ph = _project(x, sc1, sh1, g_pre, w_in_b, gq_t, gk_t, cos_x, sin_x,
                                     ones_bd, cs, rope=True, kv_only=False)
        if last:
            kc4, vc4 = _project(ctx, csc1, csh1, g_pre, w_in_b[:, k0:f0], gq_t, gk_t, cos_c, sin_c,
                                ones_bd, cs, rope=False, kv_only=True)
        else:
            qc, kc4, vc4, fac, phc = _project(ctx, csc1, csh1, g_pre, w_in_b, gq_t, gk_t, cos_c,
                                              sin_c, ones_bd, cs, rope=False, kv_only=False)
        attn_x = _attention(q, [(k4, v4), (kc4, vc4)])
        four_x = _fourier(fa, fc_x, fs_x, w_f_b)
        ghi, glo = _hyena_filter(seq, *hy_params)
        kre, kim = _filter_spectrum(ghi, glo, hc_x, hs_x)
        z, x0c = _hyena_pre(ph, hcw, hcb)
        hy_x = _hyena(z, x0c, kre, kim, hbias, hc_x, hs_x)
        x_new = _mixout(attn_x, four_x, hy_x, w_out_b, x, ga1, g_post)

        if not last:
            attn_c = _attention(qc, [(kc4, vc4)])
            four_c = _fourier(fac, fc_c, fs_c, w_f_b)
            ghi_c, glo_c = _hyena_filter(clen, *hy_params)
            kre_c, kim_c = _filter_spectrum(ghi_c, glo_c, hc_c, hs_c)
            zc, x0cc = _hyena_pre(phc, hcw, hcb)
            hy_c = _hyena(zc, x0cc, kre_c, kim_c, hbias, hc_c, hs_c)
            ctx = _mixout(attn_c, four_c, hy_c, w_out_b, ctx, cga1, g_post)
        x = x_new

        g_pf = row1(g_pre_ffn[i])
        g_of = row1(g_post_ffn[i])
        fcw = w_ffn_conv[i]
        fcb = row1(b_ffn_conv[i])
        x = _ffn(x, sc2, sh2, g_pf, w_up_b, fcw, fcb, w_down_b, ga2, g_of)
        if not last:
            ctx = _ffn(ctx, csc2, csh2, g_pf, w_up_b, fcw, fcb, w_down_b, cga2, g_of)

    return x
```

```python
import functools
import math

import numpy as np
import jax
import jax.numpy as jnp
from jax import lax
from jax.experimental import pallas as pl
from jax.experimental.pallas import tpu as pltpu

F32 = jnp.float32
BF16 = jnp.bfloat16

HEAD_DIM = 64
GQA_GROUP = 4
N_KV_HEADS = 2
GRID_W = 64
ROPE_THETA = 10000.0
FOURIER_GROUP_DIM = 64
HYENA_BANDS = 16
HYENA_FAST_DECAY = 0.3
HYENA_SLOW_DECAY = 1.5
HYENA_TARGET = 1e-2
NORM_EPS = 1e-6
LANES = 128
VMEM_LIMIT = 56 * 1024 * 1024


def _cp(sem, vmem=VMEM_LIMIT):
    return pltpu.CompilerParams(dimension_semantics=sem, vmem_limit_bytes=vmem)


def _dot(a, b):
    return jnp.dot(a, b, preferred_element_type=F32)


def _dot_nt(a, b):
    return lax.dot_general(a, b, (((1,), (1,)), ((), ())), preferred_element_type=F32)


def _split(a):
    hi = a.astype(BF16)
    lo = (a - hi.astype(F32)).astype(BF16)
    return hi, lo


def _dot3(a, b):
    ah, al = _split(a)
    bh, bl = _split(b)
    return _dot(ah, bh) + _dot(ah, bl) + _dot(al, bh)


def _rms(x, g):
    ms = jnp.mean(x * x, axis=-1, keepdims=True)
    return x * lax.rsqrt(ms + NORM_EPS) * g


def _tables_kernel(hc_ref, hs_ref, fc_ref, fs_ref, hca, hsa, fca, fsa, *, n, tk):
    i = pl.program_id(0)
    hyena_step = 2.0 * math.pi / (8 * n)
    fourier_step = 2.0 * math.pi / n

    @pl.when(i == 0)
    def _():
        k = lax.broadcasted_iota(jnp.int32, (tk, n), 0)
        s = lax.broadcasted_iota(jnp.int32, (tk, n), 1)
        a = (((2 * k + 1) * (2 * s + 1)) & (8 * n - 1)).astype(F32) * hyena_step
        hca[...] = jnp.cos(a)
        hsa[...] = jnp.sin(a)
        b = ((k * s) & (n - 1)).astype(F32) * fourier_step
        fca[...] = jnp.cos(b)
        fsa[...] = jnp.sin(b)

    s1 = lax.broadcasted_iota(jnp.int32, (1, n), 1)
    rot = (((2 * tk * i) * (2 * s1 + 1)) & (8 * n - 1)).astype(F32) * hyena_step
    cb, sb = jnp.cos(rot), jnp.sin(rot)
    hc_ref[...] = (hca[...] * cb - hsa[...] * sb).astype(BF16)
    hs_ref[...] = (hsa[...] * cb + hca[...] * sb).astype(BF16)
    rot = (((tk * i) * s1) & (n - 1)).astype(F32) * fourier_step
    cb, sb = jnp.cos(rot), jnp.sin(rot)
    fc_ref[...] = (fca[...] * cb - fsa[...] * sb).astype(BF16)
    fs_ref[...] = (fsa[...] * cb + fca[...] * sb).astype(BF16)


def _make_tables(n):
    tk = min(n, 256)
    spec = pl.BlockSpec((tk, n), lambda i: (i, 0))
    shp = jax.ShapeDtypeStruct((n, n), BF16)
    return pl.pallas_call(
        functools.partial(_tables_kernel, n=n, tk=tk),
        out_shape=(shp, shp, shp, shp),
        grid=(n // tk,),
        out_specs=(spec, spec, spec, spec),
        scratch_shapes=[pltpu.VMEM((tk, n), F32)] * 4,
        compiler_params=_cp(("arbitrary",)),
        name=f"dft_tables_{n}",
    )()


def _mod_kernel(c_ref, w_ref, b_ref, o_ref):
    c = c_ref[...]
    a = c * jax.nn.sigmoid(c)
    o_ref[0] = _dot3(a, w_ref[0]) + b_ref[0]


def _modulation(cc, w_mod, b_mod):
    depth, d, n6 = w_mod.shape
    tn = 1536
    rows = cc.shape[0]
    return pl.pallas_call(
        _mod_kernel,
        out_shape=jax.ShapeDtypeStruct((depth, rows, n6), F32),
        grid=(depth, n6 // tn),
        in_specs=[
            pl.BlockSpec((rows, d), lambda l, j: (0, 0)),
            pl.BlockSpec((1, d, tn), lambda l, j: (l, 0, j)),
            pl.BlockSpec((1, 1, tn), lambda l, j: (l, 0, j)),
        ],
        out_specs=pl.BlockSpec((1, rows, tn), lambda l, j: (l, 0, j)),
        compiler_params=_cp(("parallel", "parallel")),
        name="modulation",
    )(cc, w_mod, b_mod.reshape(depth, 1, n6))


def _head_norm(p, gain, ones_bd):
    hi, lo = _split(p * p)
    ss = _dot(hi, ones_bd) + _dot(lo, ones_bd)
    return p * lax.rsqrt(ss * (1.0 / HEAD_DIM) + NORM_EPS) * gain


def _rope(xn, cos_t, sin_t):
    lane = lax.broadcasted_iota(jnp.int32, (1, LANES), 1)
    first = (lane % 32) < 16
    outs = []
    for j in range(xn.shape[1] // LANES):
        c = xn[:, j * LANES:(j + 1) * LANES]
        sw = jnp.where(first, pltpu.roll(c, LANES - 16, 1), pltpu.roll(c, 16, 1))
        outs.append(c * cos_t + sw * sin_t)
    return outs[0] if len(outs) == 1 else jnp.concatenate(outs, axis=1)


def _tile_heads(kv):
    lane = lax.broadcasted_iota(jnp.int32, (1, LANES), 1)
    low = lane < HEAD_DIM
    r = pltpu.roll(kv, HEAD_DIM, 1)
    h0 = jnp.where(low, kv, r)
    h1 = jnp.where(low, r, kv)
    return jnp.concatenate([h0, h0], axis=1), jnp.concatenate([h1, h1], axis=1)


VALUE_ROWS = HEAD_DIM + 16


def _value_heads(v):
    lane = lax.broadcasted_iota(jnp.int32, (1, LANES), 1)
    low = lane < HEAD_DIM
    r = pltpu.roll(v, HEAD_DIM, 1)
    return jnp.where(low, v, 1.0), jnp.where(low, r, 1.0)


def _proj_kernel(x_ref, sc_ref, sh_ref, g_ref, w_ref, gq_ref, gk_ref, cos_ref, sin_ref,
                 ones_ref, cs_ref, *outs, rope, kv_only):
    x = x_ref[0]
    h = _rms(x, g_ref[...]) * (1.0 + sc_ref[0]) + sh_ref[0]
    px = _dot(h.astype(BF16), w_ref[...])
    ones_bd = ones_ref[...]
    if kv_only:
        k4_ref, v4_ref = outs
        k = px[:, 0:128]
        v = px[:, 128:256]
    else:
        q_ref, k4_ref, v4_ref, fa_ref, ph_ref = outs
        aw = ones_bd.shape[0]
        q = _head_norm(px[:, 0:aw], gq_ref[...], ones_bd)
        if rope:
            q = _rope(q, cos_ref[...], sin_ref[...])
        q_ref[0] = (q * (HEAD_DIM ** -0.5 * math.log2(math.e))).astype(BF16)
        k = px[:, aw:aw + 128]
        v = px[:, aw + 128:aw + 256]
        f0 = aw + 256
        fw = cs_ref.shape[0]
        fa_ref[0] = _dot(px[:, f0:f0 + fw].astype(BF16), cs_ref[...]).astype(BF16)
        ph_ref[0] = px[:, f0 + fw:].astype(BF16)
    k = _head_norm(k, gk_ref[...], ones_bd[0:128, 0:128])
    if rope:
        k = _rope(k, cos_ref[...], sin_ref[...])
    k0, k1 = _tile_heads(k)
    k4_ref[0, 0] = k0.astype(BF16)
    k4_ref[0, 1] = k1.astype(BF16)
    v0, v1 = _value_heads(v)
    v4_ref[0, 0] = v0.T[0:VALUE_ROWS].astype(BF16)
    v4_ref[0, 1] = v1.T[0:VALUE_ROWS].astype(BF16)


def _project(x, sc, sh, g, w, gq, gk, cos_t, sin_t, ones_bd, cs, *, rope, kv_only):
    b, n, d = x.shape
    tm = min(n, 512)
    wn = w.shape[1]
    aw = ones_bd.shape[0]
    fw = cs.shape[0]
    vec = lambda width: pl.BlockSpec((1, 1, width), lambda bi, i: (bi, 0, 0))
    full = lambda a: pl.BlockSpec(a.shape, lambda bi, i: (0,) * a.ndim)
    kv_spec = pl.BlockSpec((1, N_KV_HEADS, tm, 256), lambda bi, i: (bi, 0, i, 0))
    kv_shape = jax.ShapeDtypeStruct((b, N_KV_HEADS, n, 256), BF16)
    vl_spec = pl.BlockSpec((1, N_KV_HEADS, VALUE_ROWS, tm), lambda bi, i: (bi, 0, 0, i))
    vl_shape = jax.ShapeDtypeStruct((b, N_KV_HEADS, VALUE_ROWS, n), BF16)
    row = lambda width: pl.BlockSpec((1, tm, width), lambda bi, i: (bi, i, 0))
    if kv_only:
        out_shape = (kv_shape, vl_shape)
        out_specs = (kv_spec, vl_spec)
    else:
        hw = wn - aw - 256 - fw
        out_shape = (jax.ShapeDtypeStruct((b, n, aw), BF16), kv_shape, vl_shape,
                     jax.ShapeDtypeStruct((b, n, 2 * fw), BF16),
                     jax.ShapeDtypeStruct((b, n, hw), BF16))
        out_specs = (row(aw), kv_spec, vl_spec, row(2 * fw), row(hw))
    return pl.pallas_call(
        functools.partial(_proj_kernel, rope=rope, kv_only=kv_only),
        out_shape=out_shape,
        grid=(b, n // tm),
        in_specs=[
            row(d), vec(d), vec(d), full(g), full(w), full(gq), full(gk),
            pl.BlockSpec((tm, LANES), lambda bi, i: (i, 0)),
            pl.BlockSpec((tm, LANES), lambda bi, i: (i, 0)),
            full(ones_bd), full(cs),
        ],
        out_specs=out_specs,
        compiler_params=_cp(("parallel", "parallel")),
        name="proj_kv" if kv_only else ("proj_rope" if rope else "proj_ctx"),
    )(x, sc, sh, g, w, gq, gk, cos_t, sin_t, ones_bd, cs)


ATTN_SUB = 128


ATTN_PROB_ROWS = 128


def _attn_kernel(q_ref, *refs, lks, nsub):
    nsrc = len(lks)
    k_refs = refs[0:2 * nsrc:2]
    v_refs = refs[1:2 * nsrc:2]
    o_ref = refs[2 * nsrc]
    s_scrs = refs[2 * nsrc + 1:2 * nsrc + 3]
    p_scrs = refs[2 * nsrc + 3:2 * nsrc + 5]
    sub = ATTN_SUB
    rows = GQA_GROUP * sub
    gw = GQA_GROUP * HEAD_DIM
    group = lax.broadcasted_iota(jnp.int32, (1, gw), 1) // HEAD_DIM

    chunks = []
    off = 0
    for j, lk in enumerate(lks):
        tkc = min(lk, 512)
        for c in range(lk // tkc):
            chunks.append((j, c * tkc, tkc, off))
            off += tkc

    lane = lax.broadcasted_iota(jnp.int32, (1, LANES), 1)
    low = lane < HEAD_DIM

    def masked_q(u):
        q = q_ref[0, u * sub:(u + 1) * sub, :]
        zero = jnp.zeros_like(q)
        return jnp.concatenate([jnp.where(group == g, q, zero) for g in range(GQA_GROUP)], axis=0)

    def finish(u, acc):
        o_t = acc / acc[HEAD_DIM:HEAD_DIM + 1, :]
        o_t = jnp.concatenate([o_t, jnp.zeros((LANES - VALUE_ROWS, rows), F32)], axis=0)
        heads = [o_t[:, g * sub:(g + 1) * sub].T for g in range(GQA_GROUP)]
        left = jnp.where(low, heads[0], pltpu.roll(heads[1], HEAD_DIM, 1))
        right = jnp.where(low, heads[2], pltpu.roll(heads[3], HEAD_DIM, 1))
        o_ref[0, u * sub:(u + 1) * sub, :] = jnp.concatenate([left, right], axis=1).astype(BF16)

    def scores(u):
        s_scr = s_scrs[u % 2]
        qm = masked_q(u)
        for j, start, tkc, o in chunks:
            s_scr[o:o + tkc, :] = _dot_nt(k_refs[j][0, 0, start:start + tkc, :], qm)
        return jnp.max(s_scr[...], axis=0, keepdims=True)

    m = scores(0)
    for u in range(nsub):
        s_scr = s_scrs[u % 2]
        acc = jnp.zeros((VALUE_ROWS, rows), F32)
        if u + 1 < nsub:
            m_next = scores(u + 1)
            bits = pltpu.bitcast(m_next, jnp.uint32)
            acc = acc + ((bits >> 16) >> 16).astype(F32)
        for ci, (j, start, tkc, o) in enumerate(chunks):
            p_scr = p_scrs[ci % 2]
            for r in range(0, tkc, ATTN_PROB_ROWS):
                p_scr[r:r + ATTN_PROB_ROWS, :] = jnp.exp2(
                    s_scr[o + r:o + r + ATTN_PROB_ROWS, :] - m).astype(BF16)
            acc = acc + _dot(v_refs[j][0, 0, :, start:start + tkc], p_scr[0:tkc, :])
        finish(u, acc)
        if u + 1 < nsub:
            m = m_next


def _attention(q, kvs):
    b, lq, aw = q.shape
    tq = min(lq, 8 * ATTN_SUB)
    nsub = tq // ATTN_SUB
    lks = tuple(k.shape[2] for k, _ in kvs)
    gw = GQA_GROUP * HEAD_DIM
    in_specs = [pl.BlockSpec((1, tq, gw), lambda bi, h, i: (bi, i, h))]
    args = [q]
    for k4, vlt in kvs:
        for a in (k4, vlt):
            in_specs.append(pl.BlockSpec((1, 1) + a.shape[2:], lambda bi, h, i: (bi, h, 0, 0),
                                         pipeline_mode=pl.Buffered(1)))
        args += [k4, vlt]
    return pl.pallas_call(
        functools.partial(_attn_kernel, lks=lks, nsub=nsub),
        out_shape=jax.ShapeDtypeStruct((b, lq, aw), BF16),
        grid=(b, N_KV_HEADS, lq // tq),
        in_specs=in_specs,
        out_specs=pl.BlockSpec((1, tq, gw), lambda bi, h, i: (bi, i, h)),
        scratch_shapes=[pltpu.VMEM((sum(lks), GQA_GROUP * ATTN_SUB), F32)] * 2
        + [pltpu.VMEM((min(max(lks), 512), GQA_GROUP * ATTN_SUB), BF16)] * 2,
        compiler_params=_cp(("parallel", "parallel", "arbitrary")),
        name=f"attention_{lq}",
    )(*args)


def _fourier_kernel(c_ref, s_ref, fa_ref, w_ref, o_ref, fold_scr, nyq_scr, *, nb, scale, tk):
    fw = w_ref.shape[0]
    n = fa_ref.shape[1]
    half = n // 2
    blk = min(half, 512)

    @pl.when(pl.program_id(1) == 0)
    def _():
        r = lax.broadcasted_iota(jnp.int32, (blk, blk), 0)
        c = lax.broadcasted_iota(jnp.int32, (blk, blk), 1)
        flip = jnp.where(r + c == blk - 1, 1.0, 0.0).astype(BF16)
        row = lax.broadcasted_iota(jnp.int32, (half, 1), 0)
        sign = jnp.where(lax.broadcasted_iota(jnp.int32, (1, 2 * fw), 1) < fw, 1.0, -1.0)
        for j in range(nb):
            rev = jnp.concatenate(
                [_dot(flip, fa_ref[j, n - (b + 1) * blk:n - b * blk, :]) for b in range(half // blk)],
                axis=0)
            mirrored = jnp.where(row == 0, 0.0, pltpu.roll(rev, 1, 0))
            fold_scr[j] = (fa_ref[j, 0:half, :].astype(F32) + sign * mirrored).astype(BF16)
            nyq_scr[j] = jnp.broadcast_to(fa_ref[j, half:half + 16, 0:fw][0:1].astype(F32), (8, fw))

    ct = c_ref[...]
    st = s_ref[...]
    odd = (lax.broadcasted_iota(jnp.int32, (tk, 1), 0) & 1) == 1
    alt = jnp.where(odd, -1.0, 1.0)
    for j in range(nb):
        y = (_dot(ct, fold_scr[j, :, 0:fw]) - _dot(st, fold_scr[j, :, fw:2 * fw])
             + alt * nyq_scr[j, 0:1, :])
        o_ref[j] = _dot((y * scale).astype(BF16), w_ref[...]).astype(BF16)


def _fourier(fa, ctab, stab, w_f):
    b, n, fw2 = fa.shape
    fw = fw2 // 2
    nb = 2
    tk = min(n, 512)
    half = n // 2
    scale = 1.0 / math.sqrt(n * FOURIER_GROUP_DIM)
    return pl.pallas_call(
        functools.partial(_fourier_kernel, nb=nb, scale=scale, tk=tk),
        out_shape=jax.ShapeDtypeStruct((b, n, fw), BF16),
        grid=(b // nb, n // tk),
        in_specs=[
            pl.BlockSpec((tk, half), lambda g, i: (i, 0)),
            pl.BlockSpec((tk, half), lambda g, i: (i, 0)),
            pl.BlockSpec((nb, n, fw2), lambda g, i: (g, 0, 0)),
            pl.BlockSpec((fw, fw), lambda g, i: (0, 0)),
        ],
        out_specs=pl.BlockSpec((nb, tk, fw), lambda g, i: (g, i, 0)),
        scratch_shapes=[pltpu.VMEM((nb, half, fw2), BF16), pltpu.VMEM((nb, 8, fw), F32)],
        compiler_params=_cp(("parallel", "arbitrary")),
        name=f"fourier_{n}",
    )(ctab, stab, fa, w_f)


def _conv3(p, w, bias):
    n = p.shape[0]
    row = lax.broadcasted_iota(jnp.int32, (n, 1), 0)
    prev = jnp.where(row == 0, 0.0, pltpu.roll(p, 1, 0))
    nxt = jnp.where(row == n - 1, 0.0, pltpu.roll(p, n - 1, 0))
    return prev * w[0:1] + p * w[1:2] + nxt * w[2:3] + bias


def _hyena_pre_kernel(p0_ref, p1_ref, p2_ref, w0_ref, w1_ref, w2_ref, b0_ref, b1_ref, b2_ref,
                      z_ref, x0_ref):
    x0 = _conv3(p0_ref[0].astype(F32), w0_ref[...], b0_ref[...])
    x1 = _conv3(p1_ref[0].astype(F32), w1_ref[...], b1_ref[...])
    v = _conv3(p2_ref[0].astype(F32), w2_ref[...], b2_ref[...])
    z_ref[0] = (x1 * v).astype(BF16)
    x0_ref[0] = x0.astype(BF16)


def _hyena_pre(ph, w, bias):
    b, n, hw3 = ph.shape
    hw = hw3 // 3
    nc = hw // LANES
    pspec = lambda s: pl.BlockSpec((1, n, LANES), lambda bi, c: (bi, 0, s * nc + c))
    wspec = lambda s: pl.BlockSpec((3, LANES), lambda bi, c: (0, s * nc + c))
    bspec = lambda s: pl.BlockSpec((1, LANES), lambda bi, c: (0, s * nc + c))
    ospec = pl.BlockSpec((1, n, LANES), lambda bi, c: (bi, 0, c))
    oshape = jax.ShapeDtypeStruct((b, n, hw), BF16)
    return pl.pallas_call(
        _hyena_pre_kernel,
        out_shape=(oshape, oshape),
        grid=(b, nc),
        in_specs=[pspec(0), pspec(1), pspec(2), wspec(0), wspec(1), wspec(2),
                  bspec(0), bspec(1), bspec(2)],
        out_specs=(ospec, ospec),
        compiler_params=_cp(("parallel", "parallel")),
        name=f"hyena_pre_{n}",
    )(ph, ph, ph, w, w, w, bias, bias, bias)


def _filter_kernel(w1t_ref, w1c_ref, w1s_ref, b1_ref, fr1_ref, w2_ref, b2_ref, fr2_ref, w3_ref,
                   ghi_ref, glo_ref, *, n):
    hw = w3_ref.shape[1] // 2
    i = lax.broadcasted_iota(jnp.int32, (n, 1), 0).astype(F32)
    t = i / float(n - 1)
    jb = lax.broadcasted_iota(jnp.int32, (1, HYENA_BANDS), 1).astype(F32)
    bands = 1e-4 + jb * ((HYENA_BANDS - 1 - 1e-4) / (HYENA_BANDS - 1))
    ang = ((2.0 * math.pi) * i / float(n)) * bands
    pre = t * w1t_ref[...] + _dot3(jnp.cos(ang), w1c_ref[...]) - _dot3(jnp.sin(ang), w1s_ref[...])
    h = jnp.sin(fr1_ref[...] * (pre + b1_ref[...]))
    h = jnp.sin(fr2_ref[...] * (_dot3(h, w2_ref[...]) + b2_ref[...]))
    h = _dot3(h, w3_ref[...])
    d0 = math.log(HYENA_TARGET) / HYENA_SLOW_DECAY
    d1 = math.log(HYENA_TARGET) / HYENA_FAST_DECAY
    jd = lax.broadcasted_iota(jnp.int32, (1, hw), 1).astype(F32)
    deltas = jnp.abs(d0 + jd * ((d1 - d0) / (hw - 1)))
    decay = jnp.exp(-t * deltas)
    hf = h[:, 0:hw] * decay
    hb = jnp.where(i == 0.0, 0.0, h[:, hw:2 * hw] * decay)
    total = jnp.sum(jnp.abs(hf), axis=0, keepdims=True) + jnp.sum(jnp.abs(hb), axis=0, keepdims=True)
    g = jnp.concatenate([hf / total, hb / total], axis=1)
    hi, lo = _split(g)
    ghi_ref[...] = hi
    glo_ref[...] = lo


def _hyena_filter(n, w1, b1, fr1, w2, b2, fr2, w3):
    nb = HYENA_BANDS
    r = lambda a: a.reshape(1, -1)
    shp = jax.ShapeDtypeStruct((n, w3.shape[1]), BF16)
    return pl.pallas_call(
        functools.partial(_filter_kernel, n=n),
        out_shape=(shp, shp),
        compiler_params=pltpu.CompilerParams(vmem_limit_bytes=VMEM_LIMIT),
        name=f"hyena_filter_{n}",
    )(w1[0:1], w1[1:1 + nb], w1[1 + nb:1 + 2 * nb], r(b1), r(fr1), w2, r(b2), r(fr2), w3)


def _spectrum_kernel(c_ref, s_ref, ghi_ref, glo_ref, ck_ref, sk_ref, kre_ref, kim_ref, *, scale):
    hw = kre_ref.shape[1]
    ct = c_ref[...]
    st = s_ref[...]
    a = _dot(ct, ghi_ref[...]) + _dot(ct, glo_ref[...])
    b = _dot(st, ghi_ref[...]) + _dot(st, glo_ref[...])
    are, bre = a[:, 0:hw], a[:, hw:2 * hw]
    aim, bim = -b[:, 0:hw], -b[:, hw:2 * hw]
    ck = ck_ref[...]
    sk = sk_ref[...]
    kre_ref[...] = (ck * (are + bre) - sk * (aim + bim)) * scale
    kim_ref[...] = (ck * (aim - bim) + sk * (are - bre)) * scale


def _filter_spectrum(ghi, glo, ctab, stab):
    n, hw2 = ghi.shape
    hw = hw2 // 2
    tk = min(n, 512)
    half = (2.0 * np.arange(n, dtype=np.float64) + 1.0) * (2.0 * np.pi / (8 * n))
    ck = jnp.asarray(np.cos(half).reshape(n, 1), F32)
    sk = jnp.asarray(np.sin(half).reshape(n, 1), F32)
    tspec = pl.BlockSpec((tk, n), lambda i: (i, 0))
    gspec = pl.BlockSpec((n, hw2), lambda i: (0, 0))
    vspec = pl.BlockSpec((tk, 1), lambda i: (i, 0))
    ospec = pl.BlockSpec((tk, hw), lambda i: (i, 0))
    oshape = jax.ShapeDtypeStruct((n, hw), F32)
    return pl.pallas_call(
        functools.partial(_spectrum_kernel, scale=1.0 / n),
        out_shape=(oshape, oshape),
        grid=(n // tk,),
        in_specs=[tspec, tspec, gspec, gspec, vspec, vspec],
        out_specs=(ospec, ospec),
        compiler_params=_cp(("parallel",)),
        name=f"filter_spectrum_{n}",
    )(ctab, stab, ghi, glo, ck, sk)


def _hyena_kernel(c_ref, s_ref, z_ref, zt_ref, x0_ref, kre_ref, kim_ref, bias_ref, o_ref,
                  yre_scr, yim_scr, *, nb, tk):
    phase = pl.program_id(1)
    i = pl.program_id(2)
    ct = c_ref[...]
    st = s_ref[...]

    @pl.when(phase == 0)
    def _():
        kre = kre_ref[...]
        kim = kim_ref[...]
        r0 = pl.multiple_of(i * tk, tk)
        for j in range(nb):
            ure = _dot(ct, z_ref[j])
            uim = -_dot(st, z_ref[j])
            yre_scr[j, pl.ds(r0, tk), :] = (kre * ure - kim * uim).astype(BF16)
            yim_scr[j, pl.ds(r0, tk), :] = (kre * uim + kim * ure).astype(BF16)

    @pl.when(phase == 1)
    def _():
        for j in range(nb):
            y = _dot(ct, yre_scr[j]) - _dot(st, yim_scr[j])
            u = zt_ref[j].astype(F32)
            o_ref[j] = (x0_ref[j].astype(F32) * (y + u * bias_ref[...])).astype(BF16)


def _hyena(z, x0c, kre, kim, bias, ctab, stab):
    b, n, hw = z.shape
    nb = 2
    tk = min(n, 512)
    tspec = pl.BlockSpec((tk, n), lambda g, p, i: (i, 0))
    tile = pl.BlockSpec((nb, tk, hw), lambda g, p, i: (g, i * p, 0))
    kspec = pl.BlockSpec((tk, hw), lambda g, p, i: (i * (1 - p), 0))
    return pl.pallas_call(
        functools.partial(_hyena_kernel, nb=nb, tk=tk),
        out_shape=jax.ShapeDtypeStruct((b, n, hw), BF16),
        grid=(b // nb, 2, n // tk),
        in_specs=[
            tspec, tspec,
            pl.BlockSpec((nb, n, hw), lambda g, p, i: (g, 0, 0)),
            tile, tile, kspec, kspec,
            pl.BlockSpec((1, hw), lambda g, p, i: (0, 0)),
        ],
        out_specs=tile,
        scratch_shapes=[pltpu.VMEM((nb, n, hw), BF16), pltpu.VMEM((nb, n, hw), BF16)],
        compiler_params=_cp(("parallel", "arbitrary", "arbitrary")),
        name=f"hyena_conv_{n}",
    )(ctab, stab, z, z, x0c, kre, kim, bias)


def _mixout_kernel(a_ref, f_ref, h_ref, w_ref, x_ref, ga_ref, g_ref, o_ref):
    aw = a_ref.shape[2]
    fw = f_ref.shape[2]
    mix = (_dot(a_ref[0], w_ref[0:aw, :]) + _dot(f_ref[0], w_ref[aw:aw + fw, :])
           + _dot(h_ref[0], w_ref[aw + fw:, :]))
    o_ref[0] = x_ref[0] + ga_ref[0] * _rms(mix, g_ref[...])


def _mixout(attn, four, hy, w_out, x, ga, g):
    b, n, d = x.shape
    tm = min(n, 512)
    row = lambda width: pl.BlockSpec((1, tm, width), lambda bi, i: (bi, i, 0))
    return pl.pallas_call(
        _mixout_kernel,
        out_shape=jax.ShapeDtypeStruct((b, n, d), F32),
        grid=(b, n // tm),
        in_specs=[
            row(attn.shape[2]), row(four.shape[2]), row(hy.shape[2]),
            pl.BlockSpec(w_out.shape, lambda bi, i: (0, 0)),
            row(d),
            pl.BlockSpec((1, 1, d), lambda bi, i: (bi, 0, 0)),
            pl.BlockSpec((1, d), lambda bi, i: (0, 0)),
        ],
        out_specs=row(d),
        compiler_params=_cp(("parallel", "parallel")),
        name=f"mixout_{n}",
    )(attn, four, hy, w_out, x, ga, g)


HALO = 8


FFN_CHUNK = 256


def _ffn_kernel(xp_ref, x_ref, xn_ref, sc_ref, sh_ref, g_ref, wu_ref, cw_ref, cb_ref, wd_ref,
                ga_ref, go_ref, o_ref, *, tm):
    i = pl.program_id(1)
    nrow = pl.num_programs(1)
    rows = tm + 2 * HALO
    dff = wd_ref.shape[0]
    cf = FFN_CHUNK
    g = g_ref[...]
    sc = 1.0 + sc_ref[0]
    sh = sh_ref[0]
    keep_p = jnp.where(i > 0, 1.0, 0.0)
    keep_n = jnp.where(i < nrow - 1, 1.0, 0.0)
    x = x_ref[0]
    fx = jnp.concatenate([(_rms(xp_ref[0], g) * sc + sh) * keep_p,
                          _rms(x, g) * sc + sh,
                          (_rms(xn_ref[0], g) * sc + sh) * keep_n], axis=0).astype(BF16)

    def conv(u, w, bias):
        prev = pltpu.roll(u, 1, 0)[HALO:HALO + tm]
        nxt = pltpu.roll(u, rows - 1, 0)[HALO:HALO + tm]
        return prev * w[0:1] + u[HALO:HALO + tm] * w[1:2] + nxt * w[2:3] + bias

    acts = []
    for c in range(dff // cf):
        lo, hi = c * cf, (c + 1) * cf
        gate = conv(_dot(fx, wu_ref[:, lo:hi]), cw_ref[:, lo:hi], cb_ref[:, lo:hi])
        val = conv(_dot(fx, wu_ref[:, dff + lo:dff + hi]), cw_ref[:, dff + lo:dff + hi],
                   cb_ref[:, dff + lo:dff + hi])
        acts.append((gate * jax.nn.sigmoid(gate) * val).astype(BF16))
    y = _dot(jnp.concatenate(acts, axis=1), wd_ref[...])
    o_ref[0] = x + ga_ref[0] * _rms(y, go_ref[...])


def _ffn(x, sc, sh, g, w_up, w_conv, b_conv, w_down, ga, g_post):
    b, n, d = x.shape
    tm = min(n, 512)
    hb = tm // HALO
    nhb = n // HALO
    vec = pl.BlockSpec((1, 1, d), lambda bi, i: (bi, 0, 0))
    gvec = pl.BlockSpec((1, d), lambda bi, i: (0, 0))
    resident = lambda a: pl.BlockSpec(a.shape, lambda bi, i: (0, 0), pipeline_mode=pl.Buffered(1))
    return pl.pallas_call(
        functools.partial(_ffn_kernel, tm=tm),
        out_shape=jax.ShapeDtypeStruct((b, n, d), F32),
        grid=(b, n // tm),
        in_specs=[
            pl.BlockSpec((1, HALO, d), lambda bi, i: (bi, jnp.maximum(i * hb - 1, 0), 0)),
            pl.BlockSpec((1, tm, d), lambda bi, i: (bi, i, 0)),
            pl.BlockSpec((1, HALO, d), lambda bi, i: (bi, jnp.minimum((i + 1) * hb, nhb - 1), 0)),
            vec, vec, gvec,
            resident(w_up), resident(w_conv), resident(b_conv), resident(w_down),
            vec, gvec,
        ],
        out_specs=pl.BlockSpec((1, tm, d), lambda bi, i: (bi, i, 0)),
        compiler_params=_cp(("parallel", "parallel")),
        name=f"conv_ffn_{n}",
    )(x, x, x, sc, sh, g, w_up, w_conv, b_conv, w_down, ga, g_post)


def _rope_tables(n):
    half = HEAD_DIM // 4
    inv = ROPE_THETA ** (-jnp.arange(0, 2 * half, 2, dtype=F32) / (2 * half))
    pos = jnp.arange(n, dtype=jnp.int32)
    row = (pos // GRID_W).astype(F32)
    col = (pos % GRID_W).astype(F32)
    ang_r = row[:, None] * inv[None, :]
    ang_c = col[:, None] * inv[None, :]
    cr, sr, cc, sc = jnp.cos(ang_r), jnp.sin(ang_r), jnp.cos(ang_c), jnp.sin(ang_c)
    cos64 = jnp.concatenate([cr, cr, cc, cc], axis=-1)
    sin64 = jnp.concatenate([-sr, sr, -sc, sc], axis=-1)
    return jnp.tile(cos64, (1, 2)), jnp.tile(sin64, (1, 2))


def _head_ones(width):
    idx = np.arange(width) // HEAD_DIM
    return jnp.asarray((idx[:, None] == idx[None, :]).astype(np.float32), BF16)


def _channel_dft(width):
    c = np.arange(width)
    same = (c[:, None] // FOURIER_GROUP_DIM) == (c[None, :] // FOURIER_GROUP_DIM)
    ang = 2.0 * np.pi * ((c[:, None] % FOURIER_GROUP_DIM) * (c[None, :] % FOURIER_GROUP_DIM)
                         % FOURIER_GROUP_DIM) / FOURIER_GROUP_DIM
    cs = np.concatenate([np.where(same, np.cos(ang), 0.0), np.where(same, np.sin(ang), 0.0)], axis=1)
    return jnp.asarray(cs.astype(np.float32), BF16)


def kernel(x, c, ctx, c_ctx, w_mod, b_mod, g_pre_mix, g_post_mix, g_pre_ffn, g_post_ffn, w_in, g_q, g_k, w_fourier, w_hy_conv, b_hy_conv, hy_w1, hy_b1, hy_fr1, hy_w2, hy_b2, hy_fr2, hy_w3, hy_bias, w_out, w_up, w_ffn_conv, b_ffn_conv, w_down):
    bsz, seq, d = x.shape
    clen = ctx.shape[1]
    depth = w_mod.shape[0]
    fw = w_fourier.shape[1]
    hw = hy_bias.shape[1]
    aw = w_in.shape[2] - 2 * N_KV_HEADS * HEAD_DIM - fw - 3 * hw
    k0 = aw
    f0 = aw + 2 * N_KV_HEADS * HEAD_DIM

    nrows = -(-(bsz + 1) // 8) * 8
    cc = jnp.zeros((nrows, d), F32).at[:bsz].set(c).at[bsz].set(c_ctx)
    mods = _modulation(cc, w_mod, b_mod)

    cos_x, sin_x = _rope_tables(seq)
    cos_c = jnp.ones((clen, LANES), F32)
    sin_c = jnp.zeros((clen, LANES), F32)
    ones_bd = _head_ones(aw)
    cs = _channel_dft(fw)
    hc_x, hs_x, fc_x, fs_x = _make_tables(seq)
    hc_c, hs_c, fc_c, fs_c = _make_tables(clen)

    def row1(v):
        return v.reshape(1, -1)

    for i in range(depth):
        last = i == depth - 1
        mx = mods[i, :bsz].reshape(bsz, 1, 6, d)
        mc = jnp.broadcast_to(mods[i, bsz].reshape(1, 1, 6, d), (bsz, 1, 6, d))
        sh1, sc1, ga1, sh2, sc2, ga2 = (mx[:, :, t] for t in range(6))
        csh1, csc1, cga1, csh2, csc2, cga2 = (mc[:, :, t] for t in range(6))
        w_in_b = w_in[i].astype(BF16)
        w_out_b = w_out[i].astype(BF16)
        w_up_b = w_up[i].astype(BF16)
        w_down_b = w_down[i].astype(BF16)
        w_f_b = w_fourier[i].astype(BF16)
        gq_t = jnp.tile(g_q[i], aw // HEAD_DIM).reshape(1, aw)
        gk_t = jnp.tile(g_k[i], N_KV_HEADS).reshape(1, N_KV_HEADS * HEAD_DIM)
        g_pre = row1(g_pre_mix[i])
        g_post = row1(g_post_mix[i])
        hy_params = (hy_w1[i], hy_b1[i], hy_fr1[i], hy_w2[i], hy_b2[i], hy_fr2[i], hy_w3[i])
        hbias = row1(hy_bias[i])
        hcw = w_hy_conv[i]
        hcb = row1(b_hy_conv[i])

        q, k4, v4, fa, ph = _project(x, sc1, sh1, g_pre, w_in_b, gq_t, gk_t, cos_x, sin_x,
                                     ones_bd, cs, rope=True, kv_only=False)
        if last:
            kc4, vc4 = _project(ctx, csc1, csh1, g_pre, w_in_b[:, k0:f0], gq_t, gk_t, cos_c, sin_c,
                                ones_bd, cs, rope=False, kv_only=True)
        else:
            qc, kc4, vc4, fac, phc = _project(ctx, csc1, csh1, g_pre, w_in_b, gq_t, gk_t, cos_c,
                                              sin_c, ones_bd, cs, rope=False, kv_only=False)
        attn_x = _attention(q, [(k4, v4), (kc4, vc4)])
        four_x = _fourier(fa, fc_x, fs_x, w_f_b)
        ghi, glo = _hyena_filter(seq, *hy_params)
        kre, kim = _filter_spectrum(ghi, glo, hc_x, hs_x)
        z, x0c = _hyena_pre(ph, hcw, hcb)
        hy_x = _hyena(z, x0c, kre, kim, hbias, hc_x, hs_x)
        x_new = _mixout(attn_x, four_x, hy_x, w_out_b, x, ga1, g_post)

        if not last:
            attn_c = _attention(qc, [(kc4, vc4)])
            four_c = _fourier(fac, fc_c, fs_c, w_f_b)
            ghi_c, glo_c = _hyena_filter(clen, *hy_params)
            kre_c, kim_c = _filter_spectrum(ghi_c, glo_c, hc_c, hs_c)
            zc, x0cc = _hyena_pre(phc, hcw, hcb)
            hy_c = _hyena(zc, x0cc, kre_c, kim_c, hbias, hc_c, hs_c)
            ctx = _mixout(attn_c, four_c, hy_c, w_out_b, ctx, cga1, g_post)
        x = x_new

        g_pf = row1(g_pre_ffn[i])
        g_of = row1(g_post_ffn[i])
        fcw = w_ffn_conv[i]
        fcb = row1(b_ffn_conv[i])
        x = _ffn(x, sc2, sh2, g_pf, w_up_b, fcw, fcb, w_down_b, ga2, g_of)
        if not last:
            ctx = _ffn(ctx, csc2, csh2, g_pf, w_up_b, fcw, fcb, w_down_b, cga2, g_of)

    return x
```

```python
import functools
import math

import numpy as np
import jax
import jax.numpy as jnp
from jax import lax
from jax.experimental import pallas as pl
from jax.experimental.pallas import tpu as pltpu

F32 = jnp.float32
BF16 = jnp.bfloat16

HEAD_DIM = 64
GQA_GROUP = 4
N_KV_HEADS = 2
GRID_W = 64
ROPE_THETA = 10000.0
FOURIER_GROUP_DIM = 64
HYENA_BANDS = 16
HYENA_FAST_DECAY = 0.3
HYENA_SLOW_DECAY = 1.5
HYENA_TARGET = 1e-2
NORM_EPS = 1e-6
LANES = 128
VMEM_LIMIT = 56 * 1024 * 1024


def _cp(sem, vmem=VMEM_LIMIT):
    return pltpu.CompilerParams(dimension_semantics=sem, vmem_limit_bytes=vmem)


def _dot(a, b):
    return jnp.dot(a, b, preferred_element_type=F32)


def _dot_nt(a, b):
    return lax.dot_general(a, b, (((1,), (1,)), ((), ())), preferred_element_type=F32)


def _split(a):
    hi = a.astype(BF16)
    lo = (a - hi.astype(F32)).astype(BF16)
    return hi, lo


def _dot3(a, b):
    ah, al = _split(a)
    bh, bl = _split(b)
    return _dot(ah, bh) + _dot(ah, bl) + _dot(al, bh)


def _rms(x, g):
    ms = jnp.mean(x * x, axis=-1, keepdims=True)
    return x * lax.rsqrt(ms + NORM_EPS) * g


def _tables_kernel(hc_ref, hs_ref, fc_ref, fs_ref, hca, hsa, fca, fsa, *, n, tk):
    i = pl.program_id(0)
    hyena_step = 2.0 * math.pi / (8 * n)
    fourier_step = 2.0 * math.pi / n

    @pl.when(i == 0)
    def _():
        k = lax.broadcasted_iota(jnp.int32, (tk, n), 0)
        s = lax.broadcasted_iota(jnp.int32, (tk, n), 1)
        a = (((2 * k + 1) * (2 * s + 1)) & (8 * n - 1)).astype(F32) * hyena_step
        hca[...] = jnp.cos(a)
        hsa[...] = jnp.sin(a)
        kh = lax.broadcasted_iota(jnp.int32, (tk, n // 2), 0)
        sh = lax.broadcasted_iota(jnp.int32, (tk, n // 2), 1)
        b = ((kh * sh) & (n - 1)).astype(F32) * fourier_step
        fca[...] = jnp.cos(b)
        fsa[...] = jnp.sin(b)

    s1 = lax.broadcasted_iota(jnp.int32, (1, n), 1)
    rot = (((2 * tk * i) * (2 * s1 + 1)) & (8 * n - 1)).astype(F32) * hyena_step
    cb, sb = jnp.cos(rot), jnp.sin(rot)
    hc_ref[...] = (hca[...] * cb - hsa[...] * sb).astype(BF16)
    hs_ref[...] = (hsa[...] * cb + hca[...] * sb).astype(BF16)
    s2 = lax.broadcasted_iota(jnp.int32, (1, n // 2), 1)
    rot = (((tk * i) * s2) & (n - 1)).astype(F32) * fourier_step
    cb, sb = jnp.cos(rot), jnp.sin(rot)
    fc_ref[...] = (fca[...] * cb - fsa[...] * sb).astype(BF16)
    fs_ref[...] = (fsa[...] * cb + fca[...] * sb).astype(BF16)


def _make_tables(n):
    tk = min(n, 256)
    spec = pl.BlockSpec((tk, n), lambda i: (i, 0))
    shp = jax.ShapeDtypeStruct((n, n), BF16)
    hspec = pl.BlockSpec((tk, n // 2), lambda i: (i, 0))
    hshp = jax.ShapeDtypeStruct((n, n // 2), BF16)
    return pl.pallas_call(
        functools.partial(_tables_kernel, n=n, tk=tk),
        out_shape=(shp, shp, hshp, hshp),
        grid=(n // tk,),
        out_specs=(spec, spec, hspec, hspec),
        scratch_shapes=[pltpu.VMEM((tk, n), F32)] * 2 + [pltpu.VMEM((tk, n // 2), F32)] * 2,
        compiler_params=_cp(("arbitrary",)),
        name=f"dft_tables_{n}",
    )()


def _mod_kernel(c_ref, w_ref, b_ref, o_ref):
    c = c_ref[...]
    a = c * jax.nn.sigmoid(c)
    o_ref[0] = _dot3(a, w_ref[0]) + b_ref[0]


def _modulation(cc, w_mod, b_mod):
    depth, d, n6 = w_mod.shape
    tn = 1536
    rows = cc.shape[0]
    return pl.pallas_call(
        _mod_kernel,
        out_shape=jax.ShapeDtypeStruct((depth, rows, n6), F32),
        grid=(depth, n6 // tn),
        in_specs=[
            pl.BlockSpec((rows, d), lambda l, j: (0, 0)),
            pl.BlockSpec((1, d, tn), lambda l, j: (l, 0, j)),
            pl.BlockSpec((1, 1, tn), lambda l, j: (l, 0, j)),
        ],
        out_specs=pl.BlockSpec((1, rows, tn), lambda l, j: (l, 0, j)),
        compiler_params=_cp(("parallel", "parallel")),
        name="modulation",
    )(cc, w_mod, b_mod.reshape(depth, 1, n6))


def _head_norm(p, gain, ones_bd):
    hi, lo = _split(p * p)
    ss = _dot(hi, ones_bd) + _dot(lo, ones_bd)
    return p * lax.rsqrt(ss * (1.0 / HEAD_DIM) + NORM_EPS) * gain


def _rope(xn, cos_t, sin_t):
    lane = lax.broadcasted_iota(jnp.int32, (1, LANES), 1)
    first = (lane % 32) < 16
    outs = []
    for j in range(xn.shape[1] // LANES):
        c = xn[:, j * LANES:(j + 1) * LANES]
        sw = jnp.where(first, pltpu.roll(c, LANES - 16, 1), pltpu.roll(c, 16, 1))
        outs.append(c * cos_t + sw * sin_t)
    return outs[0] if len(outs) == 1 else jnp.concatenate(outs, axis=1)


def _tile_heads(kv):
    lane = lax.broadcasted_iota(jnp.int32, (1, LANES), 1)
    low = lane < HEAD_DIM
    r = pltpu.roll(kv, HEAD_DIM, 1)
    h0 = jnp.where(low, kv, r)
    h1 = jnp.where(low, r, kv)
    return jnp.concatenate([h0, h0], axis=1), jnp.concatenate([h1, h1], axis=1)


VALUE_ROWS = 2 * HEAD_DIM


def _value_heads(v):
    lane = lax.broadcasted_iota(jnp.int32, (1, LANES), 1)
    low = lane < HEAD_DIM
    r = pltpu.roll(v, HEAD_DIM, 1)
    return jnp.where(low, v, 1.0), jnp.where(low, r, 1.0)


def _proj_kernel(x_ref, sc_ref, sh_ref, g_ref, w_ref, gq_ref, gk_ref, cos_ref, sin_ref,
                 ones_ref, cs_ref, *outs, rope, kv_only):
    x = x_ref[0]
    h = _rms(x, g_ref[...]) * (1.0 + sc_ref[0]) + sh_ref[0]
    px = _dot(h.astype(BF16), w_ref[...])
    ones_bd = ones_ref[...]
    if kv_only:
        k4_ref, v4_ref = outs
        k = px[:, 0:128]
        v = px[:, 128:256]
    else:
        q_ref, k4_ref, v4_ref, fa_ref, ph_ref = outs
        aw = ones_bd.shape[0]
        q = _head_norm(px[:, 0:aw], gq_ref[...], ones_bd)
        if rope:
            q = _rope(q, cos_ref[...], sin_ref[...])
        q_ref[0] = (q * (HEAD_DIM ** -0.5 * math.log2(math.e))).astype(BF16)
        k = px[:, aw:aw + 128]
        v = px[:, aw + 128:aw + 256]
        f0 = aw + 256
        fw = cs_ref.shape[0]
        fa_ref[0] = _dot(px[:, f0:f0 + fw].astype(BF16), cs_ref[...]).astype(BF16)
        ph_ref[0] = px[:, f0 + fw:].astype(BF16)
    k = _head_norm(k, gk_ref[...], ones_bd[0:128, 0:128])
    if rope:
        k = _rope(k, cos_ref[...], sin_ref[...])
    k0, k1 = _tile_heads(k)
    k4_ref[0, 0] = k0.astype(BF16)
    k4_ref[0, 1] = k1.astype(BF16)
    v0, v1 = _value_heads(v)
    v4_ref[0, 0] = v0.T[0:VALUE_ROWS].astype(BF16)
    v4_ref[0, 1] = v1.T[0:VALUE_ROWS].astype(BF16)


def _project(x, sc, sh, g, w, gq, gk, cos_t, sin_t, ones_bd, cs, *, rope, kv_only):
    b, n, d = x.shape
    tm = min(n, 512)
    wn = w.shape[1]
    aw = ones_bd.shape[0]
    fw = cs.shape[0]
    vec = lambda width: pl.BlockSpec((1, 1, width), lambda bi, i: (bi, 0, 0))
    full = lambda a: pl.BlockSpec(a.shape, lambda bi, i: (0,) * a.ndim)
    kv_spec = pl.BlockSpec((1, N_KV_HEADS, tm, 256), lambda bi, i: (bi, 0, i, 0))
    kv_shape = jax.ShapeDtypeStruct((b, N_KV_HEADS, n, 256), BF16)
    vl_spec = pl.BlockSpec((1, N_KV_HEADS, VALUE_ROWS, tm), lambda bi, i: (bi, 0, 0, i))
    vl_shape = jax.ShapeDtypeStruct((b, N_KV_HEADS, VALUE_ROWS, n), BF16)
    row = lambda width: pl.BlockSpec((1, tm, width), lambda bi, i: (bi, i, 0))
    if kv_only:
        out_shape = (kv_shape, vl_shape)
        out_specs = (kv_spec, vl_spec)
    else:
        hw = wn - aw - 256 - fw
        out_shape = (jax.ShapeDtypeStruct((b, n, aw), BF16), kv_shape, vl_shape,
                     jax.ShapeDtypeStruct((b, n, 2 * fw), BF16),
                     jax.ShapeDtypeStruct((b, n, hw), BF16))
        out_specs = (row(aw), kv_spec, vl_spec, row(2 * fw), row(hw))
    return pl.pallas_call(
        functools.partial(_proj_kernel, rope=rope, kv_only=kv_only),
        out_shape=out_shape,
        grid=(b, n // tm),
        in_specs=[
            row(d), vec(d), vec(d), full(g), full(w), full(gq), full(gk),
            pl.BlockSpec((tm, LANES), lambda bi, i: (i, 0)),
            pl.BlockSpec((tm, LANES), lambda bi, i: (i, 0)),
            full(ones_bd), full(cs),
        ],
        out_specs=out_specs,
        compiler_params=_cp(("parallel", "parallel")),
        name="proj_kv" if kv_only else ("proj_rope" if rope else "proj_ctx"),
    )(x, sc, sh, g, w, gq, gk, cos_t, sin_t, ones_bd, cs)


ATTN_SUB = 128


ATTN_PROB_ROWS = 128


def _attn_kernel(q_ref, *refs, lks, nsub):
    nsrc = len(lks)
    k_refs = refs[0:2 * nsrc:2]
    v_refs = refs[1:2 * nsrc:2]
    o_ref = refs[2 * nsrc]
    s_scrs = refs[2 * nsrc + 1:2 * nsrc + 3]
    p_scrs = refs[2 * nsrc + 3:2 * nsrc + 5]
    sub = ATTN_SUB
    rows = GQA_GROUP * sub
    gw = GQA_GROUP * HEAD_DIM
    group = lax.broadcasted_iota(jnp.int32, (1, gw), 1) // HEAD_DIM

    chunks = []
    off = 0
    for j, lk in enumerate(lks):
        tkc = min(lk, 512)
        for c in range(lk // tkc):
            chunks.append((j, c * tkc, tkc, off))
            off += tkc

    lane = lax.broadcasted_iota(jnp.int32, (1, LANES), 1)
    low = lane < HEAD_DIM

    def masked_q(u):
        q = q_ref[0, u * sub:(u + 1) * sub, :]
        zero = jnp.zeros_like(q)
        return jnp.concatenate([jnp.where(group == g, q, zero) for g in range(GQA_GROUP)], axis=0)

    def finish(u, acc):
        o_t = acc / acc[HEAD_DIM:HEAD_DIM + 1, :]
        heads = [o_t[:, g * sub:(g + 1) * sub].T for g in range(GQA_GROUP)]
        left = jnp.where(low, heads[0], pltpu.roll(heads[1], HEAD_DIM, 1))
        right = jnp.where(low, heads[2], pltpu.roll(heads[3], HEAD_DIM, 1))
        o_ref[0, u * sub:(u + 1) * sub, :] = jnp.concatenate([left, right], axis=1).astype(BF16)

    def scores(u):
        s_scr = s_scrs[u % 2]
        qm = masked_q(u)
        for j, start, tkc, o in chunks:
            s_scr[o:o + tkc, :] = _dot_nt(k_refs[j][0, 0, start:start + tkc, :], qm)
        return jnp.max(s_scr[...], axis=0, keepdims=True)

    m = scores(0)
    for u in range(nsub):
        s_scr = s_scrs[u % 2]
        acc = jnp.zeros((VALUE_ROWS, rows), F32)
        if u + 1 < nsub:
            m_next = scores(u + 1)
            bits = pltpu.bitcast(m_next, jnp.uint32)
            acc = acc + ((bits >> 16) >> 16).astype(F32)
        for ci, (j, start, tkc, o) in enumerate(chunks):
            p_scr = p_scrs[ci % 2]
            for r in range(0, tkc, ATTN_PROB_ROWS):
                p_scr[r:r + ATTN_PROB_ROWS, :] = jnp.exp2(
                    s_scr[o + r:o + r + ATTN_PROB_ROWS, :] - m).astype(BF16)
            acc = acc + _dot(v_refs[j][0, 0, :, start:start + tkc], p_scr[0:tkc, :])
        finish(u, acc)
        if u + 1 < nsub:
            m = m_next


def _attention(q, kvs):
    b, lq, aw = q.shape
    tq = min(lq, 8 * ATTN_SUB)
    nsub = tq // ATTN_SUB
    lks = tuple(k.shape[2] for k, _ in kvs)
    gw = GQA_GROUP * HEAD_DIM
    in_specs = [pl.BlockSpec((1, tq, gw), lambda bi, h, i: (bi, i, h))]
    args = [q]
    for k4, vlt in kvs:
        for a in (k4, vlt):
            in_specs.append(pl.BlockSpec((1, 1) + a.shape[2:], lambda bi, h, i: (bi, h, 0, 0),
                                         pipeline_mode=pl.Buffered(1)))
        args += [k4, vlt]
    return pl.pallas_call(
        functools.partial(_attn_kernel, lks=lks, nsub=nsub),
        out_shape=jax.ShapeDtypeStruct((b, lq, aw), BF16),
        grid=(b, N_KV_HEADS, lq // tq),
        in_specs=in_specs,
        out_specs=pl.BlockSpec((1, tq, gw), lambda bi, h, i: (bi, i, h)),
        scratch_shapes=[pltpu.VMEM((sum(lks), GQA_GROUP * ATTN_SUB), F32)] * 2
        + [pltpu.VMEM((min(max(lks), 512), GQA_GROUP * ATTN_SUB), BF16)] * 2,
        compiler_params=_cp(("parallel", "parallel", "arbitrary")),
        name=f"attention_{lq}",
    )(*args)


def _fourier_kernel(c_ref, s_ref, fa_ref, w_ref, o_ref, fold_scr, nyq_scr, *, nb, scale, tk):
    fw = w_ref.shape[0]
    n = fa_ref.shape[1]
    half = n // 2
    blk = min(half, 512)

    @pl.when(pl.program_id(1) == 0)
    def _():
        r = lax.broadcasted_iota(jnp.int32, (blk, blk), 0)
        c = lax.broadcasted_iota(jnp.int32, (blk, blk), 1)
        flip = jnp.where(r + c == blk - 1, 1.0, 0.0).astype(BF16)
        row = lax.broadcasted_iota(jnp.int32, (half, 1), 0)
        sign = jnp.where(lax.broadcasted_iota(jnp.int32, (1, 2 * fw), 1) < fw, 1.0, -1.0)
        for j in range(nb):
            rev = jnp.concatenate(
                [_dot(flip, fa_ref[j, n - (b + 1) * blk:n - b * blk, :]) for b in range(half // blk)],
                axis=0)
            mirrored = jnp.where(row == 0, 0.0, pltpu.roll(rev, 1, 0))
            fold_scr[j] = (fa_ref[j, 0:half, :].astype(F32) + sign * mirrored).astype(BF16)
            nyq_scr[j] = jnp.broadcast_to(fa_ref[j, half:half + 16, 0:fw][0:1].astype(F32), (8, fw))

    ct = c_ref[...]
    st = s_ref[...]
    odd = (lax.broadcasted_iota(jnp.int32, (tk, 1), 0) & 1) == 1
    alt = jnp.where(odd, -1.0, 1.0)
    for j in range(nb):
        y = (_dot(ct, fold_scr[j, :, 0:fw]) - _dot(st, fold_scr[j, :, fw:2 * fw])
             + alt * nyq_scr[j, 0:1, :])
        o_ref[j] = _dot((y * scale).astype(BF16), w_ref[...]).astype(BF16)


def _fourier(fa, ctab, stab, w_f):
    b, n, fw2 = fa.shape
    fw = fw2 // 2
    nb = 2
    tk = min(n, 512)
    half = n // 2
    scale = 1.0 / math.sqrt(n * FOURIER_GROUP_DIM)
    return pl.pallas_call(
        functools.partial(_fourier_kernel, nb=nb, scale=scale, tk=tk),
        out_shape=jax.ShapeDtypeStruct((b, n, fw), BF16),
        grid=(b // nb, n // tk),
        in_specs=[
            pl.BlockSpec((tk, half), lambda g, i: (i, 0)),
            pl.BlockSpec((tk, half), lambda g, i: (i, 0)),
            pl.BlockSpec((nb, n, fw2), lambda g, i: (g, 0, 0)),
            pl.BlockSpec((fw, fw), lambda g, i: (0, 0)),
        ],
        out_specs=pl.BlockSpec((nb, tk, fw), lambda g, i: (g, i, 0)),
        scratch_shapes=[pltpu.VMEM((nb, half, fw2), BF16), pltpu.VMEM((nb, 8, fw), F32)],
        compiler_params=_cp(("parallel", "arbitrary")),
        name=f"fourier_{n}",
    )(ctab, stab, fa, w_f)


def _conv3(p, w, bias):
    n = p.shape[0]
    row = lax.broadcasted_iota(jnp.int32, (n, 1), 0)
    prev = jnp.where(row == 0, 0.0, pltpu.roll(p, 1, 0))
    nxt = jnp.where(row == n - 1, 0.0, pltpu.roll(p, n - 1, 0))
    return prev * w[0:1] + p * w[1:2] + nxt * w[2:3] + bias


def _hyena_pre_kernel(p0_ref, p1_ref, p2_ref, w0_ref, w1_ref, w2_ref, b0_ref, b1_ref, b2_ref,
                      z_ref, x0_ref):
    x0 = _conv3(p0_ref[0].astype(F32), w0_ref[...], b0_ref[...])
    x1 = _conv3(p1_ref[0].astype(F32), w1_ref[...], b1_ref[...])
    v = _conv3(p2_ref[0].astype(F32), w2_ref[...], b2_ref[...])
    z_ref[0] = (x1 * v).astype(BF16)
    x0_ref[0] = x0.astype(BF16)


def _hyena_pre(ph, w, bias):
    b, n, hw3 = ph.shape
    hw = hw3 // 3
    nc = hw // LANES
    pspec = lambda s: pl.BlockSpec((1, n, LANES), lambda bi, c: (bi, 0, s * nc + c))
    wspec = lambda s: pl.BlockSpec((3, LANES), lambda bi, c: (0, s * nc + c))
    bspec = lambda s: pl.BlockSpec((1, LANES), lambda bi, c: (0, s * nc + c))
    ospec = pl.BlockSpec((1, n, LANES), lambda bi, c: (bi, 0, c))
    oshape = jax.ShapeDtypeStruct((b, n, hw), BF16)
    return pl.pallas_call(
        _hyena_pre_kernel,
        out_shape=(oshape, oshape),
        grid=(b, nc),
        in_specs=[pspec(0), pspec(1), pspec(2), wspec(0), wspec(1), wspec(2),
                  bspec(0), bspec(1), bspec(2)],
        out_specs=(ospec, ospec),
        compiler_params=_cp(("parallel", "parallel")),
        name=f"hyena_pre_{n}",
    )(ph, ph, ph, w, w, w, bias, bias, bias)


def _filter_kernel(w1t_ref, w1c_ref, w1s_ref, b1_ref, fr1_ref, w2_ref, b2_ref, fr2_ref, w3_ref,
                   g_ref, *, n):
    hw = w3_ref.shape[1] // 2
    i = lax.broadcasted_iota(jnp.int32, (n, 1), 0).astype(F32)
    t = i / float(n - 1)
    jb = lax.broadcasted_iota(jnp.int32, (1, HYENA_BANDS), 1).astype(F32)
    bands = 1e-4 + jb * ((HYENA_BANDS - 1 - 1e-4) / (HYENA_BANDS - 1))
    ang = ((2.0 * math.pi) * i / float(n)) * bands
    pre = t * w1t_ref[...] + _dot3(jnp.cos(ang), w1c_ref[...]) - _dot3(jnp.sin(ang), w1s_ref[...])
    h = jnp.sin(fr1_ref[...] * (pre + b1_ref[...]))
    h = jnp.sin(fr2_ref[...] * (_dot3(h, w2_ref[...]) + b2_ref[...]))
    h = _dot3(h, w3_ref[...])
    d0 = math.log(HYENA_TARGET) / HYENA_SLOW_DECAY
    d1 = math.log(HYENA_TARGET) / HYENA_FAST_DECAY
    jd = lax.broadcasted_iota(jnp.int32, (1, hw), 1).astype(F32)
    deltas = jnp.abs(d0 + jd * ((d1 - d0) / (hw - 1)))
    decay = jnp.exp(-t * deltas)
    hf = h[:, 0:hw] * decay
    hb = jnp.where(i == 0.0, 0.0, h[:, hw:2 * hw] * decay)
    total = jnp.sum(jnp.abs(hf), axis=0, keepdims=True) + jnp.sum(jnp.abs(hb), axis=0, keepdims=True)
    g_ref[...] = jnp.concatenate([hf / total, hb / total], axis=1).astype(BF16)


def _hyena_filter(n, w1, b1, fr1, w2, b2, fr2, w3):
    nb = HYENA_BANDS
    r = lambda a: a.reshape(1, -1)
    shp = jax.ShapeDtypeStruct((n, w3.shape[1]), BF16)
    return pl.pallas_call(
        functools.partial(_filter_kernel, n=n),
        out_shape=shp,
        compiler_params=pltpu.CompilerParams(vmem_limit_bytes=VMEM_LIMIT),
        name=f"hyena_filter_{n}",
    )(w1[0:1], w1[1:1 + nb], w1[1 + nb:1 + 2 * nb], r(b1), r(fr1), w2, r(b2), r(fr2), w3)


def _spectrum_kernel(c_ref, s_ref, g_ref, ck_ref, sk_ref, kre_ref, kim_ref, *, scale):
    hw = kre_ref.shape[1]
    a = _dot(c_ref[...], g_ref[...])
    b = _dot(s_ref[...], g_ref[...])
    are, bre = a[:, 0:hw], a[:, hw:2 * hw]
    aim, bim = -b[:, 0:hw], -b[:, hw:2 * hw]
    ck = ck_ref[...]
    sk = sk_ref[...]
    kre_ref[...] = (ck * (are + bre) - sk * (aim + bim)) * scale
    kim_ref[...] = (ck * (aim - bim) + sk * (are - bre)) * scale


def _filter_spectrum(g, ctab, stab):
    n, hw2 = g.shape
    hw = hw2 // 2
    tk = min(n, 512)
    half = (2.0 * np.arange(n, dtype=np.float64) + 1.0) * (2.0 * np.pi / (8 * n))
    ck = jnp.asarray(np.cos(half).reshape(n, 1), F32)
    sk = jnp.asarray(np.sin(half).reshape(n, 1), F32)
    tspec = pl.BlockSpec((tk, n), lambda i: (i, 0))
    gspec = pl.BlockSpec((n, hw2), lambda i: (0, 0))
    vspec = pl.BlockSpec((tk, 1), lambda i: (i, 0))
    ospec = pl.BlockSpec((tk, hw), lambda i: (i, 0))
    oshape = jax.ShapeDtypeStruct((n, hw), F32)
    return pl.pallas_call(
        functools.partial(_spectrum_kernel, scale=1.0 / n),
        out_shape=(oshape, oshape),
        grid=(n // tk,),
        in_specs=[tspec, tspec, gspec, vspec, vspec],
        out_specs=(ospec, ospec),
        compiler_params=_cp(("parallel",)),
        name=f"filter_spectrum_{n}",
    )(ctab, stab, g, ck, sk)


def _hyena_kernel(c_ref, s_ref, z_ref, zt_ref, x0_ref, kre_ref, kim_ref, bias_ref, o_ref,
                  yre_scr, yim_scr, *, nb, tk):
    phase = pl.program_id(1)
    i = pl.program_id(2)
    ct = c_ref[...]
    st = s_ref[...]

    @pl.when(phase == 0)
    def _():
        kre = kre_ref[...]
        kim = kim_ref[...]
        r0 = pl.multiple_of(i * tk, tk)
        for j in range(nb):
            ure = _dot(ct, z_ref[j])
            uim = -_dot(st, z_ref[j])
            yre_scr[j, pl.ds(r0, tk), :] = (kre * ure - kim * uim).astype(BF16)
            yim_scr[j, pl.ds(r0, tk), :] = (kre * uim + kim * ure).astype(BF16)

    @pl.when(phase == 1)
    def _():
        for j in range(nb):
            y = _dot(ct, yre_scr[j]) - _dot(st, yim_scr[j])
            u = zt_ref[j].astype(F32)
            o_ref[j] = (x0_ref[j].astype(F32) * (y + u * bias_ref[...])).astype(BF16)


def _hyena(z, x0c, kre, kim, bias, ctab, stab):
    b, n, hw = z.shape
    nb = 2
    tk = min(n, 512)
    tspec = pl.BlockSpec((tk, n), lambda g, p, i: (i, 0))
    tile = pl.BlockSpec((nb, tk, hw), lambda g, p, i: (g, i * p, 0))
    kspec = pl.BlockSpec((tk, hw), lambda g, p, i: (i * (1 - p), 0))
    return pl.pallas_call(
        functools.partial(_hyena_kernel, nb=nb, tk=tk),
        out_shape=jax.ShapeDtypeStruct((b, n, hw), BF16),
        grid=(b // nb, 2, n // tk),
        in_specs=[
            tspec, tspec,
            pl.BlockSpec((nb, n, hw), lambda g, p, i: (g, 0, 0)),
            tile, tile, kspec, kspec,
            pl.BlockSpec((1, hw), lambda g, p, i: (0, 0)),
        ],
        out_specs=tile,
        scratch_shapes=[pltpu.VMEM((nb, n, hw), BF16), pltpu.VMEM((nb, n, hw), BF16)],
        compiler_params=_cp(("parallel", "arbitrary", "arbitrary")),
        name=f"hyena_conv_{n}",
    )(ctab, stab, z, z, x0c, kre, kim, bias)


def _mixout_kernel(a_ref, f_ref, h_ref, w_ref, x_ref, ga_ref, g_ref, o_ref):
    aw = a_ref.shape[2]
    fw = f_ref.shape[2]
    mix = (_dot(a_ref[0], w_ref[0:aw, :]) + _dot(f_ref[0], w_ref[aw:aw + fw, :])
           + _dot(h_ref[0], w_ref[aw + fw:, :]))
    o_ref[0] = x_ref[0] + ga_ref[0] * _rms(mix, g_ref[...])


def _mixout(attn, four, hy, w_out, x, ga, g):
    b, n, d = x.shape
    tm = min(n, 1024)
    row = lambda width: pl.BlockSpec((1, tm, width), lambda bi, i: (bi, i, 0))
    return pl.pallas_call(
        _mixout_kernel,
        out_shape=jax.ShapeDtypeStruct((b, n, d), F32),
        grid=(b, n // tm),
        in_specs=[
            row(attn.shape[2]), row(four.shape[2]), row(hy.shape[2]),
            pl.BlockSpec(w_out.shape, lambda bi, i: (0, 0)),
            row(d),
            pl.BlockSpec((1, 1, d), lambda bi, i: (bi, 0, 0)),
            pl.BlockSpec((1, d), lambda bi, i: (0, 0)),
        ],
        out_specs=row(d),
        compiler_params=_cp(("parallel", "parallel")),
        name=f"mixout_{n}",
    )(attn, four, hy, w_out, x, ga, g)


HALO = 8


FFN_CHUNK = 256


def _ffn_kernel(xp_ref, x_ref, xn_ref, sc_ref, sh_ref, g_ref, wu_ref, cw_ref, cb_ref, wd_ref,
                ga_ref, go_ref, o_ref, *, tm):
    i = pl.program_id(1)
    nrow = pl.num_programs(1)
    rows = tm + 2 * HALO
    dff = wd_ref.shape[0]
    cf = FFN_CHUNK
    g = g_ref[...]
    sc = 1.0 + sc_ref[0]
    sh = sh_ref[0]
    keep_p = jnp.where(i > 0, 1.0, 0.0)
    keep_n = jnp.where(i < nrow - 1, 1.0, 0.0)
    x = x_ref[0]
    fx = jnp.concatenate([(_rms(xp_ref[0], g) * sc + sh) * keep_p,
                          _rms(x, g) * sc + sh,
                          (_rms(xn_ref[0], g) * sc + sh) * keep_n], axis=0).astype(BF16)

    def conv(u, w, bias):
        prev = pltpu.roll(u, 1, 0)[HALO:HALO + tm]
        nxt = pltpu.roll(u, rows - 1, 0)[HALO:HALO + tm]
        return prev * w[0:1] + u[HALO:HALO + tm] * w[1:2] + nxt * w[2:3] + bias

    acts = []
    for c in range(dff // cf):
        lo, hi = c * cf, (c + 1) * cf
        gate = conv(_dot(fx, wu_ref[:, lo:hi]), cw_ref[:, lo:hi], cb_ref[:, lo:hi])
        val = conv(_dot(fx, wu_ref[:, dff + lo:dff + hi]), cw_ref[:, dff + lo:dff + hi],
                   cb_ref[:, dff + lo:dff + hi])
        acts.append((gate * jax.nn.sigmoid(gate) * val).astype(BF16))
    y = _dot(jnp.concatenate(acts, axis=1), wd_ref[...])
    o_ref[0] = x + ga_ref[0] * _rms(y, go_ref[...])


def _ffn(x, sc, sh, g, w_up, w_conv, b_conv, w_down, ga, g_post):
    b, n, d = x.shape
    tm = min(n, 512)
    hb = tm // HALO
    nhb = n // HALO
    vec = pl.BlockSpec((1, 1, d), lambda bi, i: (bi, 0, 0))
    gvec = pl.BlockSpec((1, d), lambda bi, i: (0, 0))
    resident = lambda a: pl.BlockSpec(a.shape, lambda bi, i: (0, 0), pipeline_mode=pl.Buffered(1))
    return pl.pallas_call(
        functools.partial(_ffn_kernel, tm=tm),
        out_shape=jax.ShapeDtypeStruct((b, n, d), F32),
        grid=(b, n // tm),
        in_specs=[
            pl.BlockSpec((1, HALO, d), lambda bi, i: (bi, jnp.maximum(i * hb - 1, 0), 0)),
            pl.BlockSpec((1, tm, d), lambda bi, i: (bi, i, 0)),
            pl.BlockSpec((1, HALO, d), lambda bi, i: (bi, jnp.minimum((i + 1) * hb, nhb - 1), 0)),
            vec, vec, gvec,
            resident(w_up), resident(w_conv), resident(b_conv), resident(w_down),
            vec, gvec,
        ],
        out_specs=pl.BlockSpec((1, tm, d), lambda bi, i: (bi, i, 0)),
        compiler_params=_cp(("parallel", "parallel")),
        name=f"conv_ffn_{n}",
    )(x, x, x, sc, sh, g, w_up, w_conv, b_conv, w_down, ga, g_post)


def _rope_tables(n):
    half = HEAD_DIM // 4
    inv = ROPE_THETA ** (-jnp.arange(0, 2 * half, 2, dtype=F32) / (2 * half))
    pos = jnp.arange(n, dtype=jnp.int32)
    row = (pos // GRID_W).astype(F32)
    col = (pos % GRID_W).astype(F32)
    ang_r = row[:, None] * inv[None, :]
    ang_c = col[:, None] * inv[None, :]
    cr, sr, cc, sc = jnp.cos(ang_r), jnp.sin(ang_r), jnp.cos(ang_c), jnp.sin(ang_c)
    cos64 = jnp.concatenate([cr, cr, cc, cc], axis=-1)
    sin64 = jnp.concatenate([-sr, sr, -sc, sc], axis=-1)
    return jnp.tile(cos64, (1, 2)), jnp.tile(sin64, (1, 2))


def _head_ones(width):
    idx = np.arange(width) // HEAD_DIM
    return jnp.asarray((idx[:, None] == idx[None, :]).astype(np.float32), BF16)


def _channel_dft(width):
    c = np.arange(width)
    same = (c[:, None] // FOURIER_GROUP_DIM) == (c[None, :] // FOURIER_GROUP_DIM)
    ang = 2.0 * np.pi * ((c[:, None] % FOURIER_GROUP_DIM) * (c[None, :] % FOURIER_GROUP_DIM)
                         % FOURIER_GROUP_DIM) / FOURIER_GROUP_DIM
    cs = np.concatenate([np.where(same, np.cos(ang), 0.0), np.where(same, np.sin(ang), 0.0)], axis=1)
    return jnp.asarray(cs.astype(np.float32), BF16)


def kernel(x, c, ctx, c_ctx, w_mod, b_mod, g_pre_mix, g_post_mix, g_pre_ffn, g_post_ffn, w_in, g_q, g_k, w_fourier, w_hy_conv, b_hy_conv, hy_w1, hy_b1, hy_fr1, hy_w2, hy_b2, hy_fr2, hy_w3, hy_bias, w_out, w_up, w_ffn_conv, b_ffn_conv, w_down):
    bsz, seq, d = x.shape
    clen = ctx.shape[1]
    depth = w_mod.shape[0]
    fw = w_fourier.shape[1]
    hw = hy_bias.shape[1]
    aw = w_in.shape[2] - 2 * N_KV_HEADS * HEAD_DIM - fw - 3 * hw
    k0 = aw
    f0 = aw + 2 * N_KV_HEADS * HEAD_DIM

    nrows = -(-(bsz + 1) // 8) * 8
    cc = jnp.zeros((nrows, d), F32).at[:bsz].set(c).at[bsz].set(c_ctx)
    mods = _modulation(cc, w_mod, b_mod)

    cos_x, sin_x = _rope_tables(seq)
    cos_c = jnp.ones((clen, LANES), F32)
    sin_c = jnp.zeros((clen, LANES), F32)
    ones_bd = _head_ones(aw)
    cs = _channel_dft(fw)
    hc_x, hs_x, fc_x, fs_x = _make_tables(seq)
    hc_c, hs_c, fc_c, fs_c = _make_tables(clen)

    def row1(v):
        return v.reshape(1, -1)

    for i in range(depth):
        last = i == depth - 1
        mx = mods[i, :bsz].reshape(bsz, 1, 6, d)
        mc = jnp.broadcast_to(mods[i, bsz].reshape(1, 1, 6, d), (bsz, 1, 6, d))
        sh1, sc1, ga1, sh2, sc2, ga2 = (mx[:, :, t] for t in range(6))
        csh1, csc1, cga1, csh2, csc2, cga2 = (mc[:, :, t] for t in range(6))
        w_in_b = w_in[i].astype(BF16)
        w_out_b = w_out[i].astype(BF16)
        w_up_b = w_up[i].astype(BF16)
        w_down_b = w_down[i].astype(BF16)
        w_f_b = w_fourier[i].astype(BF16)
        gq_t = jnp.tile(g_q[i], aw // HEAD_DIM).reshape(1, aw)
        gk_t = jnp.tile(g_k[i], N_KV_HEADS).reshape(1, N_KV_HEADS * HEAD_DIM)
        g_pre = row1(g_pre_mix[i])
        g_post = row1(g_post_mix[i])
        hy_params = (hy_w1[i], hy_b1[i], hy_fr1[i], hy_w2[i], hy_b2[i], hy_fr2[i], hy_w3[i])
        hbias = row1(hy_bias[i])
        hcw = w_hy_conv[i]
        hcb = row1(b_hy_conv[i])

        q, k4, v4, fa, ph = _project(x, sc1, sh1, g_pre, w_in_b, gq_t, gk_t, cos_x, sin_x,
                                     ones_bd, cs, rope=True, kv_only=False)
        if last:
            kc4, vc4 = _project(ctx, csc1, csh1, g_pre, w_in_b[:, k0:f0], gq_t, gk_t, cos_c, sin_c,
                                ones_bd, cs, rope=False, kv_only=True)
        else:
            qc, kc4, vc4, fac, phc = _project(ctx, csc1, csh1, g_pre, w_in_b, gq_t, gk_t, cos_c,
                                              sin_c, ones_bd, cs, rope=False, kv_only=False)
        attn_x = _attention(q, [(k4, v4), (kc4, vc4)])
        four_x = _fourier(fa, fc_x, fs_x, w_f_b)
        kre, kim = _filter_spectrum(_hyena_filter(seq, *hy_params), hc_x, hs_x)
        z, x0c = _hyena_pre(ph, hcw, hcb)
        hy_x = _hyena(z, x0c, kre, kim, hbias, hc_x, hs_x)
        x_new = _mixout(attn_x, four_x, hy_x, w_out_b, x, ga1, g_post)

        if not last:
            attn_c = _attention(qc, [(kc4, vc4)])
            four_c = _fourier(fac, fc_c, fs_c, w_f_b)
            kre_c, kim_c = _filter_spectrum(_hyena_filter(clen, *hy_params), hc_c, hs_c)
            zc, x0cc = _hyena_pre(phc, hcw, hcb)
            hy_c = _hyena(zc, x0cc, kre_c, kim_c, hbias, hc_c, hs_c)
            ctx = _mixout(attn_c, four_c, hy_c, w_out_b, ctx, cga1, g_post)
        x = x_new

        g_pf = row1(g_pre_ffn[i])
        g_of = row1(g_post_ffn[i])
        fcw = w_ffn_conv[i]
        fcb = row1(b_ffn_conv[i])
        x = _ffn(x, sc2, sh2, g_pf, w_up_b, fcw, fcb, w_down_b, ga2, g_of)
        if not last:
            ctx = _ffn(ctx, csc2, csh2, g_pf, w_up_b, fcw, fcb, w_down_b, cga2, g_of)

    return x
```

```python
import functools
import math

import numpy as np
import jax
import jax.numpy as jnp
from jax import lax
from jax.experimental import pallas as pl
from jax.experimental.pallas import tpu as pltpu

F32 = jnp.float32
BF16 = jnp.bfloat16

HEAD_DIM = 64
GQA_GROUP = 4
N_KV_HEADS = 2
GRID_W = 64
ROPE_THETA = 10000.0
FOURIER_GROUP_DIM = 64
HYENA_BANDS = 16
HYENA_FAST_DECAY = 0.3
HYENA_SLOW_DECAY = 1.5
HYENA_TARGET = 1e-2
NORM_EPS = 1e-6
LANES = 128
VMEM_LIMIT = 56 * 1024 * 1024


def _cp(sem, vmem=VMEM_LIMIT):
    return pltpu.CompilerParams(dimension_semantics=sem, vmem_limit_bytes=vmem)


def _dot(a, b):
    return jnp.dot(a, b, preferred_element_type=F32)


def _dot_nt(a, b):
    return lax.dot_general(a, b, (((1,), (1,)), ((), ())), preferred_element_type=F32)


def _split(a):
    hi = a.astype(BF16)
    lo = (a - hi.astype(F32)).astype(BF16)
    return hi, lo


def _dot3(a, b):
    ah, al = _split(a)
    bh, bl = _split(b)
    return _dot(ah, bh) + _dot(ah, bl) + _dot(al, bh)


def _rms(x, g):
    ms = jnp.mean(x * x, axis=-1, keepdims=True)
    return x * lax.rsqrt(ms + NORM_EPS) * g


def _mod_spec(mod, layer, row_fn, slot):
    return pl.BlockSpec((None, None, None, 1, mod.shape[-1]),
                        lambda bi, i: (layer, row_fn(bi), slot, 0, 0))


def _layer_spec(w, layer, col_block=None, **kw):
    if col_block is None:
        return pl.BlockSpec((None,) + w.shape[1:], lambda bi, i: (layer, 0, 0), **kw)
    start, width = col_block
    return pl.BlockSpec((None, w.shape[1], width), lambda bi, i: (layer, 0, start // width), **kw)


def _tables_kernel(hc_ref, hs_ref, fc_ref, fs_ref, hca, hsa, fca, fsa, *, n, tk):
    i = pl.program_id(0)
    hyena_step = 2.0 * math.pi / (8 * n)
    fourier_step = 2.0 * math.pi / n

    @pl.when(i == 0)
    def _():
        k = lax.broadcasted_iota(jnp.int32, (tk, n), 0)
        s = lax.broadcasted_iota(jnp.int32, (tk, n), 1)
        a = (((2 * k + 1) * (2 * s + 1)) & (8 * n - 1)).astype(F32) * hyena_step
        hca[...] = jnp.cos(a)
        hsa[...] = jnp.sin(a)
        kh = lax.broadcasted_iota(jnp.int32, (tk, n // 2), 0)
        sh = lax.broadcasted_iota(jnp.int32, (tk, n // 2), 1)
        b = ((kh * sh) & (n - 1)).astype(F32) * fourier_step
        fca[...] = jnp.cos(b)
        fsa[...] = jnp.sin(b)

    s1 = lax.broadcasted_iota(jnp.int32, (1, n), 1)
    rot = (((2 * tk * i) * (2 * s1 + 1)) & (8 * n - 1)).astype(F32) * hyena_step
    cb, sb = jnp.cos(rot), jnp.sin(rot)
    hc_ref[...] = (hca[...] * cb - hsa[...] * sb).astype(BF16)
    hs_ref[...] = (hsa[...] * cb + hca[...] * sb).astype(BF16)
    s2 = lax.broadcasted_iota(jnp.int32, (1, n // 2), 1)
    rot = (((tk * i) * s2) & (n - 1)).astype(F32) * fourier_step
    cb, sb = jnp.cos(rot), jnp.sin(rot)
    fc_ref[...] = (fca[...] * cb - fsa[...] * sb).astype(BF16)
    fs_ref[...] = (fsa[...] * cb + fca[...] * sb).astype(BF16)


def _make_tables(n):
    tk = min(n, 256)
    spec = pl.BlockSpec((tk, n), lambda i: (i, 0))
    shp = jax.ShapeDtypeStruct((n, n), BF16)
    hspec = pl.BlockSpec((tk, n // 2), lambda i: (i, 0))
    hshp = jax.ShapeDtypeStruct((n, n // 2), BF16)
    return pl.pallas_call(
        functools.partial(_tables_kernel, n=n, tk=tk),
        out_shape=(shp, shp, hshp, hshp),
        grid=(n // tk,),
        out_specs=(spec, spec, hspec, hspec),
        scratch_shapes=[pltpu.VMEM((tk, n), F32)] * 2 + [pltpu.VMEM((tk, n // 2), F32)] * 2,
        compiler_params=_cp(("arbitrary",)),
        name=f"dft_tables_{n}",
    )()


def _mod_kernel(c_ref, w_ref, b_ref, o_ref):
    c = c_ref[...]
    a = c * jax.nn.sigmoid(c)
    o_ref[0] = _dot3(a, w_ref[0]) + b_ref[0]


def _modulation(cc, w_mod, b_mod):
    depth, d, n6 = w_mod.shape
    tn = 1536
    rows = cc.shape[0]
    return pl.pallas_call(
        _mod_kernel,
        out_shape=jax.ShapeDtypeStruct((depth, rows, n6), F32),
        grid=(depth, n6 // tn),
        in_specs=[
            pl.BlockSpec((rows, d), lambda l, j: (0, 0)),
            pl.BlockSpec((1, d, tn), lambda l, j: (l, 0, j)),
            pl.BlockSpec((1, 1, tn), lambda l, j: (l, 0, j)),
        ],
        out_specs=pl.BlockSpec((1, rows, tn), lambda l, j: (l, 0, j)),
        compiler_params=_cp(("parallel", "parallel")),
        name="modulation",
    )(cc, w_mod, b_mod.reshape(depth, 1, n6))


def _head_norm(p, gain, ones_bd):
    hi, lo = _split(p * p)
    ss = _dot(hi, ones_bd) + _dot(lo, ones_bd)
    return p * lax.rsqrt(ss * (1.0 / HEAD_DIM) + NORM_EPS) * gain


def _rope(xn, cos_t, sin_t):
    lane = lax.broadcasted_iota(jnp.int32, (1, LANES), 1)
    first = (lane % 32) < 16
    outs = []
    for j in range(xn.shape[1] // LANES):
        c = xn[:, j * LANES:(j + 1) * LANES]
        sw = jnp.where(first, pltpu.roll(c, LANES - 16, 1), pltpu.roll(c, 16, 1))
        outs.append(c * cos_t + sw * sin_t)
    return outs[0] if len(outs) == 1 else jnp.concatenate(outs, axis=1)


def _tile_heads(kv):
    lane = lax.broadcasted_iota(jnp.int32, (1, LANES), 1)
    low = lane < HEAD_DIM
    r = pltpu.roll(kv, HEAD_DIM, 1)
    h0 = jnp.where(low, kv, r)
    h1 = jnp.where(low, r, kv)
    return jnp.concatenate([h0, h0], axis=1), jnp.concatenate([h1, h1], axis=1)


VALUE_ROWS = 2 * HEAD_DIM


def _value_heads(v):
    lane = lax.broadcasted_iota(jnp.int32, (1, LANES), 1)
    low = lane < HEAD_DIM
    r = pltpu.roll(v, HEAD_DIM, 1)
    return jnp.where(low, v, 1.0), jnp.where(low, r, 1.0)


def _proj_kernel(x_ref, sc_ref, sh_ref, g_ref, w_ref, gq_ref, gk_ref, cos_ref, sin_ref,
                 ones_ref, cs_ref, *outs, rope, kv_only):
    x = x_ref[0]
    h = _rms(x, g_ref[...]) * (1.0 + sc_ref[...]) + sh_ref[...]
    px = _dot(h.astype(BF16), w_ref[...])
    ones_bd = ones_ref[...]
    if kv_only:
        k4_ref, v4_ref = outs
        k = px[:, 0:128]
        v = px[:, 128:256]
    else:
        q_ref, k4_ref, v4_ref, fa_ref, ph_ref = outs
        aw = ones_bd.shape[0]
        q = _head_norm(px[:, 0:aw], gq_ref[...], ones_bd)
        if rope:
            q = _rope(q, cos_ref[...], sin_ref[...])
        q_ref[0] = (q * (HEAD_DIM ** -0.5 * math.log2(math.e))).astype(BF16)
        k = px[:, aw:aw + 128]
        v = px[:, aw + 128:aw + 256]
        f0 = aw + 256
        fw = cs_ref.shape[0]
        fa_ref[0] = _dot(px[:, f0:f0 + fw].astype(BF16), cs_ref[...]).astype(BF16)
        ph_ref[0] = px[:, f0 + fw:].astype(BF16)
    k = _head_norm(k, gk_ref[...], ones_bd[0:128, 0:128])
    if rope:
        k = _rope(k, cos_ref[...], sin_ref[...])
    k0, k1 = _tile_heads(k)
    k4_ref[0, 0] = k0.astype(BF16)
    k4_ref[0, 1] = k1.astype(BF16)
    v0, v1 = _value_heads(v)
    v4_ref[0, 0] = v0.T[0:VALUE_ROWS].astype(BF16)
    v4_ref[0, 1] = v1.T[0:VALUE_ROWS].astype(BF16)


def _project(x, sc, sh, g, w, gq, gk, cos_t, sin_t, ones_bd, cs, *, rope, kv_only):
    b, n, d = x.shape
    tm = min(n, 512)
    wn = w[0].shape[2]
    aw = ones_bd.shape[0]
    fw = cs.shape[0]
    full = lambda a: pl.BlockSpec(a.shape, lambda bi, i: (0,) * a.ndim)
    kv_spec = pl.BlockSpec((1, N_KV_HEADS, tm, 256), lambda bi, i: (bi, 0, i, 0))
    kv_shape = jax.ShapeDtypeStruct((b, N_KV_HEADS, n, 256), BF16)
    vl_spec = pl.BlockSpec((1, N_KV_HEADS, VALUE_ROWS, tm), lambda bi, i: (bi, 0, 0, i))
    vl_shape = jax.ShapeDtypeStruct((b, N_KV_HEADS, VALUE_ROWS, n), BF16)
    row = lambda width: pl.BlockSpec((1, tm, width), lambda bi, i: (bi, i, 0))
    if kv_only:
        out_shape = (kv_shape, vl_shape)
        out_specs = (kv_spec, vl_spec)
    else:
        hw = wn - aw - 256 - fw
        out_shape = (jax.ShapeDtypeStruct((b, n, aw), BF16), kv_shape, vl_shape,
                     jax.ShapeDtypeStruct((b, n, 2 * fw), BF16),
                     jax.ShapeDtypeStruct((b, n, hw), BF16))
        out_specs = (row(aw), kv_spec, vl_spec, row(2 * fw), row(hw))
    return pl.pallas_call(
        functools.partial(_proj_kernel, rope=rope, kv_only=kv_only),
        out_shape=out_shape,
        grid=(b, n // tm),
        in_specs=[
            row(d), _mod_spec(*sc), _mod_spec(*sh), full(g), _layer_spec(*w), full(gq), full(gk),
            pl.BlockSpec((tm, LANES), lambda bi, i: (i, 0)),
            pl.BlockSpec((tm, LANES), lambda bi, i: (i, 0)),
            full(ones_bd), full(cs),
        ],
        out_specs=out_specs,
        compiler_params=_cp(("parallel", "parallel")),
        name="proj_kv" if kv_only else ("proj_rope" if rope else "proj_ctx"),
    )(x, sc[0], sh[0], g, w[0], gq, gk, cos_t, sin_t, ones_bd, cs)


ATTN_SUB = 128
ATTN_KEY_CHUNK = 1024


ATTN_PROB_ROWS = 128


def _attn_kernel(q_ref, *refs, lks, nsub):
    nsrc = len(lks)
    k_refs = refs[0:2 * nsrc:2]
    v_refs = refs[1:2 * nsrc:2]
    o_ref = refs[2 * nsrc]
    s_scrs = refs[2 * nsrc + 1:2 * nsrc + 3]
    p_scrs = refs[2 * nsrc + 3:2 * nsrc + 5]
    sub = ATTN_SUB
    rows = GQA_GROUP * sub
    gw = GQA_GROUP * HEAD_DIM
    group = lax.broadcasted_iota(jnp.int32, (1, gw), 1) // HEAD_DIM

    chunks = []
    off = 0
    for j, lk in enumerate(lks):
        tkc = min(lk, ATTN_KEY_CHUNK)
        for c in range(lk // tkc):
            chunks.append((j, c * tkc, tkc, off))
            off += tkc

    lane = lax.broadcasted_iota(jnp.int32, (1, LANES), 1)
    low = lane < HEAD_DIM

    def masked_q(u):
        q = q_ref[0, u * sub:(u + 1) * sub, :]
        zero = jnp.zeros_like(q)
        return jnp.concatenate([jnp.where(group == g, q, zero) for g in range(GQA_GROUP)], axis=0)

    def finish(u, acc):
        o_t = acc / acc[HEAD_DIM:HEAD_DIM + 1, :]
        heads = [o_t[:, g * sub:(g + 1) * sub].T for g in range(GQA_GROUP)]
        left = jnp.where(low, heads[0], pltpu.roll(heads[1], HEAD_DIM, 1))
        right = jnp.where(low, heads[2], pltpu.roll(heads[3], HEAD_DIM, 1))
        o_ref[0, u * sub:(u + 1) * sub, :] = jnp.concatenate([left, right], axis=1).astype(BF16)

    def scores(u):
        s_scr = s_scrs[u % 2]
        qm = masked_q(u)
        for j, start, tkc, o in chunks:
            s_scr[o:o + tkc, :] = _dot_nt(k_refs[j][0, 0, start:start + tkc, :], qm)
        return jnp.max(s_scr[...], axis=0, keepdims=True)

    m = scores(0)
    for u in range(nsub):
        s_scr = s_scrs[u % 2]
        acc = jnp.zeros((VALUE_ROWS, rows), F32)
        if u + 1 < nsub:
            m_next = scores(u + 1)
            bits = pltpu.bitcast(m_next, jnp.uint32)
            acc = acc + ((bits >> 16) >> 16).astype(F32)
        for ci, (j, start, tkc, o) in enumerate(chunks):
            p_scr = p_scrs[ci % 2]
            for r in range(0, tkc, ATTN_PROB_ROWS):
                p_scr[r:r + ATTN_PROB_ROWS, :] = jnp.exp2(
                    s_scr[o + r:o + r + ATTN_PROB_ROWS, :] - m).astype(BF16)
            acc = acc + _dot(v_refs[j][0, 0, :, start:start + tkc], p_scr[0:tkc, :])
        finish(u, acc)
        if u + 1 < nsub:
            m = m_next


def _attention(q, kvs):
    b, lq, aw = q.shape
    tq = min(lq, 8 * ATTN_SUB)
    nsub = tq // ATTN_SUB
    lks = tuple(k.shape[2] for k, _ in kvs)
    gw = GQA_GROUP * HEAD_DIM
    in_specs = [pl.BlockSpec((1, tq, gw), lambda bi, h, i: (bi, i, h))]
    args = [q]
    for k4, vlt in kvs:
        for a in (k4, vlt):
            in_specs.append(pl.BlockSpec((1, 1) + a.shape[2:], lambda bi, h, i: (bi, h, 0, 0),
                                         pipeline_mode=pl.Buffered(1)))
        args += [k4, vlt]
    return pl.pallas_call(
        functools.partial(_attn_kernel, lks=lks, nsub=nsub),
        out_shape=jax.ShapeDtypeStruct((b, lq, aw), BF16),
        grid=(b, N_KV_HEADS, lq // tq),
        in_specs=in_specs,
        out_specs=pl.BlockSpec((1, tq, gw), lambda bi, h, i: (bi, i, h)),
        scratch_shapes=[pltpu.VMEM((sum(lks), GQA_GROUP * ATTN_SUB), F32)] * 2
        + [pltpu.VMEM((min(max(lks), ATTN_KEY_CHUNK), GQA_GROUP * ATTN_SUB), BF16)] * 2,
        compiler_params=_cp(("parallel", "parallel", "arbitrary")),
        name=f"attention_{lq}",
    )(*args)


def _fourier_kernel(c_ref, s_ref, fa_ref, w_ref, o_ref, fold_scr, nyq_scr, *, nb, scale, tk):
    fw = w_ref.shape[0]
    n = fa_ref.shape[1]
    half = n // 2
    blk = min(half, 512)

    @pl.when(pl.program_id(1) == 0)
    def _():
        r = lax.broadcasted_iota(jnp.int32, (blk, blk), 0)
        c = lax.broadcasted_iota(jnp.int32, (blk, blk), 1)
        flip = jnp.where(r + c == blk - 1, 1.0, 0.0).astype(BF16)
        row = lax.broadcasted_iota(jnp.int32, (half, 1), 0)
        sign = jnp.where(lax.broadcasted_iota(jnp.int32, (1, 2 * fw), 1) < fw, 1.0, -1.0)
        for j in range(nb):
            rev = jnp.concatenate(
                [_dot(flip, fa_ref[j, n - (b + 1) * blk:n - b * blk, :]) for b in range(half // blk)],
                axis=0)
            mirrored = jnp.where(row == 0, 0.0, pltpu.roll(rev, 1, 0))
            fold_scr[j] = (fa_ref[j, 0:half, :].astype(F32) + sign * mirrored).astype(BF16)
            nyq_scr[j] = jnp.broadcast_to(fa_ref[j, half:half + 16, 0:fw][0:1].astype(F32), (8, fw))

    ct = c_ref[...]
    st = s_ref[...]
    odd = (lax.broadcasted_iota(jnp.int32, (tk, 1), 0) & 1) == 1
    alt = jnp.where(odd, -1.0, 1.0)
    for j in range(nb):
        y = (_dot(ct, fold_scr[j, :, 0:fw]) - _dot(st, fold_scr[j, :, fw:2 * fw])
             + alt * nyq_scr[j, 0:1, :])
        o_ref[j] = _dot((y * scale).astype(BF16), w_ref[...]).astype(BF16)


def _fourier(fa, ctab, stab, w_f):
    b, n, fw2 = fa.shape
    fw = fw2 // 2
    nb = 2
    tk = min(n, 512)
    half = n // 2
    scale = 1.0 / math.sqrt(n * FOURIER_GROUP_DIM)
    return pl.pallas_call(
        functools.partial(_fourier_kernel, nb=nb, scale=scale, tk=tk),
        out_shape=jax.ShapeDtypeStruct((b, n, fw), BF16),
        grid=(b // nb, n // tk),
        in_specs=[
            pl.BlockSpec((tk, half), lambda g, i: (i, 0)),
            pl.BlockSpec((tk, half), lambda g, i: (i, 0)),
            pl.BlockSpec((nb, n, fw2), lambda g, i: (g, 0, 0)),
            pl.BlockSpec((fw, fw), lambda g, i: (0, 0)),
        ],
        out_specs=pl.BlockSpec((nb, tk, fw), lambda g, i: (g, i, 0)),
        scratch_shapes=[pltpu.VMEM((nb, half, fw2), BF16), pltpu.VMEM((nb, 8, fw), F32)],
        compiler_params=_cp(("parallel", "arbitrary")),
        name=f"fourier_{n}",
    )(ctab, stab, fa, w_f)


def _conv3(p, w, bias):
    n = p.shape[0]
    row = lax.broadcasted_iota(jnp.int32, (n, 1), 0)
    prev = jnp.where(row == 0, 0.0, pltpu.roll(p, 1, 0))
    nxt = jnp.where(row == n - 1, 0.0, pltpu.roll(p, n - 1, 0))
    return prev * w[0:1] + p * w[1:2] + nxt * w[2:3] + bias


def _hyena_pre_kernel(p0_ref, p1_ref, p2_ref, w0_ref, w1_ref, w2_ref, b0_ref, b1_ref, b2_ref,
                      z_ref, x0_ref):
    x0 = _conv3(p0_ref[0].astype(F32), w0_ref[...], b0_ref[...])
    x1 = _conv3(p1_ref[0].astype(F32), w1_ref[...], b1_ref[...])
    v = _conv3(p2_ref[0].astype(F32), w2_ref[...], b2_ref[...])
    z_ref[0] = (x1 * v).astype(BF16)
    x0_ref[0] = x0.astype(BF16)


def _hyena_pre(ph, w, bias):
    b, n, hw3 = ph.shape
    hw = hw3 // 3
    nc = hw // LANES
    pspec = lambda s: pl.BlockSpec((1, n, LANES), lambda bi, c: (bi, 0, s * nc + c))
    wspec = lambda s: pl.BlockSpec((3, LANES), lambda bi, c: (0, s * nc + c))
    bspec = lambda s: pl.BlockSpec((1, LANES), lambda bi, c: (0, s * nc + c))
    ospec = pl.BlockSpec((1, n, LANES), lambda bi, c: (bi, 0, c))
    oshape = jax.ShapeDtypeStruct((b, n, hw), BF16)
    return pl.pallas_call(
        _hyena_pre_kernel,
        out_shape=(oshape, oshape),
        grid=(b, nc),
        in_specs=[pspec(0), pspec(1), pspec(2), wspec(0), wspec(1), wspec(2),
                  bspec(0), bspec(1), bspec(2)],
        out_specs=(ospec, ospec),
        compiler_params=_cp(("parallel", "parallel")),
        name=f"hyena_pre_{n}",
    )(ph, ph, ph, w, w, w, bias, bias, bias)


def _filter_kernel(w1t_ref, w1c_ref, w1s_ref, b1_ref, fr1_ref, w2_ref, b2_ref, fr2_ref, w3_ref,
                   g_ref, *, n):
    hw = w3_ref.shape[1] // 2
    i = lax.broadcasted_iota(jnp.int32, (n, 1), 0).astype(F32)
    t = i / float(n - 1)
    jb = lax.broadcasted_iota(jnp.int32, (1, HYENA_BANDS), 1).astype(F32)
    bands = 1e-4 + jb * ((HYENA_BANDS - 1 - 1e-4) / (HYENA_BANDS - 1))
    ang = ((2.0 * math.pi) * i / float(n)) * bands
    pre = t * w1t_ref[...] + _dot3(jnp.cos(ang), w1c_ref[...]) - _dot3(jnp.sin(ang), w1s_ref[...])
    h = jnp.sin(fr1_ref[...] * (pre + b1_ref[...]))
    h = jnp.sin(fr2_ref[...] * (_dot3(h, w2_ref[...]) + b2_ref[...]))
    h = _dot3(h, w3_ref[...])
    d0 = math.log(HYENA_TARGET) / HYENA_SLOW_DECAY
    d1 = math.log(HYENA_TARGET) / HYENA_FAST_DECAY
    jd = lax.broadcasted_iota(jnp.int32, (1, hw), 1).astype(F32)
    deltas = jnp.abs(d0 + jd * ((d1 - d0) / (hw - 1)))
    decay = jnp.exp(-t * deltas)
    hf = h[:, 0:hw] * decay
    hb = jnp.where(i == 0.0, 0.0, h[:, hw:2 * hw] * decay)
    total = jnp.sum(jnp.abs(hf), axis=0, keepdims=True) + jnp.sum(jnp.abs(hb), axis=0, keepdims=True)
    g_ref[...] = jnp.concatenate([hf / total, hb / total], axis=1).astype(BF16)


def _hyena_filter(n, w1, b1, fr1, w2, b2, fr2, w3):
    nb = HYENA_BANDS
    r = lambda a: a.reshape(1, -1)
    shp = jax.ShapeDtypeStruct((n, w3.shape[1]), BF16)
    return pl.pallas_call(
        functools.partial(_filter_kernel, n=n),
        out_shape=shp,
        compiler_params=pltpu.CompilerParams(vmem_limit_bytes=VMEM_LIMIT),
        name=f"hyena_filter_{n}",
    )(w1[0:1], w1[1:1 + nb], w1[1 + nb:1 + 2 * nb], r(b1), r(fr1), w2, r(b2), r(fr2), w3)


def _spectrum_kernel(c_ref, s_ref, g_ref, ck_ref, sk_ref, kre_ref, kim_ref, *, scale):
    hw = kre_ref.shape[1]
    a = _dot(c_ref[...], g_ref[...])
    b = _dot(s_ref[...], g_ref[...])
    are, bre = a[:, 0:hw], a[:, hw:2 * hw]
    aim, bim = -b[:, 0:hw], -b[:, hw:2 * hw]
    ck = ck_ref[...]
    sk = sk_ref[...]
    kre_ref[...] = (ck * (are + bre) - sk * (aim + bim)) * scale
    kim_ref[...] = (ck * (aim - bim) + sk * (are - bre)) * scale


def _filter_spectrum(g, ctab, stab):
    n, hw2 = g.shape
    hw = hw2 // 2
    tk = min(n, 512)
    half = (2.0 * np.arange(n, dtype=np.float64) + 1.0) * (2.0 * np.pi / (8 * n))
    ck = jnp.asarray(np.cos(half).reshape(n, 1), F32)
    sk = jnp.asarray(np.sin(half).reshape(n, 1), F32)
    tspec = pl.BlockSpec((tk, n), lambda i: (i, 0))
    gspec = pl.BlockSpec((n, hw2), lambda i: (0, 0))
    vspec = pl.BlockSpec((tk, 1), lambda i: (i, 0))
    ospec = pl.BlockSpec((tk, hw), lambda i: (i, 0))
    oshape = jax.ShapeDtypeStruct((n, hw), F32)
    return pl.pallas_call(
        functools.partial(_spectrum_kernel, scale=1.0 / n),
        out_shape=(oshape, oshape),
        grid=(n // tk,),
        in_specs=[tspec, tspec, gspec, vspec, vspec],
        out_specs=(ospec, ospec),
        compiler_params=_cp(("parallel",)),
        name=f"filter_spectrum_{n}",
    )(ctab, stab, g, ck, sk)


def _hyena_kernel(c_ref, s_ref, z_ref, zt_ref, x0_ref, kre_ref, kim_ref, bias_ref, o_ref,
                  yre_scr, yim_scr, *, nb, tk):
    phase = pl.program_id(1)
    i = pl.program_id(2)
    ct = c_ref[...]
    st = s_ref[...]

    @pl.when(phase == 0)
    def _():
        kre = kre_ref[...]
        kim = kim_ref[...]
        r0 = pl.multiple_of(i * tk, tk)
        for j in range(nb):
            ure = _dot(ct, z_ref[j])
            uim = -_dot(st, z_ref[j])
            yre_scr[j, pl.ds(r0, tk), :] = (kre * ure - kim * uim).astype(BF16)
            yim_scr[j, pl.ds(r0, tk), :] = (kre * uim + kim * ure).astype(BF16)

    @pl.when(phase == 1)
    def _():
        for j in range(nb):
            y = _dot(ct, yre_scr[j]) - _dot(st, yim_scr[j])
            u = zt_ref[j].astype(F32)
            o_ref[j] = (x0_ref[j].astype(F32) * (y + u * bias_ref[...])).astype(BF16)


def _hyena(z, x0c, kre, kim, bias, ctab, stab):
    b, n, hw = z.shape
    nb = 2
    tk = min(n, 512)
    tspec = pl.BlockSpec((tk, n), lambda g, p, i: (i, 0))
    tile = pl.BlockSpec((nb, tk, hw), lambda g, p, i: (g, i * p, 0))
    kspec = pl.BlockSpec((tk, hw), lambda g, p, i: (i * (1 - p), 0))
    return pl.pallas_call(
        functools.partial(_hyena_kernel, nb=nb, tk=tk),
        out_shape=jax.ShapeDtypeStruct((b, n, hw), BF16),
        grid=(b // nb, 2, n // tk),
        in_specs=[
            tspec, tspec,
            pl.BlockSpec((nb, n, hw), lambda g, p, i: (g, 0, 0)),
            tile, tile, kspec, kspec,
            pl.BlockSpec((1, hw), lambda g, p, i: (0, 0)),
        ],
        out_specs=tile,
        scratch_shapes=[pltpu.VMEM((nb, n, hw), BF16), pltpu.VMEM((nb, n, hw), BF16)],
        compiler_params=_cp(("parallel", "arbitrary", "arbitrary")),
        name=f"hyena_conv_{n}",
    )(ctab, stab, z, z, x0c, kre, kim, bias)


def _mixout_kernel(a_ref, f_ref, h_ref, w_ref, x_ref, ga_ref, g_ref, o_ref):
    aw = a_ref.shape[2]
    fw = f_ref.shape[2]
    mix = (_dot(a_ref[0], w_ref[0:aw, :]) + _dot(f_ref[0], w_ref[aw:aw + fw, :])
           + _dot(h_ref[0], w_ref[aw + fw:, :]))
    o_ref[0] = x_ref[0] + ga_ref[...] * _rms(mix, g_ref[...])


def _mixout(attn, four, hy, w_out, x, ga, g):
    b, n, d = x.shape
    tm = min(n, 1024)
    row = lambda width: pl.BlockSpec((1, tm, width), lambda bi, i: (bi, i, 0))
    return pl.pallas_call(
        _mixout_kernel,
        out_shape=jax.ShapeDtypeStruct((b, n, d), F32),
        grid=(b, n // tm),
        in_specs=[
            row(attn.shape[2]), row(four.shape[2]), row(hy.shape[2]),
            _layer_spec(*w_out),
            row(d),
            _mod_spec(*ga),
            pl.BlockSpec((1, d), lambda bi, i: (0, 0)),
        ],
        out_specs=row(d),
        compiler_params=_cp(("parallel", "parallel")),
        name=f"mixout_{n}",
    )(attn, four, hy, w_out[0], x, ga[0], g)


HALO = 8


FFN_CHUNK = 256


def _ffn_kernel(xp_ref, x_ref, xn_ref, sc_ref, sh_ref, g_ref, wu_ref, cw_ref, cb_ref, wd_ref,
                ga_ref, go_ref, o_ref, *, tm):
    i = pl.program_id(1)
    nrow = pl.num_programs(1)
    rows = tm + 2 * HALO
    dff = wd_ref.shape[0]
    cf = FFN_CHUNK
    g = g_ref[...]
    sc = 1.0 + sc_ref[...]
    sh = sh_ref[...]
    keep_p = jnp.where(i > 0, 1.0, 0.0)
    keep_n = jnp.where(i < nrow - 1, 1.0, 0.0)
    x = x_ref[0]
    fx = jnp.concatenate([(_rms(xp_ref[0], g) * sc + sh) * keep_p,
                          _rms(x, g) * sc + sh,
                          (_rms(xn_ref[0], g) * sc + sh) * keep_n], axis=0).astype(BF16)

    def conv(u, w, bias):
        prev = pltpu.roll(u, 1, 0)[HALO:HALO + tm]
        nxt = pltpu.roll(u, rows - 1, 0)[HALO:HALO + tm]
        return prev * w[0:1] + u[HALO:HALO + tm] * w[1:2] + nxt * w[2:3] + bias

    acts = []
    for c in range(dff // cf):
        lo, hi = c * cf, (c + 1) * cf
        gate = conv(_dot(fx, wu_ref[:, lo:hi]), cw_ref[:, lo:hi], cb_ref[:, lo:hi])
        val = conv(_dot(fx, wu_ref[:, dff + lo:dff + hi]), cw_ref[:, dff + lo:dff + hi],
                   cb_ref[:, dff + lo:dff + hi])
        acts.append((gate * jax.nn.sigmoid(gate) * val).astype(BF16))
    y = _dot(jnp.concatenate(acts, axis=1), wd_ref[...])
    o_ref[0] = x + ga_ref[...] * _rms(y, go_ref[...])


def _ffn(x, sc, sh, g, w_up, w_conv, b_conv, w_down, ga, g_post):
    b, n, d = x.shape
    tm = min(n, 512)
    hb = tm // HALO
    nhb = n // HALO
    gvec = pl.BlockSpec((1, d), lambda bi, i: (0, 0))
    resident = lambda a: pl.BlockSpec(a.shape, lambda bi, i: (0, 0), pipeline_mode=pl.Buffered(1))
    stacked = lambda w: _layer_spec(*w, pipeline_mode=pl.Buffered(1))
    return pl.pallas_call(
        functools.partial(_ffn_kernel, tm=tm),
        out_shape=jax.ShapeDtypeStruct((b, n, d), F32),
        grid=(b, n // tm),
        in_specs=[
            pl.BlockSpec((1, HALO, d), lambda bi, i: (bi, jnp.maximum(i * hb - 1, 0), 0)),
            pl.BlockSpec((1, tm, d), lambda bi, i: (bi, i, 0)),
            pl.BlockSpec((1, HALO, d), lambda bi, i: (bi, jnp.minimum((i + 1) * hb, nhb - 1), 0)),
            _mod_spec(*sc), _mod_spec(*sh), gvec,
            stacked(w_up), resident(w_conv), resident(b_conv), stacked(w_down),
            _mod_spec(*ga), gvec,
        ],
        out_specs=pl.BlockSpec((1, tm, d), lambda bi, i: (bi, i, 0)),
        compiler_params=_cp(("parallel", "parallel")),
        name=f"conv_ffn_{n}",
    )(x, x, x, sc[0], sh[0], g, w_up[0], w_conv, b_conv, w_down[0], ga[0], g_post)


def _rope_tables(n):
    half = HEAD_DIM // 4
    inv = ROPE_THETA ** (-jnp.arange(0, 2 * half, 2, dtype=F32) / (2 * half))
    pos = jnp.arange(n, dtype=jnp.int32)
    row = (pos // GRID_W).astype(F32)
    col = (pos % GRID_W).astype(F32)
    ang_r = row[:, None] * inv[None, :]
    ang_c = col[:, None] * inv[None, :]
    cr, sr, cc, sc = jnp.cos(ang_r), jnp.sin(ang_r), jnp.cos(ang_c), jnp.sin(ang_c)
    cos64 = jnp.concatenate([cr, cr, cc, cc], axis=-1)
    sin64 = jnp.concatenate([-sr, sr, -sc, sc], axis=-1)
    return jnp.tile(cos64, (1, 2)), jnp.tile(sin64, (1, 2))


def _head_ones(width):
    idx = np.arange(width) // HEAD_DIM
    return jnp.asarray((idx[:, None] == idx[None, :]).astype(np.float32), BF16)


def _channel_dft(width):
    c = np.arange(width)
    same = (c[:, None] // FOURIER_GROUP_DIM) == (c[None, :] // FOURIER_GROUP_DIM)
    ang = 2.0 * np.pi * ((c[:, None] % FOURIER_GROUP_DIM) * (c[None, :] % FOURIER_GROUP_DIM)
                         % FOURIER_GROUP_DIM) / FOURIER_GROUP_DIM
    cs = np.concatenate([np.where(same, np.cos(ang), 0.0), np.where(same, np.sin(ang), 0.0)], axis=1)
    return jnp.asarray(cs.astype(np.float32), BF16)


def kernel(x, c, ctx, c_ctx, w_mod, b_mod, g_pre_mix, g_post_mix, g_pre_ffn, g_post_ffn, w_in, g_q, g_k, w_fourier, w_hy_conv, b_hy_conv, hy_w1, hy_b1, hy_fr1, hy_w2, hy_b2, hy_fr2, hy_w3, hy_bias, w_out, w_up, w_ffn_conv, b_ffn_conv, w_down):
    bsz, seq, d = x.shape
    clen = ctx.shape[1]
    depth = w_mod.shape[0]
    fw = w_fourier.shape[1]
    hw = hy_bias.shape[1]
    aw = w_in.shape[2] - 2 * N_KV_HEADS * HEAD_DIM - fw - 3 * hw
    k0 = aw
    f0 = aw + 2 * N_KV_HEADS * HEAD_DIM

    nrows = -(-(bsz + 1) // 8) * 8
    cc = jnp.zeros((nrows, d), F32).at[:bsz].set(c).at[bsz].set(c_ctx)
    mods = _modulation(cc, w_mod, b_mod).reshape(depth, nrows, 6, 1, d)
    w_in_b = w_in.astype(BF16)
    w_out_b = w_out.astype(BF16)
    w_up_b = w_up.astype(BF16)
    w_down_b = w_down.astype(BF16)

    cos_x, sin_x = _rope_tables(seq)
    cos_c = jnp.ones((clen, LANES), F32)
    sin_c = jnp.zeros((clen, LANES), F32)
    ones_bd = _head_ones(aw)
    cs = _channel_dft(fw)
    hc_x, hs_x, fc_x, fs_x = _make_tables(seq)
    hc_c, hs_c, fc_c, fs_c = _make_tables(clen)

    def row1(v):
        return v.reshape(1, -1)

    for i in range(depth):
        last = i == depth - 1
        sh1, sc1, ga1, sh2, sc2, ga2 = ((mods, i, lambda bi: bi, t) for t in range(6))
        csh1, csc1, cga1, csh2, csc2, cga2 = ((mods, i, lambda bi: bsz, t) for t in range(6))
        win, wout, wup, wdown = ((w, i, None) for w in (w_in_b, w_out_b, w_up_b, w_down_b))
        w_f_b = w_fourier[i].astype(BF16)
        gq_t = jnp.tile(g_q[i], aw // HEAD_DIM).reshape(1, aw)
        gk_t = jnp.tile(g_k[i], N_KV_HEADS).reshape(1, N_KV_HEADS * HEAD_DIM)
        g_pre = row1(g_pre_mix[i])
        g_post = row1(g_post_mix[i])
        hy_params = (hy_w1[i], hy_b1[i], hy_fr1[i], hy_w2[i], hy_b2[i], hy_fr2[i], hy_w3[i])
        hbias = row1(hy_bias[i])
        hcw = w_hy_conv[i]
        hcb = row1(b_hy_conv[i])

        q, k4, v4, fa, ph = _project(x, sc1, sh1, g_pre, win, gq_t, gk_t, cos_x, sin_x,
                                     ones_bd, cs, rope=True, kv_only=False)
        if last:
            kc4, vc4 = _project(ctx, csc1, csh1, g_pre, (w_in_b, i, (k0, f0 - k0)), gq_t, gk_t,
                                cos_c, sin_c, ones_bd, cs, rope=False, kv_only=True)
        else:
            qc, kc4, vc4, fac, phc = _project(ctx, csc1, csh1, g_pre, win, gq_t, gk_t, cos_c,
                                              sin_c, ones_bd, cs, rope=False, kv_only=False)
        attn_x = _attention(q, [(k4, v4), (kc4, vc4)])
        four_x = _fourier(fa, fc_x, fs_x, w_f_b)
        kre, kim = _filter_spectrum(_hyena_filter(seq, *hy_params), hc_x, hs_x)
        z, x0c = _hyena_pre(ph, hcw, hcb)
        hy_x = _hyena(z, x0c, kre, kim, hbias, hc_x, hs_x)
        x_new = _mixout(attn_x, four_x, hy_x, wout, x, ga1, g_post)

        if not last:
            attn_c = _attention(qc, [(kc4, vc4)])
            four_c = _fourier(fac, fc_c, fs_c, w_f_b)
            kre_c, kim_c = _filter_spectrum(_hyena_filter(clen, *hy_params), hc_c, hs_c)
            zc, x0cc = _hyena_pre(phc, hcw, hcb)
            hy_c = _hyena(zc, x0cc, kre_c, kim_c, hbias, hc_c, hs_c)
            ctx = _mixout(attn_c, four_c, hy_c, wout, ctx, cga1, g_post)
        x = x_new

        g_pf = row1(g_pre_ffn[i])
        g_of = row1(g_post_ffn[i])
        fcw = w_ffn_conv[i]
        fcb = row1(b_ffn_conv[i])
        x = _ffn(x, sc2, sh2, g_pf, wup, fcw, fcb, wdown, ga2, g_of)
        if not last:
            ctx = _ffn(ctx, csc2, csh2, g_pf, wup, fcw, fcb, wdown, cga2, g_of)

    return x
```

```python
import functools
import math

import numpy as np
import jax
import jax.numpy as jnp
from jax import lax
from jax.experimental import pallas as pl
from jax.experimental.pallas import tpu as pltpu

F32 = jnp.float32
BF16 = jnp.bfloat16

HEAD_DIM = 64
GQA_GROUP = 4
N_KV_HEADS = 2
GRID_W = 64
ROPE_THETA = 10000.0
FOURIER_GROUP_DIM = 64
HYENA_BANDS = 16
HYENA_FAST_DECAY = 0.3
HYENA_SLOW_DECAY = 1.5
HYENA_TARGET = 1e-2
NORM_EPS = 1e-6
LANES = 128
VMEM_LIMIT = 56 * 1024 * 1024


def _cp(sem, vmem=VMEM_LIMIT):
    return pltpu.CompilerParams(dimension_semantics=sem, vmem_limit_bytes=vmem)


def _dot(a, b):
    return jnp.dot(a, b, preferred_element_type=F32)


def _dot_nt(a, b):
    return lax.dot_general(a, b, (((1,), (1,)), ((), ())), preferred_element_type=F32)


def _split(a):
    hi = a.astype(BF16)
    lo = (a - hi.astype(F32)).astype(BF16)
    return hi, lo


def _dot3(a, b):
    ah, al = _split(a)
    bh, bl = _split(b)
    return _dot(ah, bh) + _dot(ah, bl) + _dot(al, bh)


def _rms(x, g):
    ms = jnp.mean(x * x, axis=-1, keepdims=True)
    return x * lax.rsqrt(ms + NORM_EPS) * g


def _mod_spec(mod, layer, row_fn, slot):
    return pl.BlockSpec((None, None, None, 1, mod.shape[-1]),
                        lambda bi, i: (layer, row_fn(bi), slot, 0, 0))


def _layer_spec(w, layer, col_block=None, **kw):
    if col_block is None:
        return pl.BlockSpec((None,) + w.shape[1:], lambda bi, i: (layer, 0, 0), **kw)
    start, width = col_block
    return pl.BlockSpec((None, w.shape[1], width), lambda bi, i: (layer, 0, start // width), **kw)


def _tables_kernel(hc_ref, hs_ref, fc_ref, fs_ref, hca, hsa, fca, fsa, *, n, tk):
    i = pl.program_id(0)
    hyena_step = 2.0 * math.pi / (8 * n)
    fourier_step = 2.0 * math.pi / n

    @pl.when(i == 0)
    def _():
        k = lax.broadcasted_iota(jnp.int32, (tk, n), 0)
        s = lax.broadcasted_iota(jnp.int32, (tk, n), 1)
        a = (((2 * k + 1) * (2 * s + 1)) & (8 * n - 1)).astype(F32) * hyena_step
        hca[...] = jnp.cos(a)
        hsa[...] = jnp.sin(a)
        kh = lax.broadcasted_iota(jnp.int32, (tk, n // 2), 0)
        sh = lax.broadcasted_iota(jnp.int32, (tk, n // 2), 1)
        b = ((kh * sh) & (n - 1)).astype(F32) * fourier_step
        fca[...] = jnp.cos(b)
        fsa[...] = jnp.sin(b)

    s1 = lax.broadcasted_iota(jnp.int32, (1, n), 1)
    rot = (((2 * tk * i) * (2 * s1 + 1)) & (8 * n - 1)).astype(F32) * hyena_step
    cb, sb = jnp.cos(rot), jnp.sin(rot)
    hc_ref[...] = (hca[...] * cb - hsa[...] * sb).astype(BF16)
    hs_ref[...] = (hsa[...] * cb + hca[...] * sb).astype(BF16)
    s2 = lax.broadcasted_iota(jnp.int32, (1, n // 2), 1)
    rot = (((tk * i) * s2) & (n - 1)).astype(F32) * fourier_step
    cb, sb = jnp.cos(rot), jnp.sin(rot)
    fc_ref[...] = (fca[...] * cb - fsa[...] * sb).astype(BF16)
    fs_ref[...] = (fsa[...] * cb + fca[...] * sb).astype(BF16)


def _make_tables(n):
    tk = min(n, 256)
    spec = pl.BlockSpec((tk, n), lambda i: (i, 0))
    shp = jax.ShapeDtypeStruct((n, n), BF16)
    hspec = pl.BlockSpec((tk, n // 2), lambda i: (i, 0))
    hshp = jax.ShapeDtypeStruct((n, n // 2), BF16)
    return pl.pallas_call(
        functools.partial(_tables_kernel, n=n, tk=tk),
        out_shape=(shp, shp, hshp, hshp),
        grid=(n // tk,),
        out_specs=(spec, spec, hspec, hspec),
        scratch_shapes=[pltpu.VMEM((tk, n), F32)] * 2 + [pltpu.VMEM((tk, n // 2), F32)] * 2,
        compiler_params=_cp(("arbitrary",)),
        name=f"dft_tables_{n}",
    )()


def _mod_kernel(c_ref, w_ref, b_ref, o_ref):
    c = c_ref[...]
    a = c * jax.nn.sigmoid(c)
    o_ref[0] = _dot3(a, w_ref[0]) + b_ref[0]


def _modulation(cc, w_mod, b_mod):
    depth, d, n6 = w_mod.shape
    tn = 1536
    rows = cc.shape[0]
    return pl.pallas_call(
        _mod_kernel,
        out_shape=jax.ShapeDtypeStruct((depth, rows, n6), F32),
        grid=(depth, n6 // tn),
        in_specs=[
            pl.BlockSpec((rows, d), lambda l, j: (0, 0)),
            pl.BlockSpec((1, d, tn), lambda l, j: (l, 0, j)),
            pl.BlockSpec((1, 1, tn), lambda l, j: (l, 0, j)),
        ],
        out_specs=pl.BlockSpec((1, rows, tn), lambda l, j: (l, 0, j)),
        compiler_params=_cp(("parallel", "parallel")),
        name="modulation",
    )(cc, w_mod, b_mod.reshape(depth, 1, n6))


def _head_norm(p, gain, ones_bd):
    hi, lo = _split(p * p)
    ss = _dot(hi, ones_bd) + _dot(lo, ones_bd)
    return p * lax.rsqrt(ss * (1.0 / HEAD_DIM) + NORM_EPS) * gain


def _rope(xn, cos_t, sin_t):
    lane = lax.broadcasted_iota(jnp.int32, (1, LANES), 1)
    first = (lane % 32) < 16
    outs = []
    for j in range(xn.shape[1] // LANES):
        c = xn[:, j * LANES:(j + 1) * LANES]
        sw = jnp.where(first, pltpu.roll(c, LANES - 16, 1), pltpu.roll(c, 16, 1))
        outs.append(c * cos_t + sw * sin_t)
    return outs[0] if len(outs) == 1 else jnp.concatenate(outs, axis=1)


def _tile_heads(kv):
    lane = lax.broadcasted_iota(jnp.int32, (1, LANES), 1)
    low = lane < HEAD_DIM
    r = pltpu.roll(kv, HEAD_DIM, 1)
    h0 = jnp.where(low, kv, r)
    h1 = jnp.where(low, r, kv)
    return jnp.concatenate([h0, h0], axis=1), jnp.concatenate([h1, h1], axis=1)


VALUE_ROWS = 2 * HEAD_DIM


def _value_heads(v):
    lane = lax.broadcasted_iota(jnp.int32, (1, LANES), 1)
    low = lane < HEAD_DIM
    r = pltpu.roll(v, HEAD_DIM, 1)
    return jnp.where(low, v, 1.0), jnp.where(low, r, 1.0)


PROJ_SUB = 512


def _proj_kernel(x_ref, sc_ref, sh_ref, g_ref, w_ref, gq_ref, gk_ref, cos_ref, sin_ref,
                 ones_ref, cs_ref, *outs, rope, kv_only):
    tm = x_ref.shape[1]
    sub = min(tm, PROJ_SUB)
    ones_bd = ones_ref[...]
    for r0 in range(0, tm, sub):
        rs = slice(r0, r0 + sub)
        x = x_ref[0, rs, :]
        h = _rms(x, g_ref[...]) * (1.0 + sc_ref[...]) + sh_ref[...]
        px = _dot(h.astype(BF16), w_ref[...])
        if kv_only:
            k4_ref, v4_ref = outs
            k = px[:, 0:128]
            v = px[:, 128:256]
        else:
            q_ref, k4_ref, v4_ref, fa_ref, ph_ref = outs
            aw = ones_bd.shape[0]
            q = _head_norm(px[:, 0:aw], gq_ref[...], ones_bd)
            if rope:
                q = _rope(q, cos_ref[rs, :], sin_ref[rs, :])
            q_ref[0, rs, :] = (q * (HEAD_DIM ** -0.5 * math.log2(math.e))).astype(BF16)
            k = px[:, aw:aw + 128]
            v = px[:, aw + 128:aw + 256]
            f0 = aw + 256
            fw = cs_ref.shape[0]
            fa_ref[0, rs, :] = _dot(px[:, f0:f0 + fw].astype(BF16), cs_ref[...]).astype(BF16)
            ph_ref[0, rs, :] = px[:, f0 + fw:].astype(BF16)
        k = _head_norm(k, gk_ref[...], ones_bd[0:128, 0:128])
        if rope:
            k = _rope(k, cos_ref[rs, :], sin_ref[rs, :])
        k0, k1 = _tile_heads(k)
        k4_ref[0, 0, rs, :] = k0.astype(BF16)
        k4_ref[0, 1, rs, :] = k1.astype(BF16)
        v0, v1 = _value_heads(v)
        v4_ref[0, 0, :, rs] = v0.T[0:VALUE_ROWS].astype(BF16)
        v4_ref[0, 1, :, rs] = v1.T[0:VALUE_ROWS].astype(BF16)


def _project(x, sc, sh, g, w, gq, gk, cos_t, sin_t, ones_bd, cs, *, rope, kv_only):
    b, n, d = x.shape
    tm = min(n, 2 * PROJ_SUB)
    wn = w[0].shape[2]
    aw = ones_bd.shape[0]
    fw = cs.shape[0]
    full = lambda a: pl.BlockSpec(a.shape, lambda bi, i: (0,) * a.ndim)
    kv_spec = pl.BlockSpec((1, N_KV_HEADS, tm, 256), lambda bi, i: (bi, 0, i, 0))
    kv_shape = jax.ShapeDtypeStruct((b, N_KV_HEADS, n, 256), BF16)
    vl_spec = pl.BlockSpec((1, N_KV_HEADS, VALUE_ROWS, tm), lambda bi, i: (bi, 0, 0, i))
    vl_shape = jax.ShapeDtypeStruct((b, N_KV_HEADS, VALUE_ROWS, n), BF16)
    row = lambda width: pl.BlockSpec((1, tm, width), lambda bi, i: (bi, i, 0))
    if kv_only:
        out_shape = (kv_shape, vl_shape)
        out_specs = (kv_spec, vl_spec)
    else:
        hw = wn - aw - 256 - fw
        out_shape = (jax.ShapeDtypeStruct((b, n, aw), BF16), kv_shape, vl_shape,
                     jax.ShapeDtypeStruct((b, n, 2 * fw), BF16),
                     jax.ShapeDtypeStruct((b, n, hw), BF16))
        out_specs = (row(aw), kv_spec, vl_spec, row(2 * fw), row(hw))
    return pl.pallas_call(
        functools.partial(_proj_kernel, rope=rope, kv_only=kv_only),
        out_shape=out_shape,
        grid=(b, n // tm),
        in_specs=[
            row(d), _mod_spec(*sc), _mod_spec(*sh), full(g), _layer_spec(*w), full(gq), full(gk),
            pl.BlockSpec((tm, LANES), lambda bi, i: (i, 0)),
            pl.BlockSpec((tm, LANES), lambda bi, i: (i, 0)),
            full(ones_bd), full(cs),
        ],
        out_specs=out_specs,
        compiler_params=_cp(("parallel", "parallel")),
        name="proj_kv" if kv_only else ("proj_rope" if rope else "proj_ctx"),
    )(x, sc[0], sh[0], g, w[0], gq, gk, cos_t, sin_t, ones_bd, cs)


ATTN_SUB = 128
ATTN_KEY_CHUNK = 1024


ATTN_PROB_ROWS = 128


def _attn_kernel(q_ref, *refs, lks, nsub):
    nsrc = len(lks)
    k_refs = refs[0:2 * nsrc:2]
    v_refs = refs[1:2 * nsrc:2]
    o_ref = refs[2 * nsrc]
    s_scrs = refs[2 * nsrc + 1:2 * nsrc + 3]
    p_scrs = refs[2 * nsrc + 3:2 * nsrc + 5]
    sub = ATTN_SUB
    rows = GQA_GROUP * sub
    gw = GQA_GROUP * HEAD_DIM
    group = lax.broadcasted_iota(jnp.int32, (1, gw), 1) // HEAD_DIM

    chunks = []
    off = 0
    for j, lk in enumerate(lks):
        tkc = min(lk, ATTN_KEY_CHUNK)
        for c in range(lk // tkc):
            chunks.append((j, c * tkc, tkc, off))
            off += tkc

    lane = lax.broadcasted_iota(jnp.int32, (1, LANES), 1)
    low = lane < HEAD_DIM

    def masked_q(u):
        q = q_ref[0, u * sub:(u + 1) * sub, :]
        zero = jnp.zeros_like(q)
        return jnp.concatenate([jnp.where(group == g, q, zero) for g in range(GQA_GROUP)], axis=0)

    def finish(u, acc):
        o_t = acc / acc[HEAD_DIM:HEAD_DIM + 1, :]
        heads = [o_t[:, g * sub:(g + 1) * sub].T for g in range(GQA_GROUP)]
        left = jnp.where(low, heads[0], pltpu.roll(heads[1], HEAD_DIM, 1))
        right = jnp.where(low, heads[2], pltpu.roll(heads[3], HEAD_DIM, 1))
        o_ref[0, u * sub:(u + 1) * sub, :] = jnp.concatenate([left, right], axis=1).astype(BF16)

    def scores(u):
        s_scr = s_scrs[u % 2]
        qm = masked_q(u)
        for j, start, tkc, o in chunks:
            s_scr[o:o + tkc, :] = _dot_nt(k_refs[j][0, 0, start:start + tkc, :], qm)
        return jnp.max(s_scr[...], axis=0, keepdims=True)

    m = scores(0)
    for u in range(nsub):
        s_scr = s_scrs[u % 2]
        acc = jnp.zeros((VALUE_ROWS, rows), F32)
        if u + 1 < nsub:
            m_next = scores(u + 1)
            bits = pltpu.bitcast(m_next, jnp.uint32)
            acc = acc + ((bits >> 16) >> 16).astype(F32)
        for ci, (j, start, tkc, o) in enumerate(chunks):
            p_scr = p_scrs[ci % 2]
            for r in range(0, tkc, ATTN_PROB_ROWS):
                p_scr[r:r + ATTN_PROB_ROWS, :] = jnp.exp2(
                    s_scr[o + r:o + r + ATTN_PROB_ROWS, :] - m).astype(BF16)
            acc = acc + _dot(v_refs[j][0, 0, :, start:start + tkc], p_scr[0:tkc, :])
        finish(u, acc)
        if u + 1 < nsub:
            m = m_next


def _attention(q, kvs):
    b, lq, aw = q.shape
    tq = min(lq, 8 * ATTN_SUB)
    nsub = tq // ATTN_SUB
    lks = tuple(k.shape[2] for k, _ in kvs)
    gw = GQA_GROUP * HEAD_DIM
    in_specs = [pl.BlockSpec((1, tq, gw), lambda bi, h, i: (bi, i, h))]
    args = [q]
    for k4, vlt in kvs:
        for a in (k4, vlt):
            in_specs.append(pl.BlockSpec((1, 1) + a.shape[2:], lambda bi, h, i: (bi, h, 0, 0),
                                         pipeline_mode=pl.Buffered(1)))
        args += [k4, vlt]
    return pl.pallas_call(
        functools.partial(_attn_kernel, lks=lks, nsub=nsub),
        out_shape=jax.ShapeDtypeStruct((b, lq, aw), BF16),
        grid=(b, N_KV_HEADS, lq // tq),
        in_specs=in_specs,
        out_specs=pl.BlockSpec((1, tq, gw), lambda bi, h, i: (bi, i, h)),
        scratch_shapes=[pltpu.VMEM((sum(lks), GQA_GROUP * ATTN_SUB), F32)] * 2
        + [pltpu.VMEM((min(max(lks), ATTN_KEY_CHUNK), GQA_GROUP * ATTN_SUB), BF16)] * 2,
        compiler_params=_cp(("parallel", "parallel", "arbitrary")),
        name=f"attention_{lq}",
    )(*args)


def _fourier_kernel(c_ref, s_ref, fa_ref, w_ref, o_ref, fold_scr, nyq_scr, *, nb, scale, tk):
    fw = w_ref.shape[0]
    n = fa_ref.shape[1]
    half = n // 2
    blk = min(half, 512)

    @pl.when(pl.program_id(1) == 0)
    def _():
        r = lax.broadcasted_iota(jnp.int32, (blk, blk), 0)
        c = lax.broadcasted_iota(jnp.int32, (blk, blk), 1)
        flip = jnp.where(r + c == blk - 1, 1.0, 0.0).astype(BF16)
        row = lax.broadcasted_iota(jnp.int32, (half, 1), 0)
        sign = jnp.where(lax.broadcasted_iota(jnp.int32, (1, 2 * fw), 1) < fw, 1.0, -1.0)
        for j in range(nb):
            rev = jnp.concatenate(
                [_dot(flip, fa_ref[j, n - (b + 1) * blk:n - b * blk, :]) for b in range(half // blk)],
                axis=0)
            mirrored = jnp.where(row == 0, 0.0, pltpu.roll(rev, 1, 0))
            fold_scr[j] = (fa_ref[j, 0:half, :].astype(F32) + sign * mirrored).astype(BF16)
            nyq_scr[j] = jnp.broadcast_to(fa_ref[j, half:half + 16, 0:fw][0:1].astype(F32), (8, fw))

    ct = c_ref[...]
    st = s_ref[...]
    odd = (lax.broadcasted_iota(jnp.int32, (tk, 1), 0) & 1) == 1
    alt = jnp.where(odd, -1.0, 1.0)
    for j in range(nb):
        y = (_dot(ct, fold_scr[j, :, 0:fw]) - _dot(st, fold_scr[j, :, fw:2 * fw])
             + alt * nyq_scr[j, 0:1, :])
        o_ref[j] = _dot((y * scale).astype(BF16), w_ref[...]).astype(BF16)


def _fourier(fa, ctab, stab, w_f):
    b, n, fw2 = fa.shape
    fw = fw2 // 2
    nb = 2
    tk = min(n, 512)
    half = n // 2
    scale = 1.0 / math.sqrt(n * FOURIER_GROUP_DIM)
    return pl.pallas_call(
        functools.partial(_fourier_kernel, nb=nb, scale=scale, tk=tk),
        out_shape=jax.ShapeDtypeStruct((b, n, fw), BF16),
        grid=(b // nb, n // tk),
        in_specs=[
            pl.BlockSpec((tk, half), lambda g, i: (i, 0)),
            pl.BlockSpec((tk, half), lambda g, i: (i, 0)),
            pl.BlockSpec((nb, n, fw2), lambda g, i: (g, 0, 0)),
            pl.BlockSpec((fw, fw), lambda g, i: (0, 0)),
        ],
        out_specs=pl.BlockSpec((nb, tk, fw), lambda g, i: (g, i, 0)),
        scratch_shapes=[pltpu.VMEM((nb, half, fw2), BF16), pltpu.VMEM((nb, 8, fw), F32)],
        compiler_params=_cp(("parallel", "arbitrary")),
        name=f"fourier_{n}",
    )(ctab, stab, fa, w_f)


def _conv3(p, w, bias):
    n = p.shape[0]
    row = lax.broadcasted_iota(jnp.int32, (n, 1), 0)
    prev = jnp.where(row == 0, 0.0, pltpu.roll(p, 1, 0))
    nxt = jnp.where(row == n - 1, 0.0, pltpu.roll(p, n - 1, 0))
    return prev * w[0:1] + p * w[1:2] + nxt * w[2:3] + bias


def _hyena_pre_kernel(p0_ref, p1_ref, p2_ref, w0_ref, w1_ref, w2_ref, b0_ref, b1_ref, b2_ref,
                      z_ref, x0_ref):
    x0 = _conv3(p0_ref[0].astype(F32), w0_ref[...], b0_ref[...])
    x1 = _conv3(p1_ref[0].astype(F32), w1_ref[...], b1_ref[...])
    v = _conv3(p2_ref[0].astype(F32), w2_ref[...], b2_ref[...])
    z_ref[0] = (x1 * v).astype(BF16)
    x0_ref[0] = x0.astype(BF16)


def _hyena_pre(ph, w, bias):
    b, n, hw3 = ph.shape
    hw = hw3 // 3
    nc = hw // LANES
    pspec = lambda s: pl.BlockSpec((1, n, LANES), lambda bi, c: (bi, 0, s * nc + c))
    wspec = lambda s: pl.BlockSpec((3, LANES), lambda bi, c: (0, s * nc + c))
    bspec = lambda s: pl.BlockSpec((1, LANES), lambda bi, c: (0, s * nc + c))
    ospec = pl.BlockSpec((1, n, LANES), lambda bi, c: (bi, 0, c))
    oshape = jax.ShapeDtypeStruct((b, n, hw), BF16)
    return pl.pallas_call(
        _hyena_pre_kernel,
        out_shape=(oshape, oshape),
        grid=(b, nc),
        in_specs=[pspec(0), pspec(1), pspec(2), wspec(0), wspec(1), wspec(2),
                  bspec(0), bspec(1), bspec(2)],
        out_specs=(ospec, ospec),
        compiler_params=_cp(("parallel", "parallel")),
        name=f"hyena_pre_{n}",
    )(ph, ph, ph, w, w, w, bias, bias, bias)


def _filter_kernel(w1t_ref, w1c_ref, w1s_ref, b1_ref, fr1_ref, w2_ref, b2_ref, fr2_ref, w3_ref,
                   g_ref, *, n):
    hw = w3_ref.shape[1] // 2
    i = lax.broadcasted_iota(jnp.int32, (n, 1), 0).astype(F32)
    t = i / float(n - 1)
    jb = lax.broadcasted_iota(jnp.int32, (1, HYENA_BANDS), 1).astype(F32)
    bands = 1e-4 + jb * ((HYENA_BANDS - 1 - 1e-4) / (HYENA_BANDS - 1))
    ang = ((2.0 * math.pi) * i / float(n)) * bands
    pre = t * w1t_ref[...] + _dot3(jnp.cos(ang), w1c_ref[...]) - _dot3(jnp.sin(ang), w1s_ref[...])
    h = jnp.sin(fr1_ref[...] * (pre + b1_ref[...]))
    h = jnp.sin(fr2_ref[...] * (_dot3(h, w2_ref[...]) + b2_ref[...]))
    h = _dot3(h, w3_ref[...])
    d0 = math.log(HYENA_TARGET) / HYENA_SLOW_DECAY
    d1 = math.log(HYENA_TARGET) / HYENA_FAST_DECAY
    jd = lax.broadcasted_iota(jnp.int32, (1, hw), 1).astype(F32)
    deltas = jnp.abs(d0 + jd * ((d1 - d0) / (hw - 1)))
    decay = jnp.exp(-t * deltas)
    hf = h[:, 0:hw] * decay
    hb = jnp.where(i == 0.0, 0.0, h[:, hw:2 * hw] * decay)
    total = jnp.sum(jnp.abs(hf), axis=0, keepdims=True) + jnp.sum(jnp.abs(hb), axis=0, keepdims=True)
    g_ref[...] = jnp.concatenate([hf / total, hb / total], axis=1).astype(BF16)


def _hyena_filter(n, w1, b1, fr1, w2, b2, fr2, w3):
    nb = HYENA_BANDS
    r = lambda a: a.reshape(1, -1)
    shp = jax.ShapeDtypeStruct((n, w3.shape[1]), BF16)
    return pl.pallas_call(
        functools.partial(_filter_kernel, n=n),
        out_shape=shp,
        compiler_params=pltpu.CompilerParams(vmem_limit_bytes=VMEM_LIMIT),
        name=f"hyena_filter_{n}",
    )(w1[0:1], w1[1:1 + nb], w1[1 + nb:1 + 2 * nb], r(b1), r(fr1), w2, r(b2), r(fr2), w3)


def _spectrum_kernel(c_ref, s_ref, g_ref, ck_ref, sk_ref, kre_ref, kim_ref, *, scale):
    hw = kre_ref.shape[1]
    a = _dot(c_ref[...], g_ref[...])
    b = _dot(s_ref[...], g_ref[...])
    are, bre = a[:, 0:hw], a[:, hw:2 * hw]
    aim, bim = -b[:, 0:hw], -b[:, hw:2 * hw]
    ck = ck_ref[...]
    sk = sk_ref[...]
    kre_ref[...] = (ck * (are + bre) - sk * (aim + bim)) * scale
    kim_ref[...] = (ck * (aim - bim) + sk * (are - bre)) * scale


def _filter_spectrum(g, ctab, stab):
    n, hw2 = g.shape
    hw = hw2 // 2
    tk = min(n, 512)
    half = (2.0 * np.arange(n, dtype=np.float64) + 1.0) * (2.0 * np.pi / (8 * n))
    ck = jnp.asarray(np.cos(half).reshape(n, 1), F32)
    sk = jnp.asarray(np.sin(half).reshape(n, 1), F32)
    tspec = pl.BlockSpec((tk, n), lambda i: (i, 0))
    gspec = pl.BlockSpec((n, hw2), lambda i: (0, 0))
    vspec = pl.BlockSpec((tk, 1), lambda i: (i, 0))
    ospec = pl.BlockSpec((tk, hw), lambda i: (i, 0))
    oshape = jax.ShapeDtypeStruct((n, hw), F32)
    return pl.pallas_call(
        functools.partial(_spectrum_kernel, scale=1.0 / n),
        out_shape=(oshape, oshape),
        grid=(n // tk,),
        in_specs=[tspec, tspec, gspec, vspec, vspec],
        out_specs=(ospec, ospec),
        compiler_params=_cp(("parallel",)),
        name=f"filter_spectrum_{n}",
    )(ctab, stab, g, ck, sk)


def _hyena_kernel(c_ref, s_ref, z_ref, zt_ref, x0_ref, kre_ref, kim_ref, bias_ref, o_ref,
                  yre_scr, yim_scr, *, nb, tk):
    phase = pl.program_id(1)
    i = pl.program_id(2)
    ct = c_ref[...]
    st = s_ref[...]

    @pl.when(phase == 0)
    def _():
        kre = kre_ref[...]
        kim = kim_ref[...]
        r0 = pl.multiple_of(i * tk, tk)
        for j in range(nb):
            ure = _dot(ct, z_ref[j])
            uim = -_dot(st, z_ref[j])
            yre_scr[j, pl.ds(r0, tk), :] = (kre * ure - kim * uim).astype(BF16)
            yim_scr[j, pl.ds(r0, tk), :] = (kre * uim + kim * ure).astype(BF16)

    @pl.when(phase == 1)
    def _():
        for j in range(nb):
            y = _dot(ct, yre_scr[j]) - _dot(st, yim_scr[j])
            u = zt_ref[j].astype(F32)
            o_ref[j] = (x0_ref[j].astype(F32) * (y + u * bias_ref[...])).astype(BF16)


def _hyena(z, x0c, kre, kim, bias, ctab, stab):
    b, n, hw = z.shape
    nb = 2
    tk = min(n, 512)
    tspec = pl.BlockSpec((tk, n), lambda g, p, i: (i, 0))
    tile = pl.BlockSpec((nb, tk, hw), lambda g, p, i: (g, i * p, 0))
    kspec = pl.BlockSpec((tk, hw), lambda g, p, i: (i * (1 - p), 0))
    return pl.pallas_call(
        functools.partial(_hyena_kernel, nb=nb, tk=tk),
        out_shape=jax.ShapeDtypeStruct((b, n, hw), BF16),
        grid=(b // nb, 2, n // tk),
        in_specs=[
            tspec, tspec,
            pl.BlockSpec((nb, n, hw), lambda g, p, i: (g, 0, 0)),
            tile, tile, kspec, kspec,
            pl.BlockSpec((1, hw), lambda g, p, i: (0, 0)),
        ],
        out_specs=tile,
        scratch_shapes=[pltpu.VMEM((nb, n, hw), BF16), pltpu.VMEM((nb, n, hw), BF16)],
        compiler_params=_cp(("parallel", "arbitrary", "arbitrary")),
        name=f"hyena_conv_{n}",
    )(ctab, stab, z, z, x0c, kre, kim, bias)


def _mixout_kernel(a_ref, f_ref, h_ref, w_ref, x_ref, ga_ref, g_ref, o_ref):
    aw = a_ref.shape[2]
    fw = f_ref.shape[2]
    mix = (_dot(a_ref[0], w_ref[0:aw, :]) + _dot(f_ref[0], w_ref[aw:aw + fw, :])
           + _dot(h_ref[0], w_ref[aw + fw:, :]))
    o_ref[0] = x_ref[0] + ga_ref[...] * _rms(mix, g_ref[...])


def _mixout(attn, four, hy, w_out, x, ga, g):
    b, n, d = x.shape
    tm = min(n, 1024)
    row = lambda width: pl.BlockSpec((1, tm, width), lambda bi, i: (bi, i, 0))
    return pl.pallas_call(
        _mixout_kernel,
        out_shape=jax.ShapeDtypeStruct((b, n, d), F32),
        grid=(b, n // tm),
        in_specs=[
            row(attn.shape[2]), row(four.shape[2]), row(hy.shape[2]),
            _layer_spec(*w_out),
            row(d),
            _mod_spec(*ga),
            pl.BlockSpec((1, d), lambda bi, i: (0, 0)),
        ],
        out_specs=row(d),
        compiler_params=_cp(("parallel", "parallel")),
        name=f"mixout_{n}",
    )(attn, four, hy, w_out[0], x, ga[0], g)


HALO = 8


FFN_CHUNK = 256


def _ffn_kernel(xp_ref, x_ref, xn_ref, sc_ref, sh_ref, g_ref, wu_ref, cw_ref, cb_ref, wd_ref,
                ga_ref, go_ref, o_ref, *, tm):
    i = pl.program_id(1)
    nrow = pl.num_programs(1)
    rows = tm + 2 * HALO
    dff = wd_ref.shape[0]
    cf = FFN_CHUNK
    g = g_ref[...]
    sc = 1.0 + sc_ref[...]
    sh = sh_ref[...]
    keep_p = jnp.where(i > 0, 1.0, 0.0)
    keep_n = jnp.where(i < nrow - 1, 1.0, 0.0)
    x = x_ref[0]
    fx = jnp.concatenate([(_rms(xp_ref[0], g) * sc + sh) * keep_p,
                          _rms(x, g) * sc + sh,
                          (_rms(xn_ref[0], g) * sc + sh) * keep_n], axis=0).astype(BF16)

    def conv(u, w, bias):
        prev = pltpu.roll(u, 1, 0)[HALO:HALO + tm]
        nxt = pltpu.roll(u, rows - 1, 0)[HALO:HALO + tm]
        return prev * w[0:1] + u[HALO:HALO + tm] * w[1:2] + nxt * w[2:3] + bias

    acts = []
    for c in range(dff // cf):
        lo, hi = c * cf, (c + 1) * cf
        gate = conv(_dot(fx, wu_ref[:, lo:hi]), cw_ref[:, lo:hi], cb_ref[:, lo:hi])
        val = conv(_dot(fx, wu_ref[:, dff + lo:dff + hi]), cw_ref[:, dff + lo:dff + hi],
                   cb_ref[:, dff + lo:dff + hi])
        acts.append((gate * jax.nn.sigmoid(gate) * val).astype(BF16))
    y = _dot(jnp.concatenate(acts, axis=1), wd_ref[...])
    o_ref[0] = x + ga_ref[...] * _rms(y, go_ref[...])


def _ffn(x, sc, sh, g, w_up, w_conv, b_conv, w_down, ga, g_post):
    b, n, d = x.shape
    tm = min(n, 512)
    hb = tm // HALO
    nhb = n // HALO
    gvec = pl.BlockSpec((1, d), lambda bi, i: (0, 0))
    resident = lambda a: pl.BlockSpec(a.shape, lambda bi, i: (0, 0), pipeline_mode=pl.Buffered(1))
    stacked = lambda w: _layer_spec(*w, pipeline_mode=pl.Buffered(1))
    return pl.pallas_call(
        functools.partial(_ffn_kernel, tm=tm),
        out_shape=jax.ShapeDtypeStruct((b, n, d), F32),
        grid=(b, n // tm),
        in_specs=[
            pl.BlockSpec((1, HALO, d), lambda bi, i: (bi, jnp.maximum(i * hb - 1, 0), 0)),
            pl.BlockSpec((1, tm, d), lambda bi, i: (bi, i, 0)),
            pl.BlockSpec((1, HALO, d), lambda bi, i: (bi, jnp.minimum((i + 1) * hb, nhb - 1), 0)),
            _mod_spec(*sc), _mod_spec(*sh), gvec,
            stacked(w_up), resident(w_conv), resident(b_conv), stacked(w_down),
            _mod_spec(*ga), gvec,
        ],
        out_specs=pl.BlockSpec((1, tm, d), lambda bi, i: (bi, i, 0)),
        compiler_params=_cp(("parallel", "parallel")),
        name=f"conv_ffn_{n}",
    )(x, x, x, sc[0], sh[0], g, w_up[0], w_conv, b_conv, w_down[0], ga[0], g_post)


def _rope_tables(n):
    half = HEAD_DIM // 4
    inv = ROPE_THETA ** (-jnp.arange(0, 2 * half, 2, dtype=F32) / (2 * half))
    pos = jnp.arange(n, dtype=jnp.int32)
    row = (pos // GRID_W).astype(F32)
    col = (pos % GRID_W).astype(F32)
    ang_r = row[:, None] * inv[None, :]
    ang_c = col[:, None] * inv[None, :]
    cr, sr, cc, sc = jnp.cos(ang_r), jnp.sin(ang_r), jnp.cos(ang_c), jnp.sin(ang_c)
    cos64 = jnp.concatenate([cr, cr, cc, cc], axis=-1)
    sin64 = jnp.concatenate([-sr, sr, -sc, sc], axis=-1)
    return jnp.tile(cos64, (1, 2)), jnp.tile(sin64, (1, 2))


def _head_ones(width):
    idx = np.arange(width) // HEAD_DIM
    return jnp.asarray((idx[:, None] == idx[None, :]).astype(np.float32), BF16)


def _channel_dft(width):
    c = np.arange(width)
    same = (c[:, None] // FOURIER_GROUP_DIM) == (c[None, :] // FOURIER_GROUP_DIM)
    ang = 2.0 * np.pi * ((c[:, None] % FOURIER_GROUP_DIM) * (c[None, :] % FOURIER_GROUP_DIM)
                         % FOURIER_GROUP_DIM) / FOURIER_GROUP_DIM
    cs = np.concatenate([np.where(same, np.cos(ang), 0.0), np.where(same, np.sin(ang), 0.0)], axis=1)
    return jnp.asarray(cs.astype(np.float32), BF16)


def kernel(x, c, ctx, c_ctx, w_mod, b_mod, g_pre_mix, g_post_mix, g_pre_ffn, g_post_ffn, w_in, g_q, g_k, w_fourier, w_hy_conv, b_hy_conv, hy_w1, hy_b1, hy_fr1, hy_w2, hy_b2, hy_fr2, hy_w3, hy_bias, w_out, w_up, w_ffn_conv, b_ffn_conv, w_down):
    bsz, seq, d = x.shape
    clen = ctx.shape[1]
    depth = w_mod.shape[0]
    fw = w_fourier.shape[1]
    hw = hy_bias.shape[1]
    aw = w_in.shape[2] - 2 * N_KV_HEADS * HEAD_DIM - fw - 3 * hw
    k0 = aw
    f0 = aw + 2 * N_KV_HEADS * HEAD_DIM

    nrows = -(-(bsz + 1) // 8) * 8
    cc = jnp.zeros((nrows, d), F32).at[:bsz].set(c).at[bsz].set(c_ctx)
    mods = _modulation(cc, w_mod, b_mod).reshape(depth, nrows, 6, 1, d)
    w_in_b = w_in.astype(BF16)
    w_out_b = w_out.astype(BF16)
    w_up_b = w_up.astype(BF16)
    w_down_b = w_down.astype(BF16)

    cos_x, sin_x = _rope_tables(seq)
    cos_c = jnp.ones((clen, LANES), F32)
    sin_c = jnp.zeros((clen, LANES), F32)
    ones_bd = _head_ones(aw)
    cs = _channel_dft(fw)
    hc_x, hs_x, fc_x, fs_x = _make_tables(seq)
    hc_c, hs_c, fc_c, fs_c = _make_tables(clen)

    def row1(v):
        return v.reshape(1, -1)

    for i in range(depth):
        last = i == depth - 1
        sh1, sc1, ga1, sh2, sc2, ga2 = ((mods, i, lambda bi: bi, t) for t in range(6))
        csh1, csc1, cga1, csh2, csc2, cga2 = ((mods, i, lambda bi: bsz, t) for t in range(6))
        win, wout, wup, wdown = ((w, i, None) for w in (w_in_b, w_out_b, w_up_b, w_down_b))
        w_f_b = w_fourier[i].astype(BF16)
        gq_t = jnp.tile(g_q[i], aw // HEAD_DIM).reshape(1, aw)
        gk_t = jnp.tile(g_k[i], N_KV_HEADS).reshape(1, N_KV_HEADS * HEAD_DIM)
        g_pre = row1(g_pre_mix[i])
        g_post = row1(g_post_mix[i])
        hy_params = (hy_w1[i], hy_b1[i], hy_fr1[i], hy_w2[i], hy_b2[i], hy_fr2[i], hy_w3[i])
        hbias = row1(hy_bias[i])
        hcw = w_hy_conv[i]
        hcb = row1(b_hy_conv[i])

        q, k4, v4, fa, ph = _project(x, sc1, sh1, g_pre, win, gq_t, gk_t, cos_x, sin_x,
                                     ones_bd, cs, rope=True, kv_only=False)
        if last:
            kc4, vc4 = _project(ctx, csc1, csh1, g_pre, (w_in_b, i, (k0, f0 - k0)), gq_t, gk_t,
                                cos_c, sin_c, ones_bd, cs, rope=False, kv_only=True)
        else:
            qc, kc4, vc4, fac, phc = _project(ctx, csc1, csh1, g_pre, win, gq_t, gk_t, cos_c,
                                              sin_c, ones_bd, cs, rope=False, kv_only=False)
        attn_x = _attention(q, [(k4, v4), (kc4, vc4)])
        four_x = _fourier(fa, fc_x, fs_x, w_f_b)
        kre, kim = _filter_spectrum(_hyena_filter(seq, *hy_params), hc_x, hs_x)
        z, x0c = _hyena_pre(ph, hcw, hcb)
        hy_x = _hyena(z, x0c, kre, kim, hbias, hc_x, hs_x)
        x_new = _mixout(attn_x, four_x, hy_x, wout, x, ga1, g_post)

        if not last:
            attn_c = _attention(qc, [(kc4, vc4)])
            four_c = _fourier(fac, fc_c, fs_c, w_f_b)
            kre_c, kim_c = _filter_spectrum(_hyena_filter(clen, *hy_params), hc_c, hs_c)
            zc, x0cc = _hyena_pre(phc, hcw, hcb)
            hy_c = _hyena(zc, x0cc, kre_c, kim_c, hbias, hc_c, hs_c)
            ctx = _mixout(attn_c, four_c, hy_c, wout, ctx, cga1, g_post)
        x = x_new

        g_pf = row1(g_pre_ffn[i])
        g_of = row1(g_post_ffn[i])
        fcw = w_ffn_conv[i]
        fcb = row1(b_ffn_conv[i])
        x = _ffn(x, sc2, sh2, g_pf, wup, fcw, fcb, wdown, ga2, g_of)
        if not last:
            ctx = _ffn(ctx, csc2, csh2, g_pf, wup, fcw, fcb, wdown, cga2, g_of)

    return x
```

```python
import functools
import math

import numpy as np
import jax
import jax.numpy as jnp
from jax import lax
from jax.experimental import pallas as pl
from jax.experimental.pallas import tpu as pltpu

F32 = jnp.float32
BF16 = jnp.bfloat16

HEAD_DIM = 64
GQA_GROUP = 4
N_KV_HEADS = 2
GRID_W = 64
ROPE_THETA = 10000.0
FOURIER_GROUP_DIM = 64
HYENA_BANDS = 16
HYENA_FAST_DECAY = 0.3
HYENA_SLOW_DECAY = 1.5
HYENA_TARGET = 1e-2
NORM_EPS = 1e-6
LANES = 128
VMEM_LIMIT = 56 * 1024 * 1024


def _cp(sem, vmem=VMEM_LIMIT):
    return pltpu.CompilerParams(dimension_semantics=sem, vmem_limit_bytes=vmem)


def _dot(a, b):
    return jnp.dot(a, b, preferred_element_type=F32)


def _dot_nt(a, b):
    return lax.dot_general(a, b, (((1,), (1,)), ((), ())), preferred_element_type=F32)


def _split(a):
    hi = a.astype(BF16)
    lo = (a - hi.astype(F32)).astype(BF16)
    return hi, lo


def _dot3(a, b):
    ah, al = _split(a)
    bh, bl = _split(b)
    return _dot(ah, bh) + _dot(ah, bl) + _dot(al, bh)


def _rms(x, g):
    ms = jnp.mean(x * x, axis=-1, keepdims=True)
    return x * lax.rsqrt(ms + NORM_EPS) * g


def _mod_spec(mod, layer, row_fn, slot):
    return pl.BlockSpec((None, None, None, 1, mod.shape[-1]),
                        lambda bi, i: (layer, row_fn(bi), slot, 0, 0))


def _layer_spec(w, layer, col_block=None, **kw):
    if col_block is None:
        return pl.BlockSpec((None,) + w.shape[1:], lambda bi, i: (layer, 0, 0), **kw)
    start, width = col_block
    return pl.BlockSpec((None, w.shape[1], width), lambda bi, i: (layer, 0, start // width), **kw)


def _tables_kernel(hc_ref, hs_ref, fc_ref, fs_ref, hca, hsa, fca, fsa, *, n, tk):
    i = pl.program_id(0)
    hyena_step = 2.0 * math.pi / (8 * n)
    fourier_step = 2.0 * math.pi / n

    @pl.when(i == 0)
    def _():
        k = lax.broadcasted_iota(jnp.int32, (tk, n), 0)
        s = lax.broadcasted_iota(jnp.int32, (tk, n), 1)
        a = (((2 * k + 1) * (2 * s + 1)) & (8 * n - 1)).astype(F32) * hyena_step
        hca[...] = jnp.cos(a)
        hsa[...] = jnp.sin(a)
        kh = lax.broadcasted_iota(jnp.int32, (tk, n // 2), 0)
        sh = lax.broadcasted_iota(jnp.int32, (tk, n // 2), 1)
        b = ((kh * sh) & (n - 1)).astype(F32) * fourier_step
        fca[...] = jnp.cos(b)
        fsa[...] = jnp.sin(b)

    s1 = lax.broadcasted_iota(jnp.int32, (1, n), 1)
    rot = (((2 * tk * i) * (2 * s1 + 1)) & (8 * n - 1)).astype(F32) * hyena_step
    cb, sb = jnp.cos(rot), jnp.sin(rot)
    hc_ref[...] = (hca[...] * cb - hsa[...] * sb).astype(BF16)
    hs_ref[...] = (hsa[...] * cb + hca[...] * sb).astype(BF16)
    s2 = lax.broadcasted_iota(jnp.int32, (1, n // 2), 1)
    rot = (((tk * i) * s2) & (n - 1)).astype(F32) * fourier_step
    cb, sb = jnp.cos(rot), jnp.sin(rot)
    fc_ref[...] = (fca[...] * cb - fsa[...] * sb).astype(BF16)
    fs_ref[...] = (fsa[...] * cb + fca[...] * sb).astype(BF16)


def _make_tables(n):
    tk = min(n, 256)
    spec = pl.BlockSpec((tk, n), lambda i: (i, 0))
    shp = jax.ShapeDtypeStruct((n, n), BF16)
    hspec = pl.BlockSpec((tk, n // 2), lambda i: (i, 0))
    hshp = jax.ShapeDtypeStruct((n, n // 2), BF16)
    return pl.pallas_call(
        functools.partial(_tables_kernel, n=n, tk=tk),
        out_shape=(shp, shp, hshp, hshp),
        grid=(n // tk,),
        out_specs=(spec, spec, hspec, hspec),
        scratch_shapes=[pltpu.VMEM((tk, n), F32)] * 2 + [pltpu.VMEM((tk, n // 2), F32)] * 2,
        compiler_params=_cp(("arbitrary",)),
        name=f"dft_tables_{n}",
    )()


def _mod_kernel(c_ref, w_ref, b_ref, o_ref):
    c = c_ref[...]
    a = c * jax.nn.sigmoid(c)
    o_ref[0] = _dot3(a, w_ref[0]) + b_ref[0]


def _modulation(cc, w_mod, b_mod):
    depth, d, n6 = w_mod.shape
    tn = 1536
    rows = cc.shape[0]
    return pl.pallas_call(
        _mod_kernel,
        out_shape=jax.ShapeDtypeStruct((depth, rows, n6), F32),
        grid=(depth, n6 // tn),
        in_specs=[
            pl.BlockSpec((rows, d), lambda l, j: (0, 0)),
            pl.BlockSpec((1, d, tn), lambda l, j: (l, 0, j)),
            pl.BlockSpec((1, 1, tn), lambda l, j: (l, 0, j)),
        ],
        out_specs=pl.BlockSpec((1, rows, tn), lambda l, j: (l, 0, j)),
        compiler_params=_cp(("parallel", "parallel")),
        name="modulation",
    )(cc, w_mod, b_mod.reshape(depth, 1, n6))


def _head_norm(p, gain, ones_bd):
    hi, lo = _split(p * p)
    ss = _dot(hi, ones_bd) + _dot(lo, ones_bd)
    return p * lax.rsqrt(ss * (1.0 / HEAD_DIM) + NORM_EPS) * gain


def _rope(xn, cos_t, sin_t):
    lane = lax.broadcasted_iota(jnp.int32, (1, LANES), 1)
    first = (lane % 32) < 16
    outs = []
    for j in range(xn.shape[1] // LANES):
        c = xn[:, j * LANES:(j + 1) * LANES]
        sw = jnp.where(first, pltpu.roll(c, LANES - 16, 1), pltpu.roll(c, 16, 1))
        outs.append(c * cos_t + sw * sin_t)
    return outs[0] if len(outs) == 1 else jnp.concatenate(outs, axis=1)


def _tile_heads(kv):
    lane = lax.broadcasted_iota(jnp.int32, (1, LANES), 1)
    low = lane < HEAD_DIM
    r = pltpu.roll(kv, HEAD_DIM, 1)
    h0 = jnp.where(low, kv, r)
    h1 = jnp.where(low, r, kv)
    return jnp.concatenate([h0, h0], axis=1), jnp.concatenate([h1, h1], axis=1)


VALUE_ROWS = 2 * HEAD_DIM


def _value_heads(v):
    lane = lax.broadcasted_iota(jnp.int32, (1, LANES), 1)
    low = lane < HEAD_DIM
    r = pltpu.roll(v, HEAD_DIM, 1)
    return jnp.where(low, v, 1.0), jnp.where(low, r, 1.0)


PROJ_SUB = 512


def _proj_kernel(x_ref, sc_ref, sh_ref, g_ref, w_ref, gq_ref, gk_ref, cos_ref, sin_ref,
                 ones_ref, cs_ref, *outs, rope, kv_only):
    tm = x_ref.shape[1]
    sub = min(tm, PROJ_SUB)
    ones_bd = ones_ref[...]
    for r0 in range(0, tm, sub):
        rs = slice(r0, r0 + sub)
        x = x_ref[0, rs, :]
        h = _rms(x, g_ref[...]) * (1.0 + sc_ref[...]) + sh_ref[...]
        px = _dot(h.astype(BF16), w_ref[...])
        if kv_only:
            k4_ref, v4_ref = outs
            k = px[:, 0:128]
            v = px[:, 128:256]
        else:
            q_ref, k4_ref, v4_ref, fa_ref, ph_ref = outs
            aw = ones_bd.shape[0]
            q = _head_norm(px[:, 0:aw], gq_ref[...], ones_bd)
            if rope:
                q = _rope(q, cos_ref[rs, :], sin_ref[rs, :])
            q_ref[0, rs, :] = (q * (HEAD_DIM ** -0.5 * math.log2(math.e))).astype(BF16)
            k = px[:, aw:aw + 128]
            v = px[:, aw + 128:aw + 256]
            f0 = aw + 256
            fw = cs_ref.shape[0]
            fa_ref[0, rs, :] = _dot(px[:, f0:f0 + fw].astype(BF16), cs_ref[...]).astype(BF16)
            ph_ref[0, rs, :] = px[:, f0 + fw:].astype(BF16)
        k = _head_norm(k, gk_ref[...], ones_bd[0:128, 0:128])
        if rope:
            k = _rope(k, cos_ref[rs, :], sin_ref[rs, :])
        k0, k1 = _tile_heads(k)
        k4_ref[0, 0, rs, :] = k0.astype(BF16)
        k4_ref[0, 1, rs, :] = k1.astype(BF16)
        v0, v1 = _value_heads(v)
        v4_ref[0, 0, :, rs] = v0.T[0:VALUE_ROWS].astype(BF16)
        v4_ref[0, 1, :, rs] = v1.T[0:VALUE_ROWS].astype(BF16)


def _project(x, sc, sh, g, w, gq, gk, cos_t, sin_t, ones_bd, cs, *, rope, kv_only):
    b, n, d = x.shape
    tm = min(n, 2 * PROJ_SUB)
    wn = w[0].shape[2]
    aw = ones_bd.shape[0]
    fw = cs.shape[0]
    full = lambda a: pl.BlockSpec(a.shape, lambda bi, i: (0,) * a.ndim)
    kv_spec = pl.BlockSpec((1, N_KV_HEADS, tm, 256), lambda bi, i: (bi, 0, i, 0))
    kv_shape = jax.ShapeDtypeStruct((b, N_KV_HEADS, n, 256), BF16)
    vl_spec = pl.BlockSpec((1, N_KV_HEADS, VALUE_ROWS, tm), lambda bi, i: (bi, 0, 0, i))
    vl_shape = jax.ShapeDtypeStruct((b, N_KV_HEADS, VALUE_ROWS, n), BF16)
    row = lambda width: pl.BlockSpec((1, tm, width), lambda bi, i: (bi, i, 0))
    if kv_only:
        out_shape = (kv_shape, vl_shape)
        out_specs = (kv_spec, vl_spec)
    else:
        hw = wn - aw - 256 - fw
        out_shape = (jax.ShapeDtypeStruct((b, n, aw), BF16), kv_shape, vl_shape,
                     jax.ShapeDtypeStruct((b, n, 2 * fw), BF16),
                     jax.ShapeDtypeStruct((b, n, hw), BF16))
        out_specs = (row(aw), kv_spec, vl_spec, row(2 * fw), row(hw))
    return pl.pallas_call(
        functools.partial(_proj_kernel, rope=rope, kv_only=kv_only),
        out_shape=out_shape,
        grid=(b, n // tm),
        in_specs=[
            row(d), _mod_spec(*sc), _mod_spec(*sh), full(g), _layer_spec(*w), full(gq), full(gk),
            pl.BlockSpec((tm, LANES), lambda bi, i: (i, 0)),
            pl.BlockSpec((tm, LANES), lambda bi, i: (i, 0)),
            full(ones_bd), full(cs),
        ],
        out_specs=out_specs,
        compiler_params=_cp(("parallel", "parallel")),
        name="proj_kv" if kv_only else ("proj_rope" if rope else "proj_ctx"),
    )(x, sc[0], sh[0], g, w[0], gq, gk, cos_t, sin_t, ones_bd, cs)


ATTN_SUB = 128
ATTN_KEY_CHUNK = 1024


ATTN_PROB_ROWS = 128


def _attn_kernel(q_ref, *refs, lks, nsub):
    nsrc = len(lks)
    k_refs = refs[0:2 * nsrc:2]
    v_refs = refs[1:2 * nsrc:2]
    o_ref = refs[2 * nsrc]
    s_scrs = refs[2 * nsrc + 1:2 * nsrc + 3]
    p_scrs = refs[2 * nsrc + 3:2 * nsrc + 5]
    sub = ATTN_SUB
    rows = GQA_GROUP * sub
    gw = GQA_GROUP * HEAD_DIM
    group = lax.broadcasted_iota(jnp.int32, (1, gw), 1) // HEAD_DIM

    chunks = []
    off = 0
    for j, lk in enumerate(lks):
        tkc = min(lk, ATTN_KEY_CHUNK)
        for c in range(lk // tkc):
            chunks.append((j, c * tkc, tkc, off))
            off += tkc

    lane = lax.broadcasted_iota(jnp.int32, (1, LANES), 1)
    low = lane < HEAD_DIM

    def masked_q(u):
        q = q_ref[0, u * sub:(u + 1) * sub, :]
        zero = jnp.zeros_like(q)
        return jnp.concatenate([jnp.where(group == g, q, zero) for g in range(GQA_GROUP)], axis=0)

    def finish(u, acc):
        o_t = acc / acc[HEAD_DIM:HEAD_DIM + 1, :]
        heads = [o_t[:, g * sub:(g + 1) * sub].T for g in range(GQA_GROUP)]
        left = jnp.where(low, heads[0], pltpu.roll(heads[1], HEAD_DIM, 1))
        right = jnp.where(low, heads[2], pltpu.roll(heads[3], HEAD_DIM, 1))
        o_ref[0, u * sub:(u + 1) * sub, :] = jnp.concatenate([left, right], axis=1).astype(BF16)

    def scores(u):
        s_scr = s_scrs[u % 2]
        qm = masked_q(u)
        for j, start, tkc, o in chunks:
            s = _dot_nt(k_refs[j][0, 0, start:start + tkc, :], qm)
            s_scr[o:o + tkc, :] = s
        return jnp.max(s_scr[...], axis=0, keepdims=True), s[tkc - 8:tkc, :]

    m, _ = scores(0)
    for u in range(nsub):
        s_scr = s_scrs[u % 2]
        acc = jnp.zeros((VALUE_ROWS, rows), F32)
        if u + 1 < nsub:
            m_next, tail = scores(u + 1)
            bits = pltpu.bitcast(tail, jnp.uint32)
            acc = jnp.concatenate([((bits >> 16) >> 16).astype(F32),
                                   jnp.zeros((VALUE_ROWS - 8, rows), F32)], axis=0)
        for ci, (j, start, tkc, o) in enumerate(chunks):
            p_scr = p_scrs[ci % 2]
            for r in range(0, tkc, ATTN_PROB_ROWS):
                p_scr[r:r + ATTN_PROB_ROWS, :] = jnp.exp2(
                    s_scr[o + r:o + r + ATTN_PROB_ROWS, :] - m).astype(BF16)
            acc = acc + _dot(v_refs[j][0, 0, :, start:start + tkc], p_scr[0:tkc, :])
        finish(u, acc)
        if u + 1 < nsub:
            m = m_next


def _attention(q, kvs):
    b, lq, aw = q.shape
    tq = min(lq, 8 * ATTN_SUB)
    nsub = tq // ATTN_SUB
    lks = tuple(k.shape[2] for k, _ in kvs)
    gw = GQA_GROUP * HEAD_DIM
    in_specs = [pl.BlockSpec((1, tq, gw), lambda bi, h, i: (bi, i, h))]
    args = [q]
    for k4, vlt in kvs:
        for a in (k4, vlt):
            in_specs.append(pl.BlockSpec((1, 1) + a.shape[2:], lambda bi, h, i: (bi, h, 0, 0),
                                         pipeline_mode=pl.Buffered(1)))
        args += [k4, vlt]
    return pl.pallas_call(
        functools.partial(_attn_kernel, lks=lks, nsub=nsub),
        out_shape=jax.ShapeDtypeStruct((b, lq, aw), BF16),
        grid=(b, N_KV_HEADS, lq // tq),
        in_specs=in_specs,
        out_specs=pl.BlockSpec((1, tq, gw), lambda bi, h, i: (bi, i, h)),
        scratch_shapes=[pltpu.VMEM((sum(lks), GQA_GROUP * ATTN_SUB), F32)] * 2
        + [pltpu.VMEM((min(max(lks), ATTN_KEY_CHUNK), GQA_GROUP * ATTN_SUB), BF16)] * 2,
        compiler_params=_cp(("parallel", "parallel", "arbitrary")),
        name=f"attention_{lq}",
    )(*args)


def _fourier_kernel(c_ref, s_ref, fa_ref, w_ref, o_ref, fold_scr, nyq_scr, *, nb, scale, tk):
    fw = w_ref.shape[0]
    n = fa_ref.shape[1]
    half = n // 2
    blk = min(half, 512)

    @pl.when(pl.program_id(1) == 0)
    def _():
        r = lax.broadcasted_iota(jnp.int32, (blk, blk), 0)
        c = lax.broadcasted_iota(jnp.int32, (blk, blk), 1)
        flip = jnp.where(r + c == blk - 1, 1.0, 0.0).astype(BF16)
        row = lax.broadcasted_iota(jnp.int32, (half, 1), 0)
        sign = jnp.where(lax.broadcasted_iota(jnp.int32, (1, 2 * fw), 1) < fw, 1.0, -1.0)
        for j in range(nb):
            rev = jnp.concatenate(
                [_dot(flip, fa_ref[j, n - (b + 1) * blk:n - b * blk, :]) for b in range(half // blk)],
                axis=0)
            mirrored = jnp.where(row == 0, 0.0, pltpu.roll(rev, 1, 0))
            fold_scr[j] = (fa_ref[j, 0:half, :].astype(F32) + sign * mirrored).astype(BF16)
            nyq_scr[j] = jnp.broadcast_to(fa_ref[j, half:half + 16, 0:fw][0:1].astype(F32), (8, fw))

    ct = c_ref[...]
    st = s_ref[...]
    odd = (lax.broadcasted_iota(jnp.int32, (tk, 1), 0) & 1) == 1
    alt = jnp.where(odd, -1.0, 1.0)
    for j in range(nb):
        y = (_dot(ct, fold_scr[j, :, 0:fw]) - _dot(st, fold_scr[j, :, fw:2 * fw])
             + alt * nyq_scr[j, 0:1, :])
        o_ref[j] = _dot((y * scale).astype(BF16), w_ref[...]).astype(BF16)


def _fourier(fa, ctab, stab, w_f):
    b, n, fw2 = fa.shape
    fw = fw2 // 2
    nb = 2
    tk = min(n, 512)
    half = n // 2
    scale = 1.0 / math.sqrt(n * FOURIER_GROUP_DIM)
    return pl.pallas_call(
        functools.partial(_fourier_kernel, nb=nb, scale=scale, tk=tk),
        out_shape=jax.ShapeDtypeStruct((b, n, fw), BF16),
        grid=(b // nb, n // tk),
        in_specs=[
            pl.BlockSpec((tk, half), lambda g, i: (i, 0)),
            pl.BlockSpec((tk, half), lambda g, i: (i, 0)),
            pl.BlockSpec((nb, n, fw2), lambda g, i: (g, 0, 0)),
            pl.BlockSpec((fw, fw), lambda g, i: (0, 0)),
        ],
        out_specs=pl.BlockSpec((nb, tk, fw), lambda g, i: (g, i, 0)),
        scratch_shapes=[pltpu.VMEM((nb, half, fw2), BF16), pltpu.VMEM((nb, 8, fw), F32)],
        compiler_params=_cp(("parallel", "arbitrary")),
        name=f"fourier_{n}",
    )(ctab, stab, fa, w_f)


def _conv3(p, w, bias):
    n = p.shape[0]
    row = lax.broadcasted_iota(jnp.int32, (n, 1), 0)
    prev = jnp.where(row == 0, 0.0, pltpu.roll(p, 1, 0))
    nxt = jnp.where(row == n - 1, 0.0, pltpu.roll(p, n - 1, 0))
    return prev * w[0:1] + p * w[1:2] + nxt * w[2:3] + bias


def _hyena_pre_kernel(p0_ref, p1_ref, p2_ref, w0_ref, w1_ref, w2_ref, b0_ref, b1_ref, b2_ref,
                      z_ref, x0_ref):
    x0 = _conv3(p0_ref[0].astype(F32), w0_ref[...], b0_ref[...])
    x1 = _conv3(p1_ref[0].astype(F32), w1_ref[...], b1_ref[...])
    v = _conv3(p2_ref[0].astype(F32), w2_ref[...], b2_ref[...])
    z_ref[0] = (x1 * v).astype(BF16)
    x0_ref[0] = x0.astype(BF16)


def _hyena_pre(ph, w, bias):
    b, n, hw3 = ph.shape
    hw = hw3 // 3
    nc = hw // LANES
    pspec = lambda s: pl.BlockSpec((1, n, LANES), lambda bi, c: (bi, 0, s * nc + c))
    wspec = lambda s: pl.BlockSpec((3, LANES), lambda bi, c: (0, s * nc + c))
    bspec = lambda s: pl.BlockSpec((1, LANES), lambda bi, c: (0, s * nc + c))
    ospec = pl.BlockSpec((1, n, LANES), lambda bi, c: (bi, 0, c))
    oshape = jax.ShapeDtypeStruct((b, n, hw), BF16)
    return pl.pallas_call(
        _hyena_pre_kernel,
        out_shape=(oshape, oshape),
        grid=(b, nc),
        in_specs=[pspec(0), pspec(1), pspec(2), wspec(0), wspec(1), wspec(2),
                  bspec(0), bspec(1), bspec(2)],
        out_specs=(ospec, ospec),
        compiler_params=_cp(("parallel", "parallel")),
        name=f"hyena_pre_{n}",
    )(ph, ph, ph, w, w, w, bias, bias, bias)


def _filter_kernel(w1t_ref, w1c_ref, w1s_ref, b1_ref, fr1_ref, w2_ref, b2_ref, fr2_ref, w3_ref,
                   g_ref, *, n):
    hw = w3_ref.shape[1] // 2
    i = lax.broadcasted_iota(jnp.int32, (n, 1), 0).astype(F32)
    t = i / float(n - 1)
    jb = lax.broadcasted_iota(jnp.int32, (1, HYENA_BANDS), 1).astype(F32)
    bands = 1e-4 + jb * ((HYENA_BANDS - 1 - 1e-4) / (HYENA_BANDS - 1))
    ang = ((2.0 * math.pi) * i / float(n)) * bands
    pre = t * w1t_ref[...] + _dot3(jnp.cos(ang), w1c_ref[...]) - _dot3(jnp.sin(ang), w1s_ref[...])
    h = jnp.sin(fr1_ref[...] * (pre + b1_ref[...]))
    h = jnp.sin(fr2_ref[...] * (_dot3(h, w2_ref[...]) + b2_ref[...]))
    h = _dot3(h, w3_ref[...])
    d0 = math.log(HYENA_TARGET) / HYENA_SLOW_DECAY
    d1 = math.log(HYENA_TARGET) / HYENA_FAST_DECAY
    jd = lax.broadcasted_iota(jnp.int32, (1, hw), 1).astype(F32)
    deltas = jnp.abs(d0 + jd * ((d1 - d0) / (hw - 1)))
    decay = jnp.exp(-t * deltas)
    hf = h[:, 0:hw] * decay
    hb = jnp.where(i == 0.0, 0.0, h[:, hw:2 * hw] * decay)
    total = jnp.sum(jnp.abs(hf), axis=0, keepdims=True) + jnp.sum(jnp.abs(hb), axis=0, keepdims=True)
    g_ref[...] = jnp.concatenate([hf / total, hb / total], axis=1).astype(BF16)


def _hyena_filter(n, w1, b1, fr1, w2, b2, fr2, w3):
    nb = HYENA_BANDS
    r = lambda a: a.reshape(1, -1)
    shp = jax.ShapeDtypeStruct((n, w3.shape[1]), BF16)
    return pl.pallas_call(
        functools.partial(_filter_kernel, n=n),
        out_shape=shp,
        compiler_params=pltpu.CompilerParams(vmem_limit_bytes=VMEM_LIMIT),
        name=f"hyena_filter_{n}",
    )(w1[0:1], w1[1:1 + nb], w1[1 + nb:1 + 2 * nb], r(b1), r(fr1), w2, r(b2), r(fr2), w3)


def _spectrum_kernel(c_ref, s_ref, g_ref, ck_ref, sk_ref, kre_ref, kim_ref, *, scale):
    hw = kre_ref.shape[1]
    a = _dot(c_ref[...], g_ref[...])
    b = _dot(s_ref[...], g_ref[...])
    are, bre = a[:, 0:hw], a[:, hw:2 * hw]
    aim, bim = -b[:, 0:hw], -b[:, hw:2 * hw]
    ck = ck_ref[...]
    sk = sk_ref[...]
    kre_ref[...] = (ck * (are + bre) - sk * (aim + bim)) * scale
    kim_ref[...] = (ck * (aim - bim) + sk * (are - bre)) * scale


def _filter_spectrum(g, ctab, stab):
    n, hw2 = g.shape
    hw = hw2 // 2
    tk = min(n, 512)
    half = (2.0 * np.arange(n, dtype=np.float64) + 1.0) * (2.0 * np.pi / (8 * n))
    ck = jnp.asarray(np.cos(half).reshape(n, 1), F32)
    sk = jnp.asarray(np.sin(half).reshape(n, 1), F32)
    tspec = pl.BlockSpec((tk, n), lambda i: (i, 0))
    gspec = pl.BlockSpec((n, hw2), lambda i: (0, 0))
    vspec = pl.BlockSpec((tk, 1), lambda i: (i, 0))
    ospec = pl.BlockSpec((tk, hw), lambda i: (i, 0))
    oshape = jax.ShapeDtypeStruct((n, hw), F32)
    return pl.pallas_call(
        functools.partial(_spectrum_kernel, scale=1.0 / n),
        out_shape=(oshape, oshape),
        grid=(n // tk,),
        in_specs=[tspec, tspec, gspec, vspec, vspec],
        out_specs=(ospec, ospec),
        compiler_params=_cp(("parallel",)),
        name=f"filter_spectrum_{n}",
    )(ctab, stab, g, ck, sk)


def _hyena_kernel(c_ref, s_ref, z_ref, zt_ref, x0_ref, kre_ref, kim_ref, bias_ref, o_ref,
                  yre_scr, yim_scr, *, nb, tk):
    phase = pl.program_id(1)
    i = pl.program_id(2)
    ct = c_ref[...]
    st = s_ref[...]

    @pl.when(phase == 0)
    def _():
        kre = kre_ref[...]
        kim = kim_ref[...]
        r0 = pl.multiple_of(i * tk, tk)
        for j in range(nb):
            ure = _dot(ct, z_ref[j])
            uim = -_dot(st, z_ref[j])
            yre_scr[j, pl.ds(r0, tk), :] = (kre * ure - kim * uim).astype(BF16)
            yim_scr[j, pl.ds(r0, tk), :] = (kre * uim + kim * ure).astype(BF16)

    @pl.when(phase == 1)
    def _():
        for j in range(nb):
            y = _dot(ct, yre_scr[j]) - _dot(st, yim_scr[j])
            u = zt_ref[j].astype(F32)
            o_ref[j] = (x0_ref[j].astype(F32) * (y + u * bias_ref[...])).astype(BF16)


def _hyena(z, x0c, kre, kim, bias, ctab, stab):
    b, n, hw = z.shape
    nb = 2
    tk = min(n, 512)
    tspec = pl.BlockSpec((tk, n), lambda g, p, i: (i, 0))
    tile = pl.BlockSpec((nb, tk, hw), lambda g, p, i: (g, i * p, 0))
    kspec = pl.BlockSpec((tk, hw), lambda g, p, i: (i * (1 - p), 0))
    return pl.pallas_call(
        functools.partial(_hyena_kernel, nb=nb, tk=tk),
        out_shape=jax.ShapeDtypeStruct((b, n, hw), BF16),
        grid=(b // nb, 2, n // tk),
        in_specs=[
            tspec, tspec,
            pl.BlockSpec((nb, n, hw), lambda g, p, i: (g, 0, 0)),
            tile, tile, kspec, kspec,
            pl.BlockSpec((1, hw), lambda g, p, i: (0, 0)),
        ],
        out_specs=tile,
        scratch_shapes=[pltpu.VMEM((nb, n, hw), BF16), pltpu.VMEM((nb, n, hw), BF16)],
        compiler_params=_cp(("parallel", "arbitrary", "arbitrary")),
        name=f"hyena_conv_{n}",
    )(ctab, stab, z, z, x0c, kre, kim, bias)


def _mixout_kernel(a_ref, f_ref, h_ref, w_ref, x_ref, ga_ref, g_ref, o_ref):
    aw = a_ref.shape[2]
    fw = f_ref.shape[2]
    mix = (_dot(a_ref[0], w_ref[0:aw, :]) + _dot(f_ref[0], w_ref[aw:aw + fw, :])
           + _dot(h_ref[0], w_ref[aw + fw:, :]))
    o_ref[0] = x_ref[0] + ga_ref[...] * _rms(mix, g_ref[...])


def _mixout(attn, four, hy, w_out, x, ga, g):
    b, n, d = x.shape
    tm = min(n, 1024)
    row = lambda width: pl.BlockSpec((1, tm, width), lambda bi, i: (bi, i, 0))
    return pl.pallas_call(
        _mixout_kernel,
        out_shape=jax.ShapeDtypeStruct((b, n, d), F32),
        grid=(b, n // tm),
        in_specs=[
            row(attn.shape[2]), row(four.shape[2]), row(hy.shape[2]),
            _layer_spec(*w_out),
            row(d),
            _mod_spec(*ga),
            pl.BlockSpec((1, d), lambda bi, i: (0, 0)),
        ],
        out_specs=row(d),
        compiler_params=_cp(("parallel", "parallel")),
        name=f"mixout_{n}",
    )(attn, four, hy, w_out[0], x, ga[0], g)


HALO = 8


FFN_CHUNK = 256
FFN_SUB = 512


def _ffn_kernel(xp_ref, x_ref, xn_ref, sc_ref, sh_ref, g_ref, wu_ref, cw_ref, cb_ref, wd_ref,
                ga_ref, go_ref, o_ref, *, tm):
    i = pl.program_id(1)
    nrow = pl.num_programs(1)
    sub = min(tm, FFN_SUB)
    rows = sub + 2 * HALO
    dff = wd_ref.shape[0]
    cf = FFN_CHUNK
    g = g_ref[...]
    sc = 1.0 + sc_ref[...]
    sh = sh_ref[...]

    def pre(xr):
        return _rms(xr, g) * sc + sh

    def conv(u, w, bias):
        prev = pltpu.roll(u, 1, 0)[HALO:HALO + sub]
        nxt = pltpu.roll(u, rows - 1, 0)[HALO:HALO + sub]
        return prev * w[0:1] + u[HALO:HALO + sub] * w[1:2] + nxt * w[2:3] + bias

    for r0 in range(0, tm, sub):
        x = x_ref[0, r0:r0 + sub, :]
        if r0 == 0:
            before = pre(xp_ref[0]) * jnp.where(i > 0, 1.0, 0.0)
        else:
            before = pre(x_ref[0, r0 - HALO:r0, :])
        if r0 + sub == tm:
            after = pre(xn_ref[0]) * jnp.where(i < nrow - 1, 1.0, 0.0)
        else:
            after = pre(x_ref[0, r0 + sub:r0 + sub + HALO, :])
        fx = jnp.concatenate([before, pre(x), after], axis=0).astype(BF16)
        acts = []
        for c in range(dff // cf):
            lo, hi = c * cf, (c + 1) * cf
            gate = conv(_dot(fx, wu_ref[:, lo:hi]), cw_ref[:, lo:hi], cb_ref[:, lo:hi])
            val = conv(_dot(fx, wu_ref[:, dff + lo:dff + hi]), cw_ref[:, dff + lo:dff + hi],
                       cb_ref[:, dff + lo:dff + hi])
            acts.append((gate * jax.nn.sigmoid(gate) * val).astype(BF16))
        y = _dot(jnp.concatenate(acts, axis=1), wd_ref[...])
        o_ref[0, r0:r0 + sub, :] = x + ga_ref[...] * _rms(y, go_ref[...])


def _ffn(x, sc, sh, g, w_up, w_conv, b_conv, w_down, ga, g_post):
    b, n, d = x.shape
    tm = min(n, 2 * FFN_SUB)
    hb = tm // HALO
    nhb = n // HALO
    gvec =pl.BlockSpec((1, d), lambda bi, i: (0, 0))
    resident = lambda a: pl.BlockSpec(a.shape, lambda bi, i: (0, 0), pipeline_mode=pl.Buffered(1))
    stacked = lambda w: _layer_spec(*w, pipeline_mode=pl.Buffered(1))
    return pl.pallas_call(
        functools.partial(_ffn_kernel, tm=tm),
        out_shape=jax.ShapeDtypeStruct((b, n, d), F32),
        grid=(b, n // tm),
        in_specs=[
            pl.BlockSpec((1, HALO, d), lambda bi, i: (bi, jnp.maximum(i * hb - 1, 0), 0)),
            pl.BlockSpec((1, tm, d), lambda bi, i: (bi, i, 0)),
            pl.BlockSpec((1, HALO, d), lambda bi, i: (bi, jnp.minimum((i + 1) * hb, nhb - 1), 0)),
            _mod_spec(*sc), _mod_spec(*sh), gvec,
            stacked(w_up), resident(w_conv), resident(b_conv), stacked(w_down),
            _mod_spec(*ga), gvec,
        ],
        out_specs=pl.BlockSpec((1, tm, d), lambda bi, i: (bi, i, 0)),
        compiler_params=_cp(("parallel", "parallel")),
        name=f"conv_ffn_{n}",
    )(x, x, x, sc[0], sh[0], g, w_up[0], w_conv, b_conv, w_down[0], ga[0], g_post)


def _rope_tables(n):
    half = HEAD_DIM // 4
    inv = ROPE_THETA ** (-jnp.arange(0, 2 * half, 2, dtype=F32) / (2 * half))
    pos = jnp.arange(n, dtype=jnp.int32)
    row = (pos // GRID_W).astype(F32)
    col = (pos % GRID_W).astype(F32)
    ang_r = row[:, None] * inv[None, :]
    ang_c = col[:, None] * inv[None, :]
    cr, sr, cc, sc = jnp.cos(ang_r), jnp.sin(ang_r), jnp.cos(ang_c), jnp.sin(ang_c)
    cos64 = jnp.concatenate([cr, cr, cc, cc], axis=-1)
    sin64 = jnp.concatenate([-sr, sr, -sc, sc], axis=-1)
    return jnp.tile(cos64, (1, 2)), jnp.tile(sin64, (1, 2))


def _head_ones(width):
    idx = np.arange(width) // HEAD_DIM
    return jnp.asarray((idx[:, None] == idx[None, :]).astype(np.float32), BF16)


def _channel_dft(width):
    c = np.arange(width)
    same = (c[:, None] // FOURIER_GROUP_DIM) == (c[None, :] // FOURIER_GROUP_DIM)
    ang = 2.0 * np.pi * ((c[:, None] % FOURIER_GROUP_DIM) * (c[None, :] % FOURIER_GROUP_DIM)
                         % FOURIER_GROUP_DIM) / FOURIER_GROUP_DIM
    cs = np.concatenate([np.where(same, np.cos(ang), 0.0), np.where(same, np.sin(ang), 0.0)], axis=1)
    return jnp.asarray(cs.astype(np.float32), BF16)


def kernel(x, c, ctx, c_ctx, w_mod, b_mod, g_pre_mix, g_post_mix, g_pre_ffn, g_post_ffn, w_in, g_q, g_k, w_fourier, w_hy_conv, b_hy_conv, hy_w1, hy_b1, hy_fr1, hy_w2, hy_b2, hy_fr2, hy_w3, hy_bias, w_out, w_up, w_ffn_conv, b_ffn_conv, w_down):
    bsz, seq, d = x.shape
    clen = ctx.shape[1]
    depth = w_mod.shape[0]
    fw = w_fourier.shape[1]
    hw = hy_bias.shape[1]
    aw = w_in.shape[2] - 2 * N_KV_HEADS * HEAD_DIM - fw - 3 * hw
    k0 = aw
    f0 = aw + 2 * N_KV_HEADS * HEAD_DIM

    nrows = -(-(bsz + 1) // 8) * 8
    cc = jnp.zeros((nrows, d), F32).at[:bsz].set(c).at[bsz].set(c_ctx)
    mods = _modulation(cc, w_mod, b_mod).reshape(depth, nrows, 6, 1, d)
    w_in_b = w_in.astype(BF16)
    w_out_b = w_out.astype(BF16)
    w_up_b = w_up.astype(BF16)
    w_down_b = w_down.astype(BF16)

    cos_x, sin_x = _rope_tables(seq)
    cos_c = jnp.ones((clen, LANES), F32)
    sin_c = jnp.zeros((clen, LANES), F32)
    ones_bd = _head_ones(aw)
    cs = _channel_dft(fw)
    hc_x, hs_x, fc_x, fs_x = _make_tables(seq)
    hc_c, hs_c, fc_c, fs_c = _make_tables(clen)

    def row1(v):
        return v.reshape(1, -1)

    for i in range(depth):
        last = i == depth - 1
        sh1, sc1, ga1, sh2, sc2, ga2 = ((mods, i, lambda bi: bi, t) for t in range(6))
        csh1, csc1, cga1, csh2, csc2, cga2 = ((mods, i, lambda bi: bsz, t) for t in range(6))
        win, wout, wup, wdown = ((w, i, None) for w in (w_in_b, w_out_b, w_up_b, w_down_b))
        w_f_b = w_fourier[i].astype(BF16)
        gq_t = jnp.tile(g_q[i], aw // HEAD_DIM).reshape(1, aw)
        gk_t = jnp.tile(g_k[i], N_KV_HEADS).reshape(1, N_KV_HEADS * HEAD_DIM)
        g_pre = row1(g_pre_mix[i])
        g_post = row1(g_post_mix[i])
        hy_params = (hy_w1[i], hy_b1[i], hy_fr1[i], hy_w2[i], hy_b2[i], hy_fr2[i], hy_w3[i])
        hbias = row1(hy_bias[i])
        hcw = w_hy_conv[i]
        hcb = row1(b_hy_conv[i])

        q, k4, v4, fa, ph = _project(x, sc1, sh1, g_pre, win, gq_t, gk_t, cos_x, sin_x,
                                     ones_bd, cs, rope=True, kv_only=False)
        if last:
            kc4, vc4 = _project(ctx, csc1, csh1, g_pre, (w_in_b, i, (k0, f0 - k0)), gq_t, gk_t,
                                cos_c, sin_c, ones_bd, cs, rope=False, kv_only=True)
        else:
            qc, kc4, vc4, fac, phc = _project(ctx, csc1, csh1, g_pre, win, gq_t, gk_t, cos_c,
                                              sin_c, ones_bd, cs, rope=False, kv_only=False)
        attn_x = _attention(q, [(k4, v4), (kc4, vc4)])
        four_x = _fourier(fa, fc_x, fs_x, w_f_b)
        kre, kim = _filter_spectrum(_hyena_filter(seq, *hy_params), hc_x, hs_x)
        z, x0c = _hyena_pre(ph, hcw, hcb)
        hy_x = _hyena(z, x0c, kre, kim, hbias, hc_x, hs_x)
        x_new = _mixout(attn_x, four_x, hy_x, wout, x, ga1, g_post)

        if not last:
            attn_c = _attention(qc, [(kc4, vc4)])
            four_c = _fourier(fac, fc_c, fs_c, w_f_b)
            kre_c, kim_c = _filter_spectrum(_hyena_filter(clen, *hy_params), hc_c, hs_c)
            zc, x0cc = _hyena_pre(phc, hcw, hcb)
            hy_c = _hyena(zc, x0cc, kre_c, kim_c, hbias, hc_c, hs_c)
            ctx = _mixout(attn_c, four_c, hy_c, wout, ctx, cga1, g_post)
        x = x_new

        g_pf = row1(g_pre_ffn[i])
        g_of = row1(g_post_ffn[i])
        fcw = w_ffn_conv[i]
        fcb = row1(b_ffn_conv[i])
        x = _ffn(x, sc2, sh2, g_pf, wup, fcw, fcb, wdown, ga2, g_of)
        if not last:
            ctx = _ffn(ctx, csc2, csh2, g_pf, wup, fcw, fcb, wdown, cga2, g_of)

    return x
```

```python
import functools
import math

import numpy as np
import jax
import jax.numpy as jnp
from jax import lax
from jax.experimental import pallas as pl
from jax.experimental.pallas import tpu as pltpu

F32 = jnp.float32
BF16 = jnp.bfloat16

HEAD_DIM = 64
GQA_GROUP = 4
N_KV_HEADS = 2
GRID_W = 64
ROPE_THETA = 10000.0
FOURIER_GROUP_DIM = 64
HYENA_BANDS = 16
HYENA_FAST_DECAY = 0.3
HYENA_SLOW_DECAY = 1.5
HYENA_TARGET = 1e-2
NORM_EPS = 1e-6
LANES = 128
VMEM_LIMIT = 56 * 1024 * 1024


def _cp(sem, vmem=VMEM_LIMIT):
    return pltpu.CompilerParams(dimension_semantics=sem, vmem_limit_bytes=vmem)


def _dot(a, b):
    return jnp.dot(a, b, preferred_element_type=F32)


def _dot_nt(a, b):
    return lax.dot_general(a, b, (((1,), (1,)), ((), ())), preferred_element_type=F32)


def _split(a):
    hi = a.astype(BF16)
    lo = (a - hi.astype(F32)).astype(BF16)
    return hi, lo


def _dot3(a, b):
    ah, al = _split(a)
    bh, bl = _split(b)
    return _dot(ah, bh) + _dot(ah, bl) + _dot(al, bh)


def _rms(x, g):
    ms = jnp.mean(x * x, axis=-1, keepdims=True)
    return x * lax.rsqrt(ms + NORM_EPS) * g


def _mod_spec(mod, layer, row_fn, slot):
    return pl.BlockSpec((None, None, None, 1, mod.shape[-1]),
                        lambda bi, i: (layer, row_fn(bi), slot, 0, 0))


def _layer_spec(w, layer, col_block=None, **kw):
    if col_block is None:
        return pl.BlockSpec((None,) + w.shape[1:], lambda bi, i: (layer, 0, 0), **kw)
    start, width = col_block
    return pl.BlockSpec((None, w.shape[1], width), lambda bi, i: (layer, 0, start // width), **kw)


def _tables_kernel(hc_ref, hs_ref, fc_ref, fs_ref, hca, hsa, fca, fsa, *, n, tk):
    i = pl.program_id(0)
    hyena_step = 2.0 * math.pi / (8 * n)
    fourier_step = 2.0 * math.pi / n

    @pl.when(i == 0)
    def _():
        k = lax.broadcasted_iota(jnp.int32, (tk, n), 0)
        s = lax.broadcasted_iota(jnp.int32, (tk, n), 1)
        a = (((2 * k + 1) * (2 * s + 1)) & (8 * n - 1)).astype(F32) * hyena_step
        hca[...] = jnp.cos(a)
        hsa[...] = jnp.sin(a)
        kh = lax.broadcasted_iota(jnp.int32, (tk, n // 2), 0)
        sh = lax.broadcasted_iota(jnp.int32, (tk, n // 2), 1)
        b = ((kh * sh) & (n - 1)).astype(F32) * fourier_step
        fca[...] = jnp.cos(b)
        fsa[...] = jnp.sin(b)

    s1 = lax.broadcasted_iota(jnp.int32, (1, n), 1)
    rot = (((2 * tk * i) * (2 * s1 + 1)) & (8 * n - 1)).astype(F32) * hyena_step
    cb, sb = jnp.cos(rot), jnp.sin(rot)
    hc_ref[...] = (hca[...] * cb - hsa[...] * sb).astype(BF16)
    hs_ref[...] = (hsa[...] * cb + hca[...] * sb).astype(BF16)
    s2 = lax.broadcasted_iota(jnp.int32, (1, n // 2), 1)
    rot = (((tk * i) * s2) & (n - 1)).astype(F32) * fourier_step
    cb, sb = jnp.cos(rot), jnp.sin(rot)
    fc_ref[...] = (fca[...] * cb - fsa[...] * sb).astype(BF16)
    fs_ref[...] = (fsa[...] * cb + fca[...] * sb).astype(BF16)


def _make_tables(n):
    tk = min(n, 256)
    spec = pl.BlockSpec((tk, n), lambda i: (i, 0))
    shp = jax.ShapeDtypeStruct((n, n), BF16)
    hspec = pl.BlockSpec((tk, n // 2), lambda i: (i, 0))
    hshp = jax.ShapeDtypeStruct((n, n // 2), BF16)
    return pl.pallas_call(
        functools.partial(_tables_kernel, n=n, tk=tk),
        out_shape=(shp, shp, hshp, hshp),
        grid=(n // tk,),
        out_specs=(spec, spec, hspec, hspec),
        scratch_shapes=[pltpu.VMEM((tk, n), F32)] * 2 + [pltpu.VMEM((tk, n // 2), F32)] * 2,
        compiler_params=_cp(("arbitrary",)),
        name=f"dft_tables_{n}",
    )()


def _mod_kernel(c_ref, w_ref, b_ref, o_ref):
    c = c_ref[...]
    a = c * jax.nn.sigmoid(c)
    o_ref[0] = _dot3(a, w_ref[0]) + b_ref[0]


def _modulation(cc, w_mod, b_mod):
    depth, d, n6 = w_mod.shape
    tn = 1536
    rows = cc.shape[0]
    return pl.pallas_call(
        _mod_kernel,
        out_shape=jax.ShapeDtypeStruct((depth, rows, n6), F32),
        grid=(depth, n6 // tn),
        in_specs=[
            pl.BlockSpec((rows, d), lambda l, j: (0, 0)),
            pl.BlockSpec((1, d, tn), lambda l, j: (l, 0, j)),
            pl.BlockSpec((1, 1, tn), lambda l, j: (l, 0, j)),
        ],
        out_specs=pl.BlockSpec((1, rows, tn), lambda l, j: (l, 0, j)),
        compiler_params=_cp(("parallel", "parallel")),
        name="modulation",
    )(cc, w_mod, b_mod.reshape(depth, 1, n6))


def _head_norm(p, gain, ones_bd):
    hi, lo = _split(p * p)
    ss = _dot(hi, ones_bd) + _dot(lo, ones_bd)
    return p * lax.rsqrt(ss * (1.0 / HEAD_DIM) + NORM_EPS) * gain


def _rope(xn, cos_t, sin_t):
    lane = lax.broadcasted_iota(jnp.int32, (1, LANES), 1)
    first = (lane % 32) < 16
    outs = []
    for j in range(xn.shape[1] // LANES):
        c = xn[:, j * LANES:(j + 1) * LANES]
        sw = jnp.where(first, pltpu.roll(c, LANES - 16, 1), pltpu.roll(c, 16, 1))
        outs.append(c * cos_t + sw * sin_t)
    return outs[0] if len(outs) == 1 else jnp.concatenate(outs, axis=1)


def _tile_heads(kv):
    lane = lax.broadcasted_iota(jnp.int32, (1, LANES), 1)
    low = lane < HEAD_DIM
    r = pltpu.roll(kv, HEAD_DIM, 1)
    h0 = jnp.where(low, kv, r)
    h1 = jnp.where(low, r, kv)
    return jnp.concatenate([h0, h0], axis=1), jnp.concatenate([h1, h1], axis=1)


VALUE_ROWS = 2 * HEAD_DIM


def _value_heads(v):
    lane = lax.broadcasted_iota(jnp.int32, (1, LANES), 1)
    low = lane < HEAD_DIM
    r = pltpu.roll(v, HEAD_DIM, 1)
    return jnp.where(low, v, 1.0), jnp.where(low, r, 1.0)


PROJ_SUB = 512


def _proj_kernel(x_ref, sc_ref, sh_ref, g_ref, w_ref, gq_ref, gk_ref, cos_ref, sin_ref,
                 ones_ref, cs_ref, *outs, rope, kv_only):
    tm = x_ref.shape[1]
    sub = min(tm, PROJ_SUB)
    ones_bd = ones_ref[...]
    for r0 in range(0, tm, sub):
        rs = slice(r0, r0 + sub)
        x = x_ref[0, rs, :]
        h = _rms(x, g_ref[...]) * (1.0 + sc_ref[...]) + sh_ref[...]
        px = _dot(h.astype(BF16), w_ref[...])
        if kv_only:
            k4_ref, v4_ref = outs
            k = px[:, 0:128]
            v = px[:, 128:256]
        else:
            q_ref, k4_ref, v4_ref, fa_ref, ph_ref = outs
            aw = ones_bd.shape[0]
            q = _head_norm(px[:, 0:aw], gq_ref[...], ones_bd)
            if rope:
                q = _rope(q, cos_ref[rs, :], sin_ref[rs, :])
            q_ref[0, rs, :] = (q * (HEAD_DIM ** -0.5 * math.log2(math.e))).astype(BF16)
            k = px[:, aw:aw + 128]
            v = px[:, aw + 128:aw + 256]
            f0 = aw + 256
            fw = cs_ref.shape[0]
            fa_ref[0, rs, :] = _dot(px[:, f0:f0 + fw].astype(BF16), cs_ref[...]).astype(BF16)
            ph_ref[0, rs, :] = px[:, f0 + fw:].astype(BF16)
        k = _head_norm(k, gk_ref[...], ones_bd[0:128, 0:128])
        if rope:
            k = _rope(k, cos_ref[rs, :], sin_ref[rs, :])
        k0, k1 = _tile_heads(k)
        k4_ref[0, 0, rs, :] = k0.astype(BF16)
        k4_ref[0, 1, rs, :] = k1.astype(BF16)
        v0, v1 = _value_heads(v)
        v4_ref[0, 0, :, rs] = v0.T[0:VALUE_ROWS].astype(BF16)
        v4_ref[0, 1, :, rs] = v1.T[0:VALUE_ROWS].astype(BF16)


def _project(x, sc, sh, g, w, gq, gk, cos_t, sin_t, ones_bd, cs, *, rope, kv_only):
    b, n, d = x.shape
    tm = min(n, 2 * PROJ_SUB)
    wn = w[0].shape[2]
    aw = ones_bd.shape[0]
    fw = cs.shape[0]
    full = lambda a: pl.BlockSpec(a.shape, lambda bi, i: (0,) * a.ndim)
    kv_spec = pl.BlockSpec((1, N_KV_HEADS, tm, 256), lambda bi, i: (bi, 0, i, 0))
    kv_shape = jax.ShapeDtypeStruct((b, N_KV_HEADS, n, 256), BF16)
    vl_spec = pl.BlockSpec((1, N_KV_HEADS, VALUE_ROWS, tm), lambda bi, i: (bi, 0, 0, i))
    vl_shape = jax.ShapeDtypeStruct((b, N_KV_HEADS, VALUE_ROWS, n), BF16)
    row = lambda width: pl.BlockSpec((1, tm, width), lambda bi, i: (bi, i, 0))
    if kv_only:
        out_shape = (kv_shape, vl_shape)
        out_specs = (kv_spec, vl_spec)
    else:
        hw = wn - aw - 256 - fw
        out_shape = (jax.ShapeDtypeStruct((b, n, aw), BF16), kv_shape, vl_shape,
                     jax.ShapeDtypeStruct((b, n, 2 * fw), BF16),
                     jax.ShapeDtypeStruct((b, n, hw), BF16))
        out_specs = (row(aw), kv_spec, vl_spec, row(2 * fw), row(hw))
    return pl.pallas_call(
        functools.partial(_proj_kernel, rope=rope, kv_only=kv_only),
        out_shape=out_shape,
        grid=(b, n // tm),
        in_specs=[
            row(d), _mod_spec(*sc), _mod_spec(*sh), full(g), _layer_spec(*w), full(gq), full(gk),
            pl.BlockSpec((tm, LANES), lambda bi, i: (i, 0)),
            pl.BlockSpec((tm, LANES), lambda bi, i: (i, 0)),
            full(ones_bd), full(cs),
        ],
        out_specs=out_specs,
        compiler_params=_cp(("parallel", "parallel")),
        name="proj_kv" if kv_only else ("proj_rope" if rope else "proj_ctx"),
    )(x, sc[0], sh[0], g, w[0], gq, gk, cos_t, sin_t, ones_bd, cs)


ATTN_SUB = 128
ATTN_KEY_CHUNK = 1024


ATTN_PROB_ROWS = 128


def _attn_kernel(q_ref, *refs, lks, nsub):
    nsrc = len(lks)
    k_refs = refs[0:2 * nsrc:2]
    v_refs = refs[1:2 * nsrc:2]
    o_ref = refs[2 * nsrc]
    s_scrs = refs[2 * nsrc + 1:2 * nsrc + 3]
    p_scrs = refs[2 * nsrc + 3:2 * nsrc + 5]
    sub = ATTN_SUB
    rows = GQA_GROUP * sub
    gw = GQA_GROUP * HEAD_DIM
    group = lax.broadcasted_iota(jnp.int32, (1, gw), 1) // HEAD_DIM

    chunks = []
    off = 0
    for j, lk in enumerate(lks):
        tkc = min(lk, ATTN_KEY_CHUNK)
        for c in range(lk // tkc):
            chunks.append((j, c * tkc, tkc, off))
            off += tkc

    lane = lax.broadcasted_iota(jnp.int32, (1, LANES), 1)
    low = lane < HEAD_DIM

    def masked_q(u):
        q = q_ref[0, u * sub:(u + 1) * sub, :]
        zero = jnp.zeros_like(q)
        return jnp.concatenate([jnp.where(group == g, q, zero) for g in range(GQA_GROUP)], axis=0)

    def finish(u, acc):
        o_t = acc / acc[HEAD_DIM:HEAD_DIM + 1, :]
        heads = [o_t[:, g * sub:(g + 1) * sub].T for g in range(GQA_GROUP)]
        left = jnp.where(low, heads[0], pltpu.roll(heads[1], HEAD_DIM, 1))
        right = jnp.where(low, heads[2], pltpu.roll(heads[3], HEAD_DIM, 1))
        o_ref[0, u * sub:(u + 1) * sub, :] = jnp.concatenate([left, right], axis=1).astype(BF16)

    def scores(u):
        s_scr = s_scrs[u % 2]
        qm = masked_q(u)
        part = jnp.full((8, rows), -jnp.inf, F32)
        for j, start, tkc, o in chunks:
            s = _dot_nt(k_refs[j][0, 0, start:start + tkc, :], qm)
            s_scr[o:o + tkc, :] = s
            part = jnp.maximum(part, jnp.max(s.reshape(tkc // 8, 8, rows), axis=0))
        return jnp.max(part, axis=0, keepdims=True), s[tkc - 8:tkc, :]

    m, _ = scores(0)
    for u in range(nsub):
        s_scr = s_scrs[u % 2]
        acc = jnp.zeros((VALUE_ROWS, rows), F32)
        if u + 1 < nsub:
            m_next, tail = scores(u + 1)
            bits = pltpu.bitcast(tail, jnp.uint32)
            acc = jnp.concatenate([((bits >> 16) >> 16).astype(F32),
                                   jnp.zeros((VALUE_ROWS - 8, rows), F32)], axis=0)
        for ci, (j, start, tkc, o) in enumerate(chunks):
            p_scr = p_scrs[ci % 2]
            for r in range(0, tkc, ATTN_PROB_ROWS):
                p_scr[r:r + ATTN_PROB_ROWS, :] = jnp.exp2(
                    s_scr[o + r:o + r + ATTN_PROB_ROWS, :] - m).astype(BF16)
            acc = acc + _dot(v_refs[j][0, 0, :, start:start + tkc], p_scr[0:tkc, :])
        finish(u, acc)
        if u + 1 < nsub:
            m = m_next


def _attention(q, kvs):
    b, lq, aw = q.shape
    tq = min(lq, 8 * ATTN_SUB)
    nsub = tq // ATTN_SUB
    lks = tuple(k.shape[2] for k, _ in kvs)
    gw = GQA_GROUP * HEAD_DIM
    in_specs = [pl.BlockSpec((1, tq, gw), lambda bi, h, i: (bi, i, h))]
    args = [q]
    for k4, vlt in kvs:
        for a in (k4, vlt):
            in_specs.append(pl.BlockSpec((1, 1) + a.shape[2:], lambda bi, h, i: (bi, h, 0, 0),
                                         pipeline_mode=pl.Buffered(1)))
        args += [k4, vlt]
    return pl.pallas_call(
        functools.partial(_attn_kernel, lks=lks, nsub=nsub),
        out_shape=jax.ShapeDtypeStruct((b, lq, aw), BF16),
        grid=(b, N_KV_HEADS, lq // tq),
        in_specs=in_specs,
        out_specs=pl.BlockSpec((1, tq, gw), lambda bi, h, i: (bi, i, h)),
        scratch_shapes=[pltpu.VMEM((sum(lks), GQA_GROUP * ATTN_SUB), F32)] * 2
        + [pltpu.VMEM((min(max(lks), ATTN_KEY_CHUNK), GQA_GROUP * ATTN_SUB), BF16)] * 2,
        compiler_params=_cp(("parallel", "parallel", "arbitrary")),
        name=f"attention_{lq}",
    )(*args)


def _fourier_kernel(c_ref, s_ref, fa_ref, w_ref, o_ref, fold_scr, nyq_scr, *, nb, scale, tk):
    fw = w_ref.shape[0]
    n = fa_ref.shape[1]
    half = n // 2
    blk = min(half, 512)

    @pl.when(pl.program_id(1) == 0)
    def _():
        r = lax.broadcasted_iota(jnp.int32, (blk, blk), 0)
        c = lax.broadcasted_iota(jnp.int32, (blk, blk), 1)
        flip = jnp.where(r + c == blk - 1, 1.0, 0.0).astype(BF16)
        row = lax.broadcasted_iota(jnp.int32, (half, 1), 0)
        sign = jnp.where(lax.broadcasted_iota(jnp.int32, (1, 2 * fw), 1) < fw, 1.0, -1.0)
        for j in range(nb):
            rev = jnp.concatenate(
                [_dot(flip, fa_ref[j, n - (b + 1) * blk:n - b * blk, :]) for b in range(half // blk)],
                axis=0)
            mirrored = jnp.where(row == 0, 0.0, pltpu.roll(rev, 1, 0))
            fold_scr[j] = (fa_ref[j, 0:half, :].astype(F32) + sign * mirrored).astype(BF16)
            nyq_scr[j] = jnp.broadcast_to(fa_ref[j, half:half + 16, 0:fw][0:1].astype(F32), (8, fw))

    ct = c_ref[...]
    st = s_ref[...]
    odd = (lax.broadcasted_iota(jnp.int32, (tk, 1), 0) & 1) == 1
    alt = jnp.where(odd, -1.0, 1.0)
    for j in range(nb):
        y = (_dot(ct, fold_scr[j, :, 0:fw]) - _dot(st, fold_scr[j, :, fw:2 * fw])
             + alt * nyq_scr[j, 0:1, :])
        o_ref[j] = _dot((y * scale).astype(BF16), w_ref[...]).astype(BF16)


def _fourier(fa, ctab, stab, w_f):
    b, n, fw2 = fa.shape
    fw = fw2 // 2
    nb = 2
    tk = min(n, 512)
    half = n // 2
    scale = 1.0 / math.sqrt(n * FOURIER_GROUP_DIM)
    return pl.pallas_call(
        functools.partial(_fourier_kernel, nb=nb, scale=scale, tk=tk),
        out_shape=jax.ShapeDtypeStruct((b, n, fw), BF16),
        grid=(b // nb, n // tk),
        in_specs=[
            pl.BlockSpec((tk, half), lambda g, i: (i, 0)),
            pl.BlockSpec((tk, half), lambda g, i: (i, 0)),
            pl.BlockSpec((nb, n, fw2), lambda g, i: (g, 0, 0)),
            pl.BlockSpec((fw, fw), lambda g, i: (0, 0)),
        ],
        out_specs=pl.BlockSpec((nb, tk, fw), lambda g, i: (g, i, 0)),
        scratch_shapes=[pltpu.VMEM((nb, half, fw2), BF16), pltpu.VMEM((nb, 8, fw), F32)],
        compiler_params=_cp(("parallel", "arbitrary")),
        name=f"fourier_{n}",
    )(ctab, stab, fa, w_f)


def _conv3(p, w, bias):
    n = p.shape[0]
    row = lax.broadcasted_iota(jnp.int32, (n, 1), 0)
    prev = jnp.where(row == 0, 0.0, pltpu.roll(p, 1, 0))
    nxt = jnp.where(row == n - 1, 0.0, pltpu.roll(p, n - 1, 0))
    return prev * w[0:1] + p * w[1:2] + nxt * w[2:3] + bias


def _hyena_pre_kernel(p0_ref, p1_ref, p2_ref, w0_ref, w1_ref, w2_ref, b0_ref, b1_ref, b2_ref,
                      z_ref, x0_ref):
    x0 = _conv3(p0_ref[0].astype(F32), w0_ref[...], b0_ref[...])
    x1 = _conv3(p1_ref[0].astype(F32), w1_ref[...], b1_ref[...])
    v = _conv3(p2_ref[0].astype(F32), w2_ref[...], b2_ref[...])
    z_ref[0] = (x1 * v).astype(BF16)
    x0_ref[0] = x0.astype(BF16)


def _hyena_pre(ph, w, bias):
    b, n, hw3 = ph.shape
    hw = hw3 // 3
    nc = hw // LANES
    pspec = lambda s: pl.BlockSpec((1, n, LANES), lambda bi, c: (bi, 0, s * nc + c))
    wspec = lambda s: pl.BlockSpec((3, LANES), lambda bi, c: (0, s * nc + c))
    bspec = lambda s: pl.BlockSpec((1, LANES), lambda bi, c: (0, s * nc + c))
    ospec = pl.BlockSpec((1, n, LANES), lambda bi, c: (bi, 0, c))
    oshape = jax.ShapeDtypeStruct((b, n, hw), BF16)
    return pl.pallas_call(
        _hyena_pre_kernel,
        out_shape=(oshape, oshape),
        grid=(b, nc),
        in_specs=[pspec(0), pspec(1), pspec(2), wspec(0), wspec(1), wspec(2),
                  bspec(0), bspec(1), bspec(2)],
        out_specs=(ospec, ospec),
        compiler_params=_cp(("parallel", "parallel")),
        name=f"hyena_pre_{n}",
    )(ph, ph, ph, w, w, w, bias, bias, bias)


def _filter_kernel(w1t_ref, w1c_ref, w1s_ref, b1_ref, fr1_ref, w2_ref, b2_ref, fr2_ref, w3_ref,
                   g_ref, *, n):
    hw = w3_ref.shape[1] // 2
    i = lax.broadcasted_iota(jnp.int32, (n, 1), 0).astype(F32)
    t = i / float(n - 1)
    jb = lax.broadcasted_iota(jnp.int32, (1, HYENA_BANDS), 1).astype(F32)
    bands = 1e-4 + jb * ((HYENA_BANDS - 1 - 1e-4) / (HYENA_BANDS - 1))
    ang = ((2.0 * math.pi) * i / float(n)) * bands
    pre = t * w1t_ref[...] + _dot3(jnp.cos(ang), w1c_ref[...]) - _dot3(jnp.sin(ang), w1s_ref[...])
    h = jnp.sin(fr1_ref[...] * (pre + b1_ref[...]))
    h = jnp.sin(fr2_ref[...] * (_dot3(h, w2_ref[...]) + b2_ref[...]))
    h = _dot3(h, w3_ref[...])
    d0 = math.log(HYENA_TARGET) / HYENA_SLOW_DECAY
    d1 = math.log(HYENA_TARGET) / HYENA_FAST_DECAY
    jd = lax.broadcasted_iota(jnp.int32, (1, hw), 1).astype(F32)
    deltas = jnp.abs(d0 + jd * ((d1 - d0) / (hw - 1)))
    decay = jnp.exp(-t * deltas)
    hf = h[:, 0:hw] * decay
    hb = jnp.where(i == 0.0, 0.0, h[:, hw:2 * hw] * decay)
    total = jnp.sum(jnp.abs(hf), axis=0, keepdims=True) + jnp.sum(jnp.abs(hb), axis=0, keepdims=True)
    g_ref[...] = jnp.concatenate([hf / total, hb / total], axis=1).astype(BF16)


def _hyena_filter(n, w1, b1, fr1, w2, b2, fr2, w3):
    nb = HYENA_BANDS
    r = lambda a: a.reshape(1, -1)
    shp = jax.ShapeDtypeStruct((n, w3.shape[1]), BF16)
    return pl.pallas_call(
        functools.partial(_filter_kernel, n=n),
        out_shape=shp,
        compiler_params=pltpu.CompilerParams(vmem_limit_bytes=VMEM_LIMIT),
        name=f"hyena_filter_{n}",
    )(w1[0:1], w1[1:1 + nb], w1[1 + nb:1 + 2 * nb], r(b1), r(fr1), w2, r(b2), r(fr2), w3)


def _spectrum_kernel(c_ref, s_ref, g_ref, ck_ref, sk_ref, kre_ref, kim_ref, *, scale):
    hw = kre_ref.shape[1]
    a = _dot(c_ref[...], g_ref[...])
    b = _dot(s_ref[...], g_ref[...])
    are, bre = a[:, 0:hw], a[:, hw:2 * hw]
    aim, bim = -b[:, 0:hw], -b[:, hw:2 * hw]
    ck = ck_ref[...]
    sk = sk_ref[...]
    kre_ref[...] = (ck * (are + bre) - sk * (aim + bim)) * scale
    kim_ref[...] = (ck * (aim - bim) + sk * (are - bre)) * scale


def _filter_spectrum(g, ctab, stab):
    n, hw2 = g.shape
    hw = hw2 // 2
    tk = min(n, 512)
    half = (2.0 * np.arange(n, dtype=np.float64) + 1.0) * (2.0 * np.pi / (8 * n))
    ck = jnp.asarray(np.cos(half).reshape(n, 1), F32)
    sk = jnp.asarray(np.sin(half).reshape(n, 1), F32)
    tspec = pl.BlockSpec((tk, n), lambda i: (i, 0))
    gspec = pl.BlockSpec((n, hw2), lambda i: (0, 0))
    vspec = pl.BlockSpec((tk, 1), lambda i: (i, 0))
    ospec = pl.BlockSpec((tk, hw), lambda i: (i, 0))
    oshape = jax.ShapeDtypeStruct((n, hw), F32)
    return pl.pallas_call(
        functools.partial(_spectrum_kernel, scale=1.0 / n),
        out_shape=(oshape, oshape),
        grid=(n // tk,),
        in_specs=[tspec, tspec, gspec, vspec, vspec],
        out_specs=(ospec, ospec),
        compiler_params=_cp(("parallel",)),
        name=f"filter_spectrum_{n}",
    )(ctab, stab, g, ck, sk)


def _hyena_kernel(c_ref, s_ref, z_ref, zt_ref, x0_ref, kre_ref, kim_ref, bias_ref, o_ref,
                  yre_scr, yim_scr, *, nb, tk):
    phase = pl.program_id(1)
    i = pl.program_id(2)
    ct = c_ref[...]
    st = s_ref[...]

    @pl.when(phase == 0)
    def _():
        kre = kre_ref[...]
        kim = kim_ref[...]
        r0 = pl.multiple_of(i * tk, tk)
        for j in range(nb):
            ure = _dot(ct, z_ref[j])
            uim = -_dot(st, z_ref[j])
            yre_scr[j, pl.ds(r0, tk), :] = (kre * ure - kim * uim).astype(BF16)
            yim_scr[j, pl.ds(r0, tk), :] = (kre * uim + kim * ure).astype(BF16)

    @pl.when(phase == 1)
    def _():
        for j in range(nb):
            y = _dot(ct, yre_scr[j]) - _dot(st, yim_scr[j])
            u = zt_ref[j].astype(F32)
            o_ref[j] = (x0_ref[j].astype(F32) * (y + u * bias_ref[...])).astype(BF16)


def _hyena(z, x0c, kre, kim, bias, ctab, stab):
    b, n, hw = z.shape
    nb = 2
    tk = min(n, 512)
    tspec = pl.BlockSpec((tk, n), lambda g, p, i: (i, 0))
    tile = pl.BlockSpec((nb, tk, hw), lambda g, p, i: (g, i * p, 0))
    kspec = pl.BlockSpec((tk, hw), lambda g, p, i: (i * (1 - p), 0))
    return pl.pallas_call(
        functools.partial(_hyena_kernel, nb=nb, tk=tk),
        out_shape=jax.ShapeDtypeStruct((b, n, hw), BF16),
        grid=(b // nb, 2, n // tk),
        in_specs=[
            tspec, tspec,
            pl.BlockSpec((nb, n, hw), lambda g, p, i: (g, 0, 0)),
            tile, tile, kspec, kspec,
            pl.BlockSpec((1, hw), lambda g, p, i: (0, 0)),
        ],
        out_specs=tile,
        scratch_shapes=[pltpu.VMEM((nb, n, hw), BF16), pltpu.VMEM((nb, n, hw), BF16)],
        compiler_params=_cp(("parallel", "arbitrary", "arbitrary")),
        name=f"hyena_conv_{n}",
    )(ctab, stab, z, z, x0c, kre, kim, bias)


def _mixout_kernel(a_ref, f_ref, h_ref, w_ref, x_ref, ga_ref, g_ref, o_ref):
    aw = a_ref.shape[2]
    fw = f_ref.shape[2]
    mix = (_dot(a_ref[0], w_ref[0:aw, :]) + _dot(f_ref[0], w_ref[aw:aw + fw, :])
           + _dot(h_ref[0], w_ref[aw + fw:, :]))
    o_ref[0] = x_ref[0] + ga_ref[...] * _rms(mix, g_ref[...])


def _mixout(attn, four, hy, w_out, x, ga, g):
    b, n, d = x.shape
    tm = min(n, 1024)
    row = lambda width: pl.BlockSpec((1, tm, width), lambda bi, i: (bi, i, 0))
    return pl.pallas_call(
        _mixout_kernel,
        out_shape=jax.ShapeDtypeStruct((b, n, d), F32),
        grid=(b, n // tm),
        in_specs=[
            row(attn.shape[2]), row(four.shape[2]), row(hy.shape[2]),
            _layer_spec(*w_out),
            row(d),
            _mod_spec(*ga),
            pl.BlockSpec((1, d), lambda bi, i: (0, 0)),
        ],
        out_specs=row(d),
        compiler_params=_cp(("parallel", "parallel")),
        name=f"mixout_{n}",
    )(attn, four, hy, w_out[0], x, ga[0], g)


HALO = 8


FFN_CHUNK = 256
FFN_SUB = 512


def _ffn_kernel(xp_ref, x_ref, xn_ref, sc_ref, sh_ref, g_ref, wu_ref, cw_ref, cb_ref, wd_ref,
                ga_ref, go_ref, o_ref, *, tm):
    i = pl.program_id(1)
    nrow = pl.num_programs(1)
    sub = min(tm, FFN_SUB)
    rows = sub + 2 * HALO
    dff = wd_ref.shape[0]
    cf = FFN_CHUNK
    g = g_ref[...]
    sc = 1.0 + sc_ref[...]
    sh = sh_ref[...]

    def pre(xr):
        return _rms(xr, g) * sc + sh

    def conv(u, w, bias):
        prev = pltpu.roll(u, 1, 0)[HALO:HALO + sub]
        nxt = pltpu.roll(u, rows - 1, 0)[HALO:HALO + sub]
        return prev * w[0:1] + u[HALO:HALO + sub] * w[1:2] + nxt * w[2:3] + bias

    for r0 in range(0, tm, sub):
        x = x_ref[0, r0:r0 + sub, :]
        if r0 == 0:
            before = pre(xp_ref[0]) * jnp.where(i > 0, 1.0, 0.0)
        else:
            before = pre(x_ref[0, r0 - HALO:r0, :])
        if r0 + sub == tm:
            after = pre(xn_ref[0]) * jnp.where(i < nrow - 1, 1.0, 0.0)
        else:
            after = pre(x_ref[0, r0 + sub:r0 + sub + HALO, :])
        fx = jnp.concatenate([before, pre(x), after], axis=0).astype(BF16)
        acts = []
        for c in range(dff // cf):
            lo, hi = c * cf, (c + 1) * cf
            gate = conv(_dot(fx, wu_ref[:, lo:hi]), cw_ref[:, lo:hi], cb_ref[:, lo:hi])
            val = conv(_dot(fx, wu_ref[:, dff + lo:dff + hi]), cw_ref[:, dff + lo:dff + hi],
                       cb_ref[:, dff + lo:dff + hi])
            acts.append((gate * jax.nn.sigmoid(gate) * val).astype(BF16))
        y = _dot(jnp.concatenate(acts, axis=1), wd_ref[...])
        o_ref[0, r0:r0 + sub, :] = x + ga_ref[...] * _rms(y, go_ref[...])


def _ffn(x, sc, sh, g, w_up, w_conv, b_conv, w_down, ga, g_post):
    b, n, d = x.shape
    tm = min(n, 2 * FFN_SUB)
    hb = tm // HALO
    nhb = n // HALO
    gvec =pl.BlockSpec((1, d), lambda bi, i: (0, 0))
    resident = lambda a: pl.BlockSpec(a.shape, lambda bi, i: (0, 0), pipeline_mode=pl.Buffered(1))
    stacked = lambda w: _layer_spec(*w, pipeline_mode=pl.Buffered(1))
    return pl.pallas_call(
        functools.partial(_ffn_kernel, tm=tm),
        out_shape=jax.ShapeDtypeStruct((b, n, d), F32),
        grid=(b, n // tm),
        in_specs=[
            pl.BlockSpec((1, HALO, d), lambda bi, i: (bi, jnp.maximum(i * hb - 1, 0), 0)),
            pl.BlockSpec((1, tm, d), lambda bi, i: (bi, i, 0)),
            pl.BlockSpec((1, HALO, d), lambda bi, i: (bi, jnp.minimum((i + 1) * hb, nhb - 1), 0)),
            _mod_spec(*sc), _mod_spec(*sh), gvec,
            stacked(w_up), resident(w_conv), resident(b_conv), stacked(w_down),
            _mod_spec(*ga), gvec,
        ],
        out_specs=pl.BlockSpec((1, tm, d), lambda bi, i: (bi, i, 0)),
        compiler_params=_cp(("parallel", "parallel")),
        name=f"conv_ffn_{n}",
    )(x, x, x, sc[0], sh[0], g, w_up[0], w_conv, b_conv, w_down[0], ga[0], g_post)


def _rope_tables(n):
    half = HEAD_DIM // 4
    inv = ROPE_THETA ** (-jnp.arange(0, 2 * half, 2, dtype=F32) / (2 * half))
    pos = jnp.arange(n, dtype=jnp.int32)
    row = (pos // GRID_W).astype(F32)
    col = (pos % GRID_W).astype(F32)
    ang_r = row[:, None] * inv[None, :]
    ang_c = col[:, None] * inv[None, :]
    cr, sr, cc, sc = jnp.cos(ang_r), jnp.sin(ang_r), jnp.cos(ang_c), jnp.sin(ang_c)
    cos64 = jnp.concatenate([cr, cr, cc, cc], axis=-1)
    sin64 = jnp.concatenate([-sr, sr, -sc, sc], axis=-1)
    return jnp.tile(cos64, (1, 2)), jnp.tile(sin64, (1, 2))


def _head_ones(width):
    idx = np.arange(width) // HEAD_DIM
    return jnp.asarray((idx[:, None] == idx[None, :]).astype(np.float32), BF16)


def _channel_dft(width):
    c = np.arange(width)
    same = (c[:, None] // FOURIER_GROUP_DIM) == (c[None, :] // FOURIER_GROUP_DIM)
    ang = 2.0 * np.pi * ((c[:, None] % FOURIER_GROUP_DIM) * (c[None, :] % FOURIER_GROUP_DIM)
                         % FOURIER_GROUP_DIM) / FOURIER_GROUP_DIM
    cs = np.concatenate([np.where(same, np.cos(ang), 0.0), np.where(same, np.sin(ang), 0.0)], axis=1)
    return jnp.asarray(cs.astype(np.float32), BF16)


def kernel(x, c, ctx, c_ctx, w_mod, b_mod, g_pre_mix, g_post_mix, g_pre_ffn, g_post_ffn, w_in, g_q, g_k, w_fourier, w_hy_conv, b_hy_conv, hy_w1, hy_b1, hy_fr1, hy_w2, hy_b2, hy_fr2, hy_w3, hy_bias, w_out, w_up, w_ffn_conv, b_ffn_conv, w_down):
    bsz, seq, d = x.shape
    clen = ctx.shape[1]
    depth = w_mod.shape[0]
    fw = w_fourier.shape[1]
    hw = hy_bias.shape[1]
    aw = w_in.shape[2] - 2 * N_KV_HEADS * HEAD_DIM - fw - 3 * hw
    k0 = aw
    f0 = aw + 2 * N_KV_HEADS * HEAD_DIM

    nrows = -(-(bsz + 1) // 8) * 8
    cc = jnp.zeros((nrows, d), F32).at[:bsz].set(c).at[bsz].set(c_ctx)
    mods = _modulation(cc, w_mod, b_mod).reshape(depth, nrows, 6, 1, d)
    w_in_b = w_in.astype(BF16)
    w_out_b = w_out.astype(BF16)
    w_up_b = w_up.astype(BF16)
    w_down_b = w_down.astype(BF16)

    cos_x, sin_x = _rope_tables(seq)
    cos_c = jnp.ones((clen, LANES), F32)
    sin_c = jnp.zeros((clen, LANES), F32)
    ones_bd = _head_ones(aw)
    cs = _channel_dft(fw)
    hc_x, hs_x, fc_x, fs_x = _make_tables(seq)
    hc_c, hs_c, fc_c, fs_c = _make_tables(clen)

    def row1(v):
        return v.reshape(1, -1)

    for i in range(depth):
        last = i == depth - 1
        sh1, sc1, ga1, sh2, sc2, ga2 = ((mods, i, lambda bi: bi, t) for t in range(6))
        csh1, csc1, cga1, csh2, csc2, cga2 = ((mods, i, lambda bi: bsz, t) for t in range(6))
        win, wout, wup, wdown = ((w, i, None) for w in (w_in_b, w_out_b, w_up_b, w_down_b))
        w_f_b = w_fourier[i].astype(BF16)
        gq_t = jnp.tile(g_q[i], aw // HEAD_DIM).reshape(1, aw)
        gk_t = jnp.tile(g_k[i], N_KV_HEADS).reshape(1, N_KV_HEADS * HEAD_DIM)
        g_pre = row1(g_pre_mix[i])
        g_post = row1(g_post_mix[i])
        hy_params = (hy_w1[i], hy_b1[i], hy_fr1[i], hy_w2[i], hy_b2[i], hy_fr2[i], hy_w3[i])
        hbias = row1(hy_bias[i])
        hcw = w_hy_conv[i]
        hcb = row1(b_hy_conv[i])

        q, k4, v4, fa, ph = _project(x, sc1, sh1, g_pre, win, gq_t, gk_t, cos_x, sin_x,
                                     ones_bd, cs, rope=True, kv_only=False)
        if last:
            kc4, vc4 = _project(ctx, csc1, csh1, g_pre, (w_in_b, i, (k0, f0 - k0)), gq_t, gk_t,
                                cos_c, sin_c, ones_bd, cs, rope=False, kv_only=True)
        else:
            qc, kc4, vc4, fac, phc = _project(ctx, csc1, csh1, g_pre, win, gq_t, gk_t, cos_c,
                                              sin_c, ones_bd, cs, rope=False, kv_only=False)
        attn_x = _attention(q, [(k4, v4), (kc4, vc4)])
        four_x = _fourier(fa, fc_x, fs_x, w_f_b)
        kre, kim = _filter_spectrum(_hyena_filter(seq, *hy_params), hc_x, hs_x)
        z, x0c = _hyena_pre(ph, hcw, hcb)
        hy_x = _hyena(z, x0c, kre, kim, hbias, hc_x, hs_x)
        x_new = _mixout(attn_x, four_x, hy_x, wout, x, ga1, g_post)

        if not last:
            attn_c = _attention(qc, [(kc4, vc4)])
            four_c = _fourier(fac, fc_c, fs_c, w_f_b)
            kre_c, kim_c = _filter_spectrum(_hyena_filter(clen, *hy_params), hc_c, hs_c)
            zc, x0cc = _hyena_pre(phc, hcw, hcb)
            hy_c = _hyena(zc, x0cc, kre_c, kim_c, hbias, hc_c, hs_c)
            ctx = _mixout(attn_c, four_c, hy_c, wout, ctx, cga1, g_post)
        x = x_new

        g_pf = row1(g_pre_ffn[i])
        g_of = row1(g_post_ffn[i])
        fcw = w_ffn_conv[i]
        fcb = row1(b_ffn_conv[i])
        x = _ffn(x, sc2, sh2, g_pf, wup, fcw, fcb, wdown, ga2, g_of)
        if not last:
            ctx = _ffn(ctx, csc2, csh2, g_pf, wup, fcw, fcb, wdown, cga2, g_of)

    return x
```

```python
import functools
import math

import numpy as np
import jax
import jax.numpy as jnp
from jax import lax
from jax.experimental import pallas as pl
from jax.experimental.pallas import tpu as pltpu

F32 = jnp.float32
BF16 = jnp.bfloat16

HEAD_DIM = 64
GQA_GROUP = 4
N_KV_HEADS = 2
GRID_W = 64
ROPE_THETA = 10000.0
FOURIER_GROUP_DIM = 64
HYENA_BANDS = 16
HYENA_FAST_DECAY = 0.3
HYENA_SLOW_DECAY = 1.5
HYENA_TARGET = 1e-2
NORM_EPS = 1e-6
LANES = 128
VMEM_LIMIT = 56 * 1024 * 1024


def _cp(sem, vmem=VMEM_LIMIT):
    return pltpu.CompilerParams(dimension_semantics=sem, vmem_limit_bytes=vmem)


def _dot(a, b):
    return jnp.dot(a, b, preferred_element_type=F32)


def _dot_nt(a, b):
    return lax.dot_general(a, b, (((1,), (1,)), ((), ())), preferred_element_type=F32)


def _split(a):
    hi = a.astype(BF16)
    lo = (a - hi.astype(F32)).astype(BF16)
    return hi, lo


def _dot3(a, b):
    ah, al = _split(a)
    bh, bl = _split(b)
    return _dot(ah, bh) + _dot(ah, bl) + _dot(al, bh)


def _rms(x, g):
    ms = jnp.mean(x * x, axis=-1, keepdims=True)
    return x * lax.rsqrt(ms + NORM_EPS) * g


def _mod_spec(mod, layer, row_fn, slot):
    return pl.BlockSpec((None, None, None, 1, mod.shape[-1]),
                        lambda bi, i: (layer, row_fn(bi), slot, 0, 0))


def _layer_spec(w, layer, col_block=None, **kw):
    if col_block is None:
        return pl.BlockSpec((None,) + w.shape[1:], lambda bi, i: (layer, 0, 0), **kw)
    start, width = col_block
    return pl.BlockSpec((None, w.shape[1], width), lambda bi, i: (layer, 0, start // width), **kw)


def _tables_kernel(hc_ref, hs_ref, fc_ref, fs_ref, hca, hsa, fca, fsa, *, n, tk):
    i = pl.program_id(0)
    hyena_step = 2.0 * math.pi / (8 * n)
    fourier_step = 2.0 * math.pi / n

    @pl.when(i == 0)
    def _():
        k = lax.broadcasted_iota(jnp.int32, (tk, n), 0)
        s = lax.broadcasted_iota(jnp.int32, (tk, n), 1)
        a = (((2 * k + 1) * (2 * s + 1)) & (8 * n - 1)).astype(F32) * hyena_step
        hca[...] = jnp.cos(a)
        hsa[...] = jnp.sin(a)
        kh = lax.broadcasted_iota(jnp.int32, (tk, n // 2), 0)
        sh = lax.broadcasted_iota(jnp.int32, (tk, n // 2), 1)
        b = ((kh * sh) & (n - 1)).astype(F32) * fourier_step
        fca[...] = jnp.cos(b)
        fsa[...] = jnp.sin(b)

    s1 = lax.broadcasted_iota(jnp.int32, (1, n), 1)
    rot = (((2 * tk * i) * (2 * s1 + 1)) & (8 * n - 1)).astype(F32) * hyena_step
    cb, sb = jnp.cos(rot), jnp.sin(rot)
    hc_ref[...] = (hca[...] * cb - hsa[...] * sb).astype(BF16)
    hs_ref[...] = (hsa[...] * cb + hca[...] * sb).astype(BF16)
    s2 = lax.broadcasted_iota(jnp.int32, (1, n // 2), 1)
    rot = (((tk * i) * s2) & (n - 1)).astype(F32) * fourier_step
    cb, sb = jnp.cos(rot), jnp.sin(rot)
    fc_ref[...] = (fca[...] * cb - fsa[...] * sb).astype(BF16)
    fs_ref[...] = (fsa[...] * cb + fca[...] * sb).astype(BF16)


def _make_tables(n):
    tk = min(n, 128)
    spec = pl.BlockSpec((tk, n), lambda i: (i, 0))
    shp = jax.ShapeDtypeStruct((n, n), BF16)
    hspec = pl.BlockSpec((tk, n // 2), lambda i: (i, 0))
    hshp = jax.ShapeDtypeStruct((n, n // 2), BF16)
    return pl.pallas_call(
        functools.partial(_tables_kernel, n=n, tk=tk),
        out_shape=(shp, shp, hshp, hshp),
        grid=(n // tk,),
        out_specs=(spec, spec, hspec, hspec),
        scratch_shapes=[pltpu.VMEM((tk, n), F32)] * 2 + [pltpu.VMEM((tk, n // 2), F32)] * 2,
        compiler_params=_cp(("arbitrary",)),
        name=f"dft_tables_{n}",
    )()


def _mod_kernel(c_ref, w_ref, b_ref, o_ref):
    c = c_ref[...]
    a = c * jax.nn.sigmoid(c)
    o_ref[0] = _dot3(a, w_ref[0]) + b_ref[0]


def _modulation(cc, w_mod, b_mod):
    depth, d, n6 = w_mod.shape
    tn = 1536
    rows = cc.shape[0]
    return pl.pallas_call(
        _mod_kernel,
        out_shape=jax.ShapeDtypeStruct((depth, rows, n6), F32),
        grid=(depth, n6 // tn),
        in_specs=[
            pl.BlockSpec((rows, d), lambda l, j: (0, 0)),
            pl.BlockSpec((1, d, tn), lambda l, j: (l, 0, j)),
            pl.BlockSpec((1, 1, tn), lambda l, j: (l, 0, j)),
        ],
        out_specs=pl.BlockSpec((1, rows, tn), lambda l, j: (l, 0, j)),
        compiler_params=_cp(("parallel", "parallel")),
        name="modulation",
    )(cc, w_mod, b_mod.reshape(depth, 1, n6))


def _head_norm(p, gain, ones_bd):
    hi, lo = _split(p * p)
    ss = _dot(hi, ones_bd) + _dot(lo, ones_bd)
    return p * lax.rsqrt(ss * (1.0 / HEAD_DIM) + NORM_EPS) * gain


def _rope(xn, cos_t, sin_t):
    lane = lax.broadcasted_iota(jnp.int32, (1, LANES), 1)
    first = (lane % 32) < 16
    outs = []
    for j in range(xn.shape[1] // LANES):
        c = xn[:, j * LANES:(j + 1) * LANES]
        sw = jnp.where(first, pltpu.roll(c, LANES - 16, 1), pltpu.roll(c, 16, 1))
        outs.append(c * cos_t + sw * sin_t)
    return outs[0] if len(outs) == 1 else jnp.concatenate(outs, axis=1)


def _tile_heads(kv):
    lane = lax.broadcasted_iota(jnp.int32, (1, LANES), 1)
    low = lane < HEAD_DIM
    r = pltpu.roll(kv, HEAD_DIM, 1)
    h0 = jnp.where(low, kv, r)
    h1 = jnp.where(low, r, kv)
    return jnp.concatenate([h0, h0], axis=1), jnp.concatenate([h1, h1], axis=1)


VALUE_ROWS = 2 * HEAD_DIM


def _value_heads(v):
    lane = lax.broadcasted_iota(jnp.int32, (1, LANES), 1)
    low = lane < HEAD_DIM
    r = pltpu.roll(v, HEAD_DIM, 1)
    return jnp.where(low, v, 1.0), jnp.where(low, r, 1.0)


PROJ_SUB = 512


def _proj_kernel(x_ref, sc_ref, sh_ref, g_ref, w_ref, gq_ref, gk_ref, cos_ref, sin_ref,
                 ones_ref, cs_ref, *outs, rope, kv_only):
    tm = x_ref.shape[1]
    sub = min(tm, PROJ_SUB)
    ones_bd = ones_ref[...]
    for r0 in range(0, tm, sub):
        rs = slice(r0, r0 + sub)
        x = x_ref[0, rs, :]
        h = _rms(x, g_ref[...]) * (1.0 + sc_ref[...]) + sh_ref[...]
        px = _dot(h.astype(BF16), w_ref[...])
        if kv_only:
            k4_ref, v4_ref = outs
            k = px[:, 0:128]
            v = px[:, 128:256]
        else:
            q_ref, k4_ref, v4_ref, fa_ref, ph_ref = outs
            aw = ones_bd.shape[0]
            q = _head_norm(px[:, 0:aw], gq_ref[...], ones_bd)
            if rope:
                q = _rope(q, cos_ref[rs, :], sin_ref[rs, :])
            q_ref[0, rs, :] = (q * (HEAD_DIM ** -0.5 * math.log2(math.e))).astype(BF16)
            k = px[:, aw:aw + 128]
            v = px[:, aw + 128:aw + 256]
            f0 = aw + 256
            fw = cs_ref.shape[0]
            fa_ref[0, rs, :] = _dot(px[:, f0:f0 + fw].astype(BF16), cs_ref[...]).astype(BF16)
            ph_ref[0, rs, :] = px[:, f0 + fw:].astype(BF16)
        k = _head_norm(k, gk_ref[...], ones_bd[0:128, 0:128])
        if rope:
            k = _rope(k, cos_ref[rs, :], sin_ref[rs, :])
        k0, k1 = _tile_heads(k)
        k4_ref[0, 0, rs, :] = k0.astype(BF16)
        k4_ref[0, 1, rs, :] = k1.astype(BF16)
        v0, v1 = _value_heads(v)
        v4_ref[0, 0, :, rs] = v0.T[0:VALUE_ROWS].astype(BF16)
        v4_ref[0, 1, :, rs] = v1.T[0:VALUE_ROWS].astype(BF16)


def _project(x, sc, sh, g, w, gq, gk, cos_t, sin_t, ones_bd, cs, *, rope, kv_only):
    b, n, d = x.shape
    tm = min(n, 2 * PROJ_SUB)
    wn = w[0].shape[2]
    aw = ones_bd.shape[0]
    fw = cs.shape[0]
    full = lambda a: pl.BlockSpec(a.shape, lambda bi, i: (0,) * a.ndim)
    kv_spec = pl.BlockSpec((1, N_KV_HEADS, tm, 256), lambda bi, i: (bi, 0, i, 0))
    kv_shape = jax.ShapeDtypeStruct((b, N_KV_HEADS, n, 256), BF16)
    vl_spec = pl.BlockSpec((1, N_KV_HEADS, VALUE_ROWS, tm), lambda bi, i: (bi, 0, 0, i))
    vl_shape = jax.ShapeDtypeStruct((b, N_KV_HEADS, VALUE_ROWS, n), BF16)
    row = lambda width: pl.BlockSpec((1, tm, width), lambda bi, i: (bi, i, 0))
    if kv_only:
        out_shape = (kv_shape, vl_shape)
        out_specs = (kv_spec, vl_spec)
    else:
        hw = wn - aw - 256 - fw
        out_shape = (jax.ShapeDtypeStruct((b, n, aw), BF16), kv_shape, vl_shape,
                     jax.ShapeDtypeStruct((b, n, 2 * fw), BF16),
                     jax.ShapeDtypeStruct((b, n, hw), BF16))
        out_specs = (row(aw), kv_spec, vl_spec, row(2 * fw), row(hw))
    return pl.pallas_call(
        functools.partial(_proj_kernel, rope=rope, kv_only=kv_only),
        out_shape=out_shape,
        grid=(b, n // tm),
        in_specs=[
            row(d), _mod_spec(*sc), _mod_spec(*sh), full(g), _layer_spec(*w), full(gq), full(gk),
            pl.BlockSpec((tm, LANES), lambda bi, i: (i, 0)),
            pl.BlockSpec((tm, LANES), lambda bi, i: (i, 0)),
            full(ones_bd), full(cs),
        ],
        out_specs=out_specs,
        compiler_params=_cp(("parallel", "parallel")),
        name="proj_kv" if kv_only else ("proj_rope" if rope else "proj_ctx"),
    )(x, sc[0], sh[0], g, w[0], gq, gk, cos_t, sin_t, ones_bd, cs)


ATTN_SUB = 128
ATTN_KEY_CHUNK = 1024


ATTN_PROB_ROWS = 128


def _attn_kernel(q_ref, *refs, lks, nsub):
    nsrc = len(lks)
    k_refs = refs[0:2 * nsrc:2]
    v_refs = refs[1:2 * nsrc:2]
    o_ref = refs[2 * nsrc]
    s_scrs = refs[2 * nsrc + 1:2 * nsrc + 3]
    p_scrs = refs[2 * nsrc + 3:2 * nsrc + 5]
    sub = ATTN_SUB
    rows = GQA_GROUP * sub
    gw = GQA_GROUP * HEAD_DIM
    group = lax.broadcasted_iota(jnp.int32, (1, gw), 1) // HEAD_DIM

    chunks = []
    off = 0
    for j, lk in enumerate(lks):
        tkc = min(lk, ATTN_KEY_CHUNK)
        for c in range(lk // tkc):
            chunks.append((j, c * tkc, tkc, off))
            off += tkc

    lane = lax.broadcasted_iota(jnp.int32, (1, LANES), 1)
    low = lane < HEAD_DIM

    def masked_q(u):
        q = q_ref[0, u * sub:(u + 1) * sub, :]
        zero = jnp.zeros_like(q)
        return jnp.concatenate([jnp.where(group == g, q, zero) for g in range(GQA_GROUP)], axis=0)

    def finish(u, acc):
        o_t = acc / acc[HEAD_DIM:HEAD_DIM + 1, :]
        heads = [o_t[:, g * sub:(g + 1) * sub].T for g in range(GQA_GROUP)]
        left = jnp.where(low, heads[0], pltpu.roll(heads[1], HEAD_DIM, 1))
        right = jnp.where(low, heads[2], pltpu.roll(heads[3], HEAD_DIM, 1))
        o_ref[0, u * sub:(u + 1) * sub, :] = jnp.concatenate([left, right], axis=1).astype(BF16)

    def scores(u):
        s_scr = s_scrs[u % 2]
        qm = masked_q(u)
        part = jnp.full((8, rows), -jnp.inf, F32)
        for j, start, tkc, o in chunks:
            s = _dot_nt(k_refs[j][0, 0, start:start + tkc, :], qm)
            s_scr[o:o + tkc, :] = s
            part = jnp.maximum(part, jnp.max(s.reshape(tkc // 8, 8, rows), axis=0))
        return jnp.max(part, axis=0, keepdims=True), s[tkc - 8:tkc, :]

    m, _ = scores(0)
    for u in range(nsub):
        s_scr = s_scrs[u % 2]
        acc = jnp.zeros((VALUE_ROWS, rows), F32)
        if u + 1 < nsub:
            m_next, tail = scores(u + 1)
            bits = pltpu.bitcast(tail, jnp.uint32)
            acc = jnp.concatenate([((bits >> 16) >> 16).astype(F32),
                                   jnp.zeros((VALUE_ROWS - 8, rows), F32)], axis=0)
        for ci, (j, start, tkc, o) in enumerate(chunks):
            p_scr = p_scrs[ci % 2]
            for r in range(0, tkc, ATTN_PROB_ROWS):
                p_scr[r:r + ATTN_PROB_ROWS, :] = jnp.exp2(
                    s_scr[o + r:o + r + ATTN_PROB_ROWS, :] - m).astype(BF16)
            acc = acc + _dot(v_refs[j][0, 0, :, start:start + tkc], p_scr[0:tkc, :])
        finish(u, acc)
        if u + 1 < nsub:
            m = m_next


def _attention(q, kvs):
    b, lq, aw = q.shape
    tq = min(lq, 8 * ATTN_SUB)
    nsub = tq // ATTN_SUB
    lks = tuple(k.shape[2] for k, _ in kvs)
    gw = GQA_GROUP * HEAD_DIM
    in_specs = [pl.BlockSpec((1, tq, gw), lambda bi, h, i: (bi, i, h))]
    args = [q]
    for k4, vlt in kvs:
        for a in (k4, vlt):
            in_specs.append(pl.BlockSpec((1, 1) + a.shape[2:], lambda bi, h, i: (bi, h, 0, 0),
                                         pipeline_mode=pl.Buffered(1)))
        args += [k4, vlt]
    return pl.pallas_call(
        functools.partial(_attn_kernel, lks=lks, nsub=nsub),
        out_shape=jax.ShapeDtypeStruct((b, lq, aw), BF16),
        grid=(b, N_KV_HEADS, lq // tq),
        in_specs=in_specs,
        out_specs=pl.BlockSpec((1, tq, gw), lambda bi, h, i: (bi, i, h)),
        scratch_shapes=[pltpu.VMEM((sum(lks), GQA_GROUP * ATTN_SUB), F32)] * 2
        + [pltpu.VMEM((min(max(lks), ATTN_KEY_CHUNK), GQA_GROUP * ATTN_SUB), BF16)] * 2,
        compiler_params=_cp(("parallel", "parallel", "arbitrary")),
        name=f"attention_{lq}",
    )(*args)


def _fourier_kernel(c_ref, s_ref, fa_ref, w_ref, o_ref, fold_scr, nyq_scr, *, nb, scale, tk):
    fw = w_ref.shape[0]
    n = fa_ref.shape[1]
    half = n // 2
    blk = min(half, 512)

    @pl.when(pl.program_id(1) == 0)
    def _():
        r = lax.broadcasted_iota(jnp.int32, (blk, blk), 0)
        c = lax.broadcasted_iota(jnp.int32, (blk, blk), 1)
        flip = jnp.where(r + c == blk - 1, 1.0, 0.0).astype(BF16)
        row = lax.broadcasted_iota(jnp.int32, (half, 1), 0)
        sign = jnp.where(lax.broadcasted_iota(jnp.int32, (1, 2 * fw), 1) < fw, 1.0, -1.0)
        for j in range(nb):
            rev = jnp.concatenate(
                [_dot(flip, fa_ref[j, n - (b + 1) * blk:n - b * blk, :]) for b in range(half // blk)],
                axis=0)
            mirrored = jnp.where(row == 0, 0.0, pltpu.roll(rev, 1, 0))
            fold_scr[j] = (fa_ref[j, 0:half, :].astype(F32) + sign * mirrored).astype(BF16)
            nyq_scr[j] = jnp.broadcast_to(fa_ref[j, half:half + 16, 0:fw][0:1].astype(F32), (8, fw))

    ct = c_ref[...]
    st = s_ref[...]
    odd = (lax.broadcasted_iota(jnp.int32, (tk, 1), 0) & 1) == 1
    alt = jnp.where(odd, -1.0, 1.0)
    for j in range(nb):
        y = (_dot(ct, fold_scr[j, :, 0:fw]) - _dot(st, fold_scr[j, :, fw:2 * fw])
             + alt * nyq_scr[j, 0:1, :])
        o_ref[j] = _dot((y * scale).astype(BF16), w_ref[...]).astype(BF16)


def _fourier(fa, ctab, stab, w_f):
    b, n, fw2 = fa.shape
    fw = fw2 // 2
    nb = 2
    tk = min(n, 1024)
    half = n // 2
    scale = 1.0 / math.sqrt(n * FOURIER_GROUP_DIM)
    return pl.pallas_call(
        functools.partial(_fourier_kernel, nb=nb, scale=scale, tk=tk),
        out_shape=jax.ShapeDtypeStruct((b, n, fw), BF16),
        grid=(b // nb, n // tk),
        in_specs=[
            pl.BlockSpec((tk, half), lambda g, i: (i, 0)),
            pl.BlockSpec((tk, half), lambda g, i: (i, 0)),
            pl.BlockSpec((nb, n, fw2), lambda g, i: (g, 0, 0)),
            pl.BlockSpec((fw, fw), lambda g, i: (0, 0)),
        ],
        out_specs=pl.BlockSpec((nb, tk, fw), lambda g, i: (g, i, 0)),
        scratch_shapes=[pltpu.VMEM((nb, half, fw2), BF16), pltpu.VMEM((nb, 8, fw), F32)],
        compiler_params=_cp(("parallel", "arbitrary")),
        name=f"fourier_{n}",
    )(ctab, stab, fa, w_f)


def _conv3(p, w, bias):
    n = p.shape[0]
    row = lax.broadcasted_iota(jnp.int32, (n, 1), 0)
    prev = jnp.where(row == 0, 0.0, pltpu.roll(p, 1, 0))
    nxt = jnp.where(row == n - 1, 0.0, pltpu.roll(p, n - 1, 0))
    return prev * w[0:1] + p * w[1:2] + nxt * w[2:3] + bias


def _hyena_pre_kernel(p0_ref, p1_ref, p2_ref, w0_ref, w1_ref, w2_ref, b0_ref, b1_ref, b2_ref,
                      z_ref, x0_ref):
    x0 = _conv3(p0_ref[0].astype(F32), w0_ref[...], b0_ref[...])
    x1 = _conv3(p1_ref[0].astype(F32), w1_ref[...], b1_ref[...])
    v = _conv3(p2_ref[0].astype(F32), w2_ref[...], b2_ref[...])
    z_ref[0] = (x1 * v).astype(BF16)
    x0_ref[0] = x0.astype(BF16)


def _hyena_pre(ph, w, bias):
    b, n, hw3 = ph.shape
    hw = hw3 // 3
    nc = hw // LANES
    pspec = lambda s: pl.BlockSpec((1, n, LANES), lambda bi, c: (bi, 0, s * nc + c))
    wspec = lambda s: pl.BlockSpec((3, LANES), lambda bi, c: (0, s * nc + c))
    bspec = lambda s: pl.BlockSpec((1, LANES), lambda bi, c: (0, s * nc + c))
    ospec = pl.BlockSpec((1, n, LANES), lambda bi, c: (bi, 0, c))
    oshape = jax.ShapeDtypeStruct((b, n, hw), BF16)
    return pl.pallas_call(
        _hyena_pre_kernel,
        out_shape=(oshape, oshape),
        grid=(b, nc),
        in_specs=[pspec(0), pspec(1), pspec(2), wspec(0), wspec(1), wspec(2),
                  bspec(0), bspec(1), bspec(2)],
        out_specs=(ospec, ospec),
        compiler_params=_cp(("parallel", "parallel")),
        name=f"hyena_pre_{n}",
    )(ph, ph, ph, w, w, w, bias, bias, bias)


def _filter_kernel(w1t_ref, w1c_ref, w1s_ref, b1_ref, fr1_ref, w2_ref, b2_ref, fr2_ref, w3_ref,
                   g_ref, *, n):
    hw = w3_ref.shape[1] // 2
    i = lax.broadcasted_iota(jnp.int32, (n, 1), 0).astype(F32)
    t = i / float(n - 1)
    jb = lax.broadcasted_iota(jnp.int32, (1, HYENA_BANDS), 1).astype(F32)
    bands = 1e-4 + jb * ((HYENA_BANDS - 1 - 1e-4) / (HYENA_BANDS - 1))
    ang = ((2.0 * math.pi) * i / float(n)) * bands
    pre = t * w1t_ref[...] + _dot3(jnp.cos(ang), w1c_ref[...]) - _dot3(jnp.sin(ang), w1s_ref[...])
    h = jnp.sin(fr1_ref[...] * (pre + b1_ref[...]))
    h = jnp.sin(fr2_ref[...] * (_dot3(h, w2_ref[...]) + b2_ref[...]))
    h = _dot3(h, w3_ref[...])
    d0 = math.log(HYENA_TARGET) / HYENA_SLOW_DECAY
    d1 = math.log(HYENA_TARGET) / HYENA_FAST_DECAY
    jd = lax.broadcasted_iota(jnp.int32, (1, hw), 1).astype(F32)
    deltas = jnp.abs(d0 + jd * ((d1 - d0) / (hw - 1)))
    decay = jnp.exp(-t * deltas)
    hf = h[:, 0:hw] * decay
    hb = jnp.where(i == 0.0, 0.0, h[:, hw:2 * hw] * decay)
    total = jnp.sum(jnp.abs(hf), axis=0, keepdims=True) + jnp.sum(jnp.abs(hb), axis=0, keepdims=True)
    g_ref[...] = jnp.concatenate([hf / total, hb / total], axis=1).astype(BF16)


def _hyena_filter(n, w1, b1, fr1, w2, b2, fr2, w3):
    nb = HYENA_BANDS
    r = lambda a: a.reshape(1, -1)
    shp = jax.ShapeDtypeStruct((n, w3.shape[1]), BF16)
    return pl.pallas_call(
        functools.partial(_filter_kernel, n=n),
        out_shape=shp,
        compiler_params=pltpu.CompilerParams(vmem_limit_bytes=VMEM_LIMIT),
        name=f"hyena_filter_{n}",
    )(w1[0:1], w1[1:1 + nb], w1[1 + nb:1 + 2 * nb], r(b1), r(fr1), w2, r(b2), r(fr2), w3)


def _spectrum_kernel(c_ref, s_ref, g_ref, ck_ref, sk_ref, kre_ref, kim_ref, *, scale):
    hw = kre_ref.shape[1]
    a = _dot(c_ref[...], g_ref[...])
    b = _dot(s_ref[...], g_ref[...])
    are, bre = a[:, 0:hw], a[:, hw:2 * hw]
    aim, bim = -b[:, 0:hw], -b[:, hw:2 * hw]
    ck = ck_ref[...]
    sk = sk_ref[...]
    kre_ref[...] = (ck * (are + bre) - sk * (aim + bim)) * scale
    kim_ref[...] = (ck * (aim - bim) + sk * (are - bre)) * scale


def _filter_spectrum(g, ctab, stab):
    n, hw2 = g.shape
    hw = hw2 // 2
    tk = min(n, 512)
    half = (2.0 * np.arange(n, dtype=np.float64) + 1.0) * (2.0 * np.pi / (8 * n))
    ck = jnp.asarray(np.cos(half).reshape(n, 1), F32)
    sk = jnp.asarray(np.sin(half).reshape(n, 1), F32)
    tspec = pl.BlockSpec((tk, n), lambda i: (i, 0))
    gspec = pl.BlockSpec((n, hw2), lambda i: (0, 0))
    vspec = pl.BlockSpec((tk, 1), lambda i: (i, 0))
    ospec = pl.BlockSpec((tk, hw), lambda i: (i, 0))
    oshape = jax.ShapeDtypeStruct((n, hw), F32)
    return pl.pallas_call(
        functools.partial(_spectrum_kernel, scale=1.0 / n),
        out_shape=(oshape, oshape),
        grid=(n // tk,),
        in_specs=[tspec, tspec, gspec, vspec, vspec],
        out_specs=(ospec, ospec),
        compiler_params=_cp(("parallel",)),
        name=f"filter_spectrum_{n}",
    )(ctab, stab, g, ck, sk)


def _hyena_kernel(c_ref, s_ref, z_ref, zt_ref, x0_ref, kre_ref, kim_ref, bias_ref, o_ref,
                  yre_scr, yim_scr, *, nb, tk):
    phase = pl.program_id(1)
    i = pl.program_id(2)
    ct = c_ref[...]
    st = s_ref[...]

    @pl.when(phase == 0)
    def _():
        kre = kre_ref[...]
        kim = kim_ref[...]
        r0 = pl.multiple_of(i * tk, tk)
        for j in range(nb):
            ure = _dot(ct, z_ref[j])
            uim = -_dot(st, z_ref[j])
            yre_scr[j, pl.ds(r0, tk), :] = (kre * ure - kim * uim).astype(BF16)
            yim_scr[j, pl.ds(r0, tk), :] = (kre * uim + kim * ure).astype(BF16)

    @pl.when(phase == 1)
    def _():
        for j in range(nb):
            y = _dot(ct, yre_scr[j]) - _dot(st, yim_scr[j])
            u = zt_ref[j].astype(F32)
            o_ref[j] = (x0_ref[j].astype(F32) * (y + u * bias_ref[...])).astype(BF16)


def _hyena(z, x0c, kre, kim, bias, ctab, stab):
    b, n, hw = z.shape
    nb = 2
    tk = min(n, 512)
    tspec = pl.BlockSpec((tk, n), lambda g, p, i: (i, 0))
    tile = pl.BlockSpec((nb, tk, hw), lambda g, p, i: (g, i * p, 0))
    kspec = pl.BlockSpec((tk, hw), lambda g, p, i: (i * (1 - p), 0))
    return pl.pallas_call(
        functools.partial(_hyena_kernel, nb=nb, tk=tk),
        out_shape=jax.ShapeDtypeStruct((b, n, hw), BF16),
        grid=(b // nb, 2, n // tk),
        in_specs=[
            tspec, tspec,
            pl.BlockSpec((nb, n, hw), lambda g, p, i: (g, 0, 0)),
            tile, tile, kspec, kspec,
            pl.BlockSpec((1, hw), lambda g, p, i: (0, 0)),
        ],
        out_specs=tile,
        scratch_shapes=[pltpu.VMEM((nb, n, hw), BF16), pltpu.VMEM((nb, n, hw), BF16)],
        compiler_params=_cp(("parallel", "arbitrary", "arbitrary")),
        name=f"hyena_conv_{n}",
    )(ctab, stab, z, z, x0c, kre, kim, bias)


def _mixout_kernel(a_ref, f_ref, h_ref, w_ref, x_ref, ga_ref, g_ref, o_ref):
    aw = a_ref.shape[2]
    fw = f_ref.shape[2]
    mix = (_dot(a_ref[0], w_ref[0:aw, :]) + _dot(f_ref[0], w_ref[aw:aw + fw, :])
           + _dot(h_ref[0], w_ref[aw + fw:, :]))
    o_ref[0] = x_ref[0] + ga_ref[...] * _rms(mix, g_ref[...])


def _mixout(attn, four, hy, w_out, x, ga, g):
    b, n, d = x.shape
    tm = min(n, 1024)
    row = lambda width: pl.BlockSpec((1, tm, width), lambda bi, i: (bi, i, 0))
    return pl.pallas_call(
        _mixout_kernel,
        out_shape=jax.ShapeDtypeStruct((b, n, d), F32),
        grid=(b, n // tm),
        in_specs=[
            row(attn.shape[2]), row(four.shape[2]), row(hy.shape[2]),
            _layer_spec(*w_out),
            row(d),
            _mod_spec(*ga),
            pl.BlockSpec((1, d), lambda bi, i: (0, 0)),
        ],
        out_specs=row(d),
        compiler_params=_cp(("parallel", "parallel")),
        name=f"mixout_{n}",
    )(attn, four, hy, w_out[0], x, ga[0], g)


HALO = 8


FFN_CHUNK = 256
FFN_SUB = 512


def _ffn_kernel(xp_ref, x_ref, xn_ref, sc_ref, sh_ref, g_ref, wu_ref, cw_ref, cb_ref, wd_ref,
                ga_ref, go_ref, o_ref, *, tm):
    i = pl.program_id(1)
    nrow = pl.num_programs(1)
    sub = min(tm, FFN_SUB)
    rows = sub + 2 * HALO
    dff = wd_ref.shape[0]
    cf = FFN_CHUNK
    g = g_ref[...]
    sc = 1.0 + sc_ref[...]
    sh = sh_ref[...]

    def pre(xr):
        return _rms(xr, g) * sc + sh

    def conv(u, w, bias):
        prev = pltpu.roll(u, 1, 0)[HALO:HALO + sub]
        nxt = pltpu.roll(u, rows - 1, 0)[HALO:HALO + sub]
        return prev * w[0:1] + u[HALO:HALO + sub] * w[1:2] + nxt * w[2:3] + bias

    for r0 in range(0, tm, sub):
        x = x_ref[0, r0:r0 + sub, :]
        if r0 == 0:
            before = pre(xp_ref[0]) * jnp.where(i > 0, 1.0, 0.0)
        else:
            before = pre(x_ref[0, r0 - HALO:r0, :])
        if r0 + sub == tm:
            after = pre(xn_ref[0]) * jnp.where(i < nrow - 1, 1.0, 0.0)
        else:
            after = pre(x_ref[0, r0 + sub:r0 + sub + HALO, :])
        fx = jnp.concatenate([before, pre(x), after], axis=0).astype(BF16)
        acts = []
        for c in range(dff // cf):
            lo, hi = c * cf, (c + 1) * cf
            gate = conv(_dot(fx, wu_ref[:, lo:hi]), cw_ref[:, lo:hi], cb_ref[:, lo:hi])
            val = conv(_dot(fx, wu_ref[:, dff + lo:dff + hi]), cw_ref[:, dff + lo:dff + hi],
                       cb_ref[:, dff + lo:dff + hi])
            acts.append((gate * jax.nn.sigmoid(gate) * val).astype(BF16))
        y = _dot(jnp.concatenate(acts, axis=1), wd_ref[...])
        o_ref[0, r0:r0 + sub, :] = x + ga_ref[...] * _rms(y, go_ref[...])


def _ffn(x, sc, sh, g, w_up, w_conv, b_conv, w_down, ga, g_post):
    b, n, d = x.shape
    tm = min(n, 2 * FFN_SUB)
    hb = tm // HALO
    nhb = n // HALO
    gvec =pl.BlockSpec((1, d), lambda bi, i: (0, 0))
    resident = lambda a: pl.BlockSpec(a.shape, lambda bi, i: (0, 0), pipeline_mode=pl.Buffered(1))
    stacked = lambda w: _layer_spec(*w, pipeline_mode=pl.Buffered(1))
    return pl.pallas_call(
        functools.partial(_ffn_kernel, tm=tm),
        out_shape=jax.ShapeDtypeStruct((b, n, d), F32),
        grid=(b, n // tm),
        in_specs=[
            pl.BlockSpec((1, HALO, d), lambda bi, i: (bi, jnp.maximum(i * hb - 1, 0), 0)),
            pl.BlockSpec((1, tm, d), lambda bi, i: (bi, i, 0)),
            pl.BlockSpec((1, HALO, d), lambda bi, i: (bi, jnp.minimum((i + 1) * hb, nhb - 1), 0)),
            _mod_spec(*sc), _mod_spec(*sh), gvec,
            stacked(w_up), resident(w_conv), resident(b_conv), stacked(w_down),
            _mod_spec(*ga), gvec,
        ],
        out_specs=pl.BlockSpec((1, tm, d), lambda bi, i: (bi, i, 0)),
        compiler_params=_cp(("parallel", "parallel")),
        name=f"conv_ffn_{n}",
    )(x, x, x, sc[0], sh[0], g, w_up[0], w_conv, b_conv, w_down[0], ga[0], g_post)


def _rope_tables(n):
    half = HEAD_DIM // 4
    inv = ROPE_THETA ** (-jnp.arange(0, 2 * half, 2, dtype=F32) / (2 * half))
    pos = jnp.arange(n, dtype=jnp.int32)
    row = (pos // GRID_W).astype(F32)
    col = (pos % GRID_W).astype(F32)
    ang_r = row[:, None] * inv[None, :]
    ang_c = col[:, None] * inv[None, :]
    cr, sr, cc, sc = jnp.cos(ang_r), jnp.sin(ang_r), jnp.cos(ang_c), jnp.sin(ang_c)
    cos64 = jnp.concatenate([cr, cr, cc, cc], axis=-1)
    sin64 = jnp.concatenate([-sr, sr, -sc, sc], axis=-1)
    return jnp.tile(cos64, (1, 2)), jnp.tile(sin64, (1, 2))


def _head_ones(width):
    idx = np.arange(width) // HEAD_DIM
    return jnp.asarray((idx[:, None] == idx[None, :]).astype(np.float32), BF16)


def _channel_dft(width):
    c = np.arange(width)
    same = (c[:, None] // FOURIER_GROUP_DIM) == (c[None, :] // FOURIER_GROUP_DIM)
    ang = 2.0 * np.pi * ((c[:, None] % FOURIER_GROUP_DIM) * (c[None, :] % FOURIER_GROUP_DIM)
                         % FOURIER_GROUP_DIM) / FOURIER_GROUP_DIM
    cs = np.concatenate([np.where(same, np.cos(ang), 0.0), np.where(same, np.sin(ang), 0.0)], axis=1)
    return jnp.asarray(cs.astype(np.float32), BF16)


def kernel(x, c, ctx, c_ctx, w_mod, b_mod, g_pre_mix, g_post_mix, g_pre_ffn, g_post_ffn, w_in, g_q, g_k, w_fourier, w_hy_conv, b_hy_conv, hy_w1, hy_b1, hy_fr1, hy_w2, hy_b2, hy_fr2, hy_w3, hy_bias, w_out, w_up, w_ffn_conv, b_ffn_conv, w_down):
    bsz, seq, d = x.shape
    clen = ctx.shape[1]
    depth = w_mod.shape[0]
    fw = w_fourier.shape[1]
    hw = hy_bias.shape[1]
    aw = w_in.shape[2] - 2 * N_KV_HEADS * HEAD_DIM - fw - 3 * hw
    k0 = aw
    f0 = aw + 2 * N_KV_HEADS * HEAD_DIM

    nrows = -(-(bsz + 1) // 8) * 8
    cc = jnp.zeros((nrows, d), F32).at[:bsz].set(c).at[bsz].set(c_ctx)
    mods = _modulation(cc, w_mod, b_mod).reshape(depth, nrows, 6, 1, d)
    w_in_b = w_in.astype(BF16)
    w_out_b = w_out.astype(BF16)
    w_up_b = w_up.astype(BF16)
    w_down_b = w_down.astype(BF16)

    cos_x, sin_x = _rope_tables(seq)
    cos_c = jnp.ones((clen, LANES), F32)
    sin_c = jnp.zeros((clen, LANES), F32)
    ones_bd = _head_ones(aw)
    cs = _channel_dft(fw)
    hc_x, hs_x, fc_x, fs_x = _make_tables(seq)
    hc_c, hs_c, fc_c, fs_c = _make_tables(clen)

    def row1(v):
        return v.reshape(1, -1)

    for i in range(depth):
        last = i == depth - 1
        sh1, sc1, ga1, sh2, sc2, ga2 = ((mods, i, lambda bi: bi, t) for t in range(6))
        csh1, csc1, cga1, csh2, csc2, cga2 = ((mods, i, lambda bi: bsz, t) for t in range(6))
        win, wout, wup, wdown = ((w, i, None) for w in (w_in_b, w_out_b, w_up_b, w_down_b))
        w_f_b = w_fourier[i].astype(BF16)
        gq_t = jnp.tile(g_q[i], aw // HEAD_DIM).reshape(1, aw)
        gk_t = jnp.tile(g_k[i], N_KV_HEADS).reshape(1, N_KV_HEADS * HEAD_DIM)
        g_pre = row1(g_pre_mix[i])
        g_post = row1(g_post_mix[i])
        hy_params = (hy_w1[i], hy_b1[i], hy_fr1[i], hy_w2[i], hy_b2[i], hy_fr2[i], hy_w3[i])
        hbias = row1(hy_bias[i])
        hcw = w_hy_conv[i]
        hcb = row1(b_hy_conv[i])

        q, k4, v4, fa, ph = _project(x, sc1, sh1, g_pre, win, gq_t, gk_t, cos_x, sin_x,
                                     ones_bd, cs, rope=True, kv_only=False)
        if last:
            kc4, vc4 = _project(ctx, csc1, csh1, g_pre, (w_in_b, i, (k0, f0 - k0)), gq_t, gk_t,
                                cos_c, sin_c, ones_bd, cs, rope=False, kv_only=True)
        else:
            qc, kc4, vc4, fac, phc = _project(ctx, csc1, csh1, g_pre, win, gq_t, gk_t, cos_c,
                                              sin_c, ones_bd, cs, rope=False, kv_only=False)
        attn_x = _attention(q, [(k4, v4), (kc4, vc4)])
        four_x = _fourier(fa, fc_x, fs_x, w_f_b)
        kre, kim = _filter_spectrum(_hyena_filter(seq, *hy_params), hc_x, hs_x)
        z, x0c = _hyena_pre(ph, hcw, hcb)
        hy_x = _hyena(z, x0c, kre, kim, hbias, hc_x, hs_x)
        x_new = _mixout(attn_x, four_x, hy_x, wout, x, ga1, g_post)

        if not last:
            attn_c = _attention(qc, [(kc4, vc4)])
            four_c = _fourier(fac, fc_c, fs_c, w_f_b)
            kre_c, kim_c = _filter_spectrum(_hyena_filter(clen, *hy_params), hc_c, hs_c)
            zc, x0cc = _hyena_pre(phc, hcw, hcb)
            hy_c = _hyena(zc, x0cc, kre_c, kim_c, hbias, hc_c, hs_c)
            ctx = _mixout(attn_c, four_c, hy_c, wout, ctx, cga1, g_post)
        x = x_new

        g_pf = row1(g_pre_ffn[i])
        g_of = row1(g_post_ffn[i])
        fcw = w_ffn_conv[i]
        fcb = row1(b_ffn_conv[i])
        x = _ffn(x, sc2, sh2, g_pf, wup, fcw, fcb, wdown, ga2, g_of)
        if not last:
            ctx = _ffn(ctx, csc2, csh2, g_pf, wup, fcw, fcb, wdown, cga2, g_of)

    return x
```

```python
import functools
import math

import numpy as np
import jax
import jax.numpy as jnp
from jax import lax
from jax.experimental import pallas as pl
from jax.experimental.pallas import tpu as pltpu

F32 = jnp.float32
BF16 = jnp.bfloat16

HEAD_DIM = 64
GQA_GROUP = 4
N_KV_HEADS = 2
GRID_W = 64
ROPE_THETA = 10000.0
FOURIER_GROUP_DIM = 64
HYENA_BANDS = 16
HYENA_FAST_DECAY = 0.3
HYENA_SLOW_DECAY = 1.5
HYENA_TARGET = 1e-2
NORM_EPS = 1e-6
LANES = 128
VMEM_LIMIT = 56 * 1024 * 1024


def _cp(sem, vmem=VMEM_LIMIT):
    return pltpu.CompilerParams(dimension_semantics=sem, vmem_limit_bytes=vmem)


def _dot(a, b):
    return jnp.dot(a, b, preferred_element_type=F32)


def _dot_nt(a, b):
    return lax.dot_general(a, b, (((1,), (1,)), ((), ())), preferred_element_type=F32)


def _split(a):
    hi = a.astype(BF16)
    lo = (a - hi.astype(F32)).astype(BF16)
    return hi, lo


def _dot3(a, b):
    ah, al = _split(a)
    bh, bl = _split(b)
    return _dot(ah, bh) + _dot(ah, bl) + _dot(al, bh)


def _rms(x, g):
    ms = jnp.mean(x * x, axis=-1, keepdims=True)
    return x * lax.rsqrt(ms + NORM_EPS) * g


def _mod_spec(mod, layer, row_fn, slot):
    return pl.BlockSpec((None, None, None, 1, mod.shape[-1]),
                        lambda bi, i: (layer, row_fn(bi), slot, 0, 0))


def _layer_spec(w, layer, col_block=None, **kw):
    if col_block is None:
        return pl.BlockSpec((None,) + w.shape[1:], lambda bi, i: (layer, 0, 0), **kw)
    start, width = col_block
    return pl.BlockSpec((None, w.shape[1], width), lambda bi, i: (layer, 0, start // width), **kw)


def _tables_kernel(hc_ref, hs_ref, fc_ref, fs_ref, hca, hsa, fca, fsa, *, n, tk):
    i = pl.program_id(0)
    hyena_step = 2.0 * math.pi / (8 * n)
    fourier_step = 2.0 * math.pi / n

    @pl.when(i == 0)
    def _():
        k = lax.broadcasted_iota(jnp.int32, (tk, n), 0)
        s = lax.broadcasted_iota(jnp.int32, (tk, n), 1)
        a = (((2 * k + 1) * (2 * s + 1)) & (8 * n - 1)).astype(F32) * hyena_step
        hca[...] = jnp.cos(a)
        hsa[...] = jnp.sin(a)
        kh = lax.broadcasted_iota(jnp.int32, (tk, n // 2), 0)
        sh = lax.broadcasted_iota(jnp.int32, (tk, n // 2), 1)
        b = ((kh * sh) & (n - 1)).astype(F32) * fourier_step
        fca[...] = jnp.cos(b)
        fsa[...] = jnp.sin(b)

    s1 = lax.broadcasted_iota(jnp.int32, (1, n), 1)
    rot = (((2 * tk * i) * (2 * s1 + 1)) & (8 * n - 1)).astype(F32) * hyena_step
    cb, sb = jnp.cos(rot), jnp.sin(rot)
    hc_ref[...] = (hca[...] * cb - hsa[...] * sb).astype(BF16)
    hs_ref[...] = (hsa[...] * cb + hca[...] * sb).astype(BF16)
    s2 = lax.broadcasted_iota(jnp.int32, (1, n // 2), 1)
    rot = (((tk * i) * s2) & (n - 1)).astype(F32) * fourier_step
    cb, sb = jnp.cos(rot), jnp.sin(rot)
    fc_ref[...] = (fca[...] * cb - fsa[...] * sb).astype(BF16)
    fs_ref[...] = (fsa[...] * cb + fca[...] * sb).astype(BF16)


def _make_tables(n):
    tk = min(n, 128)
    spec = pl.BlockSpec((tk, n), lambda i: (i, 0))
    shp = jax.ShapeDtypeStruct((n, n), BF16)
    hspec = pl.BlockSpec((tk, n // 2), lambda i: (i, 0))
    hshp = jax.ShapeDtypeStruct((n, n // 2), BF16)
    return pl.pallas_call(
        functools.partial(_tables_kernel, n=n, tk=tk),
        out_shape=(shp, shp, hshp, hshp),
        grid=(n // tk,),
        out_specs=(spec, spec, hspec, hspec),
        scratch_shapes=[pltpu.VMEM((tk, n), F32)] * 2 + [pltpu.VMEM((tk, n // 2), F32)] * 2,
        compiler_params=_cp(("arbitrary",)),
        name=f"dft_tables_{n}",
    )()


def _mod_kernel(c_ref, w_ref, b_ref, o_ref):
    c = c_ref[...]
    a = c * jax.nn.sigmoid(c)
    o_ref[0] = _dot3(a, w_ref[0]) + b_ref[0]


def _modulation(cc, w_mod, b_mod):
    depth, d, n6 = w_mod.shape
    tn = 1536
    rows = cc.shape[0]
    return pl.pallas_call(
        _mod_kernel,
        out_shape=jax.ShapeDtypeStruct((depth, rows, n6), F32),
        grid=(depth, n6 // tn),
        in_specs=[
            pl.BlockSpec((rows, d), lambda l, j: (0, 0)),
            pl.BlockSpec((1, d, tn), lambda l, j: (l, 0, j)),
            pl.BlockSpec((1, 1, tn), lambda l, j: (l, 0, j)),
        ],
        out_specs=pl.BlockSpec((1, rows, tn), lambda l, j: (l, 0, j)),
        compiler_params=_cp(("parallel", "parallel")),
        name="modulation",
    )(cc, w_mod, b_mod.reshape(depth, 1, n6))


def _head_norm(p, gain, ones_bd):
    hi, lo = _split(p * p)
    ss = _dot(hi, ones_bd) + _dot(lo, ones_bd)
    return p * lax.rsqrt(ss * (1.0 / HEAD_DIM) + NORM_EPS) * gain


def _rope(xn, cos_t, sin_t):
    lane = lax.broadcasted_iota(jnp.int32, (1, LANES), 1)
    first = (lane % 32) < 16
    outs = []
    for j in range(xn.shape[1] // LANES):
        c = xn[:, j * LANES:(j + 1) * LANES]
        sw = jnp.where(first, pltpu.roll(c, LANES - 16, 1), pltpu.roll(c, 16, 1))
        outs.append(c * cos_t + sw * sin_t)
    return outs[0] if len(outs) == 1 else jnp.concatenate(outs, axis=1)


def _tile_heads(kv):
    lane = lax.broadcasted_iota(jnp.int32, (1, LANES), 1)
    low = lane < HEAD_DIM
    r = pltpu.roll(kv, HEAD_DIM, 1)
    h0 = jnp.where(low, kv, r)
    h1 = jnp.where(low, r, kv)
    return jnp.concatenate([h0, h0], axis=1), jnp.concatenate([h1, h1], axis=1)


VALUE_ROWS = 2 * HEAD_DIM


def _value_heads(v):
    lane = lax.broadcasted_iota(jnp.int32, (1, LANES), 1)
    low = lane < HEAD_DIM
    r = pltpu.roll(v, HEAD_DIM, 1)
    return jnp.where(low, v, 1.0), jnp.where(low, r, 1.0)


PROJ_SUB = 512


def _proj_kernel(x_ref, sc_ref, sh_ref, g_ref, w_ref, gq_ref, gk_ref, cos_ref, sin_ref,
                 ones_ref, cs_ref, *outs, rope, kv_only):
    tm = x_ref.shape[1]
    sub = min(tm, PROJ_SUB)
    ones_bd = ones_ref[...]
    for r0 in range(0, tm, sub):
        rs = slice(r0, r0 + sub)
        x = x_ref[0, rs, :]
        h = _rms(x, g_ref[...]) * (1.0 + sc_ref[...]) + sh_ref[...]
        px = _dot(h.astype(BF16), w_ref[...])
        if kv_only:
            k4_ref, v4_ref = outs
            k = px[:, 0:128]
            v = px[:, 128:256]
        else:
            q_ref, k4_ref, v4_ref, fa_ref, ph_ref = outs
            aw = ones_bd.shape[0]
            q = _head_norm(px[:, 0:aw], gq_ref[...], ones_bd)
            if rope:
                q = _rope(q, cos_ref[rs, :], sin_ref[rs, :])
            q_ref[0, rs, :] = (q * (HEAD_DIM ** -0.5 * math.log2(math.e))).astype(BF16)
            k = px[:, aw:aw + 128]
            v = px[:, aw + 128:aw + 256]
            f0 = aw + 256
            fw = cs_ref.shape[0]
            fa_ref[0, rs, :] = _dot(px[:, f0:f0 + fw].astype(BF16), cs_ref[...]).astype(BF16)
            ph_ref[0, rs, :] = px[:, f0 + fw:].astype(BF16)
        k = _head_norm(k, gk_ref[...], ones_bd[0:128, 0:128])
        if rope:
            k = _rope(k, cos_ref[rs, :], sin_ref[rs, :])
        k0, k1 = _tile_heads(k)
        k4_ref[0, 0, rs, :] = k0.astype(BF16)
        k4_ref[0, 1, rs, :] = k1.astype(BF16)
        v0, v1 = _value_heads(v)
        v4_ref[0, 0, :, rs] = v0.T[0:VALUE_ROWS].astype(BF16)
        v4_ref[0, 1, :, rs] = v1.T[0:VALUE_ROWS].astype(BF16)


def _project(x, sc, sh, g, w, gq, gk, cos_t, sin_t, ones_bd, cs, *, rope, kv_only):
    b, n, d = x.shape
    tm = min(n, 2 * PROJ_SUB)
    wn = w[0].shape[2]
    aw = ones_bd.shape[0]
    fw = cs.shape[0]
    full = lambda a: pl.BlockSpec(a.shape, lambda bi, i: (0,) * a.ndim)
    kv_spec = pl.BlockSpec((1, N_KV_HEADS, tm, 256), lambda bi, i: (bi, 0, i, 0))
    kv_shape = jax.ShapeDtypeStruct((b, N_KV_HEADS, n, 256), BF16)
    vl_spec = pl.BlockSpec((1, N_KV_HEADS, VALUE_ROWS, tm), lambda bi, i: (bi, 0, 0, i))
    vl_shape = jax.ShapeDtypeStruct((b, N_KV_HEADS, VALUE_ROWS, n), BF16)
    row = lambda width: pl.BlockSpec((1, tm, width), lambda bi, i: (bi, i, 0))
    if kv_only:
        out_shape = (kv_shape, vl_shape)
        out_specs = (kv_spec, vl_spec)
    else:
        hw = wn - aw - 256 - fw
        out_shape = (jax.ShapeDtypeStruct((b, n, aw), BF16), kv_shape, vl_shape,
                     jax.ShapeDtypeStruct((b, n, 2 * fw), BF16),
                     jax.ShapeDtypeStruct((b, n, hw), BF16))
        out_specs = (row(aw), kv_spec, vl_spec, row(2 * fw), row(hw))
    return pl.pallas_call(
        functools.partial(_proj_kernel, rope=rope, kv_only=kv_only),
        out_shape=out_shape,
        grid=(b, n // tm),
        in_specs=[
            row(d), _mod_spec(*sc), _mod_spec(*sh), full(g), _layer_spec(*w), full(gq), full(gk),
            pl.BlockSpec((tm, LANES), lambda bi, i: (i, 0)),
            pl.BlockSpec((tm, LANES), lambda bi, i: (i, 0)),
            full(ones_bd), full(cs),
        ],
        out_specs=out_specs,
        compiler_params=_cp(("parallel", "parallel")),
        name="proj_kv" if kv_only else ("proj_rope" if rope else "proj_ctx"),
    )(x, sc[0], sh[0], g, w[0], gq, gk, cos_t, sin_t, ones_bd, cs)


ATTN_SUB = 128
ATTN_KEY_CHUNK = 1024


ATTN_PROB_ROWS = 128


def _attn_kernel(q_ref, *refs, lks, nsub):
    nsrc = len(lks)
    k_refs = refs[0:2 * nsrc:2]
    v_refs = refs[1:2 * nsrc:2]
    o_ref = refs[2 * nsrc]
    s_scrs = refs[2 * nsrc + 1:2 * nsrc + 3]
    p_scrs = refs[2 * nsrc + 3:2 * nsrc + 5]
    sub = ATTN_SUB
    rows = GQA_GROUP * sub
    gw = GQA_GROUP * HEAD_DIM
    group = lax.broadcasted_iota(jnp.int32, (1, gw), 1) // HEAD_DIM

    chunks = []
    off = 0
    for j, lk in enumerate(lks):
        tkc = min(lk, ATTN_KEY_CHUNK)
        for c in range(lk // tkc):
            chunks.append((j, c * tkc, tkc, off))
            off += tkc

    lane = lax.broadcasted_iota(jnp.int32, (1, LANES), 1)
    low = lane < HEAD_DIM

    def masked_q(u):
        q = q_ref[0, u * sub:(u + 1) * sub, :]
        zero = jnp.zeros_like(q)
        return jnp.concatenate([jnp.where(group == g, q, zero) for g in range(GQA_GROUP)], axis=0)

    def finish(u, acc):
        o_t = acc / acc[HEAD_DIM:HEAD_DIM + 1, :]
        heads = [o_t[:, g * sub:(g + 1) * sub].T for g in range(GQA_GROUP)]
        left = jnp.where(low, heads[0], pltpu.roll(heads[1], HEAD_DIM, 1))
        right = jnp.where(low, heads[2], pltpu.roll(heads[3], HEAD_DIM, 1))
        o_ref[0, u * sub:(u + 1) * sub, :] = jnp.concatenate([left, right], axis=1).astype(BF16)

    def scores(u):
        s_scr = s_scrs[u % 2]
        qm = masked_q(u)
        part = jnp.full((8, rows), -jnp.inf, F32)
        for j, start, tkc, o in chunks:
            s = _dot_nt(k_refs[j][0, 0, start:start + tkc, :], qm)
            s_scr[o:o + tkc, :] = s
            part = jnp.maximum(part, jnp.max(s.reshape(tkc // 8, 8, rows), axis=0))
        return jnp.max(part, axis=0, keepdims=True), s[tkc - 8:tkc, :]

    m, _ = scores(0)
    for u in range(nsub):
        s_scr = s_scrs[u % 2]
        acc = jnp.zeros((VALUE_ROWS, rows), F32)
        if u + 1 < nsub:
            m_next, tail = scores(u + 1)
            bits = pltpu.bitcast(tail, jnp.uint32)
            acc = jnp.concatenate([((bits >> 16) >> 16).astype(F32),
                                   jnp.zeros((VALUE_ROWS - 8, rows), F32)], axis=0)
        for ci, (j, start, tkc, o) in enumerate(chunks):
            p_scr = p_scrs[ci % 2]
            for r in range(0, tkc, ATTN_PROB_ROWS):
                p_scr[r:r + ATTN_PROB_ROWS, :] = jnp.exp2(
                    s_scr[o + r:o + r + ATTN_PROB_ROWS, :] - m).astype(BF16)
            acc = acc + _dot(v_refs[j][0, 0, :, start:start + tkc], p_scr[0:tkc, :])
        finish(u, acc)
        if u + 1 < nsub:
            m = m_next


def _attention(q, kvs):
    b, lq, aw = q.shape
    tq = min(lq, 8 * ATTN_SUB)
    nsub = tq // ATTN_SUB
    lks = tuple(k.shape[2] for k, _ in kvs)
    gw = GQA_GROUP * HEAD_DIM
    in_specs = [pl.BlockSpec((1, tq, gw), lambda bi, h, i: (bi, i, h))]
    args = [q]
    for k4, vlt in kvs:
        for a in (k4, vlt):
            in_specs.append(pl.BlockSpec((1, 1) + a.shape[2:], lambda bi, h, i: (bi, h, 0, 0),
                                         pipeline_mode=pl.Buffered(1)))
        args += [k4, vlt]
    return pl.pallas_call(
        functools.partial(_attn_kernel, lks=lks, nsub=nsub),
        out_shape=jax.ShapeDtypeStruct((b, lq, aw), BF16),
        grid=(b, N_KV_HEADS, lq // tq),
        in_specs=in_specs,
        out_specs=pl.BlockSpec((1, tq, gw), lambda bi, h, i: (bi, i, h)),
        scratch_shapes=[pltpu.VMEM((sum(lks), GQA_GROUP * ATTN_SUB), F32)] * 2
        + [pltpu.VMEM((min(max(lks), ATTN_KEY_CHUNK), GQA_GROUP * ATTN_SUB), BF16)] * 2,
        compiler_params=_cp(("parallel", "parallel", "arbitrary")),
        name=f"attention_{lq}",
    )(*args)


def _fourier_kernel(c_ref, s_ref, fa_ref, w_ref, o_ref, fold_scr, nyq_scr, *, nb, scale, tk):
    fw = w_ref.shape[0]
    n = fa_ref.shape[1]
    half = n // 2
    blk = min(half, 512)

    @pl.when(pl.program_id(1) == 0)
    def _():
        r = lax.broadcasted_iota(jnp.int32, (blk, blk), 0)
        c = lax.broadcasted_iota(jnp.int32, (blk, blk), 1)
        flip = jnp.where(r + c == blk - 1, 1.0, 0.0).astype(BF16)
        row = lax.broadcasted_iota(jnp.int32, (half, 1), 0)
        sign = jnp.where(lax.broadcasted_iota(jnp.int32, (1, 2 * fw), 1) < fw, 1.0, -1.0)
        for j in range(nb):
            rev = jnp.concatenate(
                [_dot(flip, fa_ref[j, n - (b + 1) * blk:n - b * blk, :]) for b in range(half // blk)],
                axis=0)
            mirrored = jnp.where(row == 0, 0.0, pltpu.roll(rev, 1, 0))
            fold_scr[j] = (fa_ref[j, 0:half, :].astype(F32) + sign * mirrored).astype(BF16)
            nyq_scr[j] = jnp.broadcast_to(fa_ref[j, half:half + 16, 0:fw][0:1].astype(F32), (8, fw))

    ct = c_ref[...]
    st = s_ref[...]
    odd = (lax.broadcasted_iota(jnp.int32, (tk, 1), 0) & 1) == 1
    alt = jnp.where(odd, -1.0, 1.0)
    for j in range(nb):
        y = (_dot(ct, fold_scr[j, :, 0:fw]) - _dot(st, fold_scr[j, :, fw:2 * fw])
             + alt * nyq_scr[j, 0:1, :])
        o_ref[j] = _dot((y * scale).astype(BF16), w_ref[...]).astype(BF16)


def _fourier(fa, ctab, stab, w_f):
    b, n, fw2 = fa.shape
    fw = fw2 // 2
    nb = 2
    tk = min(n, 1024)
    half = n // 2
    scale = 1.0 / math.sqrt(n * FOURIER_GROUP_DIM)
    return pl.pallas_call(
        functools.partial(_fourier_kernel, nb=nb, scale=scale, tk=tk),
        out_shape=jax.ShapeDtypeStruct((b, n, fw), BF16),
        grid=(b // nb, n // tk),
        in_specs=[
            pl.BlockSpec((tk, half), lambda g, i: (i, 0)),
            pl.BlockSpec((tk, half), lambda g, i: (i, 0)),
            pl.BlockSpec((nb, n, fw2), lambda g, i: (g, 0, 0)),
            pl.BlockSpec((fw, fw), lambda g, i: (0, 0)),
        ],
        out_specs=pl.BlockSpec((nb, tk, fw), lambda g, i: (g, i, 0)),
        scratch_shapes=[pltpu.VMEM((nb, half, fw2), BF16), pltpu.VMEM((nb, 8, fw), F32)],
        compiler_params=_cp(("parallel", "arbitrary")),
        name=f"fourier_{n}",
    )(ctab, stab, fa, w_f)


def _conv3(p, w, bias):
    n = p.shape[0]
    row = lax.broadcasted_iota(jnp.int32, (n, 1), 0)
    prev = jnp.where(row == 0, 0.0, pltpu.roll(p, 1, 0))
    nxt = jnp.where(row == n - 1, 0.0, pltpu.roll(p, n - 1, 0))
    return prev * w[0:1] + p * w[1:2] + nxt * w[2:3] + bias


def _hyena_pre_kernel(p0_ref, p1_ref, p2_ref, w0_ref, w1_ref, w2_ref, b0_ref, b1_ref, b2_ref,
                      z_ref, x0_ref):
    x0 = _conv3(p0_ref[0].astype(F32), w0_ref[...], b0_ref[...])
    x1 = _conv3(p1_ref[0].astype(F32), w1_ref[...], b1_ref[...])
    v = _conv3(p2_ref[0].astype(F32), w2_ref[...], b2_ref[...])
    z_ref[0] = (x1 * v).astype(BF16)
    x0_ref[0] = x0.astype(BF16)


def _hyena_pre(ph, w, bias):
    b, n, hw3 = ph.shape
    hw = hw3 // 3
    nc = hw // LANES
    pspec = lambda s: pl.BlockSpec((1, n, LANES), lambda bi, c: (bi, 0, s * nc + c))
    wspec = lambda s: pl.BlockSpec((3, LANES), lambda bi, c: (0, s * nc + c))
    bspec = lambda s: pl.BlockSpec((1, LANES), lambda bi, c: (0, s * nc + c))
    ospec = pl.BlockSpec((1, n, LANES), lambda bi, c: (bi, 0, c))
    oshape = jax.ShapeDtypeStruct((b, n, hw), BF16)
    return pl.pallas_call(
        _hyena_pre_kernel,
        out_shape=(oshape, oshape),
        grid=(b, nc),
        in_specs=[pspec(0), pspec(1), pspec(2), wspec(0), wspec(1), wspec(2),
                  bspec(0), bspec(1), bspec(2)],
        out_specs=(ospec, ospec),
        compiler_params=_cp(("parallel", "parallel")),
        name=f"hyena_pre_{n}",
    )(ph, ph, ph, w, w, w, bias, bias, bias)


def _filter_kernel(w1t_ref, w1c_ref, w1s_ref, b1_ref, fr1_ref, w2_ref, b2_ref, fr2_ref, w3_ref,
                   g_ref, *, n):
    hw = w3_ref.shape[1] // 2
    i = lax.broadcasted_iota(jnp.int32, (n, 1), 0).astype(F32)
    t = i / float(n - 1)
    jb = lax.broadcasted_iota(jnp.int32, (1, HYENA_BANDS), 1).astype(F32)
    bands = 1e-4 + jb * ((HYENA_BANDS - 1 - 1e-4) / (HYENA_BANDS - 1))
    ang = ((2.0 * math.pi) * i / float(n)) * bands
    pre = t * w1t_ref[...] + _dot3(jnp.cos(ang), w1c_ref[...]) - _dot3(jnp.sin(ang), w1s_ref[...])
    h = jnp.sin(fr1_ref[...] * (pre + b1_ref[...]))
    h = jnp.sin(fr2_ref[...] * (_dot3(h, w2_ref[...]) + b2_ref[...]))
    h = _dot3(h, w3_ref[...])
    d0 = math.log(HYENA_TARGET) / HYENA_SLOW_DECAY
    d1 = math.log(HYENA_TARGET) / HYENA_FAST_DECAY
    jd = lax.broadcasted_iota(jnp.int32, (1, hw), 1).astype(F32)
    deltas = jnp.abs(d0 + jd * ((d1 - d0) / (hw - 1)))
    decay = jnp.exp(-t * deltas)
    hf = h[:, 0:hw] * decay
    hb = jnp.where(i == 0.0, 0.0, h[:, hw:2 * hw] * decay)
    total = jnp.sum(jnp.abs(hf), axis=0, keepdims=True) + jnp.sum(jnp.abs(hb), axis=0, keepdims=True)
    g_ref[...] = jnp.concatenate([hf / total, hb / total], axis=1).astype(BF16)


def _hyena_filter(n, w1, b1, fr1, w2, b2, fr2, w3):
    nb = HYENA_BANDS
    r = lambda a: a.reshape(1, -1)
    shp = jax.ShapeDtypeStruct((n, w3.shape[1]), BF16)
    return pl.pallas_call(
        functools.partial(_filter_kernel, n=n),
        out_shape=shp,
        compiler_params=pltpu.CompilerParams(vmem_limit_bytes=VMEM_LIMIT),
        name=f"hyena_filter_{n}",
    )(w1[0:1], w1[1:1 + nb], w1[1 + nb:1 + 2 * nb], r(b1), r(fr1), w2, r(b2), r(fr2), w3)


def _spectrum_kernel(c_ref, s_ref, g_ref, ck_ref, sk_ref, kre_ref, kim_ref, *, scale):
    hw = kre_ref.shape[1]
    a = _dot(c_ref[...], g_ref[...])
    b = _dot(s_ref[...], g_ref[...])
    are, bre = a[:, 0:hw], a[:, hw:2 * hw]
    aim, bim = -b[:, 0:hw], -b[:, hw:2 * hw]
    ck = ck_ref[...]
    sk = sk_ref[...]
    kre_ref[...] = (ck * (are + bre) - sk * (aim + bim)) * scale
    kim_ref[...] = (ck * (aim - bim) + sk * (are - bre)) * scale


def _filter_spectrum(g, ctab, stab):
    n, hw2 = g.shape
    hw = hw2 // 2
    tk = min(n, 1024)
    half = (2.0 * np.arange(n, dtype=np.float64) + 1.0) * (2.0 * np.pi / (8 * n))
    ck = jnp.asarray(np.cos(half).reshape(n, 1), F32)
    sk = jnp.asarray(np.sin(half).reshape(n, 1), F32)
    tspec = pl.BlockSpec((tk, n), lambda i: (i, 0))
    gspec = pl.BlockSpec((n, hw2), lambda i: (0, 0))
    vspec = pl.BlockSpec((tk, 1), lambda i: (i, 0))
    ospec = pl.BlockSpec((tk, hw), lambda i: (i, 0))
    oshape = jax.ShapeDtypeStruct((n, hw), F32)
    return pl.pallas_call(
        functools.partial(_spectrum_kernel, scale=1.0 / n),
        out_shape=(oshape, oshape),
        grid=(n // tk,),
        in_specs=[tspec, tspec, gspec, vspec, vspec],
        out_specs=(ospec, ospec),
        compiler_params=_cp(("parallel",)),
        name=f"filter_spectrum_{n}",
    )(ctab, stab, g, ck, sk)


def _hyena_kernel(c_ref, s_ref, z_ref, zt_ref, x0_ref, kre_ref, kim_ref, bias_ref, o_ref,
                  yre_scr, yim_scr, *, nb, tk):
    phase = pl.program_id(1)
    i = pl.program_id(2)
    ct = c_ref[...]
    st = s_ref[...]

    @pl.when(phase == 0)
    def _():
        kre = kre_ref[...]
        kim = kim_ref[...]
        r0 = pl.multiple_of(i * tk, tk)
        for j in range(nb):
            ure = _dot(ct, z_ref[j])
            uim = -_dot(st, z_ref[j])
            yre_scr[j, pl.ds(r0, tk), :] = (kre * ure - kim * uim).astype(BF16)
            yim_scr[j, pl.ds(r0, tk), :] = (kre * uim + kim * ure).astype(BF16)

    @pl.when(phase == 1)
    def _():
        for j in range(nb):
            y = _dot(ct, yre_scr[j]) - _dot(st, yim_scr[j])
            u = zt_ref[j].astype(F32)
            o_ref[j] = (x0_ref[j].astype(F32) * (y + u * bias_ref[...])).astype(BF16)


def _hyena(z, x0c, kre, kim, bias, ctab, stab):
    b, n, hw = z.shape
    nb = 2
    tk = min(n, 512)
    tspec = pl.BlockSpec((tk, n), lambda g, p, i: (i, 0))
    tile = pl.BlockSpec((nb, tk, hw), lambda g, p, i: (g, i * p, 0))
    kspec = pl.BlockSpec((tk, hw), lambda g, p, i: (i * (1 - p), 0))
    return pl.pallas_call(
        functools.partial(_hyena_kernel, nb=nb, tk=tk),
        out_shape=jax.ShapeDtypeStruct((b, n, hw), BF16),
        grid=(b // nb, 2, n // tk),
        in_specs=[
            tspec, tspec,
            pl.BlockSpec((nb, n, hw), lambda g, p, i: (g, 0, 0)),
            tile, tile, kspec, kspec,
            pl.BlockSpec((1, hw), lambda g, p, i: (0, 0)),
        ],
        out_specs=tile,
        scratch_shapes=[pltpu.VMEM((nb, n, hw), BF16), pltpu.VMEM((nb, n, hw), BF16)],
        compiler_params=_cp(("parallel", "arbitrary", "arbitrary")),
        name=f"hyena_conv_{n}",
    )(ctab, stab, z, z, x0c, kre, kim, bias)


def _mixout_kernel(a_ref, f_ref, h_ref, w_ref, x_ref, ga_ref, g_ref, o_ref):
    aw = a_ref.shape[2]
    fw = f_ref.shape[2]
    mix = (_dot(a_ref[0], w_ref[0:aw, :]) + _dot(f_ref[0], w_ref[aw:aw + fw, :])
           + _dot(h_ref[0], w_ref[aw + fw:, :]))
    o_ref[0] = x_ref[0] + ga_ref[...] * _rms(mix, g_ref[...])


def _mixout(attn, four, hy, w_out, x, ga, g):
    b, n, d = x.shape
    tm = min(n, 2048)
    row = lambda width: pl.BlockSpec((1, tm, width), lambda bi, i: (bi, i, 0))
    return pl.pallas_call(
        _mixout_kernel,
        out_shape=jax.ShapeDtypeStruct((b, n, d), F32),
        grid=(b, n // tm),
        in_specs=[
            row(attn.shape[2]), row(four.shape[2]), row(hy.shape[2]),
            _layer_spec(*w_out),
            row(d),
            _mod_spec(*ga),
            pl.BlockSpec((1, d), lambda bi, i: (0, 0)),
        ],
        out_specs=row(d),
        compiler_params=_cp(("parallel", "parallel")),
        name=f"mixout_{n}",
    )(attn, four, hy, w_out[0], x, ga[0], g)


HALO = 8


FFN_CHUNK = 256
FFN_SUB = 512


def _ffn_kernel(xp_ref, x_ref, xn_ref, sc_ref, sh_ref, g_ref, wu_ref, cw_ref, cb_ref, wd_ref,
                ga_ref, go_ref, o_ref, *, tm):
    i = pl.program_id(1)
    nrow = pl.num_programs(1)
    sub = min(tm, FFN_SUB)
    rows = sub + 2 * HALO
    dff = wd_ref.shape[0]
    cf = FFN_CHUNK
    g = g_ref[...]
    sc = 1.0 + sc_ref[...]
    sh = sh_ref[...]

    def pre(xr):
        return _rms(xr, g) * sc + sh

    def conv(u, w, bias):
        prev = pltpu.roll(u, 1, 0)[HALO:HALO + sub]
        nxt = pltpu.roll(u, rows - 1, 0)[HALO:HALO + sub]
        return prev * w[0:1] + u[HALO:HALO + sub] * w[1:2] + nxt * w[2:3] + bias

    for r0 in range(0, tm, sub):
        x = x_ref[0, r0:r0 + sub, :]
        if r0 == 0:
            before = pre(xp_ref[0]) * jnp.where(i > 0, 1.0, 0.0)
        else:
            before = pre(x_ref[0, r0 - HALO:r0, :])
        if r0 + sub == tm:
            after = pre(xn_ref[0]) * jnp.where(i < nrow - 1, 1.0, 0.0)
        else:
            after = pre(x_ref[0, r0 + sub:r0 + sub + HALO, :])
        fx = jnp.concatenate([before, pre(x), after], axis=0).astype(BF16)
        acts = []
        for c in range(dff // cf):
            lo, hi = c * cf, (c + 1) * cf
            gate = conv(_dot(fx, wu_ref[:, lo:hi]), cw_ref[:, lo:hi], cb_ref[:, lo:hi])
            val = conv(_dot(fx, wu_ref[:, dff + lo:dff + hi]), cw_ref[:, dff + lo:dff + hi],
                       cb_ref[:, dff + lo:dff + hi])
            acts.append((gate * jax.nn.sigmoid(gate) * val).astype(BF16))
        y = _dot(jnp.concatenate(acts, axis=1), wd_ref[...])
        o_ref[0, r0:r0 + sub, :] = x + ga_ref[...] * _rms(y, go_ref[...])


def _ffn(x, sc, sh, g, w_up, w_conv, b_conv, w_down, ga, g_post):
    b, n, d = x.shape
    tm = min(n, 2 * FFN_SUB)
    hb = tm // HALO
    nhb = n // HALO
    gvec =pl.BlockSpec((1, d), lambda bi, i: (0, 0))
    resident = lambda a: pl.BlockSpec(a.shape, lambda bi, i: (0, 0), pipeline_mode=pl.Buffered(1))
    stacked = lambda w: _layer_spec(*w, pipeline_mode=pl.Buffered(1))
    return pl.pallas_call(
        functools.partial(_ffn_kernel, tm=tm),
        out_shape=jax.ShapeDtypeStruct((b, n, d), F32),
        grid=(b, n // tm),
        in_specs=[
            pl.BlockSpec((1, HALO, d), lambda bi, i: (bi, jnp.maximum(i * hb - 1, 0), 0)),
            pl.BlockSpec((1, tm, d), lambda bi, i: (bi, i, 0)),
            pl.BlockSpec((1, HALO, d), lambda bi, i: (bi, jnp.minimum((i + 1) * hb, nhb - 1), 0)),
            _mod_spec(*sc), _mod_spec(*sh), gvec,
            stacked(w_up), resident(w_conv), resident(b_conv), stacked(w_down),
            _mod_spec(*ga), gvec,
        ],
        out_specs=pl.BlockSpec((1, tm, d), lambda bi, i: (bi, i, 0)),
        compiler_params=_cp(("parallel", "parallel")),
        name=f"conv_ffn_{n}",
    )(x, x, x, sc[0], sh[0], g, w_up[0], w_conv, b_conv, w_down[0], ga[0], g_post)


def _rope_tables(n):
    half = HEAD_DIM // 4
    inv = ROPE_THETA ** (-jnp.arange(0, 2 * half, 2, dtype=F32) / (2 * half))
    pos = jnp.arange(n, dtype=jnp.int32)
    row = (pos // GRID_W).astype(F32)
    col = (pos % GRID_W).astype(F32)
    ang_r = row[:, None] * inv[None, :]
    ang_c = col[:, None] * inv[None, :]
    cr, sr, cc, sc = jnp.cos(ang_r), jnp.sin(ang_r), jnp.cos(ang_c), jnp.sin(ang_c)
    cos64 = jnp.concatenate([cr, cr, cc, cc], axis=-1)
    sin64 = jnp.concatenate([-sr, sr, -sc, sc], axis=-1)
    return jnp.tile(cos64, (1, 2)), jnp.tile(sin64, (1, 2))


def _head_ones(width):
    idx = np.arange(width) // HEAD_DIM
    return jnp.asarray((idx[:, None] == idx[None, :]).astype(np.float32), BF16)


def _channel_dft(width):
    c = np.arange(width)
    same = (c[:, None] // FOURIER_GROUP_DIM) == (c[None, :] // FOURIER_GROUP_DIM)
    ang = 2.0 * np.pi * ((c[:, None] % FOURIER_GROUP_DIM) * (c[None, :] % FOURIER_GROUP_DIM)
                         % FOURIER_GROUP_DIM) / FOURIER_GROUP_DIM
    cs = np.concatenate([np.where(same, np.cos(ang), 0.0), np.where(same, np.sin(ang), 0.0)], axis=1)
    return jnp.asarray(cs.astype(np.float32), BF16)


def kernel(x, c, ctx, c_ctx, w_mod, b_mod, g_pre_mix, g_post_mix, g_pre_ffn, g_post_ffn, w_in, g_q, g_k, w_fourier, w_hy_conv, b_hy_conv, hy_w1, hy_b1, hy_fr1, hy_w2, hy_b2, hy_fr2, hy_w3, hy_bias, w_out, w_up, w_ffn_conv, b_ffn_conv, w_down):
    bsz, seq, d = x.shape
    clen = ctx.shape[1]
    depth = w_mod.shape[0]
    fw = w_fourier.shape[1]
    hw = hy_bias.shape[1]
    aw = w_in.shape[2] - 2 * N_KV_HEADS * HEAD_DIM - fw - 3 * hw
    k0 = aw
    f0 = aw + 2 * N_KV_HEADS * HEAD_DIM

    nrows = -(-(bsz + 1) // 8) * 8
    cc = jnp.zeros((nrows, d), F32).at[:bsz].set(c).at[bsz].set(c_ctx)
    mods = _modulation(cc, w_mod, b_mod).reshape(depth, nrows, 6, 1, d)
    w_in_b = w_in.astype(BF16)
    w_out_b = w_out.astype(BF16)
    w_up_b = w_up.astype(BF16)
    w_down_b = w_down.astype(BF16)

    cos_x, sin_x = _rope_tables(seq)
    cos_c = jnp.ones((clen, LANES), F32)
    sin_c = jnp.zeros((clen, LANES), F32)
    ones_bd = _head_ones(aw)
    cs = _channel_dft(fw)
    hc_x, hs_x, fc_x, fs_x = _make_tables(seq)
    hc_c, hs_c, fc_c, fs_c = _make_tables(clen)

    def row1(v):
        return v.reshape(1, -1)

    for i in range(depth):
        last = i == depth - 1
        sh1, sc1, ga1, sh2, sc2, ga2 = ((mods, i, lambda bi: bi, t) for t in range(6))
        csh1, csc1, cga1, csh2, csc2, cga2 = ((mods, i, lambda bi: bsz, t) for t in range(6))
        win, wout, wup, wdown = ((w, i, None) for w in (w_in_b, w_out_b, w_up_b, w_down_b))
        w_f_b = w_fourier[i].astype(BF16)
        gq_t = jnp.tile(g_q[i], aw // HEAD_DIM).reshape(1, aw)
        gk_t = jnp.tile(g_k[i], N_KV_HEADS).reshape(1, N_KV_HEADS * HEAD_DIM)
        g_pre = row1(g_pre_mix[i])
        g_post = row1(g_post_mix[i])
        hy_params = (hy_w1[i], hy_b1[i], hy_fr1[i], hy_w2[i], hy_b2[i], hy_fr2[i], hy_w3[i])
        hbias = row1(hy_bias[i])
        hcw = w_hy_conv[i]
        hcb = row1(b_hy_conv[i])

        q, k4, v4, fa, ph = _project(x, sc1, sh1, g_pre, win, gq_t, gk_t, cos_x, sin_x,
                                     ones_bd, cs, rope=True, kv_only=False)
        if last:
            kc4, vc4 = _project(ctx, csc1, csh1, g_pre, (w_in_b, i, (k0, f0 - k0)), gq_t, gk_t,
                                cos_c, sin_c, ones_bd, cs, rope=False, kv_only=True)
        else:
            qc, kc4, vc4, fac, phc = _project(ctx, csc1, csh1, g_pre, win, gq_t, gk_t, cos_c,
                                              sin_c, ones_bd, cs, rope=False, kv_only=False)
        attn_x = _attention(q, [(k4, v4), (kc4, vc4)])
        four_x = _fourier(fa, fc_x, fs_x, w_f_b)
        kre, kim = _filter_spectrum(_hyena_filter(seq, *hy_params), hc_x, hs_x)
        z, x0c = _hyena_pre(ph, hcw, hcb)
        hy_x = _hyena(z, x0c, kre, kim, hbias, hc_x, hs_x)
        x_new = _mixout(attn_x, four_x, hy_x, wout, x, ga1, g_post)

        if not last:
            attn_c = _attention(qc, [(kc4, vc4)])
            four_c = _fourier(fac, fc_c, fs_c, w_f_b)
            kre_c, kim_c = _filter_spectrum(_hyena_filter(clen, *hy_params), hc_c, hs_c)
            zc, x0cc = _hyena_pre(phc, hcw, hcb)
            hy_c = _hyena(zc, x0cc, kre_c, kim_c, hbias, hc_c, hs_c)
            ctx = _mixout(attn_c, four_c, hy_c, wout, ctx, cga1, g_post)
        x = x_new

        g_pf = row1(g_pre_ffn[i])
        g_of = row1(g_post_ffn[i])
        fcw = w_ffn_conv[i]
        fcb = row1(b_ffn_conv[i])
        x = _ffn(x, sc2, sh2, g_pf, wup, fcw, fcb, wdown, ga2, g_of)
        if not last:
            ctx = _ffn(ctx, csc2, csh2, g_pf, wup, fcw, fcb, wdown, cga2, g_of)

    return x
```
